```python
import jax, jax.numpy as jnp
from jax import lax
import numpy as np

D_MODEL = 1024
BATCH = 8
SEQ = 4096
DEPTH = 2

MEM_LEN = 256
N_EVEN = (DEPTH + 1) // 2
N_ODD = DEPTH // 2
PF_WIDTH = D_MODEL
POOL_GROUPS = 4
POOL_WINDOWS = (2, 4, 8, 16)
POOL_DIM = PF_WIDTH // 2 // POOL_GROUPS
FOURIER_HEADS = 4
FOURIER_DIM = PF_WIDTH // 2 // FOURIER_HEADS
HG_EXPAND = 128
HG_HEADS = D_MODEL // HG_EXPAND
HG_DIM = HG_HEADS * HG_EXPAND
HG_N_PROJ = 5
HG_CHUNK = 64
XA_HEADS = 4
XA_HEAD_DIM = D_MODEL // XA_HEADS
MOE_GROUPS = 4
MOE_PER_GROUP = 8
MOE_EXPERTS = MOE_GROUPS * MOE_PER_GROUP
MOE_TOPK = 2
MOE_HIDDEN = D_MODEL // 2
N_SUBLAYERS = 3
DN_ALPHA = (2.0 * DEPTH) ** 0.25
DN_BETA = (8.0 * DEPTH) ** -0.25
LN_EPS = 1e-5

kernel_name = "hybrid_pool_fourier_hgrn2_hmoe_encoder"


def layer_norm(x, g, b):
    xf = x.astype(jnp.float32)
    mu = jnp.mean(xf, axis=-1, keepdims=True)
    var = jnp.mean(jnp.square(xf - mu), axis=-1, keepdims=True)
    y = (xf - mu) * lax.rsqrt(var + LN_EPS) * g.astype(jnp.float32) + b.astype(jnp.float32)
    return y.astype(x.dtype)


def centred_pool_minus_identity(u):
    S = u.shape[1]
    uf = u.astype(jnp.float32)
    cs = jnp.cumsum(uf, axis=1)
    cs = jnp.concatenate([jnp.zeros_like(cs[:, :1]), cs], axis=1)
    t = np.arange(S)
    outs = []
    for gi, w in enumerate(POOL_WINDOWS):
        lo = np.clip(t - w // 2, 0, S)
        hi = np.clip(t + w // 2, 0, S)
        cnt = (hi - lo).astype(np.float32)
        window_sum = cs[:, hi, gi] - cs[:, lo, gi]
        outs.append(window_sum / cnt[None, :, None])
    pooled = jnp.stack(outs, axis=2)
    return (pooled - uf).astype(u.dtype)


def fourier_heads(u, ln_g):
    uf = u.astype(jnp.float32)
    mu = jnp.mean(uf, axis=-1, keepdims=True)
    var = jnp.mean(jnp.square(uf - mu), axis=-1, keepdims=True)
    un = (uf - mu) * lax.rsqrt(var + LN_EPS) * ln_g.astype(jnp.float32)
    return jnp.real(jnp.fft.fft2(un, axes=(1, 3), norm="ortho")).astype(u.dtype)


def pool_fourier_mixer(x, w_in, pool_w, pool_scale, fourier_ln_g, fourier_w, w_out):
    B, S, _ = x.shape
    half = PF_WIDTH // 2
    u = x @ w_in
    ua = u[..., :half].reshape(B, S, POOL_GROUPS, POOL_DIM)
    ub = u[..., half:].reshape(B, S, FOURIER_HEADS, FOURIER_DIM)
    ya = jnp.einsum('bsgc,gcd->bsgd', centred_pool_minus_identity(ua), pool_w)
    ya = ya.reshape(B, S, half) * pool_scale
    yb = jnp.einsum('bshc,hcd->bshd', fourier_heads(ub, fourier_ln_g), fourier_w)
    yb = yb.reshape(B, S, half)
    return jnp.concatenate([ya, yb], axis=-1) @ w_out


def gla_chunked(q, k, v, logf):
    B, S, H, Dk = q.shape
    Dv = v.shape[-1]
    L = HG_CHUNK
    NC = S // L

    def chunks(a):
        return a.reshape(B, NC, L, H, a.shape[-1]).transpose(0, 3, 1, 2, 4)

    q, k, v, logf = chunks(q), chunks(k), chunks(v), chunks(logf)
    b = jnp.cumsum(logf, axis=3)
    b_ref = b[:, :, :, L // 2:L // 2 + 1]
    qt = q * jnp.exp(b - b_ref)
    kt = k * jnp.exp(b_ref - b)
    scores = jnp.einsum('bhnlk,bhnmk->bhnlm', qt, kt)
    tril = np.tril(np.ones((L, L), dtype=bool))
    scores = jnp.where(tril, scores, 0.0)
    o_intra = jnp.einsum('bhnlm,bhnmv->bhnlv', scores, v)
    b_last = b[:, :, :, -1:]
    u_chunk = jnp.einsum('bhnlk,bhnlv->bhnkv', k * jnp.exp(b_last - b), v)
    decay = jnp.exp(b_last[:, :, :, 0])

    def step(state, inp):
        d, u_c = inp
        return d[..., None] * state + u_c, state

    _, s_prev = lax.scan(step, jnp.zeros((B, H, Dk, Dv), jnp.float32),
                         (jnp.moveaxis(decay, 2, 0), jnp.moveaxis(u_chunk, 2, 0)))
    s_prev = jnp.moveaxis(s_prev, 0, 2)
    o_inter = jnp.einsum('bhnlk,bhnkv->bhnlv', q * jnp.exp(b), s_prev)
    o = o_intra + o_inter
    return o.transpose(0, 2, 3, 1, 4).reshape(B, S, H, Dv)


def hgrn2_mixer(x, w_in, lower_bound, norm_g, w_out):
    B, S, _ = x.shape
    proj = (x @ w_in).astype(jnp.float32).reshape(B, S, HG_N_PROJ, HG_HEADS, HG_EXPAND)
    q = jax.nn.silu(proj[:, :, 0])
    i = proj[:, :, 1]
    g = proj[:, :, 4]
    lb = lower_bound.astype(jnp.float32).reshape(2, HG_HEADS, HG_EXPAND)

    def direction(d, reverse):
        f = lb[d] + (1.0 - lb[d]) * jax.nn.sigmoid(proj[:, :, 2 + d])
        args = (q, 1.0 - f, i, jnp.log(f))
        if reverse:
            return jnp.flip(gla_chunked(*[jnp.flip(a, axis=1) for a in args]), axis=1)
        return gla_chunked(*args)

    o = direction(0, False) + direction(1, True)
    o = o * lax.rsqrt(jnp.mean(jnp.square(o), axis=-1, keepdims=True) + LN_EPS)
    o = o * norm_g.astype(jnp.float32) * jax.nn.silu(g)
    return o.reshape(B, S, HG_DIM).astype(x.dtype) @ w_out


def memory_cross_attention(x, mem, wq, wkv, wo):
    B, S, _ = x.shape
    M = mem.shape[1]
    q = (x @ wq).reshape(B, S, XA_HEADS, XA_HEAD_DIM)
    kv = (mem @ wkv).reshape(B, M, 2, XA_HEADS, XA_HEAD_DIM)
    k, v = kv[:, :, 0], kv[:, :, 1]
    s = jnp.einsum('bshd,bmhd->bhsm', q, k).astype(jnp.float32) * (XA_HEAD_DIM ** -0.5)
    p = jax.nn.softmax(s, axis=-1).astype(x.dtype)
    o = jnp.einsum('bhsm,bmhd->bshd', p, v).reshape(B, S, D_MODEL)
    return o @ wo


def hier_moe(x, w_group, b_group, w_expert, b_expert, w_gate, w_up, w_down):
    B, S, D = x.shape
    T = B * S
    xt = x.reshape(T, D)
    g_logits = (xt @ w_group).astype(jnp.float32) + b_group.astype(jnp.float32)
    p_grp, g_idx = lax.top_k(jax.nn.softmax(g_logits, axis=-1), 1)
    e_logits = ((xt @ w_expert).astype(jnp.float32) + b_expert.astype(jnp.float32))
    e_logits = e_logits.reshape(T, MOE_GROUPS, MOE_PER_GROUP)
    e_sel = jnp.take_along_axis(e_logits, g_idx[:, :, None], axis=1)[:, 0]
    p_exp, e_local = lax.top_k(jax.nn.softmax(e_sel, axis=-1), MOE_TOPK)
    gates = p_grp * p_exp / jnp.sum(p_exp, axis=-1, keepdims=True)
    experts = g_idx * MOE_PER_GROUP + e_local
    e_flat = experts.reshape(-1)
    order = jnp.argsort(e_flat)
    tok_sorted = (jnp.arange(T * MOE_TOPK, dtype=jnp.int32) // MOE_TOPK)[order]
    gate_sorted = gates.reshape(-1)[order]
    sizes = jnp.bincount(e_flat, length=MOE_EXPERTS).astype(jnp.int32)
    xs = xt[tok_sorted]
    h = jax.nn.silu(lax.ragged_dot(xs, w_gate, sizes)) * lax.ragged_dot(xs, w_up, sizes)
    y = lax.ragged_dot(h, w_down, sizes).astype(jnp.float32) * gate_sorted[:, None]
    out = jax.ops.segment_sum(y, tok_sorted, num_segments=T)
    return out.astype(x.dtype).reshape(B, S, D)


def setup_inputs(seed: int = 0) -> dict:
    key = jax.random.key(seed)
    ks = jax.random.split(key, 26)
    D = D_MODEL
    half = PF_WIDTH // 2

    def nrm(k, shape, scale):
        return jax.random.normal(k, shape, jnp.float32) * scale

    xa_wk = nrm(ks[13], (DEPTH, D, D), D ** -0.5)
    xa_wv = nrm(ks[14], (DEPTH, D, D), DN_BETA * D ** -0.5)
    return {
        "x": nrm(ks[0], (BATCH, SEQ, D), 1.0),
        "mem": nrm(ks[1], (BATCH, MEM_LEN, D), 1.0),
        "pf_w_in": nrm(ks[2], (N_EVEN, D, PF_WIDTH), D ** -0.5),
        "pf_pool_w": nrm(ks[3], (N_EVEN, POOL_GROUPS, POOL_DIM, POOL_DIM), POOL_DIM ** -0.5),
        "pf_pool_scale": 1.0 + nrm(ks[4], (N_EVEN, half), 0.02),
        "pf_fourier_ln_g": 1.0 + nrm(ks[5], (N_EVEN, FOURIER_HEADS, FOURIER_DIM), 0.02),
        "pf_fourier_w": nrm(ks[6], (N_EVEN, FOURIER_HEADS, FOURIER_DIM, FOURIER_DIM), FOURIER_DIM ** -0.5),
        "pf_w_out": nrm(ks[7], (N_EVEN, PF_WIDTH, D), DN_BETA * PF_WIDTH ** -0.5),
        "hg_w_in": nrm(ks[8], (N_ODD, D, HG_N_PROJ * HG_DIM), D ** -0.5),
        "hg_lower_bounds": 1.0 + nrm(ks[9], (DEPTH, 2, HG_DIM), 0.1),
        "hg_norm_g": 1.0 + nrm(ks[10], (N_ODD, HG_EXPAND), 0.02),
        "hg_w_out": nrm(ks[11], (N_ODD, HG_DIM, D), DN_BETA * HG_DIM ** -0.5),
        "xa_wq": nrm(ks[12], (DEPTH, D, D), D ** -0.5),
        "xa_wkv": jnp.concatenate([xa_wk, xa_wv], axis=-1),
        "xa_wo": nrm(ks[15], (DEPTH, D, D), DN_BETA * D ** -0.5),
        "moe_w_group": nrm(ks[16], (DEPTH, D, MOE_GROUPS), D ** -0.5),
        "moe_b_group": nrm(ks[17], (DEPTH, MOE_GROUPS), 0.01),
        "moe_w_expert": nrm(ks[18], (DEPTH, D, MOE_EXPERTS), D ** -0.5),
        "moe_b_expert": nrm(ks[19], (DEPTH, MOE_EXPERTS), 0.01),
        "moe_w_gate": nrm(ks[20], (DEPTH, MOE_EXPERTS, D, MOE_HIDDEN), D ** -0.5),
        "moe_w_up": nrm(ks[21], (DEPTH, MOE_EXPERTS, D, MOE_HIDDEN), D ** -0.5),
        "moe_w_down": nrm(ks[22], (DEPTH, MOE_EXPERTS, MOE_HIDDEN, D), DN_BETA * MOE_HIDDEN ** -0.5),
        "ln_g": 1.0 + nrm(ks[23], (DEPTH, N_SUBLAYERS, D), 0.02),
        "ln_b": nrm(ks[24], (DEPTH, N_SUBLAYERS, D), 0.02),
    }


def reference(x, mem, pf_w_in, pf_pool_w, pf_pool_scale, pf_fourier_ln_g, pf_fourier_w, pf_w_out,
              hg_w_in, hg_lower_bounds, hg_norm_g, hg_w_out, xa_wq, xa_wkv, xa_wo,
              moe_w_group, moe_b_group, moe_w_expert, moe_b_expert, moe_w_gate, moe_w_up, moe_w_down,
              ln_g, ln_b):
    lb_all = jnp.cumsum(jax.nn.softmax(hg_lower_bounds.astype(jnp.float32), axis=0), axis=0)
    lb_all = lb_all - lb_all[:1]
    for l in range(DEPTH):
        j = l // 2
        if l % 2 == 0:
            h = pool_fourier_mixer(x, pf_w_in[j], pf_pool_w[j], pf_pool_scale[j],
                                   pf_fourier_ln_g[j], pf_fourier_w[j], pf_w_out[j])
        else:
            h = hgrn2_mixer(x, hg_w_in[j], lb_all[l], hg_norm_g[j], hg_w_out[j])
        x = layer_norm(DN_ALPHA * x + h, ln_g[l, 0], ln_b[l, 0])
        h = memory_cross_attention(x, mem, xa_wq[l], xa_wkv[l], xa_wo[l])
        x = layer_norm(DN_ALPHA * x + h, ln_g[l, 1], ln_b[l, 1])
        h = hier_moe(x, moe_w_group[l], moe_b_group[l], moe_w_expert[l], moe_b_expert[l],
                     moe_w_gate[l], moe_w_up[l], moe_w_down[l])
        x = layer_norm(DN_ALPHA * x + h, ln_g[l, 2], ln_b[l, 2])
    return x
```

```python
import functools

import jax
import jax.numpy as jnp
import numpy as np
from jax import lax
from jax.experimental import pallas as pl
from jax.experimental.pallas import tpu as pltpu

F32 = jnp.float32
BF16 = jnp.bfloat16
I32 = jnp.int32

D_MODEL = 1024
BATCH = 8
SEQ = 4096
DEPTH = 2
TOKENS = BATCH * SEQ
MEM_LEN = 256
HALF = D_MODEL // 2
POOL_WINDOWS = (2, 4, 8, 16)
POOL_DIM = 128
FOURIER_HEADS = 4
FOURIER_DIM = 128
HG_HEADS = 8
HG_DIM = 128
HG_N_PROJ = 5
HG_CHUNK = 64
XA_HEADS = 4
XA_HEAD_DIM = 256
MOE_GROUPS = 4
MOE_PER_GROUP = 8
MOE_EXPERTS = 32
MOE_TOPK = 2
MOE_HIDDEN = 512
DN_ALPHA = (2.0 * DEPTH) ** 0.25
LN_EPS = 1e-5

V7X_LANES = 128
V7X_SUBLANES = 8
V7X_VMEM_LIMIT_BYTES = 52 * 1024 * 1024

TM = 512
TM_ROUTE = 1024
TM_PERM = 1024
TM_COMB = 512
TE = 256
SORTED_ROWS = TOKENS * MOE_TOPK + MOE_EXPERTS * TE
N_EXPERT_TILES = SORTED_ROWS // TE
GLA_GROUP = 8
GLA_ROWS = GLA_GROUP * HG_CHUNK
FOURIER_TR = 512
ROUTER_ROWS = 40

_NT = (((1,), (1,)), ((), ()))


def _params(*sem):
    return pltpu.CompilerParams(dimension_semantics=sem, vmem_limit_bytes=V7X_VMEM_LIMIT_BYTES)


def _dot(a, b):
    return jnp.dot(a, b, preferred_element_type=F32)


def _post_ln(x, h, g, b):
    y = DN_ALPHA * x + h
    mu = jnp.mean(y, axis=-1, keepdims=True)
    yc = y - mu
    var = jnp.mean(yc * yc, axis=-1, keepdims=True)
    return yc * lax.rsqrt(var + LN_EPS) * g + b


def _silu(x):
    return x * jax.nn.sigmoid(x)


def _k_pf_in(x_ref, w_ref, lng_ref, wc_ref, ua_ref, pq_ref):
    u = _dot(x_ref[...].astype(BF16), w_ref[...])
    ua_ref[...] = u[:, :HALF]
    parts = []
    for h in range(FOURIER_HEADS):
        ub = u[:, HALF + FOURIER_DIM * h:HALF + FOURIER_DIM * (h + 1)]
        mu = jnp.mean(ub, axis=-1, keepdims=True)
        uc = ub - mu
        var = jnp.mean(uc * uc, axis=-1, keepdims=True)
        parts.append(uc * lax.rsqrt(var + LN_EPS))
    un = jnp.concatenate(parts, axis=-1) * lng_ref[...]
    pq_ref[...] = _dot(un.astype(BF16), wc_ref[...]).astype(BF16)


def _pf_in(x, w_in, ln_g, wc):
    return pl.pallas_call(
        _k_pf_in,
        out_shape=(jax.ShapeDtypeStruct((TOKENS, HALF), F32),
                   jax.ShapeDtypeStruct((TOKENS, D_MODEL), BF16)),
        grid=(TOKENS // TM,),
        in_specs=[pl.BlockSpec((TM, D_MODEL), lambda i: (i, 0)),
                  pl.BlockSpec((D_MODEL, D_MODEL), lambda i: (0, 0)),
                  pl.BlockSpec((1, HALF), lambda i: (0, 0)),
                  pl.BlockSpec((HALF, D_MODEL), lambda i: (0, 0))],
        out_specs=(pl.BlockSpec((TM, HALF), lambda i: (i, 0)),
                   pl.BlockSpec((TM, D_MODEL), lambda i: (i, 0))),
        compiler_params=_params("parallel"),
        name="pf_in",
    )(x, w_in, ln_g, wc)


_POOL_HALO = 8


def _k_pool(u_ref, up_ref, un_ref, pw_ref, ps_ref, ya_ref, ext_ref):
    i = pl.program_id(0)
    tiles_per_seq = SEQ // TM
    j = lax.rem(i, tiles_per_seq)
    ext_ref[_POOL_HALO:_POOL_HALO + TM, :] = u_ref[...]
    ext_ref[0:_POOL_HALO, :] = jnp.where(j == 0, 0.0, up_ref[...])
    ext_ref[_POOL_HALO + TM:2 * _POOL_HALO + TM, :] = jnp.where(j == tiles_per_seq - 1, 0.0, un_ref[...])
    pos = lax.broadcasted_iota(I32, (TM, POOL_DIM), 0) + j * TM
    outs = []
    for g, w in enumerate(POOL_WINDOWS):
        hw = w // 2
        cols = slice(g * POOL_DIM, (g + 1) * POOL_DIM)
        acc = ext_ref[_POOL_HALO - hw:_POOL_HALO - hw + TM, cols]
        for d in range(-hw + 1, hw):
            acc = acc + ext_ref[_POOL_HALO + d:_POOL_HALO + d + TM, cols]
        cnt = jnp.minimum(pos + hw, SEQ) - jnp.maximum(pos - hw, 0)
        pooled = acc / cnt.astype(F32) - u_ref[:, cols]
        outs.append(_dot(pooled.astype(BF16), pw_ref[g]))
    ya_ref[...] = (jnp.concatenate(outs, axis=-1) * ps_ref[...]).astype(BF16)


def _pool(ua, pool_w, pool_scale):
    blocks_per_tile = TM // _POOL_HALO
    last_block = TOKENS // _POOL_HALO - 1
    return pl.pallas_call(
        _k_pool,
        out_shape=jax.ShapeDtypeStruct((TOKENS, HALF), BF16),
        grid=(TOKENS // TM,),
        in_specs=[pl.BlockSpec((TM, HALF), lambda i: (i, 0)),
                  pl.BlockSpec((_POOL_HALO, HALF), lambda i: (jnp.maximum(i * blocks_per_tile - 1, 0), 0)),
                  pl.BlockSpec((_POOL_HALO, HALF),
                               lambda i: (jnp.minimum((i + 1) * blocks_per_tile, last_block), 0)),
                  pl.BlockSpec((4, POOL_DIM, POOL_DIM), lambda i: (0, 0, 0)),
                  pl.BlockSpec((1, HALF), lambda i: (0, 0))],
        out_specs=pl.BlockSpec((TM, HALF), lambda i: (i, 0)),
        scratch_shapes=[pltpu.VMEM((TM + 2 * _POOL_HALO, HALF), F32)],
        compiler_params=_params("parallel"),
        name="pf_pool",
    )(ua, ua, ua, pool_w, pool_scale)


_FOURIER_SCALE = 1.0 / float(np.sqrt(SEQ * FOURIER_DIM))


def _k_fourier(cs_ref, ss_ref, pq_ref, fw_ref, yb_ref):
    y = _dot(cs_ref[...], pq_ref[:, :HALF]) - _dot(ss_ref[...], pq_ref[:, HALF:])
    y = y * _FOURIER_SCALE
    outs = []
    for h in range(FOURIER_HEADS):
        outs.append(_dot(y[:, FOURIER_DIM * h:FOURIER_DIM * (h + 1)].astype(BF16), fw_ref[h]))
    yb_ref[...] = jnp.concatenate(outs, axis=-1).astype(BF16)


def _fourier(cs, ss, pq, fw):
    row_tiles = SEQ // FOURIER_TR
    return pl.pallas_call(
        _k_fourier,
        out_shape=jax.ShapeDtypeStruct((TOKENS, HALF), BF16),
        grid=(row_tiles, BATCH),
        in_specs=[pl.BlockSpec((FOURIER_TR, SEQ), lambda r, b: (r, 0)),
                  pl.BlockSpec((FOURIER_TR, SEQ), lambda r, b: (r, 0)),
                  pl.BlockSpec((SEQ, D_MODEL), lambda r, b: (b, 0)),
                  pl.BlockSpec((4, FOURIER_DIM, FOURIER_DIM), lambda r, b: (0, 0, 0))],
        out_specs=pl.BlockSpec((FOURIER_TR, HALF), lambda r, b: (b * row_tiles + r, 0)),
        compiler_params=_params("parallel", "parallel"),
        name="pf_fourier",
    )(cs, ss, pq, fw)


def _k_pf_out(ya_ref, yb_ref, w_ref, x_ref, g_ref, b_ref, o_ref):
    a = jnp.concatenate([ya_ref[...], yb_ref[...]], axis=-1)
    o_ref[...] = _post_ln(x_ref[...], _dot(a, w_ref[...]), g_ref[...], b_ref[...])


def _pf_out(ya, yb, w_out, x, g, b):
    row = lambda i: (i, 0)
    fixed = lambda i: (0, 0)
    return pl.pallas_call(
        _k_pf_out,
        out_shape=jax.ShapeDtypeStruct((TOKENS, D_MODEL), F32),
        grid=(TOKENS // TM,),
        in_specs=[pl.BlockSpec((TM, HALF), row), pl.BlockSpec((TM, HALF), row),
                  pl.BlockSpec((D_MODEL, D_MODEL), fixed), pl.BlockSpec((TM, D_MODEL), row),
                  pl.BlockSpec((1, D_MODEL), fixed), pl.BlockSpec((1, D_MODEL), fixed)],
        out_specs=pl.BlockSpec((TM, D_MODEL), row),
        compiler_params=_params("parallel"),
        name="pf_out",
    )(ya, yb, w_out, x, g, b)


def _k_hg_in(x_ref, w_ref, o_ref):
    acc = _dot(x_ref[...].astype(BF16), w_ref[...])
    acc = jnp.where(pl.program_id(1) == 0, _silu(acc), acc)
    for h in range(HG_HEADS):
        o_ref[0, h] = acc[:, HG_DIM * h:HG_DIM * (h + 1)].astype(BF16)


def _hg_in(x, w_in):
    return pl.pallas_call(
        _k_hg_in,
        out_shape=jax.ShapeDtypeStruct((HG_N_PROJ, HG_HEADS, TOKENS, HG_DIM), BF16),
        grid=(TOKENS // TM, HG_N_PROJ),
        in_specs=[pl.BlockSpec((TM, D_MODEL), lambda i, j: (i, 0)),
                  pl.BlockSpec((D_MODEL, D_MODEL), lambda i, j: (0, j))],
        out_specs=pl.BlockSpec((1, HG_HEADS, TM, HG_DIM), lambda i, j: (j, 0, i, 0)),
        compiler_params=_params("parallel", "arbitrary"),
        name="hg_in",
    )(x, w_in)


def _gla_group(q, v, z, lbv, tri, st, rev):
    L, G = HG_CHUNK, GLA_GROUP
    f = lbv + (1.0 - lbv) * jax.nn.sigmoid(z.astype(F32))
    lf = jnp.log(f).reshape(G, L, HG_DIM)
    k3 = (1.0 - f).reshape(G, L, HG_DIM)
    q3 = q.astype(F32).reshape(G, L, HG_DIM)
    v3 = v.reshape(G, L, HG_DIM)
    b = jnp.einsum('glm,gmk->glk', jnp.broadcast_to(tri, (G, L, L)), lf,
                   precision=lax.Precision.HIGHEST, preferred_element_type=F32)
    mid = L // 2 if not rev else L - 1 - L // 2
    end = L - 1 if not rev else 0
    b_ref = b[:, mid:mid + 1, :]
    b_last = b[:, end:end + 1, :]
    qt = (q3 * jnp.exp(b - b_ref)).astype(BF16)
    kt = (k3 * jnp.exp(b_ref - b)).astype(BF16)
    sc = jnp.einsum('glk,gmk->glm', qt, kt, preferred_element_type=F32)
    li = lax.broadcasted_iota(I32, (L, L), 0)
    mi = lax.broadcasted_iota(I32, (L, L), 1)
    causal = (li >= mi) if not rev else (li <= mi)
    sc = jnp.where(causal[None], sc, 0.0).astype(BF16)
    o_intra = jnp.einsum('glm,gmv->glv', sc, v3, preferred_element_type=F32)
    ku = (k3 * jnp.exp(b_last - b)).astype(BF16)
    u_t = jnp.einsum('glv,glk->gvk', v3, ku, preferred_element_type=F32)
    qe = (q3 * jnp.exp(b)).astype(BF16)
    dec = jnp.exp(b_last)
    outs = [None] * G
    for n in (range(G) if not rev else range(G - 1, -1, -1)):
        o_inter = lax.dot_general(qe[n], st.astype(BF16), _NT, preferred_element_type=F32)
        outs[n] = o_intra[n] + o_inter
        st = dec[n] * st + u_t[n]
    return jnp.concatenate(outs, axis=0), st


def _k_gla(q_ref, v_ref, zf_ref, zb_ref, g_ref, lb_ref, ng_ref, lt_ref, ut_ref, o_ref, of_ref):
    n_groups = SEQ // GLA_ROWS
    zero_state = jnp.zeros((HG_DIM, HG_DIM), F32)

    def fwd(gi, st):
        rows = pl.ds(pl.multiple_of(gi * GLA_ROWS, GLA_ROWS), GLA_ROWS)
        o, st = _gla_group(q_ref[0, 0, rows, :], v_ref[0, 0, rows, :], zf_ref[0, 0, rows, :],
                           lb_ref[0, 0], lt_ref[...], st, rev=False)
        of_ref[rows, :] = o
        return st

    lax.fori_loop(0, n_groups, fwd, zero_state)

    def bwd(t, st):
        gi = n_groups - 1 - t
        rows = pl.ds(pl.multiple_of(gi * GLA_ROWS, GLA_ROWS), GLA_ROWS)
        o, st = _gla_group(q_ref[0, 0, rows, :], v_ref[0, 0, rows, :], zb_ref[0, 0, rows, :],
                           lb_ref[1, 0], ut_ref[...], st, rev=True)
        o = o + of_ref[rows, :]
        o = o * lax.rsqrt(jnp.mean(o * o, axis=-1, keepdims=True) + LN_EPS)
        o = o * ng_ref[...] * _silu(g_ref[0, 0, rows, :].astype(F32))
        o_ref[0, rows, :] = o.astype(BF16)
        return st

    lax.fori_loop(0, n_groups, bwd, zero_state)


def _gla(proj, lb, norm_g, tril, triu):
    def pspec(p):
        return pl.BlockSpec((1, 1, SEQ, HG_DIM), lambda b, h: (p, h, b, 0))
    fixed = lambda b, h: (0, 0)
    return pl.pallas_call(
        _k_gla,
        out_shape=jax.ShapeDtypeStruct((HG_HEADS, TOKENS, HG_DIM), BF16),
        grid=(BATCH, HG_HEADS),
        in_specs=[pspec(0), pspec(1), pspec(2), pspec(3), pspec(4),
                  pl.BlockSpec((2, 1, 1, HG_DIM), lambda b, h: (0, h, 0, 0)),
                  pl.BlockSpec((1, HG_DIM), fixed),
                  pl.BlockSpec((HG_CHUNK, HG_CHUNK), fixed),
                  pl.BlockSpec((HG_CHUNK, HG_CHUNK), fixed)],
        out_specs=pl.BlockSpec((1, SEQ, HG_DIM), lambda b, h: (h, b, 0)),
        scratch_shapes=[pltpu.VMEM((SEQ, HG_DIM), F32)],
        compiler_params=_params("parallel", "parallel"),
        name="hg_gla",
    )(proj, proj, proj, proj, proj, lb, norm_g, tril, triu)


def _k_hg_out(a_ref, w_ref, x_ref, g_ref, b_ref, o_ref):
    a = jnp.concatenate([a_ref[h] for h in range(HG_HEADS)], axis=-1)
    o_ref[...] = _post_ln(x_ref[...], _dot(a, w_ref[...]), g_ref[...], b_ref[...])


def _hg_out(og, w_out, x, g, b):
    row = lambda i: (i, 0)
    fixed = lambda i: (0, 0)
    return pl.pallas_call(
        _k_hg_out,
        out_shape=jax.ShapeDtypeStruct((TOKENS, D_MODEL), F32),
        grid=(TOKENS // TM,),
        in_specs=[pl.BlockSpec((HG_HEADS, TM, HG_DIM), lambda i: (0, i, 0)),
                  pl.BlockSpec((D_MODEL, D_MODEL), fixed), pl.BlockSpec((TM, D_MODEL), row),
                  pl.BlockSpec((1, D_MODEL), fixed), pl.BlockSpec((1, D_MODEL), fixed)],
        out_specs=pl.BlockSpec((TM, D_MODEL), row),
        compiler_params=_params("parallel"),
        name="hg_out",
    )(og, w_out, x, g, b)


def _k_mm(x_ref, w_ref, o_ref):
    o_ref[...] = _dot(x_ref[...].astype(BF16), w_ref[...]).astype(o_ref.dtype)


def _kv_proj(mem2d, wkv):
    rows, cols = mem2d.shape[0], wkv.shape[1]
    return pl.pallas_call(
        _k_mm,
        out_shape=jax.ShapeDtypeStruct((rows, cols), BF16),
        grid=(rows // TM, cols // D_MODEL),
        in_specs=[pl.BlockSpec((TM, D_MODEL), lambda i, j: (i, 0)),
                  pl.BlockSpec((D_MODEL, D_MODEL), lambda i, j: (0, j))],
        out_specs=pl.BlockSpec((TM, D_MODEL), lambda i, j: (i, j)),
        compiler_params=_params("parallel", "parallel"),
        name="xa_kv",
    )(mem2d, wkv)


def _k_xattn(x_ref, wq_ref, kv_ref, wo_ref, g_ref, b_ref, o_ref):
    x = x_ref[...]
    q = _dot(x.astype(BF16), wq_ref[...]).astype(BF16)
    outs = []
    for h in range(XA_HEADS):
        cols = slice(XA_HEAD_DIM * h, XA_HEAD_DIM * (h + 1))
        kh = kv_ref[:, cols]
        vh = kv_ref[:, D_MODEL + XA_HEAD_DIM * h:D_MODEL + XA_HEAD_DIM * (h + 1)]
        s = lax.dot_general(q[:, cols], kh, _NT, preferred_element_type=F32) * (XA_HEAD_DIM ** -0.5)
        e = jnp.exp(s - jnp.max(s, axis=-1, keepdims=True))
        p = e / jnp.sum(e, axis=-1, keepdims=True)
        outs.append(_dot(p.astype(BF16), vh))
    o = jnp.concatenate(outs, axis=-1).astype(BF16)
    o_ref[...] = _post_ln(x, _dot(o, wo_ref[...]), g_ref[...], b_ref[...])


def _xattn(x, wq, kv, wo, g, b):
    row = lambda i: (i, 0)
    fixed = lambda i: (0, 0)
    tiles_per_seq = SEQ // TM
    return pl.pallas_call(
        _k_xattn,
        out_shape=jax.ShapeDtypeStruct((TOKENS, D_MODEL), F32),
        grid=(TOKENS // TM,),
        in_specs=[pl.BlockSpec((TM, D_MODEL), row),
                  pl.BlockSpec((D_MODEL, D_MODEL), fixed),
                  pl.BlockSpec((MEM_LEN, 2 * D_MODEL), lambda i: (i // tiles_per_seq, 0)),
                  pl.BlockSpec((D_MODEL, D_MODEL), fixed),
                  pl.BlockSpec((1, D_MODEL), fixed), pl.BlockSpec((1, D_MODEL), fixed)],
        out_specs=pl.BlockSpec((TM, D_MODEL), row),
        compiler_params=_params("parallel"),
        name="xattn",
    )(x, wq, kv, wo, g, b)


def _first_argmax_rows(v, n_rows):
    m = jnp.max(v, axis=0, keepdims=True)
    rows = lax.broadcasted_iota(I32, v.shape, 0)
    idx = jnp.min(jnp.where(v == m, rows, n_rows), axis=0, keepdims=True)
    return m, idx


def _k_router(x_ref, wr_ref, br_ref, e_ref, g_ref):
    lt = lax.dot_general(wr_ref[...], x_ref[...].astype(BF16), _NT, preferred_element_type=F32)
    lt = lt + br_ref[...]
    gl = lt[0:MOE_GROUPS, :]
    gmax, gi = _first_argmax_rows(gl, MOE_GROUPS)
    p_grp = 1.0 / jnp.sum(jnp.exp(gl - gmax), axis=0, keepdims=True)
    sel = lt[MOE_GROUPS:MOE_GROUPS + MOE_PER_GROUP, :]
    for g in range(1, MOE_GROUPS):
        lo = MOE_GROUPS + MOE_PER_GROUP * g
        sel = jnp.where(gi == g, lt[lo:lo + MOE_PER_GROUP, :], sel)
    m1, i1 = _first_argmax_rows(sel, MOE_PER_GROUP)
    rows = lax.broadcasted_iota(I32, sel.shape, 0)
    m2, i2 = _first_argmax_rows(jnp.where(rows == i1, -jnp.inf, sel), MOE_PER_GROUP)
    e2 = jnp.exp(m2 - m1)
    g1 = p_grp / (1.0 + e2)
    e_ref[0:1, :] = gi * MOE_PER_GROUP + i1
    e_ref[1:2, :] = gi * MOE_PER_GROUP + i2
    g_ref[0:1, :] = g1
    g_ref[1:2, :] = g1 * e2


def _router(x, wr, br):
    return pl.pallas_call(
        _k_router,
        out_shape=(jax.ShapeDtypeStruct((MOE_TOPK, TOKENS), I32),
                   jax.ShapeDtypeStruct((MOE_TOPK, TOKENS), F32)),
        grid=(TOKENS // TM_ROUTE,),
        in_specs=[pl.BlockSpec((TM_ROUTE, D_MODEL), lambda i: (i, 0)),
                  pl.BlockSpec((ROUTER_ROWS, D_MODEL), lambda i: (0, 0)),
                  pl.BlockSpec((ROUTER_ROWS, 1), lambda i: (0, 0))],
        out_specs=(pl.BlockSpec((MOE_TOPK, TM_ROUTE), lambda i: (0, i)),
                   pl.BlockSpec((MOE_TOPK, TM_ROUTE), lambda i: (0, i))),
        compiler_params=_params("parallel"),
        name="moe_router",
    )(x, wr, br)


def _k_rank(e_ref, u_ref, rank_ref, cnt_ref, carry_ref):
    @pl.when(pl.program_id(0) == 0)
    def _():
        carry_ref[...] = jnp.zeros_like(carry_ref)

    rows = lax.broadcasted_iota(I32, (MOE_EXPERTS, TM_ROUTE), 0)
    oh0 = rows == e_ref[0:1, :]
    oh1 = rows == e_ref[1:2, :]
    oh = jnp.where(oh0 | oh1, 1.0, 0.0)
    before = _dot(oh.astype(BF16), u_ref[...]) + carry_ref[:, 0:1]
    rank_ref[0:1, :] = jnp.sum(jnp.where(oh0, before, 0.0), axis=0, keepdims=True).astype(I32)
    rank_ref[1:2, :] = jnp.sum(jnp.where(oh1, before, 0.0), axis=0, keepdims=True).astype(I32)
    carry_ref[...] = carry_ref[...] + jnp.sum(oh, axis=1, keepdims=True)
    cnt_ref[...] = carry_ref[...]


def _rank(eidx, upper):
    return pl.pallas_call(
        _k_rank,
        out_shape=(jax.ShapeDtypeStruct((MOE_TOPK, TOKENS), I32),
                   jax.ShapeDtypeStruct((MOE_EXPERTS, V7X_LANES), F32)),
        grid=(TOKENS // TM_ROUTE,),
        in_specs=[pl.BlockSpec((MOE_TOPK, TM_ROUTE), lambda i: (0, i)),
                  pl.BlockSpec((TM_ROUTE, TM_ROUTE), lambda i: (0, 0))],
        out_specs=(pl.BlockSpec((MOE_TOPK, TM_ROUTE), lambda i: (0, i)),
                   pl.BlockSpec((MOE_EXPERTS, V7X_LANES), lambda i: (0, 0))),
        scratch_shapes=[pltpu.VMEM((MOE_EXPERTS, V7X_LANES), F32)],
        compiler_params=_params("arbitrary"),
        name="moe_rank",
    )(eidx, upper)


def _k_pos(e_ref, rank_ref, off_ref, pos_ref):
    rows = lax.broadcasted_iota(I32, (MOE_EXPERTS, TM_ROUTE), 0)
    for k in range(MOE_TOPK):
        start = jnp.sum(jnp.where(rows == e_ref[k:k + 1, :], off_ref[...], 0), axis=0, keepdims=True)
        pos_ref[k:k + 1, :] = start + rank_ref[k:k + 1, :]


def _positions(eidx, rank, offsets):
    tok = pl.BlockSpec((MOE_TOPK, TM_ROUTE), lambda i: (0, i))
    return pl.pallas_call(
        _k_pos,
        out_shape=jax.ShapeDtypeStruct((MOE_TOPK, TOKENS), I32),
        grid=(TOKENS // TM_ROUTE,),
        in_specs=[tok, tok, pl.BlockSpec((MOE_EXPERTS, 1), lambda i: (0, 0))],
        out_specs=tok,
        compiler_params=_params("parallel"),
        name="moe_pos",
    )(eidx, rank, offsets)


def _row_copy(src_ref, src_row, dst_ref, dst_row, sem):
    return pltpu.make_async_copy(src_ref.at[pl.ds(src_row, 1)], dst_ref.at[pl.ds(dst_row, 1)], sem)


def _k_dispatch(pos_ref, x_ref, xs_in_ref, xs_ref, sem):
    del xs_in_ref
    base = pl.program_id(0) * TM_PERM

    def body(j, c):
        for k in range(MOE_TOPK):
            _row_copy(x_ref, j, xs_ref, pos_ref[k * TOKENS + base + j], sem).start()
        return c

    lax.fori_loop(0, TM_PERM, body, 0, unroll=8)
    for k in range(MOE_TOPK):
        pltpu.make_async_copy(x_ref, xs_ref.at[pl.ds(0, TM_PERM)], sem).wait()


def _dispatch(pos_flat, x, xs_zero):
    return pl.pallas_call(
        _k_dispatch,
        out_shape=jax.ShapeDtypeStruct((SORTED_ROWS, D_MODEL), F32),
        grid_spec=pltpu.PrefetchScalarGridSpec(
            num_scalar_prefetch=1,
            grid=(TOKENS // TM_PERM,),
            in_specs=[pl.BlockSpec((TM_PERM, D_MODEL), lambda i, pos: (i, 0)),
                      pl.BlockSpec(memory_space=pl.ANY)],
            out_specs=pl.BlockSpec(memory_space=pl.ANY),
            scratch_shapes=[pltpu.SemaphoreType.DMA],
        ),
        input_output_aliases={2: 0},
        compiler_params=_params("arbitrary"),
        name="moe_dispatch",
    )(pos_flat, x, xs_zero)


def _k_experts(te_ref, na_ref, xs_ref, wg_ref, wu_ref, wd_ref, ys_ref):
    del te_ref
    active = pl.program_id(0) < na_ref[0]

    @pl.when(active)
    def _():
        xb = xs_ref[...].astype(BF16)
        h = _silu(_dot(xb, wg_ref[0])) * _dot(xb, wu_ref[0])
        ys_ref[...] = _dot(h.astype(BF16), wd_ref[0])

    @pl.when(jnp.logical_not(active))
    def _():
        ys_ref[...] = jnp.zeros_like(ys_ref)


def _experts(tile_expert, n_active, xs, wg, wu, wd):
    return pl.pallas_call(
        _k_experts,
        out_shape=jax.ShapeDtypeStruct((SORTED_ROWS, D_MODEL), F32),
        grid_spec=pltpu.PrefetchScalarGridSpec(
            num_scalar_prefetch=2,
            grid=(N_EXPERT_TILES,),
            in_specs=[pl.BlockSpec((TE, D_MODEL), lambda i, te, na: (i, 0)),
                      pl.BlockSpec((1, D_MODEL, MOE_HIDDEN), lambda i, te, na: (te[i], 0, 0)),
                      pl.BlockSpec((1, D_MODEL, MOE_HIDDEN), lambda i, te, na: (te[i], 0, 0)),
                      pl.BlockSpec((1, MOE_HIDDEN, D_MODEL), lambda i, te, na: (te[i], 0, 0))],
            out_specs=pl.BlockSpec((TE, D_MODEL), lambda i, te, na: (i, 0)),
        ),
        compiler_params=_params("arbitrary"),
        name="moe_experts",
    )(tile_expert, n_active, xs, wg, wu, wd)


def _k_combine(pos_ref, ys_ref, gt_ref, x_ref, g_ref, b_ref, o_ref, buf_ref, sem):
    base = pl.program_id(0) * TM_COMB

    def body(j, c):
        for k in range(MOE_TOPK):
            _row_copy(ys_ref, pos_ref[k * TOKENS + base + j], buf_ref.at[k], j, sem).start()
        return c

    lax.fori_loop(0, TM_COMB, body, 0, unroll=8)
    for k in range(MOE_TOPK):
        pltpu.make_async_copy(ys_ref.at[pl.ds(0, TM_COMB)], buf_ref.at[k], sem).wait()
    h = gt_ref[:, 0:1] * buf_ref[0] + gt_ref[:, 1:2] * buf_ref[1]
    o_ref[...] = _post_ln(x_ref[...], h, g_ref[...], b_ref[...])


def _combine(pos_flat, ys, gates_t, x, g, b):
    row = lambda i, pos: (i, 0)
    fixed = lambda i, pos: (0, 0)
    return pl.pallas_call(
        _k_combine,
        out_shape=jax.ShapeDtypeStruct((TOKENS, D_MODEL), F32),
        grid_spec=pltpu.PrefetchScalarGridSpec(
            num_scalar_prefetch=1,
            grid=(TOKENS // TM_COMB,),
            in_specs=[pl.BlockSpec(memory_space=pl.ANY),
                      pl.BlockSpec((TM_COMB, MOE_TOPK), row),
                      pl.BlockSpec((TM_COMB, D_MODEL), row),
                      pl.BlockSpec((1, D_MODEL), fixed), pl.BlockSpec((1, D_MODEL), fixed)],
            out_specs=pl.BlockSpec((TM_COMB, D_MODEL), row),
            scratch_shapes=[pltpu.VMEM((MOE_TOPK, TM_COMB, D_MODEL), F32), pltpu.SemaphoreType.DMA],
        ),
        compiler_params=_params("arbitrary"),
        name="moe_combine",
    )(pos_flat, ys, gates_t, x, g, b)


def _moe(x, w_group, b_group, w_expert, b_expert, w_gate, w_up, w_down, g, b, upper):
    wr = jnp.concatenate([w_group.T, w_expert.T,
                          jnp.zeros((ROUTER_ROWS - MOE_GROUPS - MOE_EXPERTS, D_MODEL), F32)], axis=0)
    br = jnp.concatenate([b_group, b_expert,
                          jnp.zeros((ROUTER_ROWS - MOE_GROUPS - MOE_EXPERTS,), F32)])[:, None]
    eidx, gates = _router(x, wr.astype(BF16), br.astype(F32))
    rank, cnt = _rank(eidx, upper)
    counts = cnt[:, 0].astype(I32)
    padded = ((counts + TE - 1) // TE) * TE
    ends = jnp.cumsum(padded)
    offsets = ends - padded
    tile_start = jnp.arange(N_EXPERT_TILES, dtype=I32) * TE
    tile_expert = jnp.minimum(jnp.searchsorted(ends, tile_start, side='right'), MOE_EXPERTS - 1).astype(I32)
    n_active = (ends[-1:] // TE).astype(I32)
    pos = _positions(eidx, rank, offsets[:, None])
    pos_flat = pos.reshape(-1)
    xs = _dispatch(pos_flat, x, jnp.zeros((SORTED_ROWS, D_MODEL), F32))
    ys = _experts(tile_expert, n_active, xs, w_gate.astype(BF16), w_up.astype(BF16), w_down.astype(BF16))
    return _combine(pos_flat, ys, gates.T, x, g, b)


def _channel_dft_matrix():
    c = np.arange(FOURIER_DIM)
    ang = 2.0 * np.pi * ((c[:, None] * c[None, :]) % FOURIER_DIM) / FOURIER_DIM
    wc = np.zeros((HALF, D_MODEL), np.float32)
    for h in range(FOURIER_HEADS):
        rows = slice(FOURIER_DIM * h, FOURIER_DIM * (h + 1))
        wc[rows, FOURIER_DIM * h:FOURIER_DIM * (h + 1)] = np.cos(ang)
        wc[rows, HALF + FOURIER_DIM * h:HALF + FOURIER_DIM * (h + 1)] = np.sin(ang)
    return jnp.asarray(wc, BF16)


def _sequence_dft_matrices():
    n = jnp.arange(SEQ, dtype=I32)
    ang = ((n[:, None] * n[None, :]) % SEQ).astype(F32) * (2.0 * np.pi / SEQ)
    return jnp.cos(ang).astype(BF16), jnp.sin(ang).astype(BF16)


def kernel(x, mem, pf_w_in, pf_pool_w, pf_pool_scale, pf_fourier_ln_g, pf_fourier_w, pf_w_out, hg_w_in, hg_lower_bounds, hg_norm_g, hg_w_out, xa_wq, xa_wkv, xa_wo, moe_w_group, moe_b_group, moe_w_expert, moe_b_expert, moe_w_gate, moe_w_up, moe_w_down, ln_g, ln_b):
    bf = lambda a: a.astype(BF16)
    xt = x.reshape(TOKENS, D_MODEL)
    mem2d = mem.reshape(BATCH * MEM_LEN, D_MODEL)
    lb_all = jnp.cumsum(jax.nn.softmax(hg_lower_bounds.astype(F32), axis=0), axis=0)
    lb_all = lb_all - lb_all[:1]
    tril = jnp.asarray(np.tril(np.ones((HG_CHUNK, HG_CHUNK), np.float32)))
    triu = jnp.asarray(np.triu(np.ones((HG_CHUNK, HG_CHUNK), np.float32)))
    upper = jnp.asarray(np.triu(np.ones((TM_ROUTE, TM_ROUTE), np.float32), k=1), BF16)

    for l in range(DEPTH):
        j = l // 2
        lng = lambda s: ln_g[l, s][None, :]
        lnb = lambda s: ln_b[l, s][None, :]
        if l % 2 == 0:
            cs, ss = _sequence_dft_matrices()
            ua, pq = _pf_in(xt, bf(pf_w_in[j]), pf_fourier_ln_g[j].reshape(1, HALF), _channel_dft_matrix())
            ya = _pool(ua, bf(pf_pool_w[j]), pf_pool_scale[j][None, :])
            yb = _fourier(cs, ss, pq, bf(pf_fourier_w[j]))
            xt = _pf_out(ya, yb, bf(pf_w_out[j]), xt, lng(0), lnb(0))
        else:
            proj = _hg_in(xt, bf(hg_w_in[j]))
            lb = lb_all[l].reshape(2, HG_HEADS, 1, HG_DIM)
            og = _gla(proj, lb, hg_norm_g[j][None, :], tril, triu)
            xt = _hg_out(og, bf(hg_w_out[j]), xt, lng(0), lnb(0))
        kv = _kv_proj(mem2d, bf(xa_wkv[l]))
        xt = _xattn(xt, bf(xa_wq[l]), kv, bf(xa_wo[l]), lng(1), lnb(1))
        xt = _moe(xt, moe_w_group[l], moe_b_group[l], moe_w_expert[l], moe_b_expert[l],
                  moe_w_gate[l], moe_w_up[l], moe_w_down[l], lng(2), lnb(2), upper)
    return xt.reshape(BATCH, SEQ, D_MODEL)
```

```python
import functools

import jax
import jax.numpy as jnp
import numpy as np
from jax import lax
from jax.experimental import pallas as pl
from jax.experimental.pallas import tpu as pltpu

F32 = jnp.float32
BF16 = jnp.bfloat16
I32 = jnp.int32

D_MODEL = 1024
BATCH = 8
SEQ = 4096
DEPTH = 2
TOKENS = BATCH * SEQ
MEM_LEN = 256
HALF = D_MODEL // 2
POOL_WINDOWS = (2, 4, 8, 16)
POOL_DIM = 128
FOURIER_HEADS = 4
FOURIER_DIM = 128
HG_HEADS = 8
HG_DIM = 128
HG_N_PROJ = 5
HG_CHUNK = 64
XA_HEADS = 4
XA_HEAD_DIM = 256
MOE_GROUPS = 4
MOE_PER_GROUP = 8
MOE_EXPERTS = 32
MOE_TOPK = 2
MOE_HIDDEN = 512
DN_ALPHA = (2.0 * DEPTH) ** 0.25
LN_EPS = 1e-5

V7X_LANES = 128
V7X_SUBLANES = 8
V7X_VMEM_LIMIT_BYTES = 52 * 1024 * 1024

TM = 512
TM_ROUTE = 1024
TM_PERM = 1024
TM_COMB = 512
TE = 256
SORTED_ROWS = TOKENS * MOE_TOPK + MOE_EXPERTS * TE
N_EXPERT_TILES = SORTED_ROWS // TE
GLA_GROUP = 8
GLA_ROWS = GLA_GROUP * HG_CHUNK
FOURIER_TR = 512
ROUTER_ROWS = 40

_NT = (((1,), (1,)), ((), ()))


def _params(*sem):
    return pltpu.CompilerParams(dimension_semantics=sem, vmem_limit_bytes=V7X_VMEM_LIMIT_BYTES)


def _dot(a, b):
    return jnp.dot(a, b, preferred_element_type=F32)


def _post_ln(x, h, g, b):
    y = DN_ALPHA * x + h
    mu = jnp.mean(y, axis=-1, keepdims=True)
    yc = y - mu
    var = jnp.mean(yc * yc, axis=-1, keepdims=True)
    return yc * lax.rsqrt(var + LN_EPS) * g + b


def _silu(x):
    return x * jax.nn.sigmoid(x)


def _resident(shape, index):
    return pl.BlockSpec(shape, lambda *_: index, pipeline_mode=pl.Buffered(1))


def _cast_weight_once(w_ref, wbf_ref):
    @pl.when(pl.program_id(0) == 0)
    def _():
        wbf_ref[...] = w_ref[0].astype(BF16)


def _k_pf_in(x_ref, w_ref, lng_ref, wc_ref, ua_ref, pq_ref, wbf_ref):
    _cast_weight_once(w_ref, wbf_ref)
    u = _dot(x_ref[...].astype(BF16), wbf_ref[...])
    ua_ref[...] = u[:, :HALF]
    parts = []
    for h in range(FOURIER_HEADS):
        ub = u[:, HALF + FOURIER_DIM * h:HALF + FOURIER_DIM * (h + 1)]
        mu = jnp.mean(ub, axis=-1, keepdims=True)
        uc = ub - mu
        var = jnp.mean(uc * uc, axis=-1, keepdims=True)
        parts.append(uc * lax.rsqrt(var + LN_EPS))
    un = jnp.concatenate(parts, axis=-1) * lng_ref[...]
    pq_ref[...] = _dot(un.astype(BF16), wc_ref[...]).astype(BF16)


def _pf_in(x, w_in_all, j, ln_g, wc):
    return pl.pallas_call(
        _k_pf_in,
        out_shape=(jax.ShapeDtypeStruct((TOKENS, HALF), F32),
                   jax.ShapeDtypeStruct((TOKENS, D_MODEL), BF16)),
        grid=(TOKENS // TM,),
        in_specs=[pl.BlockSpec((TM, D_MODEL), lambda i: (i, 0)),
                  _resident((1, D_MODEL, D_MODEL), (j, 0, 0)),
                  pl.BlockSpec((1, HALF), lambda i: (0, 0)),
                  _resident((HALF, D_MODEL), (0, 0))],
        out_specs=(pl.BlockSpec((TM, HALF), lambda i: (i, 0)),
                   pl.BlockSpec((TM, D_MODEL), lambda i: (i, 0))),
        scratch_shapes=[pltpu.VMEM((D_MODEL, D_MODEL), BF16)],
        compiler_params=_params("arbitrary"),
        name="pf_in",
    )(x, w_in_all, ln_g, wc)


_POOL_HALO = 8


def _k_pool(u_ref, up_ref, un_ref, pw_ref, ps_ref, ya_ref, ext_ref):
    i = pl.program_id(0)
    tiles_per_seq = SEQ // TM
    j = lax.rem(i, tiles_per_seq)
    ext_ref[_POOL_HALO:_POOL_HALO + TM, :] = u_ref[...]
    ext_ref[0:_POOL_HALO, :] = jnp.where(j == 0, 0.0, up_ref[...])
    ext_ref[_POOL_HALO + TM:2 * _POOL_HALO + TM, :] = jnp.where(j == tiles_per_seq - 1, 0.0, un_ref[...])
    pos = lax.broadcasted_iota(I32, (TM, POOL_DIM), 0) + j * TM
    outs = []
    for g, w in enumerate(POOL_WINDOWS):
        hw = w // 2
        cols = slice(g * POOL_DIM, (g + 1) * POOL_DIM)
        acc = ext_ref[_POOL_HALO - hw:_POOL_HALO - hw + TM, cols]
        for d in range(-hw + 1, hw):
            acc = acc + ext_ref[_POOL_HALO + d:_POOL_HALO + d + TM, cols]
        cnt = jnp.minimum(pos + hw, SEQ) - jnp.maximum(pos - hw, 0)
        pooled = acc / cnt.astype(F32) - u_ref[:, cols]
        outs.append(_dot(pooled.astype(BF16), pw_ref[g]))
    ya_ref[...] = (jnp.concatenate(outs, axis=-1) * ps_ref[...]).astype(BF16)


def _pool(ua, pool_w, pool_scale):
    blocks_per_tile = TM // _POOL_HALO
    last_block = TOKENS // _POOL_HALO - 1
    return pl.pallas_call(
        _k_pool,
        out_shape=jax.ShapeDtypeStruct((TOKENS, HALF), BF16),
        grid=(TOKENS // TM,),
        in_specs=[pl.BlockSpec((TM, HALF), lambda i: (i, 0)),
                  pl.BlockSpec((_POOL_HALO, HALF), lambda i: (jnp.maximum(i * blocks_per_tile - 1, 0), 0)),
                  pl.BlockSpec((_POOL_HALO, HALF),
                               lambda i: (jnp.minimum((i + 1) * blocks_per_tile, last_block), 0)),
                  pl.BlockSpec((4, POOL_DIM, POOL_DIM), lambda i: (0, 0, 0)),
                  pl.BlockSpec((1, HALF), lambda i: (0, 0))],
        out_specs=pl.BlockSpec((TM, HALF), lambda i: (i, 0)),
        scratch_shapes=[pltpu.VMEM((TM + 2 * _POOL_HALO, HALF), F32)],
        compiler_params=_params("parallel"),
        name="pf_pool",
    )(ua, ua, ua, pool_w, pool_scale)


_FOURIER_SCALE = 1.0 / float(np.sqrt(SEQ * FOURIER_DIM))


def _k_fourier(cs_ref, ss_ref, pq_ref, fw_ref, yb_ref):
    y = _dot(cs_ref[...], pq_ref[:, :HALF]) - _dot(ss_ref[...], pq_ref[:, HALF:])
    y = y * _FOURIER_SCALE
    outs = []
    for h in range(FOURIER_HEADS):
        outs.append(_dot(y[:, FOURIER_DIM * h:FOURIER_DIM * (h + 1)].astype(BF16), fw_ref[h]))
    yb_ref[...] = jnp.concatenate(outs, axis=-1).astype(BF16)


def _fourier(cs, ss, pq, fw):
    row_tiles = SEQ // FOURIER_TR
    return pl.pallas_call(
        _k_fourier,
        out_shape=jax.ShapeDtypeStruct((TOKENS, HALF), BF16),
        grid=(row_tiles, BATCH),
        in_specs=[pl.BlockSpec((FOURIER_TR, SEQ), lambda r, b: (r, 0)),
                  pl.BlockSpec((FOURIER_TR, SEQ), lambda r, b: (r, 0)),
                  pl.BlockSpec((SEQ, D_MODEL), lambda r, b: (b, 0)),
                  pl.BlockSpec((4, FOURIER_DIM, FOURIER_DIM), lambda r, b: (0, 0, 0))],
        out_specs=pl.BlockSpec((FOURIER_TR, HALF), lambda r, b: (b * row_tiles + r, 0)),
        compiler_params=_params("parallel", "parallel"),
        name="pf_fourier",
    )(cs, ss, pq, fw)


def _k_pf_out(ya_ref, yb_ref, w_ref, x_ref, g_ref, b_ref, o_ref, wbf_ref):
    _cast_weight_once(w_ref, wbf_ref)
    a = jnp.concatenate([ya_ref[...], yb_ref[...]], axis=-1)
    o_ref[...] = _post_ln(x_ref[...], _dot(a, wbf_ref[...]), g_ref[...], b_ref[...])


def _pf_out(ya, yb, w_out_all, j, x, g, b):
    row = lambda i: (i, 0)
    fixed = lambda i: (0, 0)
    return pl.pallas_call(
        _k_pf_out,
        out_shape=jax.ShapeDtypeStruct((TOKENS, D_MODEL), F32),
        grid=(TOKENS // TM,),
        in_specs=[pl.BlockSpec((TM, HALF), row), pl.BlockSpec((TM, HALF), row),
                  _resident((1, D_MODEL, D_MODEL), (j, 0, 0)), pl.BlockSpec((TM, D_MODEL), row),
                  pl.BlockSpec((1, D_MODEL), fixed), pl.BlockSpec((1, D_MODEL), fixed)],
        out_specs=pl.BlockSpec((TM, D_MODEL), row),
        scratch_shapes=[pltpu.VMEM((D_MODEL, D_MODEL), BF16)],
        compiler_params=_params("arbitrary"),
        name="pf_out",
    )(ya, yb, w_out_all, x, g, b)


_HG_STAGE_COLS = 512


def _k_hg_in(x_ref, w_hbm, o_ref, wbf_ref, stage_ref, sem):
    @pl.when(pl.program_id(0) == 0)
    def _():
        n_chunks = HG_N_PROJ * D_MODEL // _HG_STAGE_COLS

        def chunk_copy(c):
            cols = pl.ds(c * _HG_STAGE_COLS, _HG_STAGE_COLS)
            return pltpu.make_async_copy(w_hbm.at[:, cols], stage_ref.at[c % 2], sem.at[c % 2])

        chunk_copy(0).start()
        for c in range(n_chunks):
            if c + 1 < n_chunks:
                chunk_copy(c + 1).start()
            chunk_copy(c).wait()
            wbf_ref[:, c * _HG_STAGE_COLS:(c + 1) * _HG_STAGE_COLS] = stage_ref[c % 2].astype(BF16)

    xb = x_ref[...].astype(BF16)
    for p in range(HG_N_PROJ):
        acc = _dot(xb, wbf_ref[:, p * D_MODEL:(p + 1) * D_MODEL])
        if p == 0:
            acc = _silu(acc)
        for h in range(HG_HEADS):
            o_ref[p, h] = acc[:, HG_DIM * h:HG_DIM * (h + 1)].astype(BF16)


def _hg_in(x, w_in):
    return pl.pallas_call(
        _k_hg_in,
        out_shape=jax.ShapeDtypeStruct((HG_N_PROJ, HG_HEADS, TOKENS, HG_DIM), BF16),
        grid=(TOKENS // TM,),
        in_specs=[pl.BlockSpec((TM, D_MODEL), lambda i: (i, 0)),
                  pl.BlockSpec(memory_space=pl.ANY)],
        out_specs=pl.BlockSpec((HG_N_PROJ, HG_HEADS, TM, HG_DIM), lambda i: (0, 0, i, 0)),
        scratch_shapes=[pltpu.VMEM((D_MODEL, HG_N_PROJ * D_MODEL), BF16),
                        pltpu.VMEM((2, D_MODEL, _HG_STAGE_COLS), F32),
                        pltpu.SemaphoreType.DMA((2,))],
        compiler_params=_params("arbitrary"),
        name="hg_in",
    )(x, w_in)


def _gla_group(q, v, z, lbv, tri, st, rev):
    L, G = HG_CHUNK, GLA_GROUP
    f = lbv + (1.0 - lbv) * jax.nn.sigmoid(z.astype(F32))
    lf = jnp.log(f).reshape(G, L, HG_DIM)
    k3 = (1.0 - f).reshape(G, L, HG_DIM)
    q3 = q.astype(F32).reshape(G, L, HG_DIM)
    v3 = v.reshape(G, L, HG_DIM)
    b = jnp.einsum('glm,gmk->glk', jnp.broadcast_to(tri, (G, L, L)), lf,
                   precision=lax.Precision.HIGHEST, preferred_element_type=F32)
    mid = L // 2 if not rev else L - 1 - L // 2
    end = L - 1 if not rev else 0
    b_ref = b[:, mid:mid + 1, :]
    b_last = b[:, end:end + 1, :]
    qt = (q3 * jnp.exp(b - b_ref)).astype(BF16)
    kt = (k3 * jnp.exp(b_ref - b)).astype(BF16)
    sc = jnp.einsum('glk,gmk->glm', qt, kt, preferred_element_type=F32)
    li = lax.broadcasted_iota(I32, (L, L), 0)
    mi = lax.broadcasted_iota(I32, (L, L), 1)
    causal = (li >= mi) if not rev else (li <= mi)
    sc = jnp.where(causal[None], sc, 0.0).astype(BF16)
    o_intra = jnp.einsum('glm,gmv->glv', sc, v3, preferred_element_type=F32)
    ku = (k3 * jnp.exp(b_last - b)).astype(BF16)
    u_t = jnp.einsum('glv,glk->gvk', v3, ku, preferred_element_type=F32)
    qe = (q3 * jnp.exp(b)).astype(BF16)
    dec = jnp.exp(b_last)
    outs = [None] * G
    for n in (range(G) if not rev else range(G - 1, -1, -1)):
        o_inter = lax.dot_general(qe[n], st.astype(BF16), _NT, preferred_element_type=F32)
        outs[n] = o_intra[n] + o_inter
        st = dec[n] * st + u_t[n]
    return jnp.concatenate(outs, axis=0), st


def _k_gla(q_ref, v_ref, zf_ref, zb_ref, g_ref, lb_ref, ng_ref, lt_ref, ut_ref, o_ref, of_ref):
    n_groups = SEQ // GLA_ROWS
    zero_state = jnp.zeros((HG_DIM, HG_DIM), F32)

    def fwd(gi, st):
        rows = pl.ds(pl.multiple_of(gi * GLA_ROWS, GLA_ROWS), GLA_ROWS)
        o, st = _gla_group(q_ref[0, 0, rows, :], v_ref[0, 0, rows, :], zf_ref[0, 0, rows, :],
                           lb_ref[0, 0], lt_ref[...], st, rev=False)
        of_ref[rows, :] = o
        return st

    lax.fori_loop(0, n_groups, fwd, zero_state)

    def bwd(t, st):
        gi = n_groups - 1 - t
        rows = pl.ds(pl.multiple_of(gi * GLA_ROWS, GLA_ROWS), GLA_ROWS)
        o, st = _gla_group(q_ref[0, 0, rows, :], v_ref[0, 0, rows, :], zb_ref[0, 0, rows, :],
                           lb_ref[1, 0], ut_ref[...], st, rev=True)
        o = o + of_ref[rows, :]
        o = o * lax.rsqrt(jnp.mean(o * o, axis=-1, keepdims=True) + LN_EPS)
        o = o * ng_ref[...] * _silu(g_ref[0, 0, rows, :].astype(F32))
        o_ref[0, rows, :] = o.astype(BF16)
        return st

    lax.fori_loop(0, n_groups, bwd, zero_state)


def _gla(proj, lb, norm_g, tril, triu):
    def pspec(p):
        return pl.BlockSpec((1, 1, SEQ, HG_DIM), lambda b, h: (p, h, b, 0))
    fixed = lambda b, h: (0, 0)
    return pl.pallas_call(
        _k_gla,
        out_shape=jax.ShapeDtypeStruct((HG_HEADS, TOKENS, HG_DIM), BF16),
        grid=(BATCH, HG_HEADS),
        in_specs=[pspec(0), pspec(1), pspec(2), pspec(3), pspec(4),
                  pl.BlockSpec((2, 1, 1, HG_DIM), lambda b, h: (0, h, 0, 0)),
                  pl.BlockSpec((1, HG_DIM), fixed),
                  pl.BlockSpec((HG_CHUNK, HG_CHUNK), fixed),
                  pl.BlockSpec((HG_CHUNK, HG_CHUNK), fixed)],
        out_specs=pl.BlockSpec((1, SEQ, HG_DIM), lambda b, h: (h, b, 0)),
        scratch_shapes=[pltpu.VMEM((SEQ, HG_DIM), F32)],
        compiler_params=_params("parallel", "parallel"),
        name="hg_gla",
    )(proj, proj, proj, proj, proj, lb, norm_g, tril, triu)


def _k_hg_out(a_ref, w_ref, x_ref, g_ref, b_ref, o_ref, wbf_ref):
    _cast_weight_once(w_ref, wbf_ref)
    a = jnp.concatenate([a_ref[h] for h in range(HG_HEADS)], axis=-1)
    o_ref[...] = _post_ln(x_ref[...], _dot(a, wbf_ref[...]), g_ref[...], b_ref[...])


def _hg_out(og, w_out_all, j, x, g, b):
    row = lambda i: (i, 0)
    fixed = lambda i: (0, 0)
    return pl.pallas_call(
        _k_hg_out,
        out_shape=jax.ShapeDtypeStruct((TOKENS, D_MODEL), F32),
        grid=(TOKENS // TM,),
        in_specs=[pl.BlockSpec((HG_HEADS, TM, HG_DIM), lambda i: (0, i, 0)),
                  _resident((1, D_MODEL, D_MODEL), (j, 0, 0)), pl.BlockSpec((TM, D_MODEL), row),
                  pl.BlockSpec((1, D_MODEL), fixed), pl.BlockSpec((1, D_MODEL), fixed)],
        out_specs=pl.BlockSpec((TM, D_MODEL), row),
        scratch_shapes=[pltpu.VMEM((D_MODEL, D_MODEL), BF16)],
        compiler_params=_params("arbitrary"),
        name="hg_out",
    )(og, w_out_all, x, g, b)


def _k_mm(x_ref, w_ref, o_ref):
    o_ref[...] = _dot(x_ref[...].astype(BF16), w_ref[0].astype(BF16)).astype(o_ref.dtype)


def _kv_proj(mem2d, wkv_all, l):
    rows, cols = mem2d.shape[0], wkv_all.shape[2]
    return pl.pallas_call(
        _k_mm,
        out_shape=jax.ShapeDtypeStruct((rows, cols), BF16),
        grid=(cols // D_MODEL, rows // TM),
        in_specs=[pl.BlockSpec((TM, D_MODEL), lambda j, i: (i, 0)),
                  pl.BlockSpec((1, D_MODEL, D_MODEL), lambda j, i: (l, 0, j))],
        out_specs=pl.BlockSpec((TM, D_MODEL), lambda j, i: (i, j)),
        compiler_params=_params("parallel", "parallel"),
        name="xa_kv",
    )(mem2d, wkv_all)


def _k_xattn(x_ref, wq_ref, kv_ref, wo_ref, g_ref, b_ref, o_ref, wqbf_ref, wobf_ref):
    _cast_weight_once(wq_ref, wqbf_ref)
    _cast_weight_once(wo_ref, wobf_ref)
    x = x_ref[...]
    q = _dot(x.astype(BF16), wqbf_ref[...]).astype(BF16)
    outs = []
    for h in range(XA_HEADS):
        cols = slice(XA_HEAD_DIM * h, XA_HEAD_DIM * (h + 1))
        kh = kv_ref[:, cols]
        vh = kv_ref[:, D_MODEL + XA_HEAD_DIM * h:D_MODEL + XA_HEAD_DIM * (h + 1)]
        s = lax.dot_general(q[:, cols], kh, _NT, preferred_element_type=F32) * (XA_HEAD_DIM ** -0.5)
        e = jnp.exp(s - jnp.max(s, axis=-1, keepdims=True))
        p = e / jnp.sum(e, axis=-1, keepdims=True)
        outs.append(_dot(p.astype(BF16), vh))
    o = jnp.concatenate(outs, axis=-1).astype(BF16)
    o_ref[...] = _post_ln(x, _dot(o, wobf_ref[...]), g_ref[...], b_ref[...])


def _xattn(x, wq_all, kv, wo_all, l, g, b):
    row = lambda i: (i, 0)
    fixed = lambda i: (0, 0)
    tiles_per_seq = SEQ // TM
    return pl.pallas_call(
        _k_xattn,
        out_shape=jax.ShapeDtypeStruct((TOKENS, D_MODEL), F32),
        grid=(TOKENS // TM,),
        in_specs=[pl.BlockSpec((TM, D_MODEL), row),
                  _resident((1, D_MODEL, D_MODEL), (l, 0, 0)),
                  pl.BlockSpec((MEM_LEN, 2 * D_MODEL), lambda i: (i // tiles_per_seq, 0)),
                  _resident((1, D_MODEL, D_MODEL), (l, 0, 0)),
                  pl.BlockSpec((1, D_MODEL), fixed), pl.BlockSpec((1, D_MODEL), fixed)],
        out_specs=pl.BlockSpec((TM, D_MODEL), row),
        scratch_shapes=[pltpu.VMEM((D_MODEL, D_MODEL), BF16), pltpu.VMEM((D_MODEL, D_MODEL), BF16)],
        compiler_params=_params("arbitrary"),
        name="xattn",
    )(x, wq_all, kv, wo_all, g, b)


def _first_argmax_rows(v, n_rows):
    m = jnp.max(v, axis=0, keepdims=True)
    rows = lax.broadcasted_iota(I32, v.shape, 0)
    idx = jnp.min(jnp.where(v == m, rows, n_rows), axis=0, keepdims=True)
    return m, idx


def _k_router(x_ref, wr_ref, br_ref, e_ref, g_ref):
    lt = lax.dot_general(wr_ref[...], x_ref[...].astype(BF16), _NT, preferred_element_type=F32)
    lt = lt + br_ref[...]
    gl = lt[0:MOE_GROUPS, :]
    gmax, gi = _first_argmax_rows(gl, MOE_GROUPS)
    p_grp = 1.0 / jnp.sum(jnp.exp(gl - gmax), axis=0, keepdims=True)
    sel = lt[MOE_GROUPS:MOE_GROUPS + MOE_PER_GROUP, :]
    for g in range(1, MOE_GROUPS):
        lo = MOE_GROUPS + MOE_PER_GROUP * g
        sel = jnp.where(gi == g, lt[lo:lo + MOE_PER_GROUP, :], sel)
    m1, i1 = _first_argmax_rows(sel, MOE_PER_GROUP)
    rows = lax.broadcasted_iota(I32, sel.shape, 0)
    m2, i2 = _first_argmax_rows(jnp.where(rows == i1, -jnp.inf, sel), MOE_PER_GROUP)
    e2 = jnp.exp(m2 - m1)
    g1 = p_grp / (1.0 + e2)
    e_ref[0:1, :] = gi * MOE_PER_GROUP + i1
    e_ref[1:2, :] = gi * MOE_PER_GROUP + i2
    g_ref[0:1, :] = g1
    g_ref[1:2, :] = g1 * e2


def _router(x, wr, br):
    return pl.pallas_call(
        _k_router,
        out_shape=(jax.ShapeDtypeStruct((MOE_TOPK, TOKENS), I32),
                   jax.ShapeDtypeStruct((MOE_TOPK, TOKENS), F32)),
        grid=(TOKENS // TM_ROUTE,),
        in_specs=[pl.BlockSpec((TM_ROUTE, D_MODEL), lambda i: (i, 0)),
                  pl.BlockSpec((ROUTER_ROWS, D_MODEL), lambda i: (0, 0)),
                  pl.BlockSpec((ROUTER_ROWS, 1), lambda i: (0, 0))],
        out_specs=(pl.BlockSpec((MOE_TOPK, TM_ROUTE), lambda i: (0, i)),
                   pl.BlockSpec((MOE_TOPK, TM_ROUTE), lambda i: (0, i))),
        compiler_params=_params("parallel"),
        name="moe_router",
    )(x, wr, br)


def _k_rank(e_ref, u_ref, rank_ref, cnt_ref, carry_ref):
    @pl.when(pl.program_id(0) == 0)
    def _():
        carry_ref[...] = jnp.zeros_like(carry_ref)

    rows = lax.broadcasted_iota(I32, (MOE_EXPERTS, TM_ROUTE), 0)
    oh0 = rows == e_ref[0:1, :]
    oh1 = rows == e_ref[1:2, :]
    oh = jnp.where(oh0 | oh1, 1.0, 0.0)
    before = _dot(oh.astype(BF16), u_ref[...]) + carry_ref[:, 0:1]
    rank_ref[0:1, :] = jnp.sum(jnp.where(oh0, before, 0.0), axis=0, keepdims=True).astype(I32)
    rank_ref[1:2, :] = jnp.sum(jnp.where(oh1, before, 0.0), axis=0, keepdims=True).astype(I32)
    carry_ref[...] = carry_ref[...] + jnp.sum(oh, axis=1, keepdims=True)
    cnt_ref[...] = carry_ref[...]


def _rank(eidx, upper):
    return pl.pallas_call(
        _k_rank,
        out_shape=(jax.ShapeDtypeStruct((MOE_TOPK, TOKENS), I32),
                   jax.ShapeDtypeStruct((MOE_EXPERTS, V7X_LANES), F32)),
        grid=(TOKENS // TM_ROUTE,),
        in_specs=[pl.BlockSpec((MOE_TOPK, TM_ROUTE), lambda i: (0, i)),
                  pl.BlockSpec((TM_ROUTE, TM_ROUTE), lambda i: (0, 0))],
        out_specs=(pl.BlockSpec((MOE_TOPK, TM_ROUTE), lambda i: (0, i)),
                   pl.BlockSpec((MOE_EXPERTS, V7X_LANES), lambda i: (0, 0))),
        scratch_shapes=[pltpu.VMEM((MOE_EXPERTS, V7X_LANES), F32)],
        compiler_params=_params("arbitrary"),
        name="moe_rank",
    )(eidx, upper)


def _k_pos(e_ref, rank_ref, off_ref, pos_ref):
    rows = lax.broadcasted_iota(I32, (MOE_EXPERTS, TM_ROUTE), 0)
    for k in range(MOE_TOPK):
        start = jnp.sum(jnp.where(rows == e_ref[k:k + 1, :], off_ref[...], 0), axis=0, keepdims=True)
        pos_ref[k:k + 1, :] = start + rank_ref[k:k + 1, :]


def _positions(eidx, rank, offsets):
    tok = pl.BlockSpec((MOE_TOPK, TM_ROUTE), lambda i: (0, i))
    return pl.pallas_call(
        _k_pos,
        out_shape=jax.ShapeDtypeStruct((MOE_TOPK, TOKENS), I32),
        grid=(TOKENS // TM_ROUTE,),
        in_specs=[tok, tok, pl.BlockSpec((MOE_EXPERTS, 1), lambda i: (0, 0))],
        out_specs=tok,
        compiler_params=_params("parallel"),
        name="moe_pos",
    )(eidx, rank, offsets)


def _row_copy(src_ref, src_row, dst_ref, dst_row, sem):
    return pltpu.make_async_copy(src_ref.at[pl.ds(src_row, 1)], dst_ref.at[pl.ds(dst_row, 1)], sem)


def _k_dispatch(pos_ref, ends_ref, x_ref, xs_ref, zero_ref, sem, zsem):
    base = pl.program_id(0) * TM_PERM

    @pl.when(pl.program_id(0) == 0)
    def _():
        zero_ref[...] = jnp.zeros_like(zero_ref)

        def last_tile(e):
            seg_start = ends_ref[e - 1] if e > 0 else 0
            start = pl.multiple_of(ends_ref[e] - TE, TE)
            copy = pltpu.make_async_copy(zero_ref, xs_ref.at[pl.ds(start, TE)], zsem)
            return ends_ref[e] > seg_start, copy

        def unused_tile(t):
            start = pl.multiple_of(ends_ref[MOE_EXPERTS - 1] + t * TE, TE)
            copy = pltpu.make_async_copy(zero_ref, xs_ref.at[pl.ds(start, TE)], zsem)
            return start < SORTED_ROWS, copy

        fills = [last_tile(e) for e in range(MOE_EXPERTS)] + [unused_tile(t) for t in range(MOE_EXPERTS)]
        for needed, copy in fills:
            pl.when(needed)(copy.start)
        for needed, copy in fills:
            pl.when(needed)(copy.wait)

    def body(j, c):
        for k in range(MOE_TOPK):
            _row_copy(x_ref, j, xs_ref, pos_ref[k * TOKENS + base + j], sem).start()
        return c

    lax.fori_loop(0, TM_PERM, body, 0, unroll=8)
    for k in range(MOE_TOPK):
        pltpu.make_async_copy(x_ref, xs_ref.at[pl.ds(0, TM_PERM)], sem).wait()


def _dispatch(pos_flat, ends, x):
    return pl.pallas_call(
        _k_dispatch,
        out_shape=jax.ShapeDtypeStruct((SORTED_ROWS, D_MODEL), F32),
        grid_spec=pltpu.PrefetchScalarGridSpec(
            num_scalar_prefetch=2,
            grid=(TOKENS // TM_PERM,),
            in_specs=[pl.BlockSpec((TM_PERM, D_MODEL), lambda i, pos, ends: (i, 0))],
            out_specs=pl.BlockSpec(memory_space=pl.ANY),
            scratch_shapes=[pltpu.VMEM((TE, D_MODEL), F32), pltpu.SemaphoreType.DMA,
                            pltpu.SemaphoreType.DMA],
        ),
        compiler_params=_params("arbitrary"),
        name="moe_dispatch",
    )(pos_flat, ends, x)


def _k_experts(te_ref, na_ref, xs_ref, wg_ref, wu_ref, wd_ref, ys_ref, wgbf_ref, wubf_ref, wdbf_ref):
    i = pl.program_id(0)
    active = i < na_ref[0]
    new_expert = jnp.logical_or(i == 0, te_ref[jnp.maximum(i - 1, 0)] != te_ref[i])

    @pl.when(jnp.logical_and(active, new_expert))
    def _():
        wgbf_ref[...] = wg_ref[0, 0].astype(BF16)
        wubf_ref[...] = wu_ref[0, 0].astype(BF16)
        wdbf_ref[...] = wd_ref[0, 0].astype(BF16)

    @pl.when(active)
    def _():
        xb = xs_ref[...].astype(BF16)
        h = _silu(_dot(xb, wgbf_ref[...])) * _dot(xb, wubf_ref[...])
        ys_ref[...] = _dot(h.astype(BF16), wdbf_ref[...])

    @pl.when(jnp.logical_not(active))
    def _():
        ys_ref[...] = jnp.zeros_like(ys_ref)


def _experts(tile_expert, n_active, xs, wg_all, wu_all, wd_all, l):
    def xs_index(i, te, na):
        return (jnp.minimum(i, na[0] - 1), 0)

    def w_index(i, te, na):
        return (l, te[i], 0, 0)

    return pl.pallas_call(
        _k_experts,
        out_shape=jax.ShapeDtypeStruct((SORTED_ROWS, D_MODEL), F32),
        grid_spec=pltpu.PrefetchScalarGridSpec(
            num_scalar_prefetch=2,
            grid=(N_EXPERT_TILES,),
            in_specs=[pl.BlockSpec((TE, D_MODEL), xs_index),
                      pl.BlockSpec((1, 1, D_MODEL, MOE_HIDDEN), w_index),
                      pl.BlockSpec((1, 1, D_MODEL, MOE_HIDDEN), w_index),
                      pl.BlockSpec((1, 1, MOE_HIDDEN, D_MODEL), w_index)],
            out_specs=pl.BlockSpec((TE, D_MODEL), lambda i, te, na: (i, 0)),
            scratch_shapes=[pltpu.VMEM((D_MODEL, MOE_HIDDEN), BF16),
                            pltpu.VMEM((D_MODEL, MOE_HIDDEN), BF16),
                            pltpu.VMEM((MOE_HIDDEN, D_MODEL), BF16)],
        ),
        compiler_params=_params("arbitrary"),
        name="moe_experts",
    )(tile_expert, n_active, xs, wg_all, wu_all, wd_all)


def _k_combine(pos_ref, ys_ref, gt_ref, x_ref, g_ref, b_ref, o_ref, buf_ref, sem):
    base = pl.program_id(0) * TM_COMB

    def body(j, c):
        for k in range(MOE_TOPK):
            _row_copy(ys_ref, pos_ref[k * TOKENS + base + j], buf_ref.at[k], j, sem).start()
        return c

    lax.fori_loop(0, TM_COMB, body, 0, unroll=8)
    for k in range(MOE_TOPK):
        pltpu.make_async_copy(ys_ref.at[pl.ds(0, TM_COMB)], buf_ref.at[k], sem).wait()
    h = gt_ref[:, 0:1] * buf_ref[0] + gt_ref[:, 1:2] * buf_ref[1]
    o_ref[...] = _post_ln(x_ref[...], h, g_ref[...], b_ref[...])


def _combine(pos_flat, ys, gates_t, x, g, b):
    row = lambda i, pos: (i, 0)
    fixed = lambda i, pos: (0, 0)
    return pl.pallas_call(
        _k_combine,
        out_shape=jax.ShapeDtypeStruct((TOKENS, D_MODEL), F32),
        grid_spec=pltpu.PrefetchScalarGridSpec(
            num_scalar_prefetch=1,
            grid=(TOKENS // TM_COMB,),
            in_specs=[pl.BlockSpec(memory_space=pl.ANY),
                      pl.BlockSpec((TM_COMB, MOE_TOPK), row),
                      pl.BlockSpec((TM_COMB, D_MODEL), row),
                      pl.BlockSpec((1, D_MODEL), fixed), pl.BlockSpec((1, D_MODEL), fixed)],
            out_specs=pl.BlockSpec((TM_COMB, D_MODEL), row),
            scratch_shapes=[pltpu.VMEM((MOE_TOPK, TM_COMB, D_MODEL), F32), pltpu.SemaphoreType.DMA],
        ),
        compiler_params=_params("arbitrary"),
        name="moe_combine",
    )(pos_flat, ys, gates_t, x, g, b)


def _moe(x, w_group, b_group, w_expert, b_expert, w_gate_all, w_up_all, w_down_all, l, g, b, upper):
    wr = jnp.concatenate([w_group.T, w_expert.T,
                          jnp.zeros((ROUTER_ROWS - MOE_GROUPS - MOE_EXPERTS, D_MODEL), F32)], axis=0)
    br = jnp.concatenate([b_group, b_expert,
                          jnp.zeros((ROUTER_ROWS - MOE_GROUPS - MOE_EXPERTS,), F32)])[:, None]
    eidx, gates = _router(x, wr.astype(BF16), br.astype(F32))
    rank, cnt = _rank(eidx, upper)
    counts = cnt[:, 0].astype(I32)
    padded = ((counts + TE - 1) // TE) * TE
    ends = jnp.cumsum(padded)
    offsets = ends - padded
    tile_start = jnp.arange(N_EXPERT_TILES, dtype=I32) * TE
    tile_expert = jnp.sum((tile_start[:, None] >= ends[None, :]).astype(I32), axis=1)
    tile_expert = jnp.minimum(tile_expert, MOE_EXPERTS - 1)
    n_active = (ends[-1:] // TE).astype(I32)
    pos = _positions(eidx, rank, offsets[:, None])
    pos_flat = pos.reshape(-1)
    xs = _dispatch(pos_flat, ends.astype(I32), x)
    ys = _experts(tile_expert, n_active, xs, w_gate_all, w_up_all, w_down_all, l)
    return _combine(pos_flat, ys, gates.T, x, g, b)


def _channel_dft_matrix():
    c = np.arange(FOURIER_DIM)
    ang = 2.0 * np.pi * ((c[:, None] * c[None, :]) % FOURIER_DIM) / FOURIER_DIM
    wc = np.zeros((HALF, D_MODEL), np.float32)
    for h in range(FOURIER_HEADS):
        rows = slice(FOURIER_DIM * h, FOURIER_DIM * (h + 1))
        wc[rows, FOURIER_DIM * h:FOURIER_DIM * (h + 1)] = np.cos(ang)
        wc[rows, HALF + FOURIER_DIM * h:HALF + FOURIER_DIM * (h + 1)] = np.sin(ang)
    return jnp.asarray(wc, BF16)


def _sequence_dft_matrices():
    side = 64
    j = jnp.arange(SEQ, dtype=I32)[:, None]
    k = jnp.arange(side, dtype=I32)[None, :]
    ang_a = ((j * k) % side).astype(F32) * (2.0 * np.pi / side)
    ang_b = ((j * k) % SEQ).astype(F32) * (2.0 * np.pi / SEQ)
    ca, sa = jnp.cos(ang_a)[:, :, None], jnp.sin(ang_a)[:, :, None]
    cb, sb = jnp.cos(ang_b)[:, None, :], jnp.sin(ang_b)[:, None, :]
    cs = (ca * cb - sa * sb).astype(BF16).reshape(SEQ, SEQ)
    ss = (sa * cb + ca * sb).astype(BF16).reshape(SEQ, SEQ)
    return cs, ss


def kernel(x, mem, pf_w_in, pf_pool_w, pf_pool_scale, pf_fourier_ln_g, pf_fourier_w, pf_w_out, hg_w_in, hg_lower_bounds, hg_norm_g, hg_w_out, xa_wq, xa_wkv, xa_wo, moe_w_group, moe_b_group, moe_w_expert, moe_b_expert, moe_w_gate, moe_w_up, moe_w_down, ln_g, ln_b):
    bf = lambda a: a.astype(BF16)
    xt = x.reshape(TOKENS, D_MODEL)
    mem2d = mem.reshape(BATCH * MEM_LEN, D_MODEL)
    lb_all = jnp.cumsum(jax.nn.softmax(hg_lower_bounds.astype(F32), axis=0), axis=0)
    lb_all = lb_all - lb_all[:1]
    tril = jnp.asarray(np.tril(np.ones((HG_CHUNK, HG_CHUNK), np.float32)))
    triu = jnp.asarray(np.triu(np.ones((HG_CHUNK, HG_CHUNK), np.float32)))
    upper = jnp.asarray(np.triu(np.ones((TM_ROUTE, TM_ROUTE), np.float32), k=1), BF16)

    for l in range(DEPTH):
        j = l // 2
        lng = lambda s: ln_g[l, s][None, :]
        lnb = lambda s: ln_b[l, s][None, :]
        if l % 2 == 0:
            cs, ss = _sequence_dft_matrices()
            ua, pq = _pf_in(xt, pf_w_in, j, pf_fourier_ln_g[j].reshape(1, HALF), _channel_dft_matrix())
            ya = _pool(ua, bf(pf_pool_w[j]), pf_pool_scale[j][None, :])
            yb = _fourier(cs, ss, pq, bf(pf_fourier_w[j]))
            xt = _pf_out(ya, yb, pf_w_out, j, xt, lng(0), lnb(0))
        else:
            proj = _hg_in(xt, hg_w_in[j])
            lb = lb_all[l].reshape(2, HG_HEADS, 1, HG_DIM)
            og = _gla(proj, lb, hg_norm_g[j][None, :], tril, triu)
            xt = _hg_out(og, hg_w_out, j, xt, lng(0), lnb(0))
        kv = _kv_proj(mem2d, xa_wkv, l)
        xt = _xattn(xt, xa_wq, kv, xa_wo, l, lng(1), lnb(1))
        xt = _moe(xt, moe_w_group[l], moe_b_group[l], moe_w_expert[l], moe_b_expert[l],
                  moe_w_gate, moe_w_up, moe_w_down, l, lng(2), lnb(2), upper)
    return xt.reshape(BATCH, SEQ, D_MODEL)
```

```python
import functools

import jax
import jax.numpy as jnp
import numpy as np
from jax import lax
from jax.experimental import pallas as pl
from jax.experimental.pallas import tpu as pltpu

F32 = jnp.float32
BF16 = jnp.bfloat16
I32 = jnp.int32

D_MODEL = 1024
BATCH = 8
SEQ = 4096
DEPTH = 2
TOKENS = BATCH * SEQ
MEM_LEN = 256
HALF = D_MODEL // 2
POOL_WINDOWS = (2, 4, 8, 16)
POOL_DIM = 128
FOURIER_HEADS = 4
FOURIER_DIM = 128
HG_HEADS = 8
HG_DIM = 128
HG_N_PROJ = 5
HG_CHUNK = 64
XA_HEADS = 4
XA_HEAD_DIM = 256
MOE_GROUPS = 4
MOE_PER_GROUP = 8
MOE_EXPERTS = 32
MOE_TOPK = 2
MOE_HIDDEN = 512
DN_ALPHA = (2.0 * DEPTH) ** 0.25
LN_EPS = 1e-5

V7X_LANES = 128
V7X_SUBLANES = 8
V7X_VMEM_LIMIT_BYTES = 52 * 1024 * 1024

TM = 512
TM_ROUTE = 1024
TM_PERM = 1024
TM_COMB = 512
TE = 256
SORTED_ROWS = TOKENS * MOE_TOPK + MOE_EXPERTS * TE
N_EXPERT_TILES = SORTED_ROWS // TE
GLA_GROUP = 8
GLA_ROWS = GLA_GROUP * HG_CHUNK
FOURIER_TR = 512
ROUTER_ROWS = 40

_NT = (((1,), (1,)), ((), ()))


def _params(*sem):
    return pltpu.CompilerParams(dimension_semantics=sem, vmem_limit_bytes=V7X_VMEM_LIMIT_BYTES)


def _dot(a, b):
    return jnp.dot(a, b, preferred_element_type=F32)


def _post_ln(x, h, g, b):
    y = DN_ALPHA * x + h
    mu = jnp.mean(y, axis=-1, keepdims=True)
    yc = y - mu
    var = jnp.mean(yc * yc, axis=-1, keepdims=True)
    return yc * lax.rsqrt(var + LN_EPS) * g + b


def _silu(x):
    return x * jax.nn.sigmoid(x)


def _resident(shape, index):
    return pl.BlockSpec(shape, lambda *_: index, pipeline_mode=pl.Buffered(1))


def _cast_weight_once(w_ref, wbf_ref):
    @pl.when(pl.program_id(0) == 0)
    def _():
        wbf_ref[...] = w_ref[0].astype(BF16)


def _k_pf_in(x_ref, w_ref, lng_ref, wc_ref, ua_ref, pq_ref, wbf_ref):
    _cast_weight_once(w_ref, wbf_ref)
    u = _dot(x_ref[...].astype(BF16), wbf_ref[...])
    ua_ref[...] = u[:, :HALF]
    parts = []
    for h in range(FOURIER_HEADS):
        ub = u[:, HALF + FOURIER_DIM * h:HALF + FOURIER_DIM * (h + 1)]
        mu = jnp.mean(ub, axis=-1, keepdims=True)
        uc = ub - mu
        var = jnp.mean(uc * uc, axis=-1, keepdims=True)
        parts.append(uc * lax.rsqrt(var + LN_EPS))
    un = jnp.concatenate(parts, axis=-1) * lng_ref[...]
    pq_ref[...] = _dot(un.astype(BF16), wc_ref[...]).astype(BF16)


def _pf_in(x, w_in_all, j, ln_g, wc):
    return pl.pallas_call(
        _k_pf_in,
        out_shape=(jax.ShapeDtypeStruct((TOKENS, HALF), F32),
                   jax.ShapeDtypeStruct((TOKENS, D_MODEL), BF16)),
        grid=(TOKENS // TM,),
        in_specs=[pl.BlockSpec((TM, D_MODEL), lambda i: (i, 0)),
                  _resident((1, D_MODEL, D_MODEL), (j, 0, 0)),
                  pl.BlockSpec((1, HALF), lambda i: (0, 0)),
                  _resident((HALF, D_MODEL), (0, 0))],
        out_specs=(pl.BlockSpec((TM, HALF), lambda i: (i, 0)),
                   pl.BlockSpec((TM, D_MODEL), lambda i: (i, 0))),
        scratch_shapes=[pltpu.VMEM((D_MODEL, D_MODEL), BF16)],
        compiler_params=_params("arbitrary"),
        name="pf_in",
    )(x, w_in_all, ln_g, wc)


_POOL_HALO = 8


def _k_pool(u_ref, up_ref, un_ref, pw_ref, ps_ref, ya_ref, ext_ref):
    i = pl.program_id(0)
    tiles_per_seq = SEQ // TM
    j = lax.rem(i, tiles_per_seq)
    ext_ref[_POOL_HALO:_POOL_HALO + TM, :] = u_ref[...]
    ext_ref[0:_POOL_HALO, :] = jnp.where(j == 0, 0.0, up_ref[...])
    ext_ref[_POOL_HALO + TM:2 * _POOL_HALO + TM, :] = jnp.where(j == tiles_per_seq - 1, 0.0, un_ref[...])
    pos = lax.broadcasted_iota(I32, (TM, POOL_DIM), 0) + j * TM
    outs = []
    for g, w in enumerate(POOL_WINDOWS):
        hw = w // 2
        cols = slice(g * POOL_DIM, (g + 1) * POOL_DIM)
        acc = ext_ref[_POOL_HALO - hw:_POOL_HALO - hw + TM, cols]
        for d in range(-hw + 1, hw):
            acc = acc + ext_ref[_POOL_HALO + d:_POOL_HALO + d + TM, cols]
        cnt = jnp.minimum(pos + hw, SEQ) - jnp.maximum(pos - hw, 0)
        pooled = acc / cnt.astype(F32) - u_ref[:, cols]
        outs.append(_dot(pooled.astype(BF16), pw_ref[g]))
    ya_ref[...] = (jnp.concatenate(outs, axis=-1) * ps_ref[...]).astype(BF16)


def _pool(ua, pool_w, pool_scale):
    blocks_per_tile = TM // _POOL_HALO
    last_block = TOKENS // _POOL_HALO - 1
    return pl.pallas_call(
        _k_pool,
        out_shape=jax.ShapeDtypeStruct((TOKENS, HALF), BF16),
        grid=(TOKENS // TM,),
        in_specs=[pl.BlockSpec((TM, HALF), lambda i: (i, 0)),
                  pl.BlockSpec((_POOL_HALO, HALF), lambda i: (jnp.maximum(i * blocks_per_tile - 1, 0), 0)),
                  pl.BlockSpec((_POOL_HALO, HALF),
                               lambda i: (jnp.minimum((i + 1) * blocks_per_tile, last_block), 0)),
                  pl.BlockSpec((4, POOL_DIM, POOL_DIM), lambda i: (0, 0, 0)),
                  pl.BlockSpec((1, HALF), lambda i: (0, 0))],
        out_specs=pl.BlockSpec((TM, HALF), lambda i: (i, 0)),
        scratch_shapes=[pltpu.VMEM((TM + 2 * _POOL_HALO, HALF), F32)],
        compiler_params=_params("parallel"),
        name="pf_pool",
    )(ua, ua, ua, pool_w, pool_scale)


_FOURIER_SCALE = 1.0 / float(np.sqrt(SEQ * FOURIER_DIM))


def _k_fourier(cs_ref, ss_ref, pq_ref, fw_ref, yb_ref):
    y = _dot(cs_ref[...], pq_ref[:, :HALF]) - _dot(ss_ref[...], pq_ref[:, HALF:])
    y = y * _FOURIER_SCALE
    outs = []
    for h in range(FOURIER_HEADS):
        outs.append(_dot(y[:, FOURIER_DIM * h:FOURIER_DIM * (h + 1)].astype(BF16), fw_ref[h]))
    yb_ref[...] = jnp.concatenate(outs, axis=-1).astype(BF16)


def _fourier(cs, ss, pq, fw):
    row_tiles = SEQ // FOURIER_TR
    return pl.pallas_call(
        _k_fourier,
        out_shape=jax.ShapeDtypeStruct((TOKENS, HALF), BF16),
        grid=(row_tiles, BATCH),
        in_specs=[pl.BlockSpec((FOURIER_TR, SEQ), lambda r, b: (r, 0)),
                  pl.BlockSpec((FOURIER_TR, SEQ), lambda r, b: (r, 0)),
                  pl.BlockSpec((SEQ, D_MODEL), lambda r, b: (b, 0)),
                  pl.BlockSpec((4, FOURIER_DIM, FOURIER_DIM), lambda r, b: (0, 0, 0))],
        out_specs=pl.BlockSpec((FOURIER_TR, HALF), lambda r, b: (b * row_tiles + r, 0)),
        compiler_params=_params("parallel", "parallel"),
        name="pf_fourier",
    )(cs, ss, pq, fw)


def _k_pf_out(ya_ref, yb_ref, w_ref, x_ref, g_ref, b_ref, o_ref, wbf_ref):
    _cast_weight_once(w_ref, wbf_ref)
    a = jnp.concatenate([ya_ref[...], yb_ref[...]], axis=-1)
    o_ref[...] = _post_ln(x_ref[...], _dot(a, wbf_ref[...]), g_ref[...], b_ref[...])


def _pf_out(ya, yb, w_out_all, j, x, g, b):
    row = lambda i: (i, 0)
    fixed = lambda i: (0, 0)
    return pl.pallas_call(
        _k_pf_out,
        out_shape=jax.ShapeDtypeStruct((TOKENS, D_MODEL), F32),
        grid=(TOKENS // TM,),
        in_specs=[pl.BlockSpec((TM, HALF), row), pl.BlockSpec((TM, HALF), row),
                  _resident((1, D_MODEL, D_MODEL), (j, 0, 0)), pl.BlockSpec((TM, D_MODEL), row),
                  pl.BlockSpec((1, D_MODEL), fixed), pl.BlockSpec((1, D_MODEL), fixed)],
        out_specs=pl.BlockSpec((TM, D_MODEL), row),
        scratch_shapes=[pltpu.VMEM((D_MODEL, D_MODEL), BF16)],
        compiler_params=_params("arbitrary"),
        name="pf_out",
    )(ya, yb, w_out_all, x, g, b)


_HG_STAGE_COLS = 512


def _k_hg_in(x_ref, w_hbm, o_ref, wbf_ref, stage_ref, sem):
    @pl.when(pl.program_id(0) == 0)
    def _():
        n_chunks = HG_N_PROJ * D_MODEL // _HG_STAGE_COLS

        def chunk_copy(c):
            cols = pl.ds(c * _HG_STAGE_COLS, _HG_STAGE_COLS)
            return pltpu.make_async_copy(w_hbm.at[:, cols], stage_ref.at[c % 2], sem.at[c % 2])

        chunk_copy(0).start()
        for c in range(n_chunks):
            if c + 1 < n_chunks:
                chunk_copy(c + 1).start()
            chunk_copy(c).wait()
            wbf_ref[:, c * _HG_STAGE_COLS:(c + 1) * _HG_STAGE_COLS] = stage_ref[c % 2].astype(BF16)

    xb = x_ref[...].astype(BF16)
    for p in range(HG_N_PROJ):
        acc = _dot(xb, wbf_ref[:, p * D_MODEL:(p + 1) * D_MODEL])
        if p == 0:
            acc = _silu(acc)
        for h in range(HG_HEADS):
            o_ref[p, h] = acc[:, HG_DIM * h:HG_DIM * (h + 1)].astype(BF16)


def _hg_in(x, w_in):
    return pl.pallas_call(
        _k_hg_in,
        out_shape=jax.ShapeDtypeStruct((HG_N_PROJ, HG_HEADS, TOKENS, HG_DIM), BF16),
        grid=(TOKENS // TM,),
        in_specs=[pl.BlockSpec((TM, D_MODEL), lambda i: (i, 0)),
                  pl.BlockSpec(memory_space=pl.ANY)],
        out_specs=pl.BlockSpec((HG_N_PROJ, HG_HEADS, TM, HG_DIM), lambda i: (0, 0, i, 0)),
        scratch_shapes=[pltpu.VMEM((D_MODEL, HG_N_PROJ * D_MODEL), BF16),
                        pltpu.VMEM((2, D_MODEL, _HG_STAGE_COLS), F32),
                        pltpu.SemaphoreType.DMA((2,))],
        compiler_params=_params("arbitrary"),
        name="hg_in",
    )(x, w_in)


def _gla_group(q, v, z, lbv, tri, st, rev):
    L, G = HG_CHUNK, GLA_GROUP
    f = lbv + (1.0 - lbv) * jax.nn.sigmoid(z.astype(F32))
    lf = jnp.log(f).reshape(G, L, HG_DIM)
    k3 = (1.0 - f).reshape(G, L, HG_DIM)
    q3 = q.astype(F32).reshape(G, L, HG_DIM)
    v3 = v.reshape(G, L, HG_DIM)
    lf_hi = lf.astype(BF16)
    lf_lo = (lf - lf_hi.astype(F32)).astype(BF16)
    b2 = jnp.einsum('glm,gmk->glk', jnp.broadcast_to(tri, (G, L, L)),
                    jnp.concatenate([lf_hi, lf_lo], axis=-1), preferred_element_type=F32)
    b = b2[..., :HG_DIM] + b2[..., HG_DIM:]
    mid = L // 2 if not rev else L - 1 - L // 2
    end = L - 1 if not rev else 0
    b_ref = b[:, mid:mid + 1, :]
    b_last = b[:, end:end + 1, :]
    d = b - b_ref
    qt32 = q3 * jnp.exp(d)
    kt32 = k3 * jnp.exp(-d)
    qt = qt32.astype(BF16)
    kt = kt32.astype(BF16)
    sc = jnp.einsum('glk,gmk->glm', qt, kt, preferred_element_type=F32)
    li = lax.broadcasted_iota(I32, (L, L), 0)
    mi = lax.broadcasted_iota(I32, (L, L), 1)
    causal = (li >= mi) if not rev else (li <= mi)
    sc = jnp.where(causal[None], sc, 0.0).astype(BF16)
    o_intra = jnp.einsum('glm,gmv->glv', sc, v3, preferred_element_type=F32)
    ku = (kt32 * jnp.exp(b_last - b_ref)).astype(BF16)
    u_t = jnp.einsum('glv,glk->gvk', v3, ku, preferred_element_type=F32)
    qe = (qt32 * jnp.exp(b_ref)).astype(BF16)
    dec = jnp.exp(b_last)
    outs = [None] * G
    for n in (range(G) if not rev else range(G - 1, -1, -1)):
        o_inter = lax.dot_general(qe[n], st.astype(BF16), _NT, preferred_element_type=F32)
        outs[n] = o_intra[n] + o_inter
        st = dec[n] * st + u_t[n]
    return jnp.concatenate(outs, axis=0), st


def _k_gla(q_ref, v_ref, zf_ref, zb_ref, g_ref, lb_ref, ng_ref, lt_ref, ut_ref, o_ref, of_ref, ob_ref):
    n_groups = SEQ // GLA_ROWS
    zero_state = jnp.zeros((HG_DIM, HG_DIM), F32)

    def group_rows(gi):
        return pl.ds(pl.multiple_of(gi * GLA_ROWS, GLA_ROWS), GLA_ROWS)

    def scan(t, carry):
        st_f, st_b = carry
        rf, rb = group_rows(t), group_rows(n_groups - 1 - t)
        o_f, st_f = _gla_group(q_ref[0, 0, rf, :], v_ref[0, 0, rf, :], zf_ref[0, 0, rf, :],
                               lb_ref[0, 0], lt_ref[...], st_f, rev=False)
        o_b, st_b = _gla_group(q_ref[0, 0, rb, :], v_ref[0, 0, rb, :], zb_ref[0, 0, rb, :],
                               lb_ref[1, 0], ut_ref[...], st_b, rev=True)
        of_ref[rf, :] = o_f
        ob_ref[rb, :] = o_b
        return st_f, st_b

    lax.fori_loop(0, n_groups, scan, (zero_state, zero_state))

    def finish(gi, c):
        rows = group_rows(gi)
        o = of_ref[rows, :] + ob_ref[rows, :]
        o = o * lax.rsqrt(jnp.mean(o * o, axis=-1, keepdims=True) + LN_EPS)
        o = o * ng_ref[...] * _silu(g_ref[0, 0, rows, :].astype(F32))
        o_ref[0, rows, :] = o.astype(BF16)
        return c

    lax.fori_loop(0, n_groups, finish, 0)


def _gla(proj, lb, norm_g, tril, triu):
    def pspec(p):
        return pl.BlockSpec((1, 1, SEQ, HG_DIM), lambda b, h: (p, h, b, 0))
    fixed = lambda b, h: (0, 0)
    return pl.pallas_call(
        _k_gla,
        out_shape=jax.ShapeDtypeStruct((HG_HEADS, TOKENS, HG_DIM), BF16),
        grid=(BATCH, HG_HEADS),
        in_specs=[pspec(0), pspec(1), pspec(2), pspec(3), pspec(4),
                  pl.BlockSpec((2, 1, 1, HG_DIM), lambda b, h: (0, h, 0, 0)),
                  pl.BlockSpec((1, HG_DIM), fixed),
                  pl.BlockSpec((HG_CHUNK, HG_CHUNK), fixed),
                  pl.BlockSpec((HG_CHUNK, HG_CHUNK), fixed)],
        out_specs=pl.BlockSpec((1, SEQ, HG_DIM), lambda b, h: (h, b, 0)),
        scratch_shapes=[pltpu.VMEM((SEQ, HG_DIM), F32), pltpu.VMEM((SEQ, HG_DIM), F32)],
        compiler_params=_params("parallel", "parallel"),
        name="hg_gla",
    )(proj, proj, proj, proj, proj, lb, norm_g, tril, triu)


def _k_hg_out(a_ref, w_ref, x_ref, g_ref, b_ref, o_ref, wbf_ref):
    _cast_weight_once(w_ref, wbf_ref)
    a = jnp.concatenate([a_ref[h] for h in range(HG_HEADS)], axis=-1)
    o_ref[...] = _post_ln(x_ref[...], _dot(a, wbf_ref[...]), g_ref[...], b_ref[...])


def _hg_out(og, w_out_all, j, x, g, b):
    row = lambda i: (i, 0)
    fixed = lambda i: (0, 0)
    return pl.pallas_call(
        _k_hg_out,
        out_shape=jax.ShapeDtypeStruct((TOKENS, D_MODEL), F32),
        grid=(TOKENS // TM,),
        in_specs=[pl.BlockSpec((HG_HEADS, TM, HG_DIM), lambda i: (0, i, 0)),
                  _resident((1, D_MODEL, D_MODEL), (j, 0, 0)), pl.BlockSpec((TM, D_MODEL), row),
                  pl.BlockSpec((1, D_MODEL), fixed), pl.BlockSpec((1, D_MODEL), fixed)],
        out_specs=pl.BlockSpec((TM, D_MODEL), row),
        scratch_shapes=[pltpu.VMEM((D_MODEL, D_MODEL), BF16)],
        compiler_params=_params("arbitrary"),
        name="hg_out",
    )(og, w_out_all, x, g, b)


def _k_mm(x_ref, w_ref, o_ref):
    o_ref[...] = _dot(x_ref[...].astype(BF16), w_ref[0].astype(BF16)).astype(o_ref.dtype)


def _kv_proj(mem2d, wkv_all, l):
    rows, cols = mem2d.shape[0], wkv_all.shape[2]
    return pl.pallas_call(
        _k_mm,
        out_shape=jax.ShapeDtypeStruct((rows, cols), BF16),
        grid=(cols // D_MODEL, rows // TM),
        in_specs=[pl.BlockSpec((TM, D_MODEL), lambda j, i: (i, 0)),
                  pl.BlockSpec((1, D_MODEL, D_MODEL), lambda j, i: (l, 0, j))],
        out_specs=pl.BlockSpec((TM, D_MODEL), lambda j, i: (i, j)),
        compiler_params=_params("parallel", "parallel"),
        name="xa_kv",
    )(mem2d, wkv_all)


ROW_SLABS = D_MODEL // V7X_LANES


def _to_row_tiles(ref, x):
    n = x.shape[0]
    for s in range(ROW_SLABS):
        ref[pl.ds(s, n, stride=ROW_SLABS), :] = x[:, V7X_LANES * s:V7X_LANES * (s + 1)]


def _from_row_tiles(ref):
    n = ref.shape[0] // ROW_SLABS
    return jnp.concatenate([ref[pl.ds(s, n, stride=ROW_SLABS), :] for s in range(ROW_SLABS)], axis=-1)


def _k_xattn(x_ref, wq_ref, kv_ref, wo_ref, g_ref, b_ref, o_ref, o3_ref, wqbf_ref, wobf_ref):
    _cast_weight_once(wq_ref, wqbf_ref)
    _cast_weight_once(wo_ref, wobf_ref)
    x = x_ref[...]
    q = _dot(x.astype(BF16), wqbf_ref[...]).astype(BF16)
    outs = []
    for h in range(XA_HEADS):
        cols = slice(XA_HEAD_DIM * h, XA_HEAD_DIM * (h + 1))
        kh = kv_ref[:, cols]
        vh = kv_ref[:, D_MODEL + XA_HEAD_DIM * h:D_MODEL + XA_HEAD_DIM * (h + 1)]
        s = lax.dot_general(q[:, cols], kh, _NT, preferred_element_type=F32) * (XA_HEAD_DIM ** -0.5)
        e = jnp.exp(s - jnp.max(s, axis=-1, keepdims=True))
        p = e / jnp.sum(e, axis=-1, keepdims=True)
        outs.append(_dot(p.astype(BF16), vh))
    o = jnp.concatenate(outs, axis=-1).astype(BF16)
    y = _post_ln(x, _dot(o, wobf_ref[...]), g_ref[...], b_ref[...])
    o_ref[...] = y
    _to_row_tiles(o3_ref, y)


def _xattn(x, wq_all, kv, wo_all, l, g, b):
    row = lambda i: (i, 0)
    fixed = lambda i: (0, 0)
    tiles_per_seq = SEQ // TM
    return pl.pallas_call(
        _k_xattn,
        out_shape=(jax.ShapeDtypeStruct((TOKENS, D_MODEL), F32),
                   jax.ShapeDtypeStruct((TOKENS * ROW_SLABS, V7X_LANES), F32)),
        grid=(TOKENS // TM,),
        in_specs=[pl.BlockSpec((TM, D_MODEL), row),
                  _resident((1, D_MODEL, D_MODEL), (l, 0, 0)),
                  pl.BlockSpec((MEM_LEN, 2 * D_MODEL), lambda i: (i // tiles_per_seq, 0)),
                  _resident((1, D_MODEL, D_MODEL), (l, 0, 0)),
                  pl.BlockSpec((1, D_MODEL), fixed), pl.BlockSpec((1, D_MODEL), fixed)],
        out_specs=(pl.BlockSpec((TM, D_MODEL), row),
                   pl.BlockSpec((TM * ROW_SLABS, V7X_LANES), row)),
        scratch_shapes=[pltpu.VMEM((D_MODEL, D_MODEL), BF16), pltpu.VMEM((D_MODEL, D_MODEL), BF16)],
        compiler_params=_params("arbitrary"),
        name="xattn",
    )(x, wq_all, kv, wo_all, g, b)


def _first_argmax_rows(v, n_rows):
    m = jnp.max(v, axis=0, keepdims=True)
    rows = lax.broadcasted_iota(I32, v.shape, 0)
    idx = jnp.min(jnp.where(v == m, rows, n_rows), axis=0, keepdims=True)
    return m, idx


def _k_router(x_ref, wr_ref, br_ref, e_ref, g_ref):
    lt = lax.dot_general(wr_ref[...], x_ref[...].astype(BF16), _NT, preferred_element_type=F32)
    lt = lt + br_ref[...]
    gl = lt[0:MOE_GROUPS, :]
    gmax, gi = _first_argmax_rows(gl, MOE_GROUPS)
    p_grp = 1.0 / jnp.sum(jnp.exp(gl - gmax), axis=0, keepdims=True)
    sel = lt[MOE_GROUPS:MOE_GROUPS + MOE_PER_GROUP, :]
    for g in range(1, MOE_GROUPS):
        lo = MOE_GROUPS + MOE_PER_GROUP * g
        sel = jnp.where(gi == g, lt[lo:lo + MOE_PER_GROUP, :], sel)
    m1, i1 = _first_argmax_rows(sel, MOE_PER_GROUP)
    rows = lax.broadcasted_iota(I32, sel.shape, 0)
    m2, i2 = _first_argmax_rows(jnp.where(rows == i1, -jnp.inf, sel), MOE_PER_GROUP)
    e2 = jnp.exp(m2 - m1)
    g1 = p_grp / (1.0 + e2)
    e_ref[0:1, :] = gi * MOE_PER_GROUP + i1
    e_ref[1:2, :] = gi * MOE_PER_GROUP + i2
    g_ref[0:1, :] = g1
    g_ref[1:2, :] = g1 * e2


def _router(x, wr, br):
    return pl.pallas_call(
        _k_router,
        out_shape=(jax.ShapeDtypeStruct((MOE_TOPK, TOKENS), I32),
                   jax.ShapeDtypeStruct((MOE_TOPK, TOKENS), F32)),
        grid=(TOKENS // TM_ROUTE,),
        in_specs=[pl.BlockSpec((TM_ROUTE, D_MODEL), lambda i: (i, 0)),
                  pl.BlockSpec((ROUTER_ROWS, D_MODEL), lambda i: (0, 0)),
                  pl.BlockSpec((ROUTER_ROWS, 1), lambda i: (0, 0))],
        out_specs=(pl.BlockSpec((MOE_TOPK, TM_ROUTE), lambda i: (0, i)),
                   pl.BlockSpec((MOE_TOPK, TM_ROUTE), lambda i: (0, i))),
        compiler_params=_params("parallel"),
        name="moe_router",
    )(x, wr, br)


def _k_rank(e_ref, u_ref, rank_ref, cnt_ref, carry_ref):
    @pl.when(pl.program_id(0) == 0)
    def _():
        carry_ref[...] = jnp.zeros_like(carry_ref)

    rows = lax.broadcasted_iota(I32, (MOE_EXPERTS, TM_ROUTE), 0)
    oh0 = rows == e_ref[0:1, :]
    oh1 = rows == e_ref[1:2, :]
    oh = jnp.where(oh0 | oh1, 1.0, 0.0)
    before = _dot(oh.astype(BF16), u_ref[...]) + carry_ref[:, 0:1]
    rank_ref[0:1, :] = jnp.sum(jnp.where(oh0, before, 0.0), axis=0, keepdims=True).astype(I32)
    rank_ref[1:2, :] = jnp.sum(jnp.where(oh1, before, 0.0), axis=0, keepdims=True).astype(I32)
    carry_ref[...] = carry_ref[...] + jnp.sum(oh, axis=1, keepdims=True)
    cnt_ref[...] = carry_ref[...]


def _rank(eidx, upper):
    return pl.pallas_call(
        _k_rank,
        out_shape=(jax.ShapeDtypeStruct((MOE_TOPK, TOKENS), I32),
                   jax.ShapeDtypeStruct((MOE_EXPERTS, V7X_LANES), F32)),
        grid=(TOKENS // TM_ROUTE,),
        in_specs=[pl.BlockSpec((MOE_TOPK, TM_ROUTE), lambda i: (0, i)),
                  pl.BlockSpec((TM_ROUTE, TM_ROUTE), lambda i: (0, 0))],
        out_specs=(pl.BlockSpec((MOE_TOPK, TM_ROUTE), lambda i: (0, i)),
                   pl.BlockSpec((MOE_EXPERTS, V7X_LANES), lambda i: (0, 0))),
        scratch_shapes=[pltpu.VMEM((MOE_EXPERTS, V7X_LANES), F32)],
        compiler_params=_params("arbitrary"),
        name="moe_rank",
    )(eidx, upper)


def _k_pos(e_ref, rank_ref, off_ref, pos_ref):
    rows = lax.broadcasted_iota(I32, (MOE_EXPERTS, TM_ROUTE), 0)
    for k in range(MOE_TOPK):
        start = jnp.sum(jnp.where(rows == e_ref[k:k + 1, :], off_ref[...], 0), axis=0, keepdims=True)
        pos_ref[k:k + 1, :] = start + rank_ref[k:k + 1, :]


def _positions(eidx, rank, offsets):
    tok = pl.BlockSpec((MOE_TOPK, TM_ROUTE), lambda i: (0, i))
    return pl.pallas_call(
        _k_pos,
        out_shape=jax.ShapeDtypeStruct((MOE_TOPK, TOKENS), I32),
        grid=(TOKENS // TM_ROUTE,),
        in_specs=[tok, tok, pl.BlockSpec((MOE_EXPERTS, 1), lambda i: (0, 0))],
        out_specs=tok,
        compiler_params=_params("parallel"),
        name="moe_pos",
    )(eidx, rank, offsets)


def _row_copy(src_ref, src_row, dst_ref, dst_row, sem):
    src = pl.ds(pl.multiple_of(src_row * ROW_SLABS, ROW_SLABS), ROW_SLABS)
    dst = pl.ds(pl.multiple_of(dst_row * ROW_SLABS, ROW_SLABS), ROW_SLABS)
    return pltpu.make_async_copy(src_ref.at[src], dst_ref.at[dst], sem)


def _k_dispatch(pos_ref, ends_ref, x_ref, xs_ref, zero_ref, sem, zsem):
    base = pl.program_id(0) * TM_PERM

    @pl.when(pl.program_id(0) == 0)
    def _():
        zero_ref[...] = jnp.zeros_like(zero_ref)

        def last_tile(e):
            seg_start = ends_ref[e - 1] if e > 0 else 0
            start = ends_ref[e] - TE
            return ends_ref[e] > seg_start, zero_fill(start)

        def unused_tile(t):
            start = ends_ref[MOE_EXPERTS - 1] + t * TE
            return start < SORTED_ROWS, zero_fill(start)

        def zero_fill(start_row):
            rows = pl.ds(pl.multiple_of(start_row * ROW_SLABS, TE * ROW_SLABS), TE * ROW_SLABS)
            return pltpu.make_async_copy(zero_ref, xs_ref.at[rows], zsem)

        fills = [last_tile(e) for e in range(MOE_EXPERTS)] + [unused_tile(t) for t in range(MOE_EXPERTS)]
        for needed, copy in fills:
            pl.when(needed)(copy.start)
        for needed, copy in fills:
            pl.when(needed)(copy.wait)

    def body(j, c):
        for k in range(MOE_TOPK):
            _row_copy(x_ref, j, xs_ref, pos_ref[k * TOKENS + base + j], sem).start()
        return c

    lax.fori_loop(0, TM_PERM, body, 0, unroll=8)
    for k in range(MOE_TOPK):
        pltpu.make_async_copy(x_ref, xs_ref.at[pl.ds(0, TM_PERM * ROW_SLABS)], sem).wait()


def _dispatch(pos_flat, ends, x):
    return pl.pallas_call(
        _k_dispatch,
        out_shape=jax.ShapeDtypeStruct((SORTED_ROWS * ROW_SLABS, V7X_LANES), F32),
        grid_spec=pltpu.PrefetchScalarGridSpec(
            num_scalar_prefetch=2,
            grid=(TOKENS // TM_PERM,),
            in_specs=[pl.BlockSpec((TM_PERM * ROW_SLABS, V7X_LANES), lambda i, pos, ends: (i, 0))],
            out_specs=pl.BlockSpec(memory_space=pl.ANY),
            scratch_shapes=[pltpu.VMEM((TE * ROW_SLABS, V7X_LANES), F32), pltpu.SemaphoreType.DMA,
                            pltpu.SemaphoreType.DMA],
        ),
        compiler_params=_params("arbitrary"),
        name="moe_dispatch",
    )(pos_flat, ends, x)


def _k_experts(te_ref, na_ref, xs_ref, wg_ref, wu_ref, wd_ref, ys_ref, wgubf_ref, wdbf_ref):
    i = pl.program_id(0)
    active = i < na_ref[0]
    new_expert = jnp.logical_or(i == 0, te_ref[jnp.maximum(i - 1, 0)] != te_ref[i])

    @pl.when(jnp.logical_and(active, new_expert))
    def _():
        wgubf_ref[:, :MOE_HIDDEN] = wg_ref[0, 0].astype(BF16)
        wgubf_ref[:, MOE_HIDDEN:] = wu_ref[0, 0].astype(BF16)
        wdbf_ref[...] = wd_ref[0, 0].astype(BF16)

    @pl.when(active)
    def _():
        xb = _from_row_tiles(xs_ref).astype(BF16)
        gu = _dot(xb, wgubf_ref[...])
        h = _silu(gu[:, :MOE_HIDDEN]) * gu[:, MOE_HIDDEN:]
        _to_row_tiles(ys_ref, _dot(h.astype(BF16), wdbf_ref[...]))

    @pl.when(jnp.logical_not(active))
    def _():
        ys_ref[...] = jnp.zeros_like(ys_ref)


def _experts(tile_expert, n_active, xs, wg_all, wu_all, wd_all, l):
    def xs_index(i, te, na):
        return (jnp.minimum(i, na[0] - 1), 0)

    def w_index(i, te, na):
        return (l, te[i], 0, 0)

    return pl.pallas_call(
        _k_experts,
        out_shape=jax.ShapeDtypeStruct((SORTED_ROWS * ROW_SLABS, V7X_LANES), F32),
        grid_spec=pltpu.PrefetchScalarGridSpec(
            num_scalar_prefetch=2,
            grid=(N_EXPERT_TILES,),
            in_specs=[pl.BlockSpec((TE * ROW_SLABS, V7X_LANES), xs_index),
                      pl.BlockSpec((1, 1, D_MODEL, MOE_HIDDEN), w_index),
                      pl.BlockSpec((1, 1, D_MODEL, MOE_HIDDEN), w_index),
                      pl.BlockSpec((1, 1, MOE_HIDDEN, D_MODEL), w_index)],
            out_specs=pl.BlockSpec((TE * ROW_SLABS, V7X_LANES), lambda i, te, na: (i, 0)),
            scratch_shapes=[pltpu.VMEM((D_MODEL, 2 * MOE_HIDDEN), BF16),
                            pltpu.VMEM((MOE_HIDDEN, D_MODEL), BF16)],
        ),
        compiler_params=_params("arbitrary"),
        name="moe_experts",
    )(tile_expert, n_active, xs, wg_all, wu_all, wd_all)


def _k_combine(pos_ref, ys_ref, gt_ref, x_ref, g_ref, b_ref, o_ref, buf_ref, sem):
    base = pl.program_id(0) * TM_COMB

    def body(j, c):
        for k in range(MOE_TOPK):
            _row_copy(ys_ref, pos_ref[k * TOKENS + base + j], buf_ref.at[k], j, sem).start()
        return c

    lax.fori_loop(0, TM_COMB, body, 0, unroll=8)
    for k in range(MOE_TOPK):
        pltpu.make_async_copy(ys_ref.at[pl.ds(0, TM_COMB * ROW_SLABS)], buf_ref.at[k], sem).wait()
    h = gt_ref[:, 0:1] * _from_row_tiles(buf_ref.at[0]) + gt_ref[:, 1:2] * _from_row_tiles(buf_ref.at[1])
    o_ref[...] = _post_ln(x_ref[...], h, g_ref[...], b_ref[...])


def _combine(pos_flat, ys, gates_t, x, g, b):
    row = lambda i, pos: (i, 0)
    fixed = lambda i, pos: (0, 0)
    return pl.pallas_call(
        _k_combine,
        out_shape=jax.ShapeDtypeStruct((TOKENS, D_MODEL), F32),
        grid_spec=pltpu.PrefetchScalarGridSpec(
            num_scalar_prefetch=1,
            grid=(TOKENS // TM_COMB,),
            in_specs=[pl.BlockSpec(memory_space=pl.ANY),
                      pl.BlockSpec((TM_COMB, MOE_TOPK), row),
                      pl.BlockSpec((TM_COMB, D_MODEL), row),
                      pl.BlockSpec((1, D_MODEL), fixed), pl.BlockSpec((1, D_MODEL), fixed)],
            out_specs=pl.BlockSpec((TM_COMB, D_MODEL), row),
            scratch_shapes=[pltpu.VMEM((MOE_TOPK, TM_COMB * ROW_SLABS, V7X_LANES), F32),
                            pltpu.SemaphoreType.DMA],
        ),
        compiler_params=_params("arbitrary"),
        name="moe_combine",
    )(pos_flat, ys, gates_t, x, g, b)


def _moe(x, x_tiles, w_group, b_group, w_expert, b_expert, w_gate_all, w_up_all, w_down_all, l, g, b, upper):
    wr = jnp.concatenate([w_group.T, w_expert.T,
                          jnp.zeros((ROUTER_ROWS - MOE_GROUPS - MOE_EXPERTS, D_MODEL), F32)], axis=0)
    br = jnp.concatenate([b_group, b_expert,
                          jnp.zeros((ROUTER_ROWS - MOE_GROUPS - MOE_EXPERTS,), F32)])[:, None]
    eidx, gates = _router(x, wr.astype(BF16), br.astype(F32))
    rank, cnt = _rank(eidx, upper)
    counts = cnt[:, 0].astype(I32)
    padded = ((counts + TE - 1) // TE) * TE
    ends = jnp.cumsum(padded)
    offsets = ends - padded
    tile_start = jnp.arange(N_EXPERT_TILES, dtype=I32) * TE
    tile_expert = jnp.sum((tile_start[:, None] >= ends[None, :]).astype(I32), axis=1)
    tile_expert = jnp.minimum(tile_expert, MOE_EXPERTS - 1)
    n_active = (ends[-1:] // TE).astype(I32)
    pos = _positions(eidx, rank, offsets[:, None])
    pos_flat = pos.reshape(-1)
    xs = _dispatch(pos_flat, ends.astype(I32), x_tiles)
    ys = _experts(tile_expert, n_active, xs, w_gate_all, w_up_all, w_down_all, l)
    return _combine(pos_flat, ys, gates.T, x, g, b)


def _channel_dft_matrix():
    c = np.arange(FOURIER_DIM)
    ang = 2.0 * np.pi * ((c[:, None] * c[None, :]) % FOURIER_DIM) / FOURIER_DIM
    wc = np.zeros((HALF, D_MODEL), np.float32)
    for h in range(FOURIER_HEADS):
        rows = slice(FOURIER_DIM * h, FOURIER_DIM * (h + 1))
        wc[rows, FOURIER_DIM * h:FOURIER_DIM * (h + 1)] = np.cos(ang)
        wc[rows, HALF + FOURIER_DIM * h:HALF + FOURIER_DIM * (h + 1)] = np.sin(ang)
    return jnp.asarray(wc, BF16)


def _sequence_dft_matrices():
    side = 64
    j = jnp.arange(SEQ, dtype=I32)[:, None]
    k = jnp.arange(side, dtype=I32)[None, :]
    ang_a = ((j * k) % side).astype(F32) * (2.0 * np.pi / side)
    ang_b = ((j * k) % SEQ).astype(F32) * (2.0 * np.pi / SEQ)
    ca, sa = jnp.cos(ang_a)[:, :, None], jnp.sin(ang_a)[:, :, None]
    cb, sb = jnp.cos(ang_b)[:, None, :], jnp.sin(ang_b)[:, None, :]
    cs = (ca * cb - sa * sb).astype(BF16).reshape(SEQ, SEQ)
    ss = (sa * cb + ca * sb).astype(BF16).reshape(SEQ, SEQ)
    return cs, ss


def kernel(x, mem, pf_w_in, pf_pool_w, pf_pool_scale, pf_fourier_ln_g, pf_fourier_w, pf_w_out, hg_w_in, hg_lower_bounds, hg_norm_g, hg_w_out, xa_wq, xa_wkv, xa_wo, moe_w_group, moe_b_group, moe_w_expert, moe_b_expert, moe_w_gate, moe_w_up, moe_w_down, ln_g, ln_b):
    bf = lambda a: a.astype(BF16)
    xt = x.reshape(TOKENS, D_MODEL)
    mem2d = mem.reshape(BATCH * MEM_LEN, D_MODEL)
    lb_all = jnp.cumsum(jax.nn.softmax(hg_lower_bounds.astype(F32), axis=0), axis=0)
    lb_all = lb_all - lb_all[:1]
    tril = jnp.asarray(np.tril(np.ones((HG_CHUNK, HG_CHUNK), np.float32)), BF16)
    triu = jnp.asarray(np.triu(np.ones((HG_CHUNK, HG_CHUNK), np.float32)), BF16)
    upper = jnp.asarray(np.triu(np.ones((TM_ROUTE, TM_ROUTE), np.float32), k=1), BF16)

    for l in range(DEPTH):
        j = l // 2
        lng = lambda s: ln_g[l, s][None, :]
        lnb = lambda s: ln_b[l, s][None, :]
        if l % 2 == 0:
            cs, ss = _sequence_dft_matrices()
            ua, pq = _pf_in(xt, pf_w_in, j, pf_fourier_ln_g[j].reshape(1, HALF), _channel_dft_matrix())
            ya = _pool(ua, bf(pf_pool_w[j]), pf_pool_scale[j][None, :])
            yb = _fourier(cs, ss, pq, bf(pf_fourier_w[j]))
            xt = _pf_out(ya, yb, pf_w_out, j, xt, lng(0), lnb(0))
        else:
            proj = _hg_in(xt, hg_w_in[j])
            lb = lb_all[l].reshape(2, HG_HEADS, 1, HG_DIM)
            og = _gla(proj, lb, hg_norm_g[j][None, :], tril, triu)
            xt = _hg_out(og, hg_w_out, j, xt, lng(0), lnb(0))
        kv = _kv_proj(mem2d, xa_wkv, l)
        xt, xt_tiles = _xattn(xt, xa_wq, kv, xa_wo, l, lng(1), lnb(1))
        xt = _moe(xt, xt_tiles, moe_w_group[l], moe_b_group[l], moe_w_expert[l], moe_b_expert[l],
                  moe_w_gate, moe_w_up, moe_w_down, l, lng(2), lnb(2), upper)
    return xt.reshape(BATCH, SEQ, D_MODEL)
```

```python
import functools

import jax
import jax.numpy as jnp
import numpy as np
from jax import lax
from jax.experimental import pallas as pl
from jax.experimental.pallas import tpu as pltpu

F32 = jnp.float32
BF16 = jnp.bfloat16
I32 = jnp.int32

D_MODEL = 1024
BATCH = 8
SEQ = 4096
DEPTH = 2
TOKENS = BATCH * SEQ
MEM_LEN = 256
HALF = D_MODEL // 2
POOL_WINDOWS = (2, 4, 8, 16)
POOL_DIM = 128
FOURIER_HEADS = 4
FOURIER_DIM = 128
HG_HEADS = 8
HG_DIM = 128
HG_N_PROJ = 5
HG_CHUNK = 64
XA_HEADS = 4
XA_HEAD_DIM = 256
MOE_GROUPS = 4
MOE_PER_GROUP = 8
MOE_EXPERTS = 32
MOE_TOPK = 2
MOE_HIDDEN = 512
DN_ALPHA = (2.0 * DEPTH) ** 0.25
LN_EPS = 1e-5

V7X_LANES = 128
V7X_SUBLANES = 8
V7X_VMEM_LIMIT_BYTES = 52 * 1024 * 1024

TM = 512
TM_ROUTE = 1024
TM_PERM = 1024
TM_COMB = 512
TE = 512
SORTED_ROWS = TOKENS * MOE_TOPK + MOE_EXPERTS * TE
N_EXPERT_TILES = SORTED_ROWS // TE
GLA_GROUP = 16
GLA_ROWS = GLA_GROUP * HG_CHUNK
FOURIER_TR = 512
ROUTER_ROWS = 40

_NT = (((1,), (1,)), ((), ()))


def _params(*sem):
    return pltpu.CompilerParams(dimension_semantics=sem, vmem_limit_bytes=V7X_VMEM_LIMIT_BYTES)


def _dot(a, b):
    return jnp.dot(a, b, preferred_element_type=F32)


def _post_ln(x, h, g, b):
    y = DN_ALPHA * x + h
    mu = jnp.mean(y, axis=-1, keepdims=True)
    yc = y - mu
    var = jnp.mean(yc * yc, axis=-1, keepdims=True)
    return yc * lax.rsqrt(var + LN_EPS) * g + b


def _silu(x):
    return x * jax.nn.sigmoid(x)


def _resident(shape, index):
    return pl.BlockSpec(shape, lambda *_: index, pipeline_mode=pl.Buffered(1))


def _cast_weight_once(w_ref, wbf_ref):
    @pl.when(pl.program_id(0) == 0)
    def _():
        wbf_ref[...] = w_ref[0].astype(BF16)


def _k_pf_in(x_ref, w_ref, lng_ref, wc_ref, ua_ref, pq_ref, wbf_ref):
    _cast_weight_once(w_ref, wbf_ref)
    u = _dot(x_ref[...].astype(BF16), wbf_ref[...])
    ua_ref[...] = u[:, :HALF]
    parts = []
    for h in range(FOURIER_HEADS):
        ub = u[:, HALF + FOURIER_DIM * h:HALF + FOURIER_DIM * (h + 1)]
        mu = jnp.mean(ub, axis=-1, keepdims=True)
        uc = ub - mu
        var = jnp.mean(uc * uc, axis=-1, keepdims=True)
        parts.append(uc * lax.rsqrt(var + LN_EPS))
    un = jnp.concatenate(parts, axis=-1) * lng_ref[...]
    pq_ref[...] = _dot(un.astype(BF16), wc_ref[...]).astype(BF16)


def _pf_in(x, w_in_all, j, ln_g, wc):
    return pl.pallas_call(
        _k_pf_in,
        out_shape=(jax.ShapeDtypeStruct((TOKENS, HALF), F32),
                   jax.ShapeDtypeStruct((TOKENS, D_MODEL), BF16)),
        grid=(TOKENS // TM,),
        in_specs=[pl.BlockSpec((TM, D_MODEL), lambda i: (i, 0)),
                  _resident((1, D_MODEL, D_MODEL), (j, 0, 0)),
                  pl.BlockSpec((1, HALF), lambda i: (0, 0)),
                  _resident((HALF, D_MODEL), (0, 0))],
        out_specs=(pl.BlockSpec((TM, HALF), lambda i: (i, 0)),
                   pl.BlockSpec((TM, D_MODEL), lambda i: (i, 0))),
        scratch_shapes=[pltpu.VMEM((D_MODEL, D_MODEL), BF16)],
        compiler_params=_params("arbitrary"),
        name="pf_in",
    )(x, w_in_all, ln_g, wc)


_POOL_HALO = 8


def _k_pool(u_ref, up_ref, un_ref, pw_ref, ps_ref, ya_ref, ext_ref):
    i = pl.program_id(0)
    tiles_per_seq = SEQ // TM
    j = lax.rem(i, tiles_per_seq)
    ext_ref[_POOL_HALO:_POOL_HALO + TM, :] = u_ref[...]
    ext_ref[0:_POOL_HALO, :] = jnp.where(j == 0, 0.0, up_ref[...])
    ext_ref[_POOL_HALO + TM:2 * _POOL_HALO + TM, :] = jnp.where(j == tiles_per_seq - 1, 0.0, un_ref[...])
    pos = lax.broadcasted_iota(I32, (TM, POOL_DIM), 0) + j * TM
    outs = []
    for g, w in enumerate(POOL_WINDOWS):
        hw = w // 2
        cols = slice(g * POOL_DIM, (g + 1) * POOL_DIM)
        acc = ext_ref[_POOL_HALO - hw:_POOL_HALO - hw + TM, cols]
        for d in range(-hw + 1, hw):
            acc = acc + ext_ref[_POOL_HALO + d:_POOL_HALO + d + TM, cols]
        cnt = jnp.minimum(pos + hw, SEQ) - jnp.maximum(pos - hw, 0)
        pooled = acc / cnt.astype(F32) - u_ref[:, cols]
        outs.append(_dot(pooled.astype(BF16), pw_ref[g]))
    ya_ref[...] = (jnp.concatenate(outs, axis=-1) * ps_ref[...]).astype(BF16)


def _pool(ua, pool_w, pool_scale):
    blocks_per_tile = TM // _POOL_HALO
    last_block = TOKENS // _POOL_HALO - 1
    return pl.pallas_call(
        _k_pool,
        out_shape=jax.ShapeDtypeStruct((TOKENS, HALF), BF16),
        grid=(TOKENS // TM,),
        in_specs=[pl.BlockSpec((TM, HALF), lambda i: (i, 0)),
                  pl.BlockSpec((_POOL_HALO, HALF), lambda i: (jnp.maximum(i * blocks_per_tile - 1, 0), 0)),
                  pl.BlockSpec((_POOL_HALO, HALF),
                               lambda i: (jnp.minimum((i + 1) * blocks_per_tile, last_block), 0)),
                  pl.BlockSpec((4, POOL_DIM, POOL_DIM), lambda i: (0, 0, 0)),
                  pl.BlockSpec((1, HALF), lambda i: (0, 0))],
        out_specs=pl.BlockSpec((TM, HALF), lambda i: (i, 0)),
        scratch_shapes=[pltpu.VMEM((TM + 2 * _POOL_HALO, HALF), F32)],
        compiler_params=_params("parallel"),
        name="pf_pool",
    )(ua, ua, ua, pool_w, pool_scale)


_FOURIER_SCALE = 1.0 / float(np.sqrt(SEQ * FOURIER_DIM))


def _k_fourier(cs_ref, ss_ref, pq_ref, fw_ref, yb_ref):
    y = _dot(cs_ref[...], pq_ref[:, :HALF]) - _dot(ss_ref[...], pq_ref[:, HALF:])
    y = y * _FOURIER_SCALE
    outs = []
    for h in range(FOURIER_HEADS):
        outs.append(_dot(y[:, FOURIER_DIM * h:FOURIER_DIM * (h + 1)].astype(BF16), fw_ref[h]))
    yb_ref[...] = jnp.concatenate(outs, axis=-1).astype(BF16)


def _fourier(cs, ss, pq, fw):
    row_tiles = SEQ // FOURIER_TR
    return pl.pallas_call(
        _k_fourier,
        out_shape=jax.ShapeDtypeStruct((TOKENS, HALF), BF16),
        grid=(row_tiles, BATCH),
        in_specs=[pl.BlockSpec((FOURIER_TR, SEQ), lambda r, b: (r, 0)),
                  pl.BlockSpec((FOURIER_TR, SEQ), lambda r, b: (r, 0)),
                  pl.BlockSpec((SEQ, D_MODEL), lambda r, b: (b, 0)),
                  pl.BlockSpec((4, FOURIER_DIM, FOURIER_DIM), lambda r, b: (0, 0, 0))],
        out_specs=pl.BlockSpec((FOURIER_TR, HALF), lambda r, b: (b * row_tiles + r, 0)),
        compiler_params=_params("parallel", "parallel"),
        name="pf_fourier",
    )(cs, ss, pq, fw)


def _k_pf_out(ya_ref, yb_ref, w_ref, x_ref, g_ref, b_ref, o_ref, wbf_ref):
    _cast_weight_once(w_ref, wbf_ref)
    a = jnp.concatenate([ya_ref[...], yb_ref[...]], axis=-1)
    o_ref[...] = _post_ln(x_ref[...], _dot(a, wbf_ref[...]), g_ref[...], b_ref[...])


def _pf_out(ya, yb, w_out_all, j, x, g, b):
    row = lambda i: (i, 0)
    fixed = lambda i: (0, 0)
    return pl.pallas_call(
        _k_pf_out,
        out_shape=jax.ShapeDtypeStruct((TOKENS, D_MODEL), F32),
        grid=(TOKENS // TM,),
        in_specs=[pl.BlockSpec((TM, HALF), row), pl.BlockSpec((TM, HALF), row),
                  _resident((1, D_MODEL, D_MODEL), (j, 0, 0)), pl.BlockSpec((TM, D_MODEL), row),
                  pl.BlockSpec((1, D_MODEL), fixed), pl.BlockSpec((1, D_MODEL), fixed)],
        out_specs=pl.BlockSpec((TM, D_MODEL), row),
        scratch_shapes=[pltpu.VMEM((D_MODEL, D_MODEL), BF16)],
        compiler_params=_params("arbitrary"),
        name="pf_out",
    )(ya, yb, w_out_all, x, g, b)


_HG_STAGE_COLS = 512


def _k_hg_in(x_ref, w_hbm, o_ref, wbf_ref, stage_ref, sem):
    @pl.when(pl.program_id(0) == 0)
    def _():
        n_chunks = HG_N_PROJ * D_MODEL // _HG_STAGE_COLS

        def chunk_copy(c):
            cols = pl.ds(c * _HG_STAGE_COLS, _HG_STAGE_COLS)
            return pltpu.make_async_copy(w_hbm.at[:, cols], stage_ref.at[c % 2], sem.at[c % 2])

        chunk_copy(0).start()
        for c in range(n_chunks):
            if c + 1 < n_chunks:
                chunk_copy(c + 1).start()
            chunk_copy(c).wait()
            wbf_ref[:, c * _HG_STAGE_COLS:(c + 1) * _HG_STAGE_COLS] = stage_ref[c % 2].astype(BF16)

    xb = x_ref[...].astype(BF16)
    for p in range(HG_N_PROJ):
        acc = _dot(xb, wbf_ref[:, p * D_MODEL:(p + 1) * D_MODEL])
        if p == 0:
            acc = _silu(acc)
        for h in range(HG_HEADS):
            o_ref[p, h] = acc[:, HG_DIM * h:HG_DIM * (h + 1)].astype(BF16)


def _hg_in(x, w_in):
    return pl.pallas_call(
        _k_hg_in,
        out_shape=jax.ShapeDtypeStruct((HG_N_PROJ, HG_HEADS, TOKENS, HG_DIM), BF16),
        grid=(TOKENS // TM,),
        in_specs=[pl.BlockSpec((TM, D_MODEL), lambda i: (i, 0)),
                  pl.BlockSpec(memory_space=pl.ANY)],
        out_specs=pl.BlockSpec((HG_N_PROJ, HG_HEADS, TM, HG_DIM), lambda i: (0, 0, i, 0)),
        scratch_shapes=[pltpu.VMEM((D_MODEL, HG_N_PROJ * D_MODEL), BF16),
                        pltpu.VMEM((2, D_MODEL, _HG_STAGE_COLS), F32),
                        pltpu.SemaphoreType.DMA((2,))],
        compiler_params=_params("arbitrary"),
        name="hg_in",
    )(x, w_in)


def _gla_group(q, v, z, lbv, tri, st, rev):
    L, G = HG_CHUNK, GLA_GROUP
    f = lbv + (1.0 - lbv) * jax.nn.sigmoid(z.astype(F32))
    lf = jnp.log(f).reshape(G, L, HG_DIM)
    k3 = (1.0 - f).reshape(G, L, HG_DIM)
    q3 = q.astype(F32).reshape(G, L, HG_DIM)
    v3 = v.reshape(G, L, HG_DIM)
    hi32 = lax.bitcast_convert_type(
        lax.bitcast_convert_type(lf, jnp.uint32) & jnp.uint32(0xFFFF0000), F32)
    lf_hi = hi32.astype(BF16)
    lf_lo = (lf - hi32).astype(BF16)
    b2 = jnp.einsum('glm,gmk->glk', jnp.broadcast_to(tri, (G, L, L)),
                    jnp.concatenate([lf_hi, lf_lo], axis=-1), preferred_element_type=F32)
    b = b2[..., :HG_DIM] + b2[..., HG_DIM:]
    mid = L // 2 if not rev else L - 1 - L // 2
    end = L - 1 if not rev else 0
    b_ref = b[:, mid:mid + 1, :]
    b_last = b[:, end:end + 1, :]
    e_pos = jnp.exp(b - b_ref)
    qt32 = q3 * e_pos
    kt32 = k3 * (1.0 / e_pos)
    qt = qt32.astype(BF16)
    kt = kt32.astype(BF16)
    sc = jnp.einsum('glk,gmk->glm', qt, kt, preferred_element_type=F32)
    li = lax.broadcasted_iota(I32, (L, L), 0)
    mi = lax.broadcasted_iota(I32, (L, L), 1)
    causal = (li >= mi) if not rev else (li <= mi)
    sc = jnp.where(causal[None], sc, 0.0).astype(BF16)
    o_intra = jnp.einsum('glm,gmv->glv', sc, v3, preferred_element_type=F32)
    ku = (kt32 * jnp.exp(b_last - b_ref)).astype(BF16)
    u_t = jnp.einsum('glv,glk->gvk', v3, ku, preferred_element_type=F32)
    qe = (qt32 * jnp.exp(b_ref)).astype(BF16)
    dec = jnp.exp(b_last)
    outs = [None] * G
    for n in (range(G) if not rev else range(G - 1, -1, -1)):
        o_inter = lax.dot_general(qe[n], st.astype(BF16), _NT, preferred_element_type=F32)
        outs[n] = o_intra[n] + o_inter
        st = dec[n] * st + u_t[n]
    return jnp.concatenate(outs, axis=0), st


def _k_gla(q_ref, v_ref, zf_ref, zb_ref, g_ref, lb_ref, ng_ref, lt_ref, ut_ref, o_ref, of_ref, ob_ref):
    n_groups = SEQ // GLA_ROWS
    zero_state = jnp.zeros((HG_DIM, HG_DIM), F32)

    def group_rows(gi):
        return pl.ds(pl.multiple_of(gi * GLA_ROWS, GLA_ROWS), GLA_ROWS)

    def scan(t, carry):
        st_f, st_b = carry
        rf, rb = group_rows(t), group_rows(n_groups - 1 - t)
        o_f, st_f = _gla_group(q_ref[0, 0, rf, :], v_ref[0, 0, rf, :], zf_ref[0, 0, rf, :],
                               lb_ref[0, 0], lt_ref[...], st_f, rev=False)
        o_b, st_b = _gla_group(q_ref[0, 0, rb, :], v_ref[0, 0, rb, :], zb_ref[0, 0, rb, :],
                               lb_ref[1, 0], ut_ref[...], st_b, rev=True)
        of_ref[rf, :] = o_f
        ob_ref[rb, :] = o_b
        return st_f, st_b

    lax.fori_loop(0, n_groups, scan, (zero_state, zero_state))

    def finish(gi, c):
        rows = group_rows(gi)
        o = of_ref[rows, :] + ob_ref[rows, :]
        o = o * lax.rsqrt(jnp.mean(o * o, axis=-1, keepdims=True) + LN_EPS)
        o = o * ng_ref[...] * _silu(g_ref[0, 0, rows, :].astype(F32))
        o_ref[0, rows, :] = o.astype(BF16)
        return c

    lax.fori_loop(0, n_groups, finish, 0)


def _gla(proj, lb, norm_g, tril, triu):
    def pspec(p):
        return pl.BlockSpec((1, 1, SEQ, HG_DIM), lambda b, h: (p, h, b, 0))
    fixed = lambda b, h: (0, 0)
    return pl.pallas_call(
        _k_gla,
        out_shape=jax.ShapeDtypeStruct((HG_HEADS, TOKENS, HG_DIM), BF16),
        grid=(BATCH, HG_HEADS),
        in_specs=[pspec(0), pspec(1), pspec(2), pspec(3), pspec(4),
                  pl.BlockSpec((2, 1, 1, HG_DIM), lambda b, h: (0, h, 0, 0)),
                  pl.BlockSpec((1, HG_DIM), fixed),
                  pl.BlockSpec((HG_CHUNK, HG_CHUNK), fixed),
                  pl.BlockSpec((HG_CHUNK, HG_CHUNK), fixed)],
        out_specs=pl.BlockSpec((1, SEQ, HG_DIM), lambda b, h: (h, b, 0)),
        scratch_shapes=[pltpu.VMEM((SEQ, HG_DIM), F32), pltpu.VMEM((SEQ, HG_DIM), F32)],
        compiler_params=_params("parallel", "parallel"),
        name="hg_gla",
    )(proj, proj, proj, proj, proj, lb, norm_g, tril, triu)


def _k_hg_out(a_ref, w_ref, x_ref, g_ref, b_ref, o_ref, wbf_ref):
    _cast_weight_once(w_ref, wbf_ref)
    a = jnp.concatenate([a_ref[h] for h in range(HG_HEADS)], axis=-1)
    o_ref[...] = _post_ln(x_ref[...], _dot(a, wbf_ref[...]), g_ref[...], b_ref[...])


def _hg_out(og, w_out_all, j, x, g, b):
    row = lambda i: (i, 0)
    fixed = lambda i: (0, 0)
    return pl.pallas_call(
        _k_hg_out,
        out_shape=jax.ShapeDtypeStruct((TOKENS, D_MODEL), F32),
        grid=(TOKENS // TM,),
        in_specs=[pl.BlockSpec((HG_HEADS, TM, HG_DIM), lambda i: (0, i, 0)),
                  _resident((1, D_MODEL, D_MODEL), (j, 0, 0)), pl.BlockSpec((TM, D_MODEL), row),
                  pl.BlockSpec((1, D_MODEL), fixed), pl.BlockSpec((1, D_MODEL), fixed)],
        out_specs=pl.BlockSpec((TM, D_MODEL), row),
        scratch_shapes=[pltpu.VMEM((D_MODEL, D_MODEL), BF16)],
        compiler_params=_params("arbitrary"),
        name="hg_out",
    )(og, w_out_all, x, g, b)


def _k_mm(x_ref, w_ref, o_ref):
    o_ref[...] = _dot(x_ref[...].astype(BF16), w_ref[0].astype(BF16)).astype(o_ref.dtype)


def _kv_proj(mem2d, wkv_all, l):
    rows, cols = mem2d.shape[0], wkv_all.shape[2]
    return pl.pallas_call(
        _k_mm,
        out_shape=jax.ShapeDtypeStruct((rows, cols), BF16),
        grid=(cols // D_MODEL, rows // TM),
        in_specs=[pl.BlockSpec((TM, D_MODEL), lambda j, i: (i, 0)),
                  pl.BlockSpec((1, D_MODEL, D_MODEL), lambda j, i: (l, 0, j))],
        out_specs=pl.BlockSpec((TM, D_MODEL), lambda j, i: (i, j)),
        compiler_params=_params("parallel", "parallel"),
        name="xa_kv",
    )(mem2d, wkv_all)


ROW_SLABS = D_MODEL // V7X_LANES


def _to_row_tiles(ref, x):
    n = x.shape[0]
    for s in range(ROW_SLABS):
        ref[pl.ds(s, n, stride=ROW_SLABS), :] = x[:, V7X_LANES * s:V7X_LANES * (s + 1)]


def _from_row_tiles(ref):
    n = ref.shape[0] // ROW_SLABS
    return jnp.concatenate([ref[pl.ds(s, n, stride=ROW_SLABS), :] for s in range(ROW_SLABS)], axis=-1)


def _k_xattn(x_ref, wq_ref, kv_ref, wo_ref, g_ref, b_ref, o_ref, o3_ref, wqbf_ref, wobf_ref):
    _cast_weight_once(wq_ref, wqbf_ref)
    _cast_weight_once(wo_ref, wobf_ref)
    x = x_ref[...]
    q = _dot(x.astype(BF16), wqbf_ref[...]).astype(BF16)
    outs = []
    for h in range(XA_HEADS):
        cols = slice(XA_HEAD_DIM * h, XA_HEAD_DIM * (h + 1))
        kh = kv_ref[:, cols]
        vh = kv_ref[:, D_MODEL + XA_HEAD_DIM * h:D_MODEL + XA_HEAD_DIM * (h + 1)]
        s = lax.dot_general(q[:, cols], kh, _NT, preferred_element_type=F32) * (XA_HEAD_DIM ** -0.5)
        e = jnp.exp(s - jnp.max(s, axis=-1, keepdims=True))
        p = e / jnp.sum(e, axis=-1, keepdims=True)
        outs.append(_dot(p.astype(BF16), vh))
    o = jnp.concatenate(outs, axis=-1).astype(BF16)
    y = _post_ln(x, _dot(o, wobf_ref[...]), g_ref[...], b_ref[...])
    o_ref[...] = y
    _to_row_tiles(o3_ref, y)


def _xattn(x, wq_all, kv, wo_all, l, g, b):
    row = lambda i: (i, 0)
    fixed = lambda i: (0, 0)
    tiles_per_seq = SEQ // TM
    return pl.pallas_call(
        _k_xattn,
        out_shape=(jax.ShapeDtypeStruct((TOKENS, D_MODEL), F32),
                   jax.ShapeDtypeStruct((TOKENS * ROW_SLABS, V7X_LANES), F32)),
        grid=(TOKENS // TM,),
        in_specs=[pl.BlockSpec((TM, D_MODEL), row),
                  _resident((1, D_MODEL, D_MODEL), (l, 0, 0)),
                  pl.BlockSpec((MEM_LEN, 2 * D_MODEL), lambda i: (i // tiles_per_seq, 0)),
                  _resident((1, D_MODEL, D_MODEL), (l, 0, 0)),
                  pl.BlockSpec((1, D_MODEL), fixed), pl.BlockSpec((1, D_MODEL), fixed)],
        out_specs=(pl.BlockSpec((TM, D_MODEL), row),
                   pl.BlockSpec((TM * ROW_SLABS, V7X_LANES), row)),
        scratch_shapes=[pltpu.VMEM((D_MODEL, D_MODEL), BF16), pltpu.VMEM((D_MODEL, D_MODEL), BF16)],
        compiler_params=_params("arbitrary"),
        name="xattn",
    )(x, wq_all, kv, wo_all, g, b)


def _first_argmax_rows(v, n_rows):
    m = jnp.max(v, axis=0, keepdims=True)
    rows = lax.broadcasted_iota(I32, v.shape, 0)
    idx = jnp.min(jnp.where(v == m, rows, n_rows), axis=0, keepdims=True)
    return m, idx


def _k_router(x_ref, wr_ref, br_ref, e_ref, g_ref):
    lt = lax.dot_general(wr_ref[...], x_ref[...].astype(BF16), _NT, preferred_element_type=F32)
    lt = lt + br_ref[...]
    gl = lt[0:MOE_GROUPS, :]
    gmax, gi = _first_argmax_rows(gl, MOE_GROUPS)
    p_grp = 1.0 / jnp.sum(jnp.exp(gl - gmax), axis=0, keepdims=True)
    sel = lt[MOE_GROUPS:MOE_GROUPS + MOE_PER_GROUP, :]
    for g in range(1, MOE_GROUPS):
        lo = MOE_GROUPS + MOE_PER_GROUP * g
        sel = jnp.where(gi == g, lt[lo:lo + MOE_PER_GROUP, :], sel)
    m1, i1 = _first_argmax_rows(sel, MOE_PER_GROUP)
    rows = lax.broadcasted_iota(I32, sel.shape, 0)
    m2, i2 = _first_argmax_rows(jnp.where(rows == i1, -jnp.inf, sel), MOE_PER_GROUP)
    e2 = jnp.exp(m2 - m1)
    g1 = p_grp / (1.0 + e2)
    e_ref[0:1, :] = gi * MOE_PER_GROUP + i1
    e_ref[1:2, :] = gi * MOE_PER_GROUP + i2
    g_ref[0:1, :] = g1
    g_ref[1:2, :] = g1 * e2


def _router(x, wr, br):
    return pl.pallas_call(
        _k_router,
        out_shape=(jax.ShapeDtypeStruct((MOE_TOPK, TOKENS), I32),
                   jax.ShapeDtypeStruct((MOE_TOPK, TOKENS), F32)),
        grid=(TOKENS // TM_ROUTE,),
        in_specs=[pl.BlockSpec((TM_ROUTE, D_MODEL), lambda i: (i, 0)),
                  pl.BlockSpec((ROUTER_ROWS, D_MODEL), lambda i: (0, 0)),
                  pl.BlockSpec((ROUTER_ROWS, 1), lambda i: (0, 0))],
        out_specs=(pl.BlockSpec((MOE_TOPK, TM_ROUTE), lambda i: (0, i)),
                   pl.BlockSpec((MOE_TOPK, TM_ROUTE), lambda i: (0, i))),
        compiler_params=_params("parallel"),
        name="moe_router",
    )(x, wr, br)


def _k_rank(e_ref, u_ref, rank_ref, cnt_ref, carry_ref):
    @pl.when(pl.program_id(0) == 0)
    def _():
        carry_ref[...] = jnp.zeros_like(carry_ref)

    rows = lax.broadcasted_iota(I32, (MOE_EXPERTS, TM_ROUTE), 0)
    oh0 = rows == e_ref[0:1, :]
    oh1 = rows == e_ref[1:2, :]
    oh = jnp.where(oh0 | oh1, 1.0, 0.0)
    before = _dot(oh.astype(BF16), u_ref[...]) + carry_ref[:, 0:1]
    rank_ref[0:1, :] = jnp.sum(jnp.where(oh0, before, 0.0), axis=0, keepdims=True).astype(I32)
    rank_ref[1:2, :] = jnp.sum(jnp.where(oh1, before, 0.0), axis=0, keepdims=True).astype(I32)
    carry_ref[...] = carry_ref[...] + jnp.sum(oh, axis=1, keepdims=True)
    cnt_ref[...] = carry_ref[...]


def _rank(eidx, upper):
    return pl.pallas_call(
        _k_rank,
        out_shape=(jax.ShapeDtypeStruct((MOE_TOPK, TOKENS), I32),
                   jax.ShapeDtypeStruct((MOE_EXPERTS, V7X_LANES), F32)),
        grid=(TOKENS // TM_ROUTE,),
        in_specs=[pl.BlockSpec((MOE_TOPK, TM_ROUTE), lambda i: (0, i)),
                  pl.BlockSpec((TM_ROUTE, TM_ROUTE), lambda i: (0, 0))],
        out_specs=(pl.BlockSpec((MOE_TOPK, TM_ROUTE), lambda i: (0, i)),
                   pl.BlockSpec((MOE_EXPERTS, V7X_LANES), lambda i: (0, 0))),
        scratch_shapes=[pltpu.VMEM((MOE_EXPERTS, V7X_LANES), F32)],
        compiler_params=_params("arbitrary"),
        name="moe_rank",
    )(eidx, upper)


def _k_pos(e_ref, rank_ref, off_ref, pos_ref):
    rows = lax.broadcasted_iota(I32, (MOE_EXPERTS, TM_ROUTE), 0)
    for k in range(MOE_TOPK):
        start = jnp.sum(jnp.where(rows == e_ref[k:k + 1, :], off_ref[...], 0), axis=0, keepdims=True)
        pos_ref[k:k + 1, :] = start + rank_ref[k:k + 1, :]


def _positions(eidx, rank, offsets):
    tok = pl.BlockSpec((MOE_TOPK, TM_ROUTE), lambda i: (0, i))
    return pl.pallas_call(
        _k_pos,
        out_shape=jax.ShapeDtypeStruct((MOE_TOPK, TOKENS), I32),
        grid=(TOKENS // TM_ROUTE,),
        in_specs=[tok, tok, pl.BlockSpec((MOE_EXPERTS, 1), lambda i: (0, 0))],
        out_specs=tok,
        compiler_params=_params("parallel"),
        name="moe_pos",
    )(eidx, rank, offsets)


def _row_copy(src_ref, src_row, dst_ref, dst_row, sem):
    src = pl.ds(pl.multiple_of(src_row * ROW_SLABS, ROW_SLABS), ROW_SLABS)
    dst = pl.ds(pl.multiple_of(dst_row * ROW_SLABS, ROW_SLABS), ROW_SLABS)
    return pltpu.make_async_copy(src_ref.at[src], dst_ref.at[dst], sem)


def _k_dispatch(pos_ref, ends_ref, x_ref, xs_ref, zero_ref, sem, zsem):
    base = pl.program_id(0) * TM_PERM

    @pl.when(pl.program_id(0) == 0)
    def _():
        zero_ref[...] = jnp.zeros_like(zero_ref)

        def last_tile(e):
            seg_start = ends_ref[e - 1] if e > 0 else 0
            start = ends_ref[e] - TE
            return ends_ref[e] > seg_start, zero_fill(start)

        def unused_tile(t):
            start = ends_ref[MOE_EXPERTS - 1] + t * TE
            return start < SORTED_ROWS, zero_fill(start)

        def zero_fill(start_row):
            rows = pl.ds(pl.multiple_of(start_row * ROW_SLABS, TE * ROW_SLABS), TE * ROW_SLABS)
            return pltpu.make_async_copy(zero_ref, xs_ref.at[rows], zsem)

        fills = [last_tile(e) for e in range(MOE_EXPERTS)] + [unused_tile(t) for t in range(MOE_EXPERTS)]
        for needed, copy in fills:
            pl.when(needed)(copy.start)
        for needed, copy in fills:
            pl.when(needed)(copy.wait)

    def body(j, c):
        for k in range(MOE_TOPK):
            _row_copy(x_ref, j, xs_ref, pos_ref[k * TOKENS + base + j], sem).start(priority=k)
        return c

    lax.fori_loop(0, TM_PERM, body, 0, unroll=8)
    for k in range(MOE_TOPK):
        pltpu.make_async_copy(x_ref, xs_ref.at[pl.ds(0, TM_PERM * ROW_SLABS)], sem).wait()


def _dispatch(pos_flat, ends, x):
    return pl.pallas_call(
        _k_dispatch,
        out_shape=jax.ShapeDtypeStruct((SORTED_ROWS * ROW_SLABS, V7X_LANES), F32),
        grid_spec=pltpu.PrefetchScalarGridSpec(
            num_scalar_prefetch=2,
            grid=(TOKENS // TM_PERM,),
            in_specs=[pl.BlockSpec((TM_PERM * ROW_SLABS, V7X_LANES), lambda i, pos, ends: (i, 0))],
            out_specs=pl.BlockSpec(memory_space=pl.ANY),
            scratch_shapes=[pltpu.VMEM((TE * ROW_SLABS, V7X_LANES), F32), pltpu.SemaphoreType.DMA,
                            pltpu.SemaphoreType.DMA],
        ),
        compiler_params=_params("arbitrary"),
        name="moe_dispatch",
    )(pos_flat, ends, x)


def _k_experts(te_ref, na_ref, xs_ref, wg_ref, wu_ref, wd_ref, ys_ref, wgubf_ref, wdbf_ref):
    i = pl.program_id(0)
    active = i < na_ref[0]
    new_expert = jnp.logical_or(i == 0, te_ref[jnp.maximum(i - 1, 0)] != te_ref[i])

    @pl.when(jnp.logical_and(active, new_expert))
    def _():
        wgubf_ref[:, :MOE_HIDDEN] = wg_ref[0, 0].astype(BF16)
        wgubf_ref[:, MOE_HIDDEN:] = wu_ref[0, 0].astype(BF16)
        wdbf_ref[...] = wd_ref[0, 0].astype(BF16)

    @pl.when(active)
    def _():
        xb = _from_row_tiles(xs_ref).astype(BF16)
        gu = _dot(xb, wgubf_ref[...])
        h = _silu(gu[:, :MOE_HIDDEN]) * gu[:, MOE_HIDDEN:]
        _to_row_tiles(ys_ref, _dot(h.astype(BF16), wdbf_ref[...]))

    @pl.when(jnp.logical_not(active))
    def _():
        ys_ref[...] = jnp.zeros_like(ys_ref)


def _experts(tile_expert, n_active, xs, wg_all, wu_all, wd_all, l):
    def xs_index(i, te, na):
        return (jnp.minimum(i, na[0] - 1), 0)

    def w_index(i, te, na):
        return (l, te[i], 0, 0)

    return pl.pallas_call(
        _k_experts,
        out_shape=jax.ShapeDtypeStruct((SORTED_ROWS * ROW_SLABS, V7X_LANES), F32),
        grid_spec=pltpu.PrefetchScalarGridSpec(
            num_scalar_prefetch=2,
            grid=(N_EXPERT_TILES,),
            in_specs=[pl.BlockSpec((TE * ROW_SLABS, V7X_LANES), xs_index),
                      pl.BlockSpec((1, 1, D_MODEL, MOE_HIDDEN), w_index),
                      pl.BlockSpec((1, 1, D_MODEL, MOE_HIDDEN), w_index),
                      pl.BlockSpec((1, 1, MOE_HIDDEN, D_MODEL), w_index)],
            out_specs=pl.BlockSpec((TE * ROW_SLABS, V7X_LANES), lambda i, te, na: (i, 0)),
            scratch_shapes=[pltpu.VMEM((D_MODEL, 2 * MOE_HIDDEN), BF16),
                            pltpu.VMEM((MOE_HIDDEN, D_MODEL), BF16)],
        ),
        compiler_params=_params("arbitrary"),
        name="moe_experts",
    )(tile_expert, n_active, xs, wg_all, wu_all, wd_all)


def _k_combine(pos_ref, ys_ref, gt_ref, x_ref, g_ref, b_ref, o_ref, buf_ref, sem):
    base = pl.program_id(0) * TM_COMB

    def body(j, c):
        for k in range(MOE_TOPK):
            _row_copy(ys_ref, pos_ref[k * TOKENS + base + j], buf_ref.at[k], j, sem).start(priority=k)
        return c

    lax.fori_loop(0, TM_COMB, body, 0, unroll=8)
    for k in range(MOE_TOPK):
        pltpu.make_async_copy(ys_ref.at[pl.ds(0, TM_COMB * ROW_SLABS)], buf_ref.at[k], sem).wait()
    h = gt_ref[:, 0:1] * _from_row_tiles(buf_ref.at[0]) + gt_ref[:, 1:2] * _from_row_tiles(buf_ref.at[1])
    o_ref[...] = _post_ln(x_ref[...], h, g_ref[...], b_ref[...])


def _combine(pos_flat, ys, gates_t, x, g, b):
    row = lambda i, pos: (i, 0)
    fixed = lambda i, pos: (0, 0)
    return pl.pallas_call(
        _k_combine,
        out_shape=jax.ShapeDtypeStruct((TOKENS, D_MODEL), F32),
        grid_spec=pltpu.PrefetchScalarGridSpec(
            num_scalar_prefetch=1,
            grid=(TOKENS // TM_COMB,),
            in_specs=[pl.BlockSpec(memory_space=pl.ANY),
                      pl.BlockSpec((TM_COMB, MOE_TOPK), row),
                      pl.BlockSpec((TM_COMB, D_MODEL), row),
                      pl.BlockSpec((1, D_MODEL), fixed), pl.BlockSpec((1, D_MODEL), fixed)],
            out_specs=pl.BlockSpec((TM_COMB, D_MODEL), row),
            scratch_shapes=[pltpu.VMEM((MOE_TOPK, TM_COMB * ROW_SLABS, V7X_LANES), F32),
                            pltpu.SemaphoreType.DMA],
        ),
        compiler_params=_params("arbitrary"),
        name="moe_combine",
    )(pos_flat, ys, gates_t, x, g, b)


def _moe(x, x_tiles, w_group, b_group, w_expert, b_expert, w_gate_all, w_up_all, w_down_all, l, g, b, upper):
    wr = jnp.concatenate([w_group.T, w_expert.T,
                          jnp.zeros((ROUTER_ROWS - MOE_GROUPS - MOE_EXPERTS, D_MODEL), F32)], axis=0)
    br = jnp.concatenate([b_group, b_expert,
                          jnp.zeros((ROUTER_ROWS - MOE_GROUPS - MOE_EXPERTS,), F32)])[:, None]
    eidx, gates = _router(x, wr.astype(BF16), br.astype(F32))
    rank, cnt = _rank(eidx, upper)
    counts = cnt[:, 0].astype(I32)
    padded = ((counts + TE - 1) // TE) * TE
    ends = jnp.cumsum(padded)
    offsets = ends - padded
    tile_start = jnp.arange(N_EXPERT_TILES, dtype=I32) * TE
    tile_expert = jnp.sum((tile_start[:, None] >= ends[None, :]).astype(I32), axis=1)
    tile_expert = jnp.minimum(tile_expert, MOE_EXPERTS - 1)
    n_active = (ends[-1:] // TE).astype(I32)
    pos = _positions(eidx, rank, offsets[:, None])
    pos_flat = pos.reshape(-1)
    xs = _dispatch(pos_flat, ends.astype(I32), x_tiles)
    ys = _experts(tile_expert, n_active, xs, w_gate_all, w_up_all, w_down_all, l)
    return _combine(pos_flat, ys, gates.T, x, g, b)


def _channel_dft_matrix():
    c = np.arange(FOURIER_DIM)
    ang = 2.0 * np.pi * ((c[:, None] * c[None, :]) % FOURIER_DIM) / FOURIER_DIM
    wc = np.zeros((HALF, D_MODEL), np.float32)
    for h in range(FOURIER_HEADS):
        rows = slice(FOURIER_DIM * h, FOURIER_DIM * (h + 1))
        wc[rows, FOURIER_DIM * h:FOURIER_DIM * (h + 1)] = np.cos(ang)
        wc[rows, HALF + FOURIER_DIM * h:HALF + FOURIER_DIM * (h + 1)] = np.sin(ang)
    return jnp.asarray(wc, BF16)


def _sequence_dft_matrices():
    side = 64
    j = jnp.arange(SEQ, dtype=I32)[:, None]
    k = jnp.arange(side, dtype=I32)[None, :]
    ang_a = ((j * k) % side).astype(F32) * (2.0 * np.pi / side)
    ang_b = ((j * k) % SEQ).astype(F32) * (2.0 * np.pi / SEQ)
    ca, sa = jnp.cos(ang_a)[:, :, None], jnp.sin(ang_a)[:, :, None]
    cb, sb = jnp.cos(ang_b)[:, None, :], jnp.sin(ang_b)[:, None, :]
    cs = (ca * cb - sa * sb).astype(BF16).reshape(SEQ, SEQ)
    ss = (sa * cb + ca * sb).astype(BF16).reshape(SEQ, SEQ)
    return cs, ss


def kernel(x, mem, pf_w_in, pf_pool_w, pf_pool_scale, pf_fourier_ln_g, pf_fourier_w, pf_w_out, hg_w_in, hg_lower_bounds, hg_norm_g, hg_w_out, xa_wq, xa_wkv, xa_wo, moe_w_group, moe_b_group, moe_w_expert, moe_b_expert, moe_w_gate, moe_w_up, moe_w_down, ln_g, ln_b):
    bf = lambda a: a.astype(BF16)
    xt = x.reshape(TOKENS, D_MODEL)
    mem2d = mem.reshape(BATCH * MEM_LEN, D_MODEL)
    lb_all = jnp.cumsum(jax.nn.softmax(hg_lower_bounds.astype(F32), axis=0), axis=0)
    lb_all = lb_all - lb_all[:1]
    tril = jnp.asarray(np.tril(np.ones((HG_CHUNK, HG_CHUNK), np.float32)), BF16)
    triu = jnp.asarray(np.triu(np.ones((HG_CHUNK, HG_CHUNK), np.float32)), BF16)
    upper = jnp.asarray(np.triu(np.ones((TM_ROUTE, TM_ROUTE), np.float32), k=1), BF16)

    for l in range(DEPTH):
        j = l // 2
        lng = lambda s: ln_g[l, s][None, :]
        lnb = lambda s: ln_b[l, s][None, :]
        if l % 2 == 0:
            cs, ss = _sequence_dft_matrices()
            ua, pq = _pf_in(xt, pf_w_in, j, pf_fourier_ln_g[j].reshape(1, HALF), _channel_dft_matrix())
            ya = _pool(ua, bf(pf_pool_w[j]), pf_pool_scale[j][None, :])
            yb = _fourier(cs, ss, pq, bf(pf_fourier_w[j]))
            xt = _pf_out(ya, yb, pf_w_out, j, xt, lng(0), lnb(0))
        else:
            proj = _hg_in(xt, hg_w_in[j])
            lb = lb_all[l].reshape(2, HG_HEADS, 1, HG_DIM)
            og = _gla(proj, lb, hg_norm_g[j][None, :], tril, triu)
            xt = _hg_out(og, hg_w_out, j, xt, lng(0), lnb(0))
        kv = _kv_proj(mem2d, xa_wkv, l)
        xt, xt_tiles = _xattn(xt, xa_wq, kv, xa_wo, l, lng(1), lnb(1))
        xt = _moe(xt, xt_tiles, moe_w_group[l], moe_b_group[l], moe_w_expert[l], moe_b_expert[l],
                  moe_w_gate, moe_w_up, moe_w_down, l, lng(2), lnb(2), upper)
    return xt.reshape(BATCH, SEQ, D_MODEL)
```

```python
import functools

import jax
import jax.numpy as jnp
import numpy as np
from jax import lax
from jax.experimental import pallas as pl
from jax.experimental.pallas import tpu as pltpu

F32 = jnp.float32
BF16 = jnp.bfloat16
I32 = jnp.int32

D_MODEL = 1024
BATCH = 8
SEQ = 4096
DEPTH = 2
TOKENS = BATCH * SEQ
MEM_LEN = 256
HALF = D_MODEL // 2
POOL_WINDOWS = (2, 4, 8, 16)
POOL_DIM = 128
FOURIER_HEADS = 4
FOURIER_DIM = 128
HG_HEADS = 8
HG_DIM = 128
HG_N_PROJ = 5
HG_CHUNK = 64
XA_HEADS = 4
XA_HEAD_DIM = 256
MOE_GROUPS = 4
MOE_PER_GROUP = 8
MOE_EXPERTS = 32
MOE_TOPK = 2
MOE_HIDDEN = 512
DN_ALPHA = (2.0 * DEPTH) ** 0.25
LN_EPS = 1e-5

V7X_LANES = 128
V7X_SUBLANES = 8
V7X_VMEM_LIMIT_BYTES = 52 * 1024 * 1024

TM = 512
TM_ROUTE = 1024
TM_PERM = 1024
TM_COMB = 512
TE = 512
SORTED_ROWS = TOKENS * MOE_TOPK + MOE_EXPERTS * TE
N_EXPERT_TILES = SORTED_ROWS // TE
GLA_GROUP = 16
GLA_ROWS = GLA_GROUP * HG_CHUNK
FOURIER_TR = 512
ROUTER_ROWS = 40

_NT = (((1,), (1,)), ((), ()))


def _params(*sem):
    return pltpu.CompilerParams(dimension_semantics=sem, vmem_limit_bytes=V7X_VMEM_LIMIT_BYTES)


def _dot(a, b):
    return jnp.dot(a, b, preferred_element_type=F32)


def _post_ln(x, h, g, b):
    y = DN_ALPHA * x + h
    mu = jnp.mean(y, axis=-1, keepdims=True)
    yc = y - mu
    var = jnp.mean(yc * yc, axis=-1, keepdims=True)
    return yc * lax.rsqrt(var + LN_EPS) * g + b


def _silu(x):
    return x * jax.nn.sigmoid(x)


def _resident(shape, index):
    return pl.BlockSpec(shape, lambda *_: index, pipeline_mode=pl.Buffered(1))


def _cast_weight_once(w_ref, wbf_ref):
    @pl.when(pl.program_id(0) == 0)
    def _():
        wbf_ref[...] = w_ref[0].astype(BF16)


def _k_pf_in(x_ref, w_ref, lng_ref, wc_ref, ua_ref, pq_ref, wbf_ref):
    _cast_weight_once(w_ref, wbf_ref)
    u = _dot(x_ref[...].astype(BF16), wbf_ref[...])
    ua_ref[...] = u[:, :HALF]
    parts = []
    for h in range(FOURIER_HEADS):
        ub = u[:, HALF + FOURIER_DIM * h:HALF + FOURIER_DIM * (h + 1)]
        mu = jnp.mean(ub, axis=-1, keepdims=True)
        uc = ub - mu
        var = jnp.mean(uc * uc, axis=-1, keepdims=True)
        parts.append(uc * lax.rsqrt(var + LN_EPS))
    un = jnp.concatenate(parts, axis=-1) * lng_ref[...]
    pq_ref[...] = _dot(un.astype(BF16), wc_ref[...]).astype(BF16)


def _pf_in(x, w_in_all, j, ln_g, wc):
    return pl.pallas_call(
        _k_pf_in,
        out_shape=(jax.ShapeDtypeStruct((TOKENS, HALF), F32),
                   jax.ShapeDtypeStruct((TOKENS, D_MODEL), BF16)),
        grid=(TOKENS // TM,),
        in_specs=[pl.BlockSpec((TM, D_MODEL), lambda i: (i, 0)),
                  _resident((1, D_MODEL, D_MODEL), (j, 0, 0)),
                  pl.BlockSpec((1, HALF), lambda i: (0, 0)),
                  _resident((HALF, D_MODEL), (0, 0))],
        out_specs=(pl.BlockSpec((TM, HALF), lambda i: (i, 0)),
                   pl.BlockSpec((TM, D_MODEL), lambda i: (i, 0))),
        scratch_shapes=[pltpu.VMEM((D_MODEL, D_MODEL), BF16)],
        compiler_params=_params("arbitrary"),
        name="pf_in",
    )(x, w_in_all, ln_g, wc)


_POOL_HALO = 8


def _k_pool(u_ref, up_ref, un_ref, pw_ref, ps_ref, ya_ref, ext_ref):
    i = pl.program_id(0)
    tiles_per_seq = SEQ // TM
    j = lax.rem(i, tiles_per_seq)
    ext_ref[_POOL_HALO:_POOL_HALO + TM, :] = u_ref[...]
    ext_ref[0:_POOL_HALO, :] = jnp.where(j == 0, 0.0, up_ref[...])
    ext_ref[_POOL_HALO + TM:2 * _POOL_HALO + TM, :] = jnp.where(j == tiles_per_seq - 1, 0.0, un_ref[...])
    pos = lax.broadcasted_iota(I32, (TM, POOL_DIM), 0) + j * TM
    outs = []
    for g, w in enumerate(POOL_WINDOWS):
        hw = w // 2
        cols = slice(g * POOL_DIM, (g + 1) * POOL_DIM)
        acc = ext_ref[_POOL_HALO - hw:_POOL_HALO - hw + TM, cols]
        for d in range(-hw + 1, hw):
            acc = acc + ext_ref[_POOL_HALO + d:_POOL_HALO + d + TM, cols]
        cnt = jnp.minimum(pos + hw, SEQ) - jnp.maximum(pos - hw, 0)
        pooled = acc / cnt.astype(F32) - u_ref[:, cols]
        outs.append(_dot(pooled.astype(BF16), pw_ref[g]))
    ya_ref[...] = (jnp.concatenate(outs, axis=-1) * ps_ref[...]).astype(BF16)


def _pool(ua, pool_w, pool_scale):
    blocks_per_tile = TM // _POOL_HALO
    last_block = TOKENS // _POOL_HALO - 1
    return pl.pallas_call(
        _k_pool,
        out_shape=jax.ShapeDtypeStruct((TOKENS, HALF), BF16),
        grid=(TOKENS // TM,),
        in_specs=[pl.BlockSpec((TM, HALF), lambda i: (i, 0)),
                  pl.BlockSpec((_POOL_HALO, HALF), lambda i: (jnp.maximum(i * blocks_per_tile - 1, 0), 0)),
                  pl.BlockSpec((_POOL_HALO, HALF),
                               lambda i: (jnp.minimum((i + 1) * blocks_per_tile, last_block), 0)),
                  pl.BlockSpec((4, POOL_DIM, POOL_DIM), lambda i: (0, 0, 0)),
                  pl.BlockSpec((1, HALF), lambda i: (0, 0))],
        out_specs=pl.BlockSpec((TM, HALF), lambda i: (i, 0)),
        scratch_shapes=[pltpu.VMEM((TM + 2 * _POOL_HALO, HALF), F32)],
        compiler_params=_params("parallel"),
        name="pf_pool",
    )(ua, ua, ua, pool_w, pool_scale)


_FOURIER_SCALE = 1.0 / float(np.sqrt(SEQ * FOURIER_DIM))


def _k_fourier(cs_ref, ss_ref, pq_ref, fw_ref, yb_ref):
    y = _dot(cs_ref[...], pq_ref[:, :HALF]) - _dot(ss_ref[...], pq_ref[:, HALF:])
    y = y * _FOURIER_SCALE
    outs = []
    for h in range(FOURIER_HEADS):
        outs.append(_dot(y[:, FOURIER_DIM * h:FOURIER_DIM * (h + 1)].astype(BF16), fw_ref[h]))
    yb_ref[...] = jnp.concatenate(outs, axis=-1).astype(BF16)


def _fourier(cs, ss, pq, fw):
    row_tiles = SEQ // FOURIER_TR
    return pl.pallas_call(
        _k_fourier,
        out_shape=jax.ShapeDtypeStruct((TOKENS, HALF), BF16),
        grid=(row_tiles, BATCH),
        in_specs=[pl.BlockSpec((FOURIER_TR, SEQ), lambda r, b: (r, 0)),
                  pl.BlockSpec((FOURIER_TR, SEQ), lambda r, b: (r, 0)),
                  pl.BlockSpec((SEQ, D_MODEL), lambda r, b: (b, 0)),
                  pl.BlockSpec((4, FOURIER_DIM, FOURIER_DIM), lambda r, b: (0, 0, 0))],
        out_specs=pl.BlockSpec((FOURIER_TR, HALF), lambda r, b: (b * row_tiles + r, 0)),
        compiler_params=_params("parallel", "parallel"),
        name="pf_fourier",
    )(cs, ss, pq, fw)


def _k_pf_out(ya_ref, yb_ref, w_ref, x_ref, g_ref, b_ref, o_ref, wbf_ref):
    _cast_weight_once(w_ref, wbf_ref)
    a = jnp.concatenate([ya_ref[...], yb_ref[...]], axis=-1)
    o_ref[...] = _post_ln(x_ref[...], _dot(a, wbf_ref[...]), g_ref[...], b_ref[...])


def _pf_out(ya, yb, w_out_all, j, x, g, b):
    row = lambda i: (i, 0)
    fixed = lambda i: (0, 0)
    return pl.pallas_call(
        _k_pf_out,
        out_shape=jax.ShapeDtypeStruct((TOKENS, D_MODEL), F32),
        grid=(TOKENS // TM,),
        in_specs=[pl.BlockSpec((TM, HALF), row), pl.BlockSpec((TM, HALF), row),
                  _resident((1, D_MODEL, D_MODEL), (j, 0, 0)), pl.BlockSpec((TM, D_MODEL), row),
                  pl.BlockSpec((1, D_MODEL), fixed), pl.BlockSpec((1, D_MODEL), fixed)],
        out_specs=pl.BlockSpec((TM, D_MODEL), row),
        scratch_shapes=[pltpu.VMEM((D_MODEL, D_MODEL), BF16)],
        compiler_params=_params("arbitrary"),
        name="pf_out",
    )(ya, yb, w_out_all, x, g, b)


_HG_STAGE_COLS = 512


def _k_hg_in(x_ref, w_hbm, o_ref, wbf_ref, stage_ref, sem):
    @pl.when(pl.program_id(0) == 0)
    def _():
        n_chunks = HG_N_PROJ * D_MODEL // _HG_STAGE_COLS

        def chunk_copy(c):
            cols = pl.ds(c * _HG_STAGE_COLS, _HG_STAGE_COLS)
            return pltpu.make_async_copy(w_hbm.at[:, cols], stage_ref.at[c % 2], sem.at[c % 2])

        chunk_copy(0).start()
        for c in range(n_chunks):
            if c + 1 < n_chunks:
                chunk_copy(c + 1).start()
            chunk_copy(c).wait()
            wbf_ref[:, c * _HG_STAGE_COLS:(c + 1) * _HG_STAGE_COLS] = stage_ref[c % 2].astype(BF16)

    xb = x_ref[...].astype(BF16)
    for p in range(HG_N_PROJ):
        acc = _dot(xb, wbf_ref[:, p * D_MODEL:(p + 1) * D_MODEL])
        if p == 0:
            acc = _silu(acc)
        for h in range(HG_HEADS):
            o_ref[p, h] = acc[:, HG_DIM * h:HG_DIM * (h + 1)].astype(BF16)


def _hg_in(x, w_in):
    return pl.pallas_call(
        _k_hg_in,
        out_shape=jax.ShapeDtypeStruct((HG_N_PROJ, HG_HEADS, TOKENS, HG_DIM), BF16),
        grid=(TOKENS // TM,),
        in_specs=[pl.BlockSpec((TM, D_MODEL), lambda i: (i, 0)),
                  pl.BlockSpec(memory_space=pl.ANY)],
        out_specs=pl.BlockSpec((HG_N_PROJ, HG_HEADS, TM, HG_DIM), lambda i: (0, 0, i, 0)),
        scratch_shapes=[pltpu.VMEM((D_MODEL, HG_N_PROJ * D_MODEL), BF16),
                        pltpu.VMEM((2, D_MODEL, _HG_STAGE_COLS), F32),
                        pltpu.SemaphoreType.DMA((2,))],
        compiler_params=_params("arbitrary"),
        name="hg_in",
    )(x, w_in)


def _gla_group(q, v, z, lbv, tri, st, rev):
    L, G = HG_CHUNK, GLA_GROUP
    f = lbv + (1.0 - lbv) * jax.nn.sigmoid(z.astype(F32))
    lf = jnp.log(f).reshape(G, L, HG_DIM)
    k3 = (1.0 - f).reshape(G, L, HG_DIM)
    q3 = q.astype(F32).reshape(G, L, HG_DIM)
    v3 = v.reshape(G, L, HG_DIM)
    hi32 = lax.bitcast_convert_type(
        lax.bitcast_convert_type(lf, jnp.uint32) & jnp.uint32(0xFFFF0000), F32)
    lf_hi = hi32.astype(BF16)
    lf_lo = (lf - hi32).astype(BF16)
    b2 = jnp.einsum('glm,gmk->glk', jnp.broadcast_to(tri, (G, L, L)),
                    jnp.concatenate([lf_hi, lf_lo], axis=-1), preferred_element_type=F32)
    b = b2[..., :HG_DIM] + b2[..., HG_DIM:]
    mid = L // 2 if not rev else L - 1 - L // 2
    end = L - 1 if not rev else 0
    b_ref = b[:, mid:mid + 1, :]
    b_last = b[:, end:end + 1, :]
    e_pos = jnp.exp(b - b_ref)
    qt32 = q3 * e_pos
    kt32 = k3 * (1.0 / e_pos)
    qt = qt32.astype(BF16)
    kt = kt32.astype(BF16)
    sc = jnp.einsum('glk,gmk->glm', qt, kt, preferred_element_type=F32)
    li = lax.broadcasted_iota(I32, (L, L), 0)
    mi = lax.broadcasted_iota(I32, (L, L), 1)
    causal = (li >= mi) if not rev else (li <= mi)
    sc = jnp.where(causal[None], sc, 0.0).astype(BF16)
    o_intra = jnp.einsum('glm,gmv->glv', sc, v3, preferred_element_type=F32)
    ku = (kt32 * jnp.exp(b_last - b_ref)).astype(BF16)
    u_t = jnp.einsum('glv,glk->gvk', v3, ku, preferred_element_type=F32)
    qe = (qt32 * jnp.exp(b_ref)).astype(BF16)
    dec = jnp.exp(b_last)
    outs = [None] * G
    for n in (range(G) if not rev else range(G - 1, -1, -1)):
        o_inter = lax.dot_general(qe[n], st.astype(BF16), _NT, preferred_element_type=F32)
        outs[n] = o_intra[n] + o_inter
        st = dec[n] * st + u_t[n]
    return jnp.concatenate(outs, axis=0), st


def _k_gla(q_ref, v_ref, zf_ref, zb_ref, g_ref, lb_ref, ng_ref, lt_ref, ut_ref, o_ref, of_ref, ob_ref):
    n_groups = SEQ // GLA_ROWS
    zero_state = jnp.zeros((HG_DIM, HG_DIM), F32)

    def group_rows(gi):
        return pl.ds(pl.multiple_of(gi * GLA_ROWS, GLA_ROWS), GLA_ROWS)

    def scan(t, carry):
        st_f, st_b = carry
        rf, rb = group_rows(t), group_rows(n_groups - 1 - t)
        o_f, st_f = _gla_group(q_ref[0, 0, rf, :], v_ref[0, 0, rf, :], zf_ref[0, 0, rf, :],
                               lb_ref[0, 0], lt_ref[...], st_f, rev=False)
        o_b, st_b = _gla_group(q_ref[0, 0, rb, :], v_ref[0, 0, rb, :], zb_ref[0, 0, rb, :],
                               lb_ref[1, 0], ut_ref[...], st_b, rev=True)
        of_ref[rf, :] = o_f
        ob_ref[rb, :] = o_b
        return st_f, st_b

    lax.fori_loop(0, n_groups, scan, (zero_state, zero_state))

    def finish(gi, c):
        rows = group_rows(gi)
        o = of_ref[rows, :] + ob_ref[rows, :]
        o = o * lax.rsqrt(jnp.mean(o * o, axis=-1, keepdims=True) + LN_EPS)
        o = o * ng_ref[...] * _silu(g_ref[0, 0, rows, :].astype(F32))
        o_ref[0, rows, :] = o.astype(BF16)
        return c

    lax.fori_loop(0, n_groups, finish, 0)


def _gla(proj, lb, norm_g, tril, triu):
    def pspec(p):
        return pl.BlockSpec((1, 1, SEQ, HG_DIM), lambda b, h: (p, h, b, 0))
    fixed = lambda b, h: (0, 0)
    return pl.pallas_call(
        _k_gla,
        out_shape=jax.ShapeDtypeStruct((HG_HEADS, TOKENS, HG_DIM), BF16),
        grid=(BATCH, HG_HEADS),
        in_specs=[pspec(0), pspec(1), pspec(2), pspec(3), pspec(4),
                  pl.BlockSpec((2, 1, 1, HG_DIM), lambda b, h: (0, h, 0, 0)),
                  pl.BlockSpec((1, HG_DIM), fixed),
                  pl.BlockSpec((HG_CHUNK, HG_CHUNK), fixed),
                  pl.BlockSpec((HG_CHUNK, HG_CHUNK), fixed)],
        out_specs=pl.BlockSpec((1, SEQ, HG_DIM), lambda b, h: (h, b, 0)),
        scratch_shapes=[pltpu.VMEM((SEQ, HG_DIM), F32), pltpu.VMEM((SEQ, HG_DIM), F32)],
        compiler_params=_params("parallel", "parallel"),
        name="hg_gla",
    )(proj, proj, proj, proj, proj, lb, norm_g, tril, triu)


def _k_hg_out(a_ref, w_ref, x_ref, g_ref, b_ref, o_ref, wbf_ref):
    _cast_weight_once(w_ref, wbf_ref)
    a = jnp.concatenate([a_ref[h] for h in range(HG_HEADS)], axis=-1)
    o_ref[...] = _post_ln(x_ref[...], _dot(a, wbf_ref[...]), g_ref[...], b_ref[...])


def _hg_out(og, w_out_all, j, x, g, b):
    row = lambda i: (i, 0)
    fixed = lambda i: (0, 0)
    return pl.pallas_call(
        _k_hg_out,
        out_shape=jax.ShapeDtypeStruct((TOKENS, D_MODEL), F32),
        grid=(TOKENS // TM,),
        in_specs=[pl.BlockSpec((HG_HEADS, TM, HG_DIM), lambda i: (0, i, 0)),
                  _resident((1, D_MODEL, D_MODEL), (j, 0, 0)), pl.BlockSpec((TM, D_MODEL), row),
                  pl.BlockSpec((1, D_MODEL), fixed), pl.BlockSpec((1, D_MODEL), fixed)],
        out_specs=pl.BlockSpec((TM, D_MODEL), row),
        scratch_shapes=[pltpu.VMEM((D_MODEL, D_MODEL), BF16)],
        compiler_params=_params("arbitrary"),
        name="hg_out",
    )(og, w_out_all, x, g, b)


def _k_mm(x_ref, w_ref, o_ref):
    o_ref[...] = _dot(x_ref[...].astype(BF16), w_ref[0].astype(BF16)).astype(o_ref.dtype)


def _kv_proj(mem2d, wkv_all, l):
    rows, cols = mem2d.shape[0], wkv_all.shape[2]
    return pl.pallas_call(
        _k_mm,
        out_shape=jax.ShapeDtypeStruct((rows, cols), BF16),
        grid=(cols // D_MODEL, rows // TM),
        in_specs=[pl.BlockSpec((TM, D_MODEL), lambda j, i: (i, 0)),
                  pl.BlockSpec((1, D_MODEL, D_MODEL), lambda j, i: (l, 0, j))],
        out_specs=pl.BlockSpec((TM, D_MODEL), lambda j, i: (i, j)),
        compiler_params=_params("parallel", "parallel"),
        name="xa_kv",
    )(mem2d, wkv_all)


ROW_WORDS = D_MODEL // 2
ROW_SLABS = ROW_WORDS // V7X_LANES
U32 = jnp.uint32
_HI_MASK = 0xFFFF0000


def _to_row_tiles(ref, x):
    n = x.shape[0]
    as_bits = lambda v: lax.bitcast_convert_type(v.astype(BF16).astype(F32), U32)
    words = (as_bits(x[:, :ROW_WORDS]) >> 16) | (as_bits(x[:, ROW_WORDS:]) & U32(_HI_MASK))
    for s in range(ROW_SLABS):
        ref[pl.ds(s, n, stride=ROW_SLABS), :] = words[:, V7X_LANES * s:V7X_LANES * (s + 1)]


def _from_row_tiles(ref):
    n = ref.shape[0] // ROW_SLABS
    words = jnp.concatenate([ref[pl.ds(s, n, stride=ROW_SLABS), :] for s in range(ROW_SLABS)], axis=-1)
    lo = lax.bitcast_convert_type(words << 16, F32)
    hi = lax.bitcast_convert_type(words & U32(_HI_MASK), F32)
    return jnp.concatenate([lo, hi], axis=-1)


def _k_xattn(x_ref, wq_ref, kv_ref, wo_ref, g_ref, b_ref, o_ref, o3_ref, wqbf_ref, wobf_ref):
    _cast_weight_once(wq_ref, wqbf_ref)
    _cast_weight_once(wo_ref, wobf_ref)
    x = x_ref[...]
    q = _dot(x.astype(BF16), wqbf_ref[...]).astype(BF16)
    outs = []
    for h in range(XA_HEADS):
        cols = slice(XA_HEAD_DIM * h, XA_HEAD_DIM * (h + 1))
        kh = kv_ref[:, cols]
        vh = kv_ref[:, D_MODEL + XA_HEAD_DIM * h:D_MODEL + XA_HEAD_DIM * (h + 1)]
        s = lax.dot_general(q[:, cols], kh, _NT, preferred_element_type=F32) * (XA_HEAD_DIM ** -0.5)
        e = jnp.exp(s - jnp.max(s, axis=-1, keepdims=True))
        p = e / jnp.sum(e, axis=-1, keepdims=True)
        outs.append(_dot(p.astype(BF16), vh))
    o = jnp.concatenate(outs, axis=-1).astype(BF16)
    y = _post_ln(x, _dot(o, wobf_ref[...]), g_ref[...], b_ref[...])
    o_ref[...] = y
    _to_row_tiles(o3_ref, y)


def _xattn(x, wq_all, kv, wo_all, l, g, b):
    row = lambda i: (i, 0)
    fixed = lambda i: (0, 0)
    tiles_per_seq = SEQ // TM
    return pl.pallas_call(
        _k_xattn,
        out_shape=(jax.ShapeDtypeStruct((TOKENS, D_MODEL), F32),
                   jax.ShapeDtypeStruct((TOKENS * ROW_SLABS, V7X_LANES), U32)),
        grid=(TOKENS // TM,),
        in_specs=[pl.BlockSpec((TM, D_MODEL), row),
                  _resident((1, D_MODEL, D_MODEL), (l, 0, 0)),
                  pl.BlockSpec((MEM_LEN, 2 * D_MODEL), lambda i: (i // tiles_per_seq, 0)),
                  _resident((1, D_MODEL, D_MODEL), (l, 0, 0)),
                  pl.BlockSpec((1, D_MODEL), fixed), pl.BlockSpec((1, D_MODEL), fixed)],
        out_specs=(pl.BlockSpec((TM, D_MODEL), row),
                   pl.BlockSpec((TM * ROW_SLABS, V7X_LANES), row)),
        scratch_shapes=[pltpu.VMEM((D_MODEL, D_MODEL), BF16), pltpu.VMEM((D_MODEL, D_MODEL), BF16)],
        compiler_params=_params("arbitrary"),
        name="xattn",
    )(x, wq_all, kv, wo_all, g, b)


def _first_argmax_rows(v, n_rows):
    m = jnp.max(v, axis=0, keepdims=True)
    rows = lax.broadcasted_iota(I32, v.shape, 0)
    idx = jnp.min(jnp.where(v == m, rows, n_rows), axis=0, keepdims=True)
    return m, idx


def _k_router(x_ref, wr_ref, br_ref, e_ref, g_ref):
    lt = lax.dot_general(wr_ref[...], x_ref[...].astype(BF16), _NT, preferred_element_type=F32)
    lt = lt + br_ref[...]
    gl = lt[0:MOE_GROUPS, :]
    gmax, gi = _first_argmax_rows(gl, MOE_GROUPS)
    p_grp = 1.0 / jnp.sum(jnp.exp(gl - gmax), axis=0, keepdims=True)
    sel = lt[MOE_GROUPS:MOE_GROUPS + MOE_PER_GROUP, :]
    for g in range(1, MOE_GROUPS):
        lo = MOE_GROUPS + MOE_PER_GROUP * g
        sel = jnp.where(gi == g, lt[lo:lo + MOE_PER_GROUP, :], sel)
    m1, i1 = _first_argmax_rows(sel, MOE_PER_GROUP)
    rows = lax.broadcasted_iota(I32, sel.shape, 0)
    m2, i2 = _first_argmax_rows(jnp.where(rows == i1, -jnp.inf, sel), MOE_PER_GROUP)
    e2 = jnp.exp(m2 - m1)
    g1 = p_grp / (1.0 + e2)
    e_ref[0:1, :] = gi * MOE_PER_GROUP + i1
    e_ref[1:2, :] = gi * MOE_PER_GROUP + i2
    g_ref[0:1, :] = g1
    g_ref[1:2, :] = g1 * e2


def _router(x, wr, br):
    return pl.pallas_call(
        _k_router,
        out_shape=(jax.ShapeDtypeStruct((MOE_TOPK, TOKENS), I32),
                   jax.ShapeDtypeStruct((MOE_TOPK, TOKENS), F32)),
        grid=(TOKENS // TM_ROUTE,),
        in_specs=[pl.BlockSpec((TM_ROUTE, D_MODEL), lambda i: (i, 0)),
                  pl.BlockSpec((ROUTER_ROWS, D_MODEL), lambda i: (0, 0)),
                  pl.BlockSpec((ROUTER_ROWS, 1), lambda i: (0, 0))],
        out_specs=(pl.BlockSpec((MOE_TOPK, TM_ROUTE), lambda i: (0, i)),
                   pl.BlockSpec((MOE_TOPK, TM_ROUTE), lambda i: (0, i))),
        compiler_params=_params("parallel"),
        name="moe_router",
    )(x, wr, br)


def _k_rank(e_ref, u_ref, rank_ref, cnt_ref, carry_ref):
    @pl.when(pl.program_id(0) == 0)
    def _():
        carry_ref[...] = jnp.zeros_like(carry_ref)

    rows = lax.broadcasted_iota(I32, (MOE_EXPERTS, TM_ROUTE), 0)
    oh0 = rows == e_ref[0:1, :]
    oh1 = rows == e_ref[1:2, :]
    oh = jnp.where(oh0 | oh1, 1.0, 0.0)
    before = _dot(oh.astype(BF16), u_ref[...]) + carry_ref[:, 0:1]
    rank_ref[0:1, :] = jnp.sum(jnp.where(oh0, before, 0.0), axis=0, keepdims=True).astype(I32)
    rank_ref[1:2, :] = jnp.sum(jnp.where(oh1, before, 0.0), axis=0, keepdims=True).astype(I32)
    carry_ref[...] = carry_ref[...] + jnp.sum(oh, axis=1, keepdims=True)
    cnt_ref[...] = carry_ref[...]


def _rank(eidx, upper):
    return pl.pallas_call(
        _k_rank,
        out_shape=(jax.ShapeDtypeStruct((MOE_TOPK, TOKENS), I32),
                   jax.ShapeDtypeStruct((MOE_EXPERTS, V7X_LANES), F32)),
        grid=(TOKENS // TM_ROUTE,),
        in_specs=[pl.BlockSpec((MOE_TOPK, TM_ROUTE), lambda i: (0, i)),
                  pl.BlockSpec((TM_ROUTE, TM_ROUTE), lambda i: (0, 0))],
        out_specs=(pl.BlockSpec((MOE_TOPK, TM_ROUTE), lambda i: (0, i)),
                   pl.BlockSpec((MOE_EXPERTS, V7X_LANES), lambda i: (0, 0))),
        scratch_shapes=[pltpu.VMEM((MOE_EXPERTS, V7X_LANES), F32)],
        compiler_params=_params("arbitrary"),
        name="moe_rank",
    )(eidx, upper)


def _k_pos(e_ref, rank_ref, off_ref, pos_ref):
    rows = lax.broadcasted_iota(I32, (MOE_EXPERTS, TM_ROUTE), 0)
    for k in range(MOE_TOPK):
        start = jnp.sum(jnp.where(rows == e_ref[k:k + 1, :], off_ref[...], 0), axis=0, keepdims=True)
        pos_ref[k:k + 1, :] = start + rank_ref[k:k + 1, :]


def _positions(eidx, rank, offsets):
    tok = pl.BlockSpec((MOE_TOPK, TM_ROUTE), lambda i: (0, i))
    return pl.pallas_call(
        _k_pos,
        out_shape=jax.ShapeDtypeStruct((MOE_TOPK, TOKENS), I32),
        grid=(TOKENS // TM_ROUTE,),
        in_specs=[tok, tok, pl.BlockSpec((MOE_EXPERTS, 1), lambda i: (0, 0))],
        out_specs=tok,
        compiler_params=_params("parallel"),
        name="moe_pos",
    )(eidx, rank, offsets)


_ISSUE_BATCH = 8


def _row_copy(src_ref, src_row, dst_ref, dst_row, sem):
    src = pl.ds(pl.multiple_of(src_row * ROW_SLABS, ROW_SLABS), ROW_SLABS)
    dst = pl.ds(pl.multiple_of(dst_row * ROW_SLABS, ROW_SLABS), ROW_SLABS)
    return pltpu.make_async_copy(src_ref.at[src], dst_ref.at[dst], sem)


def _k_dispatch(pos_ref, ends_ref, x_ref, xs_ref, zero_ref, sem, zsem):
    step = pl.program_id(0)
    base = step * TM_PERM

    def drain_one_step():
        for _ in range(MOE_TOPK):
            block = pl.ds(0, TM_PERM * ROW_SLABS)
            pltpu.make_async_copy(x_ref.at[block], xs_ref.at[block], sem).wait()

    @pl.when(step == 0)
    def _():
        zero_ref[...] = jnp.zeros_like(zero_ref)

        def last_tile(e):
            seg_start = ends_ref[e - 1] if e > 0 else 0
            start = ends_ref[e] - TE
            return ends_ref[e] > seg_start, zero_fill(start)

        def unused_tile(t):
            start = ends_ref[MOE_EXPERTS - 1] + t * TE
            return start < SORTED_ROWS, zero_fill(start)

        def zero_fill(start_row):
            rows = pl.ds(pl.multiple_of(start_row * ROW_SLABS, TE * ROW_SLABS), TE * ROW_SLABS)
            return pltpu.make_async_copy(zero_ref, xs_ref.at[rows], zsem)

        fills = [last_tile(e) for e in range(MOE_EXPERTS)] + [unused_tile(t) for t in range(MOE_EXPERTS)]
        for needed, copy in fills:
            pl.when(needed)(copy.start)
        for needed, copy in fills:
            pl.when(needed)(copy.wait)

    def body(jb, c):
        j0 = base + jb * _ISSUE_BATCH
        dst = [[pos_ref[k * TOKENS + j0 + u] for k in range(MOE_TOPK)] for u in range(_ISSUE_BATCH)]
        for u in range(_ISSUE_BATCH):
            for k in range(MOE_TOPK):
                _row_copy(x_ref, j0 + u, xs_ref, dst[u][k], sem).start(priority=k)
        return c

    lax.fori_loop(0, TM_PERM // _ISSUE_BATCH, body, 0)
    pl.when(step > 0)(drain_one_step)
    pl.when(step == pl.num_programs(0) - 1)(drain_one_step)


def _dispatch(pos_flat, ends, x):
    return pl.pallas_call(
        _k_dispatch,
        out_shape=jax.ShapeDtypeStruct((SORTED_ROWS * ROW_SLABS, V7X_LANES), U32),
        grid_spec=pltpu.PrefetchScalarGridSpec(
            num_scalar_prefetch=2,
            grid=(TOKENS // TM_PERM,),
            in_specs=[pl.BlockSpec(memory_space=pl.ANY)],
            out_specs=pl.BlockSpec(memory_space=pl.ANY),
            scratch_shapes=[pltpu.VMEM((TE * ROW_SLABS, V7X_LANES), U32), pltpu.SemaphoreType.DMA,
                            pltpu.SemaphoreType.DMA],
        ),
        compiler_params=_params("arbitrary"),
        name="moe_dispatch",
    )(pos_flat, ends, x)


def _k_experts(te_ref, na_ref, xs_ref, wg_ref, wu_ref, wd_ref, ys_ref, wgubf_ref, wdbf_ref):
    i = pl.program_id(0)
    active = i < na_ref[0]
    new_expert = jnp.logical_or(i == 0, te_ref[jnp.maximum(i - 1, 0)] != te_ref[i])

    @pl.when(jnp.logical_and(active, new_expert))
    def _():
        wgubf_ref[:, :MOE_HIDDEN] = wg_ref[0, 0].astype(BF16)
        wgubf_ref[:, MOE_HIDDEN:] = wu_ref[0, 0].astype(BF16)
        wdbf_ref[...] = wd_ref[0, 0].astype(BF16)

    @pl.when(active)
    def _():
        xb = _from_row_tiles(xs_ref).astype(BF16)
        gu = _dot(xb, wgubf_ref[...])
        h = _silu(gu[:, :MOE_HIDDEN]) * gu[:, MOE_HIDDEN:]
        _to_row_tiles(ys_ref, _dot(h.astype(BF16), wdbf_ref[...]))

    @pl.when(jnp.logical_not(active))
    def _():
        ys_ref[...] = jnp.zeros_like(ys_ref)


def _experts(tile_expert, n_active, xs, wg_all, wu_all, wd_all, l):
    def xs_index(i, te, na):
        return (jnp.minimum(i, na[0] - 1), 0)

    def w_index(i, te, na):
        return (l, te[i], 0, 0)

    return pl.pallas_call(
        _k_experts,
        out_shape=jax.ShapeDtypeStruct((SORTED_ROWS * ROW_SLABS, V7X_LANES), U32),
        grid_spec=pltpu.PrefetchScalarGridSpec(
            num_scalar_prefetch=2,
            grid=(N_EXPERT_TILES,),
            in_specs=[pl.BlockSpec((TE * ROW_SLABS, V7X_LANES), xs_index),
                      pl.BlockSpec((1, 1, D_MODEL, MOE_HIDDEN), w_index),
                      pl.BlockSpec((1, 1, D_MODEL, MOE_HIDDEN), w_index),
                      pl.BlockSpec((1, 1, MOE_HIDDEN, D_MODEL), w_index)],
            out_specs=pl.BlockSpec((TE * ROW_SLABS, V7X_LANES), lambda i, te, na: (i, 0)),
            scratch_shapes=[pltpu.VMEM((D_MODEL, 2 * MOE_HIDDEN), BF16),
                            pltpu.VMEM((MOE_HIDDEN, D_MODEL), BF16)],
        ),
        compiler_params=_params("arbitrary"),
        name="moe_experts",
    )(tile_expert, n_active, xs, wg_all, wu_all, wd_all)


def _k_combine(pos_ref, ys_ref, gt_ref, x_ref, g_ref, b_ref, o_ref, buf_ref, sem):
    step = pl.program_id(0)

    def start_gather(tile, slot):
        base = tile * TM_COMB

        def body(jb, c):
            j0 = jb * _ISSUE_BATCH
            src = [[pos_ref[k * TOKENS + base + j0 + u] for k in range(MOE_TOPK)]
                   for u in range(_ISSUE_BATCH)]
            for u in range(_ISSUE_BATCH):
                for k in range(MOE_TOPK):
                    _row_copy(ys_ref, src[u][k], buf_ref.at[slot, k], j0 + u,
                              sem.at[slot]).start(priority=k)
            return c

        lax.fori_loop(0, TM_COMB // _ISSUE_BATCH, body, 0)

    slot = lax.rem(step, 2)
    pl.when(step == 0)(lambda: start_gather(0, 0))
    pl.when(step + 1 < pl.num_programs(0))(lambda: start_gather(step + 1, 1 - slot))
    for k in range(MOE_TOPK):
        pltpu.make_async_copy(ys_ref.at[pl.ds(0, TM_COMB * ROW_SLABS)], buf_ref.at[slot, k],
                              sem.at[slot]).wait()
    h = (gt_ref[:, 0:1] * _from_row_tiles(buf_ref.at[slot, 0])
         + gt_ref[:, 1:2] * _from_row_tiles(buf_ref.at[slot, 1]))
    o_ref[...] = _post_ln(x_ref[...], h, g_ref[...], b_ref[...])


def _combine(pos_flat, ys, gates_t, x, g, b):
    row = lambda i, pos: (i, 0)
    fixed = lambda i, pos: (0, 0)
    return pl.pallas_call(
        _k_combine,
        out_shape=jax.ShapeDtypeStruct((TOKENS, D_MODEL), F32),
        grid_spec=pltpu.PrefetchScalarGridSpec(
            num_scalar_prefetch=1,
            grid=(TOKENS // TM_COMB,),
            in_specs=[pl.BlockSpec(memory_space=pl.ANY),
                      pl.BlockSpec((TM_COMB, MOE_TOPK), row),
                      pl.BlockSpec((TM_COMB, D_MODEL), row),
                      pl.BlockSpec((1, D_MODEL), fixed), pl.BlockSpec((1, D_MODEL), fixed)],
            out_specs=pl.BlockSpec((TM_COMB, D_MODEL), row),
            scratch_shapes=[pltpu.VMEM((2, MOE_TOPK, TM_COMB * ROW_SLABS, V7X_LANES), U32),
                            pltpu.SemaphoreType.DMA((2,))],
        ),
        compiler_params=_params("arbitrary"),
        name="moe_combine",
    )(pos_flat, ys, gates_t, x, g, b)


def _moe(x, x_tiles, w_group, b_group, w_expert, b_expert, w_gate_all, w_up_all, w_down_all, l, g, b, upper):
    wr = jnp.concatenate([w_group.T, w_expert.T,
                          jnp.zeros((ROUTER_ROWS - MOE_GROUPS - MOE_EXPERTS, D_MODEL), F32)], axis=0)
    br = jnp.concatenate([b_group, b_expert,
                          jnp.zeros((ROUTER_ROWS - MOE_GROUPS - MOE_EXPERTS,), F32)])[:, None]
    eidx, gates = _router(x, wr.astype(BF16), br.astype(F32))
    rank, cnt = _rank(eidx, upper)
    counts = cnt[:, 0].astype(I32)
    padded = ((counts + TE - 1) // TE) * TE
    ends = jnp.cumsum(padded)
    offsets = ends - padded
    tile_start = jnp.arange(N_EXPERT_TILES, dtype=I32) * TE
    tile_expert = jnp.sum((tile_start[:, None] >= ends[None, :]).astype(I32), axis=1)
    tile_expert = jnp.minimum(tile_expert, MOE_EXPERTS - 1)
    n_active = (ends[-1:] // TE).astype(I32)
    pos = _positions(eidx, rank, offsets[:, None])
    pos_flat = pos.reshape(-1)
    xs = _dispatch(pos_flat, ends.astype(I32), x_tiles)
    ys = _experts(tile_expert, n_active, xs, w_gate_all, w_up_all, w_down_all, l)
    return _combine(pos_flat, ys, gates.T, x, g, b)


def _channel_dft_matrix():
    c = np.arange(FOURIER_DIM)
    ang = 2.0 * np.pi * ((c[:, None] * c[None, :]) % FOURIER_DIM) / FOURIER_DIM
    wc = np.zeros((HALF, D_MODEL), np.float32)
    for h in range(FOURIER_HEADS):
        rows = slice(FOURIER_DIM * h, FOURIER_DIM * (h + 1))
        wc[rows, FOURIER_DIM * h:FOURIER_DIM * (h + 1)] = np.cos(ang)
        wc[rows, HALF + FOURIER_DIM * h:HALF + FOURIER_DIM * (h + 1)] = np.sin(ang)
    return jnp.asarray(wc, BF16)


def _sequence_dft_matrices():
    side = 64
    j = jnp.arange(SEQ, dtype=I32)[:, None]
    k = jnp.arange(side, dtype=I32)[None, :]
    ang_a = ((j * k) % side).astype(F32) * (2.0 * np.pi / side)
    ang_b = ((j * k) % SEQ).astype(F32) * (2.0 * np.pi / SEQ)
    ca, sa = jnp.cos(ang_a)[:, :, None], jnp.sin(ang_a)[:, :, None]
    cb, sb = jnp.cos(ang_b)[:, None, :], jnp.sin(ang_b)[:, None, :]
    cs = (ca * cb - sa * sb).astype(BF16).reshape(SEQ, SEQ)
    ss = (sa * cb + ca * sb).astype(BF16).reshape(SEQ, SEQ)
    return cs, ss


def kernel(x, mem, pf_w_in, pf_pool_w, pf_pool_scale, pf_fourier_ln_g, pf_fourier_w, pf_w_out, hg_w_in, hg_lower_bounds, hg_norm_g, hg_w_out, xa_wq, xa_wkv, xa_wo, moe_w_group, moe_b_group, moe_w_expert, moe_b_expert, moe_w_gate, moe_w_up, moe_w_down, ln_g, ln_b):
    bf = lambda a: a.astype(BF16)
    xt = x.reshape(TOKENS, D_MODEL)
    mem2d = mem.reshape(BATCH * MEM_LEN, D_MODEL)
    lb_all = jnp.cumsum(jax.nn.softmax(hg_lower_bounds.astype(F32), axis=0), axis=0)
    lb_all = lb_all - lb_all[:1]
    tril = jnp.asarray(np.tril(np.ones((HG_CHUNK, HG_CHUNK), np.float32)), BF16)
    triu = jnp.asarray(np.triu(np.ones((HG_CHUNK, HG_CHUNK), np.float32)), BF16)
    upper = jnp.asarray(np.triu(np.ones((TM_ROUTE, TM_ROUTE), np.float32), k=1), BF16)

    for l in range(DEPTH):
        j = l // 2
        lng = lambda s: ln_g[l, s][None, :]
        lnb = lambda s: ln_b[l, s][None, :]
        if l % 2 == 0:
            cs, ss = _sequence_dft_matrices()
            ua, pq = _pf_in(xt, pf_w_in, j, pf_fourier_ln_g[j].reshape(1, HALF), _channel_dft_matrix())
            ya = _pool(ua, bf(pf_pool_w[j]), pf_pool_scale[j][None, :])
            yb = _fourier(cs, ss, pq, bf(pf_fourier_w[j]))
            xt = _pf_out(ya, yb, pf_w_out, j, xt, lng(0), lnb(0))
        else:
            proj = _hg_in(xt, hg_w_in[j])
            lb = lb_all[l].reshape(2, HG_HEADS, 1, HG_DIM)
            og = _gla(proj, lb, hg_norm_g[j][None, :], tril, triu)
            xt = _hg_out(og, hg_w_out, j, xt, lng(0), lnb(0))
        kv = _kv_proj(mem2d, xa_wkv, l)
        xt, xt_tiles = _xattn(xt, xa_wq, kv, xa_wo, l, lng(1), lnb(1))
        xt = _moe(xt, xt_tiles, moe_w_group[l], moe_b_group[l], moe_w_expert[l], moe_b_expert[l],
                  moe_w_gate, moe_w_up, moe_w_down, l, lng(2), lnb(2), upper)
    return xt.reshape(BATCH, SEQ, D_MODEL)
```

```python
import functools

import jax
import jax.numpy as jnp
import numpy as np
from jax import lax
from jax.experimental import pallas as pl
from jax.experimental.pallas import tpu as pltpu

F32 = jnp.float32
BF16 = jnp.bfloat16
I32 = jnp.int32

D_MODEL = 1024
BATCH = 8
SEQ = 4096
DEPTH = 2
TOKENS = BATCH * SEQ
MEM_LEN = 256
HALF = D_MODEL // 2
POOL_WINDOWS = (2, 4, 8, 16)
POOL_DIM = 128
FOURIER_HEADS = 4
FOURIER_DIM = 128
HG_HEADS = 8
HG_DIM = 128
HG_N_PROJ = 5
HG_CHUNK = 64
XA_HEADS = 4
XA_HEAD_DIM = 256
MOE_GROUPS = 4
MOE_PER_GROUP = 8
MOE_EXPERTS = 32
MOE_TOPK = 2
MOE_HIDDEN = 512
DN_ALPHA = (2.0 * DEPTH) ** 0.25
LN_EPS = 1e-5

V7X_LANES = 128
V7X_SUBLANES = 8
V7X_VMEM_LIMIT_BYTES = 52 * 1024 * 1024

TM = 512
TM_ROUTE = 1024
TM_PERM = 1024
TM_COMB = 512
TE = 512
SORTED_ROWS = TOKENS * MOE_TOPK + MOE_EXPERTS * TE
N_EXPERT_TILES = SORTED_ROWS // TE
GLA_GROUP = 16
GLA_ROWS = GLA_GROUP * HG_CHUNK
FOURIER_TR = 512
ROUTER_ROWS = 40

_NT = (((1,), (1,)), ((), ()))


def _params(*sem):
    return pltpu.CompilerParams(dimension_semantics=sem, vmem_limit_bytes=V7X_VMEM_LIMIT_BYTES)


def _dot(a, b):
    return jnp.dot(a, b, preferred_element_type=F32)


def _post_ln(x, h, g, b):
    y = DN_ALPHA * x + h
    mu = jnp.mean(y, axis=-1, keepdims=True)
    yc = y - mu
    var = jnp.mean(yc * yc, axis=-1, keepdims=True)
    return yc * lax.rsqrt(var + LN_EPS) * g + b


def _silu(x):
    return x * jax.nn.sigmoid(x)


def _resident(shape, index):
    return pl.BlockSpec(shape, lambda *_: index, pipeline_mode=pl.Buffered(1))


def _cast_weight_once(w_ref, wbf_ref):
    @pl.when(pl.program_id(0) == 0)
    def _():
        wbf_ref[...] = w_ref[0].astype(BF16)


def _k_pf_in(x_ref, w_ref, lng_ref, wc_ref, ua_ref, pq_ref, wbf_ref):
    _cast_weight_once(w_ref, wbf_ref)
    u = _dot(x_ref[...].astype(BF16), wbf_ref[...])
    ua_ref[...] = u[:, :HALF]
    parts = []
    for h in range(FOURIER_HEADS):
        ub = u[:, HALF + FOURIER_DIM * h:HALF + FOURIER_DIM * (h + 1)]
        mu = jnp.mean(ub, axis=-1, keepdims=True)
        uc = ub - mu
        var = jnp.mean(uc * uc, axis=-1, keepdims=True)
        parts.append(uc * lax.rsqrt(var + LN_EPS))
    un = jnp.concatenate(parts, axis=-1) * lng_ref[...]
    pq_ref[...] = _dot(un.astype(BF16), wc_ref[...]).astype(BF16)


def _pf_in(x, w_in_all, j, ln_g, wc):
    return pl.pallas_call(
        _k_pf_in,
        out_shape=(jax.ShapeDtypeStruct((TOKENS, HALF), F32),
                   jax.ShapeDtypeStruct((TOKENS, D_MODEL), BF16)),
        grid=(TOKENS // TM,),
        in_specs=[pl.BlockSpec((TM, D_MODEL), lambda i: (i, 0)),
                  _resident((1, D_MODEL, D_MODEL), (j, 0, 0)),
                  pl.BlockSpec((1, HALF), lambda i: (0, 0)),
                  _resident((HALF, D_MODEL), (0, 0))],
        out_specs=(pl.BlockSpec((TM, HALF), lambda i: (i, 0)),
                   pl.BlockSpec((TM, D_MODEL), lambda i: (i, 0))),
        scratch_shapes=[pltpu.VMEM((D_MODEL, D_MODEL), BF16)],
        compiler_params=_params("arbitrary"),
        name="pf_in",
    )(x, w_in_all, ln_g, wc)


_POOL_HALO = 8


def _k_pool(u_ref, up_ref, un_ref, pw_ref, ps_ref, ya_ref, ext_ref):
    i = pl.program_id(0)
    tiles_per_seq = SEQ // TM
    j = lax.rem(i, tiles_per_seq)
    ext_ref[_POOL_HALO:_POOL_HALO + TM, :] = u_ref[...]
    ext_ref[0:_POOL_HALO, :] = jnp.where(j == 0, 0.0, up_ref[...])
    ext_ref[_POOL_HALO + TM:2 * _POOL_HALO + TM, :] = jnp.where(j == tiles_per_seq - 1, 0.0, un_ref[...])
    pos = lax.broadcasted_iota(I32, (TM, POOL_DIM), 0) + j * TM
    outs = []
    for g, w in enumerate(POOL_WINDOWS):
        hw = w // 2
        cols = slice(g * POOL_DIM, (g + 1) * POOL_DIM)
        acc = ext_ref[_POOL_HALO - hw:_POOL_HALO - hw + TM, cols]
        for d in range(-hw + 1, hw):
            acc = acc + ext_ref[_POOL_HALO + d:_POOL_HALO + d + TM, cols]
        cnt = jnp.minimum(pos + hw, SEQ) - jnp.maximum(pos - hw, 0)
        pooled = acc / cnt.astype(F32) - u_ref[:, cols]
        outs.append(_dot(pooled.astype(BF16), pw_ref[g]))
    ya_ref[...] = (jnp.concatenate(outs, axis=-1) * ps_ref[...]).astype(BF16)


def _pool(ua, pool_w, pool_scale):
    blocks_per_tile = TM // _POOL_HALO
    last_block = TOKENS // _POOL_HALO - 1
    return pl.pallas_call(
        _k_pool,
        out_shape=jax.ShapeDtypeStruct((TOKENS, HALF), BF16),
        grid=(TOKENS // TM,),
        in_specs=[pl.BlockSpec((TM, HALF), lambda i: (i, 0)),
                  pl.BlockSpec((_POOL_HALO, HALF), lambda i: (jnp.maximum(i * blocks_per_tile - 1, 0), 0)),
                  pl.BlockSpec((_POOL_HALO, HALF),
                               lambda i: (jnp.minimum((i + 1) * blocks_per_tile, last_block), 0)),
                  pl.BlockSpec((4, POOL_DIM, POOL_DIM), lambda i: (0, 0, 0)),
                  pl.BlockSpec((1, HALF), lambda i: (0, 0))],
        out_specs=pl.BlockSpec((TM, HALF), lambda i: (i, 0)),
        scratch_shapes=[pltpu.VMEM((TM + 2 * _POOL_HALO, HALF), F32)],
        compiler_params=_params("parallel"),
        name="pf_pool",
    )(ua, ua, ua, pool_w, pool_scale)


_FOURIER_SCALE = 1.0 / float(np.sqrt(SEQ * FOURIER_DIM))


def _k_fourier(cs_ref, ss_ref, pq_ref, fw_ref, yb_ref):
    y = _dot(cs_ref[...], pq_ref[:, :HALF]) - _dot(ss_ref[...], pq_ref[:, HALF:])
    y = y * _FOURIER_SCALE
    outs = []
    for h in range(FOURIER_HEADS):
        outs.append(_dot(y[:, FOURIER_DIM * h:FOURIER_DIM * (h + 1)].astype(BF16), fw_ref[h]))
    yb_ref[...] = jnp.concatenate(outs, axis=-1).astype(BF16)


def _fourier(cs, ss, pq, fw):
    row_tiles = SEQ // FOURIER_TR
    return pl.pallas_call(
        _k_fourier,
        out_shape=jax.ShapeDtypeStruct((TOKENS, HALF), BF16),
        grid=(row_tiles, BATCH),
        in_specs=[pl.BlockSpec((FOURIER_TR, SEQ), lambda r, b: (r, 0)),
                  pl.BlockSpec((FOURIER_TR, SEQ), lambda r, b: (r, 0)),
                  pl.BlockSpec((SEQ, D_MODEL), lambda r, b: (b, 0)),
                  pl.BlockSpec((4, FOURIER_DIM, FOURIER_DIM), lambda r, b: (0, 0, 0))],
        out_specs=pl.BlockSpec((FOURIER_TR, HALF), lambda r, b: (b * row_tiles + r, 0)),
        compiler_params=_params("parallel", "parallel"),
        name="pf_fourier",
    )(cs, ss, pq, fw)


def _k_pf_out(ya_ref, yb_ref, w_ref, x_ref, g_ref, b_ref, o_ref, wbf_ref):
    _cast_weight_once(w_ref, wbf_ref)
    a = jnp.concatenate([ya_ref[...], yb_ref[...]], axis=-1)
    o_ref[...] = _post_ln(x_ref[...], _dot(a, wbf_ref[...]), g_ref[...], b_ref[...])


def _pf_out(ya, yb, w_out_all, j, x, g, b):
    row = lambda i: (i, 0)
    fixed = lambda i: (0, 0)
    return pl.pallas_call(
        _k_pf_out,
        out_shape=jax.ShapeDtypeStruct((TOKENS, D_MODEL), F32),
        grid=(TOKENS // TM,),
        in_specs=[pl.BlockSpec((TM, HALF), row), pl.BlockSpec((TM, HALF), row),
                  _resident((1, D_MODEL, D_MODEL), (j, 0, 0)), pl.BlockSpec((TM, D_MODEL), row),
                  pl.BlockSpec((1, D_MODEL), fixed), pl.BlockSpec((1, D_MODEL), fixed)],
        out_specs=pl.BlockSpec((TM, D_MODEL), row),
        scratch_shapes=[pltpu.VMEM((D_MODEL, D_MODEL), BF16)],
        compiler_params=_params("arbitrary"),
        name="pf_out",
    )(ya, yb, w_out_all, x, g, b)


_HG_STAGE_COLS = 512


def _k_hg_in(x_ref, w_hbm, o_ref, wbf_ref, stage_ref, sem):
    @pl.when(pl.program_id(0) == 0)
    def _():
        n_chunks = HG_N_PROJ * D_MODEL // _HG_STAGE_COLS

        def chunk_copy(c):
            cols = pl.ds(c * _HG_STAGE_COLS, _HG_STAGE_COLS)
            return pltpu.make_async_copy(w_hbm.at[:, cols], stage_ref.at[c % 2], sem.at[c % 2])

        chunk_copy(0).start()
        for c in range(n_chunks):
            if c + 1 < n_chunks:
                chunk_copy(c + 1).start()
            chunk_copy(c).wait()
            wbf_ref[:, c * _HG_STAGE_COLS:(c + 1) * _HG_STAGE_COLS] = stage_ref[c % 2].astype(BF16)

    xb = x_ref[...].astype(BF16)
    for p in range(HG_N_PROJ):
        acc = _dot(xb, wbf_ref[:, p * D_MODEL:(p + 1) * D_MODEL])
        if p == 0:
            acc = _silu(acc)
        for h in range(HG_HEADS):
            o_ref[p, h] = acc[:, HG_DIM * h:HG_DIM * (h + 1)].astype(BF16)


def _hg_in(x, w_in):
    return pl.pallas_call(
        _k_hg_in,
        out_shape=jax.ShapeDtypeStruct((HG_N_PROJ, HG_HEADS, TOKENS, HG_DIM), BF16),
        grid=(TOKENS // TM,),
        in_specs=[pl.BlockSpec((TM, D_MODEL), lambda i: (i, 0)),
                  pl.BlockSpec(memory_space=pl.ANY)],
        out_specs=pl.BlockSpec((HG_N_PROJ, HG_HEADS, TM, HG_DIM), lambda i: (0, 0, i, 0)),
        scratch_shapes=[pltpu.VMEM((D_MODEL, HG_N_PROJ * D_MODEL), BF16),
                        pltpu.VMEM((2, D_MODEL, _HG_STAGE_COLS), F32),
                        pltpu.SemaphoreType.DMA((2,))],
        compiler_params=_params("arbitrary"),
        name="hg_in",
    )(x, w_in)


def _gla_group(q, v, z, lbv, tri, st, rev):
    L, G = HG_CHUNK, GLA_GROUP
    f = lbv + (1.0 - lbv) * jax.nn.sigmoid(z.astype(F32))
    lf = jnp.log(f).reshape(G, L, HG_DIM)
    k3 = (1.0 - f).reshape(G, L, HG_DIM)
    q3 = q.astype(F32).reshape(G, L, HG_DIM)
    v3 = v.reshape(G, L, HG_DIM)
    hi32 = lax.bitcast_convert_type(
        lax.bitcast_convert_type(lf, jnp.uint32) & jnp.uint32(0xFFFF0000), F32)
    lf_hi = hi32.astype(BF16)
    lf_lo = (lf - hi32).astype(BF16)
    b2 = jnp.einsum('glm,gmk->glk', jnp.broadcast_to(tri, (G, L, L)),
                    jnp.concatenate([lf_hi, lf_lo], axis=-1), preferred_element_type=F32)
    b = b2[..., :HG_DIM] + b2[..., HG_DIM:]
    mid = L // 2 if not rev else L - 1 - L // 2
    end = L - 1 if not rev else 0
    b_ref = b[:, mid:mid + 1, :]
    b_last = b[:, end:end + 1, :]
    e_pos = jnp.exp(b - b_ref)
    qt32 = q3 * e_pos
    kt32 = k3 * (1.0 / e_pos)
    qt = qt32.astype(BF16)
    kt = kt32.astype(BF16)
    sc = jnp.einsum('glk,gmk->glm', qt, kt, preferred_element_type=F32)
    li = lax.broadcasted_iota(I32, (L, L), 0)
    mi = lax.broadcasted_iota(I32, (L, L), 1)
    causal = (li >= mi) if not rev else (li <= mi)
    sc = jnp.where(causal[None], sc, 0.0).astype(BF16)
    o_intra = jnp.einsum('glm,gmv->glv', sc, v3, preferred_element_type=F32)
    ku = (kt32 * jnp.exp(b_last - b_ref)).astype(BF16)
    u_t = jnp.einsum('glv,glk->gvk', v3, ku, preferred_element_type=F32)
    qe = (qt32 * jnp.exp(b_ref)).astype(BF16)
    dec = jnp.exp(b_last)
    outs = [None] * G
    for n in (range(G) if not rev else range(G - 1, -1, -1)):
        o_inter = lax.dot_general(qe[n], st.astype(BF16), _NT, preferred_element_type=F32)
        outs[n] = o_intra[n] + o_inter
        st = dec[n] * st + u_t[n]
    return jnp.concatenate(outs, axis=0), st


def _k_gla(q_ref, v_ref, zf_ref, zb_ref, g_ref, lb_ref, ng_ref, lt_ref, ut_ref, o_ref, of_ref, ob_ref):
    n_groups = SEQ // GLA_ROWS
    zero_state = jnp.zeros((HG_DIM, HG_DIM), F32)

    def group_rows(gi):
        return pl.ds(pl.multiple_of(gi * GLA_ROWS, GLA_ROWS), GLA_ROWS)

    def scan(t, carry):
        st_f, st_b = carry
        rf, rb = group_rows(t), group_rows(n_groups - 1 - t)
        o_f, st_f = _gla_group(q_ref[0, 0, rf, :], v_ref[0, 0, rf, :], zf_ref[0, 0, rf, :],
                               lb_ref[0, 0], lt_ref[...], st_f, rev=False)
        o_b, st_b = _gla_group(q_ref[0, 0, rb, :], v_ref[0, 0, rb, :], zb_ref[0, 0, rb, :],
                               lb_ref[1, 0], ut_ref[...], st_b, rev=True)
        of_ref[rf, :] = o_f
        ob_ref[rb, :] = o_b
        return st_f, st_b

    lax.fori_loop(0, n_groups, scan, (zero_state, zero_state))

    def finish(gi, c):
        rows = group_rows(gi)
        o = of_ref[rows, :] + ob_ref[rows, :]
        o = o * lax.rsqrt(jnp.mean(o * o, axis=-1, keepdims=True) + LN_EPS)
        o = o * ng_ref[...] * _silu(g_ref[0, 0, rows, :].astype(F32))
        o_ref[0, rows, :] = o.astype(BF16)
        return c

    lax.fori_loop(0, n_groups, finish, 0)


def _gla(proj, lb, norm_g, tril, triu):
    def pspec(p):
        return pl.BlockSpec((1, 1, SEQ, HG_DIM), lambda b, h: (p, h, b, 0))
    fixed = lambda b, h: (0, 0)
    return pl.pallas_call(
        _k_gla,
        out_shape=jax.ShapeDtypeStruct((HG_HEADS, TOKENS, HG_DIM), BF16),
        grid=(BATCH, HG_HEADS),
        in_specs=[pspec(0), pspec(1), pspec(2), pspec(3), pspec(4),
                  pl.BlockSpec((2, 1, 1, HG_DIM), lambda b, h: (0, h, 0, 0)),
                  pl.BlockSpec((1, HG_DIM), fixed),
                  pl.BlockSpec((HG_CHUNK, HG_CHUNK), fixed),
                  pl.BlockSpec((HG_CHUNK, HG_CHUNK), fixed)],
        out_specs=pl.BlockSpec((1, SEQ, HG_DIM), lambda b, h: (h, b, 0)),
        scratch_shapes=[pltpu.VMEM((SEQ, HG_DIM), F32), pltpu.VMEM((SEQ, HG_DIM), F32)],
        compiler_params=_params("parallel", "parallel"),
        name="hg_gla",
    )(proj, proj, proj, proj, proj, lb, norm_g, tril, triu)


def _k_hg_out(a_ref, w_ref, x_ref, g_ref, b_ref, o_ref, wbf_ref):
    _cast_weight_once(w_ref, wbf_ref)
    a = jnp.concatenate([a_ref[h] for h in range(HG_HEADS)], axis=-1)
    o_ref[...] = _post_ln(x_ref[...], _dot(a, wbf_ref[...]), g_ref[...], b_ref[...])


def _hg_out(og, w_out_all, j, x, g, b):
    row = lambda i: (i, 0)
    fixed = lambda i: (0, 0)
    return pl.pallas_call(
        _k_hg_out,
        out_shape=jax.ShapeDtypeStruct((TOKENS, D_MODEL), F32),
        grid=(TOKENS // TM,),
        in_specs=[pl.BlockSpec((HG_HEADS, TM, HG_DIM), lambda i: (0, i, 0)),
                  _resident((1, D_MODEL, D_MODEL), (j, 0, 0)), pl.BlockSpec((TM, D_MODEL), row),
                  pl.BlockSpec((1, D_MODEL), fixed), pl.BlockSpec((1, D_MODEL), fixed)],
        out_specs=pl.BlockSpec((TM, D_MODEL), row),
        scratch_shapes=[pltpu.VMEM((D_MODEL, D_MODEL), BF16)],
        compiler_params=_params("arbitrary"),
        name="hg_out",
    )(og, w_out_all, x, g, b)


def _k_mm(x_ref, w_ref, o_ref):
    o_ref[...] = _dot(x_ref[...].astype(BF16), w_ref[0].astype(BF16)).astype(o_ref.dtype)


def _kv_proj(mem2d, wkv_all, l):
    rows, cols = mem2d.shape[0], wkv_all.shape[2]
    return pl.pallas_call(
        _k_mm,
        out_shape=jax.ShapeDtypeStruct((rows, cols), BF16),
        grid=(cols // D_MODEL, rows // TM),
        in_specs=[pl.BlockSpec((TM, D_MODEL), lambda j, i: (i, 0)),
                  pl.BlockSpec((1, D_MODEL, D_MODEL), lambda j, i: (l, 0, j))],
        out_specs=pl.BlockSpec((TM, D_MODEL), lambda j, i: (i, j)),
        compiler_params=_params("parallel", "parallel"),
        name="xa_kv",
    )(mem2d, wkv_all)


ROW_WORDS = D_MODEL // 2
ROW_SLABS = ROW_WORDS // V7X_LANES
U32 = jnp.uint32
_HI_MASK = 0xFFFF0000


def _to_row_tiles(ref, x):
    n = x.shape[0]
    as_bits = lambda v: lax.bitcast_convert_type(v.astype(BF16).astype(F32), U32)
    words = (as_bits(x[:, :ROW_WORDS]) >> 16) | (as_bits(x[:, ROW_WORDS:]) & U32(_HI_MASK))
    for s in range(ROW_SLABS):
        ref[pl.ds(s, n, stride=ROW_SLABS), :] = words[:, V7X_LANES * s:V7X_LANES * (s + 1)]


def _from_row_tiles(ref):
    n = ref.shape[0] // ROW_SLABS
    words = jnp.concatenate([ref[pl.ds(s, n, stride=ROW_SLABS), :] for s in range(ROW_SLABS)], axis=-1)
    lo = lax.bitcast_convert_type(words << 16, F32)
    hi = lax.bitcast_convert_type(words & U32(_HI_MASK), F32)
    return jnp.concatenate([lo, hi], axis=-1)


def _k_xattn(x_ref, wq_ref, kv_ref, wo_ref, g_ref, b_ref, o_ref, o3_ref, wqbf_ref, wobf_ref):
    _cast_weight_once(wq_ref, wqbf_ref)
    _cast_weight_once(wo_ref, wobf_ref)
    x = x_ref[...]
    q = _dot(x.astype(BF16), wqbf_ref[...]).astype(BF16)
    outs = []
    for h in range(XA_HEADS):
        cols = slice(XA_HEAD_DIM * h, XA_HEAD_DIM * (h + 1))
        kh = kv_ref[:, cols]
        vh = kv_ref[:, D_MODEL + XA_HEAD_DIM * h:D_MODEL + XA_HEAD_DIM * (h + 1)]
        s = lax.dot_general(q[:, cols], kh, _NT, preferred_element_type=F32) * (XA_HEAD_DIM ** -0.5)
        e = jnp.exp(s - jnp.max(s, axis=-1, keepdims=True))
        p = e / jnp.sum(e, axis=-1, keepdims=True)
        outs.append(_dot(p.astype(BF16), vh))
    o = jnp.concatenate(outs, axis=-1).astype(BF16)
    y = _post_ln(x, _dot(o, wobf_ref[...]), g_ref[...], b_ref[...])
    o_ref[...] = y
    _to_row_tiles(o3_ref, y)


def _xattn(x, wq_all, kv, wo_all, l, g, b):
    row = lambda i: (i, 0)
    fixed = lambda i: (0, 0)
    tiles_per_seq = SEQ // TM
    return pl.pallas_call(
        _k_xattn,
        out_shape=(jax.ShapeDtypeStruct((TOKENS, D_MODEL), F32),
                   jax.ShapeDtypeStruct((TOKENS * ROW_SLABS, V7X_LANES), U32)),
        grid=(TOKENS // TM,),
        in_specs=[pl.BlockSpec((TM, D_MODEL), row),
                  _resident((1, D_MODEL, D_MODEL), (l, 0, 0)),
                  pl.BlockSpec((MEM_LEN, 2 * D_MODEL), lambda i: (i // tiles_per_seq, 0)),
                  _resident((1, D_MODEL, D_MODEL), (l, 0, 0)),
                  pl.BlockSpec((1, D_MODEL), fixed), pl.BlockSpec((1, D_MODEL), fixed)],
        out_specs=(pl.BlockSpec((TM, D_MODEL), row),
                   pl.BlockSpec((TM * ROW_SLABS, V7X_LANES), row)),
        scratch_shapes=[pltpu.VMEM((D_MODEL, D_MODEL), BF16), pltpu.VMEM((D_MODEL, D_MODEL), BF16)],
        compiler_params=_params("arbitrary"),
        name="xattn",
    )(x, wq_all, kv, wo_all, g, b)


def _first_argmax_rows(v, n_rows):
    m = jnp.max(v, axis=0, keepdims=True)
    rows = lax.broadcasted_iota(I32, v.shape, 0)
    idx = jnp.min(jnp.where(v == m, rows, n_rows), axis=0, keepdims=True)
    return m, idx


def _k_router(x_ref, wr_ref, br_ref, e_ref, g_ref):
    lt = lax.dot_general(wr_ref[...], x_ref[...].astype(BF16), _NT, preferred_element_type=F32)
    lt = lt + br_ref[...]
    gl = lt[0:MOE_GROUPS, :]
    gmax, gi = _first_argmax_rows(gl, MOE_GROUPS)
    p_grp = 1.0 / jnp.sum(jnp.exp(gl - gmax), axis=0, keepdims=True)
    sel = lt[MOE_GROUPS:MOE_GROUPS + MOE_PER_GROUP, :]
    for g in range(1, MOE_GROUPS):
        lo = MOE_GROUPS + MOE_PER_GROUP * g
        sel = jnp.where(gi == g, lt[lo:lo + MOE_PER_GROUP, :], sel)
    m1, i1 = _first_argmax_rows(sel, MOE_PER_GROUP)
    rows = lax.broadcasted_iota(I32, sel.shape, 0)
    m2, i2 = _first_argmax_rows(jnp.where(rows == i1, -jnp.inf, sel), MOE_PER_GROUP)
    e2 = jnp.exp(m2 - m1)
    g1 = p_grp / (1.0 + e2)
    e_ref[0:1, :] = gi * MOE_PER_GROUP + i1
    e_ref[1:2, :] = gi * MOE_PER_GROUP + i2
    g_ref[0:1, :] = g1
    g_ref[1:2, :] = g1 * e2


def _router(x, wr, br):
    return pl.pallas_call(
        _k_router,
        out_shape=(jax.ShapeDtypeStruct((MOE_TOPK, TOKENS), I32),
                   jax.ShapeDtypeStruct((MOE_TOPK, TOKENS), F32)),
        grid=(TOKENS // TM_ROUTE,),
        in_specs=[pl.BlockSpec((TM_ROUTE, D_MODEL), lambda i: (i, 0)),
                  pl.BlockSpec((ROUTER_ROWS, D_MODEL), lambda i: (0, 0)),
                  pl.BlockSpec((ROUTER_ROWS, 1), lambda i: (0, 0))],
        out_specs=(pl.BlockSpec((MOE_TOPK, TM_ROUTE), lambda i: (0, i)),
                   pl.BlockSpec((MOE_TOPK, TM_ROUTE), lambda i: (0, i))),
        compiler_params=_params("parallel"),
        name="moe_router",
    )(x, wr, br)


def _k_rank(e_ref, u_ref, rank_ref, cnt_ref, carry_ref):
    @pl.when(pl.program_id(0) == 0)
    def _():
        carry_ref[...] = jnp.zeros_like(carry_ref)

    rows = lax.broadcasted_iota(I32, (MOE_EXPERTS, TM_ROUTE), 0)
    oh0 = rows == e_ref[0:1, :]
    oh1 = rows == e_ref[1:2, :]
    oh = jnp.where(oh0 | oh1, 1.0, 0.0)
    before = _dot(oh.astype(BF16), u_ref[...]) + carry_ref[:, 0:1]
    rank_ref[0:1, :] = jnp.sum(jnp.where(oh0, before, 0.0), axis=0, keepdims=True).astype(I32)
    rank_ref[1:2, :] = jnp.sum(jnp.where(oh1, before, 0.0), axis=0, keepdims=True).astype(I32)
    carry_ref[...] = carry_ref[...] + jnp.sum(oh, axis=1, keepdims=True)
    cnt_ref[...] = carry_ref[...]


def _rank(eidx, upper):
    return pl.pallas_call(
        _k_rank,
        out_shape=(jax.ShapeDtypeStruct((MOE_TOPK, TOKENS), I32),
                   jax.ShapeDtypeStruct((MOE_EXPERTS, V7X_LANES), F32)),
        grid=(TOKENS // TM_ROUTE,),
        in_specs=[pl.BlockSpec((MOE_TOPK, TM_ROUTE), lambda i: (0, i)),
                  pl.BlockSpec((TM_ROUTE, TM_ROUTE), lambda i: (0, 0))],
        out_specs=(pl.BlockSpec((MOE_TOPK, TM_ROUTE), lambda i: (0, i)),
                   pl.BlockSpec((MOE_EXPERTS, V7X_LANES), lambda i: (0, 0))),
        scratch_shapes=[pltpu.VMEM((MOE_EXPERTS, V7X_LANES), F32)],
        compiler_params=_params("arbitrary"),
        name="moe_rank",
    )(eidx, upper)


def _k_pos(e_ref, rank_ref, off_ref, pos_ref):
    rows = lax.broadcasted_iota(I32, (MOE_EXPERTS, TM_ROUTE), 0)
    for k in range(MOE_TOPK):
        start = jnp.sum(jnp.where(rows == e_ref[k:k + 1, :], off_ref[...], 0), axis=0, keepdims=True)
        pos_ref[k:k + 1, :] = start + rank_ref[k:k + 1, :]


def _positions(eidx, rank, offsets):
    tok = pl.BlockSpec((MOE_TOPK, TM_ROUTE), lambda i: (0, i))
    return pl.pallas_call(
        _k_pos,
        out_shape=jax.ShapeDtypeStruct((MOE_TOPK, TOKENS), I32),
        grid=(TOKENS // TM_ROUTE,),
        in_specs=[tok, tok, pl.BlockSpec((MOE_EXPERTS, 1), lambda i: (0, 0))],
        out_specs=tok,
        compiler_params=_params("parallel"),
        name="moe_pos",
    )(eidx, rank, offsets)


_ISSUE_BATCH = 8


def _row_copy(src_ref, src_row, dst_ref, dst_row, sem):
    src = pl.ds(pl.multiple_of(src_row * ROW_SLABS, ROW_SLABS), ROW_SLABS)
    dst = pl.ds(pl.multiple_of(dst_row * ROW_SLABS, ROW_SLABS), ROW_SLABS)
    return pltpu.make_async_copy(src_ref.at[src], dst_ref.at[dst], sem)


_DISPATCH_SLOTS = 4


def _k_dispatch(pos_ref, ends_ref, x_ref, xs_ref, zero_ref, stage_ref, row_sems, load_sems, zsem):
    step = pl.program_id(0)
    n_steps = pl.num_programs(0)
    block_rows = TM_PERM * ROW_SLABS

    def load(tile):
        slot = lax.rem(tile, _DISPATCH_SLOTS)
        src = x_ref.at[pl.ds(pl.multiple_of(tile * block_rows, block_rows), block_rows)]
        return pltpu.make_async_copy(src, stage_ref.at[slot], load_sems.at[slot])

    def drain_rows(tile):
        slot = lax.rem(tile, _DISPATCH_SLOTS)
        for _ in range(MOE_TOPK):
            pltpu.make_async_copy(stage_ref.at[slot], xs_ref.at[pl.ds(0, block_rows)],
                                  row_sems.at[slot]).wait()

    @pl.when(step == 0)
    def _():
        load(0).start()
        pl.when(n_steps > 1)(load(1).start)
        zero_ref[...] = jnp.zeros_like(zero_ref)

        def last_tile(e):
            seg_start = ends_ref[e - 1] if e > 0 else 0
            start = ends_ref[e] - TE
            return ends_ref[e] > seg_start, zero_fill(start)

        def unused_tile(t):
            start = ends_ref[MOE_EXPERTS - 1] + t * TE
            return start < SORTED_ROWS, zero_fill(start)

        def zero_fill(start_row):
            rows = pl.ds(pl.multiple_of(start_row * ROW_SLABS, TE * ROW_SLABS), TE * ROW_SLABS)
            return pltpu.make_async_copy(zero_ref, xs_ref.at[rows], zsem)

        fills = [last_tile(e) for e in range(MOE_EXPERTS)] + [unused_tile(t) for t in range(MOE_EXPERTS)]
        for needed, copy in fills:
            pl.when(needed)(copy.start)
        for needed, copy in fills:
            pl.when(needed)(copy.wait)

    pl.when(step >= 2)(lambda: drain_rows(step - 2))
    pl.when(step + 2 < n_steps)(lambda: load(step + 2).start())
    load(step).wait()

    slot = lax.rem(step, _DISPATCH_SLOTS)
    base = step * TM_PERM

    def body(jb, c):
        j0 = jb * _ISSUE_BATCH
        dst = [[pos_ref[k * TOKENS + base + j0 + u] for k in range(MOE_TOPK)]
               for u in range(_ISSUE_BATCH)]
        for u in range(_ISSUE_BATCH):
            for k in range(MOE_TOPK):
                _row_copy(stage_ref.at[slot], j0 + u, xs_ref, dst[u][k],
                          row_sems.at[slot]).start(priority=k)
        return c

    lax.fori_loop(0, TM_PERM // _ISSUE_BATCH, body, 0)

    @pl.when(step == n_steps - 1)
    def _():
        pl.when(step >= 1)(lambda: drain_rows(step - 1))
        drain_rows(step)


def _dispatch(pos_flat, ends, x):
    return pl.pallas_call(
        _k_dispatch,
        out_shape=jax.ShapeDtypeStruct((SORTED_ROWS * ROW_SLABS, V7X_LANES), U32),
        grid_spec=pltpu.PrefetchScalarGridSpec(
            num_scalar_prefetch=2,
            grid=(TOKENS // TM_PERM,),
            in_specs=[pl.BlockSpec(memory_space=pl.ANY)],
            out_specs=pl.BlockSpec(memory_space=pl.ANY),
            scratch_shapes=[pltpu.VMEM((TE * ROW_SLABS, V7X_LANES), U32),
                            pltpu.VMEM((_DISPATCH_SLOTS, TM_PERM * ROW_SLABS, V7X_LANES), U32),
                            pltpu.SemaphoreType.DMA((_DISPATCH_SLOTS,)),
                            pltpu.SemaphoreType.DMA((_DISPATCH_SLOTS,)),
                            pltpu.SemaphoreType.DMA],
        ),
        compiler_params=_params("arbitrary"),
        name="moe_dispatch",
    )(pos_flat, ends, x)


def _k_experts(te_ref, na_ref, xs_ref, wg_ref, wu_ref, wd_ref, ys_ref, wgubf_ref, wdbf_ref):
    i = pl.program_id(0)
    active = i < na_ref[0]
    new_expert = jnp.logical_or(i == 0, te_ref[jnp.maximum(i - 1, 0)] != te_ref[i])

    @pl.when(jnp.logical_and(active, new_expert))
    def _():
        wgubf_ref[:, :MOE_HIDDEN] = wg_ref[0, 0].astype(BF16)
        wgubf_ref[:, MOE_HIDDEN:] = wu_ref[0, 0].astype(BF16)
        wdbf_ref[...] = wd_ref[0, 0].astype(BF16)

    @pl.when(active)
    def _():
        xb = _from_row_tiles(xs_ref).astype(BF16)
        gu = _dot(xb, wgubf_ref[...])
        h = _silu(gu[:, :MOE_HIDDEN]) * gu[:, MOE_HIDDEN:]
        _to_row_tiles(ys_ref, _dot(h.astype(BF16), wdbf_ref[...]))

    @pl.when(jnp.logical_not(active))
    def _():
        ys_ref[...] = jnp.zeros_like(ys_ref)


def _experts(tile_expert, n_active, xs, wg_all, wu_all, wd_all, l):
    def xs_index(i, te, na):
        return (jnp.minimum(i, na[0] - 1), 0)

    def w_index(i, te, na):
        return (l, te[i], 0, 0)

    return pl.pallas_call(
        _k_experts,
        out_shape=jax.ShapeDtypeStruct((SORTED_ROWS * ROW_SLABS, V7X_LANES), U32),
        grid_spec=pltpu.PrefetchScalarGridSpec(
            num_scalar_prefetch=2,
            grid=(N_EXPERT_TILES,),
            in_specs=[pl.BlockSpec((TE * ROW_SLABS, V7X_LANES), xs_index),
                      pl.BlockSpec((1, 1, D_MODEL, MOE_HIDDEN), w_index),
                      pl.BlockSpec((1, 1, D_MODEL, MOE_HIDDEN), w_index),
                      pl.BlockSpec((1, 1, MOE_HIDDEN, D_MODEL), w_index)],
            out_specs=pl.BlockSpec((TE * ROW_SLABS, V7X_LANES), lambda i, te, na: (i, 0)),
            scratch_shapes=[pltpu.VMEM((D_MODEL, 2 * MOE_HIDDEN), BF16),
                            pltpu.VMEM((MOE_HIDDEN, D_MODEL), BF16)],
        ),
        compiler_params=_params("arbitrary"),
        name="moe_experts",
    )(tile_expert, n_active, xs, wg_all, wu_all, wd_all)


def _k_combine(pos_ref, ys_ref, gt_ref, x_ref, g_ref, b_ref, o_ref, buf_ref, sem):
    step = pl.program_id(0)

    def start_gather(tile, slot):
        base = tile * TM_COMB

        def body(jb, c):
            j0 = jb * _ISSUE_BATCH
            src = [[pos_ref[k * TOKENS + base + j0 + u] for k in range(MOE_TOPK)]
                   for u in range(_ISSUE_BATCH)]
            for u in range(_ISSUE_BATCH):
                for k in range(MOE_TOPK):
                    _row_copy(ys_ref, src[u][k], buf_ref.at[slot, k], j0 + u,
                              sem.at[slot]).start(priority=k)
            return c

        lax.fori_loop(0, TM_COMB // _ISSUE_BATCH, body, 0)

    slot = lax.rem(step, 2)
    pl.when(step == 0)(lambda: start_gather(0, 0))
    pl.when(step + 1 < pl.num_programs(0))(lambda: start_gather(step + 1, 1 - slot))
    for k in range(MOE_TOPK):
        pltpu.make_async_copy(ys_ref.at[pl.ds(0, TM_COMB * ROW_SLABS)], buf_ref.at[slot, k],
                              sem.at[slot]).wait()
    h = (gt_ref[:, 0:1] * _from_row_tiles(buf_ref.at[slot, 0])
         + gt_ref[:, 1:2] * _from_row_tiles(buf_ref.at[slot, 1]))
    o_ref[...] = _post_ln(x_ref[...], h, g_ref[...], b_ref[...])


def _combine(pos_flat, ys, gates_t, x, g, b):
    row = lambda i, pos: (i, 0)
    fixed = lambda i, pos: (0, 0)
    return pl.pallas_call(
        _k_combine,
        out_shape=jax.ShapeDtypeStruct((TOKENS, D_MODEL), F32),
        grid_spec=pltpu.PrefetchScalarGridSpec(
            num_scalar_prefetch=1,
            grid=(TOKENS // TM_COMB,),
            in_specs=[pl.BlockSpec(memory_space=pl.ANY),
                      pl.BlockSpec((TM_COMB, MOE_TOPK), row),
                      pl.BlockSpec((TM_COMB, D_MODEL), row),
                      pl.BlockSpec((1, D_MODEL), fixed), pl.BlockSpec((1, D_MODEL), fixed)],
            out_specs=pl.BlockSpec((TM_COMB, D_MODEL), row),
            scratch_shapes=[pltpu.VMEM((2, MOE_TOPK, TM_COMB * ROW_SLABS, V7X_LANES), U32),
                            pltpu.SemaphoreType.DMA((2,))],
        ),
        compiler_params=_params("arbitrary"),
        name="moe_combine",
    )(pos_flat, ys, gates_t, x, g, b)


def _moe(x, x_tiles, w_group, b_group, w_expert, b_expert, w_gate_all, w_up_all, w_down_all, l, g, b, upper):
    wr = jnp.concatenate([w_group.T, w_expert.T,
                          jnp.zeros((ROUTER_ROWS - MOE_GROUPS - MOE_EXPERTS, D_MODEL), F32)], axis=0)
    br = jnp.concatenate([b_group, b_expert,
                          jnp.zeros((ROUTER_ROWS - MOE_GROUPS - MOE_EXPERTS,), F32)])[:, None]
    eidx, gates = _router(x, wr.astype(BF16), br.astype(F32))
    rank, cnt = _rank(eidx, upper)
    counts = cnt[:, 0].astype(I32)
    padded = ((counts + TE - 1) // TE) * TE
    ends = jnp.cumsum(padded)
    offsets = ends - padded
    tile_start = jnp.arange(N_EXPERT_TILES, dtype=I32) * TE
    tile_expert = jnp.sum((tile_start[:, None] >= ends[None, :]).astype(I32), axis=1)
    tile_expert = jnp.minimum(tile_expert, MOE_EXPERTS - 1)
    n_active = (ends[-1:] // TE).astype(I32)
    pos = _positions(eidx, rank, offsets[:, None])
    pos_flat = pos.reshape(-1)
    xs = _dispatch(pos_flat, ends.astype(I32), x_tiles)
    ys = _experts(tile_expert, n_active, xs, w_gate_all, w_up_all, w_down_all, l)
    return _combine(pos_flat, ys, gates.T, x, g, b)


def _channel_dft_matrix():
    c = np.arange(FOURIER_DIM)
    ang = 2.0 * np.pi * ((c[:, None] * c[None, :]) % FOURIER_DIM) / FOURIER_DIM
    wc = np.zeros((HALF, D_MODEL), np.float32)
    for h in range(FOURIER_HEADS):
        rows = slice(FOURIER_DIM * h, FOURIER_DIM * (h + 1))
        wc[rows, FOURIER_DIM * h:FOURIER_DIM * (h + 1)] = np.cos(ang)
        wc[rows, HALF + FOURIER_DIM * h:HALF + FOURIER_DIM * (h + 1)] = np.sin(ang)
    return jnp.asarray(wc, BF16)


def _sequence_dft_matrices():
    side = 64
    j = jnp.arange(SEQ, dtype=I32)[:, None]
    k = jnp.arange(side, dtype=I32)[None, :]
    ang_a = ((j * k) % side).astype(F32) * (2.0 * np.pi / side)
    ang_b = ((j * k) % SEQ).astype(F32) * (2.0 * np.pi / SEQ)
    ca, sa = jnp.cos(ang_a)[:, :, None], jnp.sin(ang_a)[:, :, None]
    cb, sb = jnp.cos(ang_b)[:, None, :], jnp.sin(ang_b)[:, None, :]
    cs = (ca * cb - sa * sb).astype(BF16).reshape(SEQ, SEQ)
    ss = (sa * cb + ca * sb).astype(BF16).reshape(SEQ, SEQ)
    return cs, ss


def kernel(x, mem, pf_w_in, pf_pool_w, pf_pool_scale, pf_fourier_ln_g, pf_fourier_w, pf_w_out, hg_w_in, hg_lower_bounds, hg_norm_g, hg_w_out, xa_wq, xa_wkv, xa_wo, moe_w_group, moe_b_group, moe_w_expert, moe_b_expert, moe_w_gate, moe_w_up, moe_w_down, ln_g, ln_b):
    bf = lambda a: a.astype(BF16)
    xt = x.reshape(TOKENS, D_MODEL)
    mem2d = mem.reshape(BATCH * MEM_LEN, D_MODEL)
    lb_all = jnp.cumsum(jax.nn.softmax(hg_lower_bounds.astype(F32), axis=0), axis=0)
    lb_all = lb_all - lb_all[:1]
    tril = jnp.asarray(np.tril(np.ones((HG_CHUNK, HG_CHUNK), np.float32)), BF16)
    triu = jnp.asarray(np.triu(np.ones((HG_CHUNK, HG_CHUNK), np.float32)), BF16)
    upper = jnp.asarray(np.triu(np.ones((TM_ROUTE, TM_ROUTE), np.float32), k=1), BF16)

    for l in range(DEPTH):
        j = l // 2
        lng = lambda s: ln_g[l, s][None, :]
        lnb = lambda s: ln_b[l, s][None, :]
        if l % 2 == 0:
            cs, ss = _sequence_dft_matrices()
            ua, pq = _pf_in(xt, pf_w_in, j, pf_fourier_ln_g[j].reshape(1, HALF), _channel_dft_matrix())
            ya = _pool(ua, bf(pf_pool_w[j]), pf_pool_scale[j][None, :])
            yb = _fourier(cs, ss, pq, bf(pf_fourier_w[j]))
            xt = _pf_out(ya, yb, pf_w_out, j, xt, lng(0), lnb(0))
        else:
            proj = _hg_in(xt, hg_w_in[j])
            lb = lb_all[l].reshape(2, HG_HEADS, 1, HG_DIM)
            og = _gla(proj, lb, hg_norm_g[j][None, :], tril, triu)
            xt = _hg_out(og, hg_w_out, j, xt, lng(0), lnb(0))
        kv = _kv_proj(mem2d, xa_wkv, l)
        xt, xt_tiles = _xattn(xt, xa_wq, kv, xa_wo, l, lng(1), lnb(1))
        xt = _moe(xt, xt_tiles, moe_w_group[l], moe_b_group[l], moe_w_expert[l], moe_b_expert[l],
                  moe_w_gate, moe_w_up, moe_w_down, l, lng(2), lnb(2), upper)
    return xt.reshape(BATCH, SEQ, D_MODEL)
```

```python
import functools

import jax
import jax.numpy as jnp
import numpy as np
from jax import lax
from jax.experimental import pallas as pl
from jax.experimental.pallas import tpu as pltpu

F32 = jnp.float32
BF16 = jnp.bfloat16
I32 = jnp.int32

D_MODEL = 1024
BATCH = 8
SEQ = 4096
DEPTH = 2
TOKENS = BATCH * SEQ
MEM_LEN = 256
HALF = D_MODEL // 2
POOL_WINDOWS = (2, 4, 8, 16)
POOL_DIM = 128
FOURIER_HEADS = 4
FOURIER_DIM = 128
HG_HEADS = 8
HG_DIM = 128
HG_N_PROJ = 5
HG_CHUNK = 64
XA_HEADS = 4
XA_HEAD_DIM = 256
MOE_GROUPS = 4
MOE_PER_GROUP = 8
MOE_EXPERTS = 32
MOE_TOPK = 2
MOE_HIDDEN = 512
DN_ALPHA = (2.0 * DEPTH) ** 0.25
LN_EPS = 1e-5

V7X_LANES = 128
V7X_SUBLANES = 8
V7X_VMEM_LIMIT_BYTES = 52 * 1024 * 1024

TM = 512
TM_ROUTE = 1024
TM_PERM = 1024
TM_COMB = 512
TE = 512
SORTED_ROWS = TOKENS * MOE_TOPK + MOE_EXPERTS * TE
N_EXPERT_TILES = SORTED_ROWS // TE
GLA_GROUP = 16
GLA_ROWS = GLA_GROUP * HG_CHUNK
FOURIER_TR = 512
ROUTER_ROWS = 40

_NT = (((1,), (1,)), ((), ()))


def _params(*sem):
    return pltpu.CompilerParams(dimension_semantics=sem, vmem_limit_bytes=V7X_VMEM_LIMIT_BYTES)


def _dot(a, b):
    return jnp.dot(a, b, preferred_element_type=F32)


def _post_ln(x, h, g, b):
    y = DN_ALPHA * x + h
    mu = jnp.mean(y, axis=-1, keepdims=True)
    yc = y - mu
    var = jnp.mean(yc * yc, axis=-1, keepdims=True)
    return yc * lax.rsqrt(var + LN_EPS) * g + b


def _silu(x):
    return x * jax.nn.sigmoid(x)


def _resident(shape, index):
    return pl.BlockSpec(shape, lambda *_: index, pipeline_mode=pl.Buffered(1))


def _cast_weight_once(w_ref, wbf_ref):
    @pl.when(pl.program_id(0) == 0)
    def _():
        wbf_ref[...] = w_ref[0].astype(BF16)


def _k_pf_in(x_ref, w_ref, lng_ref, wc_ref, ua_ref, pq_ref, wbf_ref):
    _cast_weight_once(w_ref, wbf_ref)
    u = _dot(x_ref[...].astype(BF16), wbf_ref[...])
    ua_ref[...] = u[:, :HALF]
    parts = []
    for h in range(FOURIER_HEADS):
        ub = u[:, HALF + FOURIER_DIM * h:HALF + FOURIER_DIM * (h + 1)]
        mu = jnp.mean(ub, axis=-1, keepdims=True)
        uc = ub - mu
        var = jnp.mean(uc * uc, axis=-1, keepdims=True)
        parts.append(uc * lax.rsqrt(var + LN_EPS))
    un = jnp.concatenate(parts, axis=-1) * lng_ref[...]
    pq_ref[...] = _dot(un.astype(BF16), wc_ref[...]).astype(BF16)


def _pf_in(x, w_in_all, j, ln_g, wc):
    return pl.pallas_call(
        _k_pf_in,
        out_shape=(jax.ShapeDtypeStruct((TOKENS, HALF), F32),
                   jax.ShapeDtypeStruct((TOKENS, D_MODEL), BF16)),
        grid=(TOKENS // TM,),
        in_specs=[pl.BlockSpec((TM, D_MODEL), lambda i: (i, 0)),
                  _resident((1, D_MODEL, D_MODEL), (j, 0, 0)),
                  pl.BlockSpec((1, HALF), lambda i: (0, 0)),
                  _resident((HALF, D_MODEL), (0, 0))],
        out_specs=(pl.BlockSpec((TM, HALF), lambda i: (i, 0)),
                   pl.BlockSpec((TM, D_MODEL), lambda i: (i, 0))),
        scratch_shapes=[pltpu.VMEM((D_MODEL, D_MODEL), BF16)],
        compiler_params=_params("arbitrary"),
        name="pf_in",
    )(x, w_in_all, ln_g, wc)


_POOL_HALO = 8


def _k_pool(u_ref, up_ref, un_ref, pw_ref, ps_ref, ya_ref, ext_ref):
    i = pl.program_id(0)
    tiles_per_seq = SEQ // TM
    j = lax.rem(i, tiles_per_seq)
    ext_ref[_POOL_HALO:_POOL_HALO + TM, :] = u_ref[...]
    ext_ref[0:_POOL_HALO, :] = jnp.where(j == 0, 0.0, up_ref[...])
    ext_ref[_POOL_HALO + TM:2 * _POOL_HALO + TM, :] = jnp.where(j == tiles_per_seq - 1, 0.0, un_ref[...])
    pos = lax.broadcasted_iota(I32, (TM, POOL_DIM), 0) + j * TM
    outs = []
    for g, w in enumerate(POOL_WINDOWS):
        hw = w // 2
        cols = slice(g * POOL_DIM, (g + 1) * POOL_DIM)
        acc = ext_ref[_POOL_HALO - hw:_POOL_HALO - hw + TM, cols]
        for d in range(-hw + 1, hw):
            acc = acc + ext_ref[_POOL_HALO + d:_POOL_HALO + d + TM, cols]
        cnt = jnp.minimum(pos + hw, SEQ) - jnp.maximum(pos - hw, 0)
        pooled = acc / cnt.astype(F32) - u_ref[:, cols]
        outs.append(_dot(pooled.astype(BF16), pw_ref[g]))
    ya_ref[...] = (jnp.concatenate(outs, axis=-1) * ps_ref[...]).astype(BF16)


def _pool(ua, pool_w, pool_scale):
    blocks_per_tile = TM // _POOL_HALO
    last_block = TOKENS // _POOL_HALO - 1
    return pl.pallas_call(
        _k_pool,
        out_shape=jax.ShapeDtypeStruct((TOKENS, HALF), BF16),
        grid=(TOKENS // TM,),
        in_specs=[pl.BlockSpec((TM, HALF), lambda i: (i, 0)),
                  pl.BlockSpec((_POOL_HALO, HALF), lambda i: (jnp.maximum(i * blocks_per_tile - 1, 0), 0)),
                  pl.BlockSpec((_POOL_HALO, HALF),
                               lambda i: (jnp.minimum((i + 1) * blocks_per_tile, last_block), 0)),
                  pl.BlockSpec((4, POOL_DIM, POOL_DIM), lambda i: (0, 0, 0)),
                  pl.BlockSpec((1, HALF), lambda i: (0, 0))],
        out_specs=pl.BlockSpec((TM, HALF), lambda i: (i, 0)),
        scratch_shapes=[pltpu.VMEM((TM + 2 * _POOL_HALO, HALF), F32)],
        compiler_params=_params("parallel"),
        name="pf_pool",
    )(ua, ua, ua, pool_w, pool_scale)


_FOURIER_SCALE = 1.0 / float(np.sqrt(SEQ * FOURIER_DIM))


_DFT_SIDE = 64
_DFT_COL_CHUNK = 1024


def _k_fourier(tab_ref, e1_ref, e2_ref, pq_ref, fw_ref, yb_ref, cs_ref, ss_ref):
    @pl.when(pl.program_id(1) == 0)
    def _():
        def expand(v, e):
            hi = v.astype(BF16)
            lo = (v - hi.astype(F32)).astype(BF16)
            return _dot(hi, e) + _dot(lo, e)

        tab = tab_ref[...]
        s = _DFT_SIDE
        for c in range(SEQ // _DFT_COL_CHUNK):
            cols = slice(c * _DFT_COL_CHUNK, (c + 1) * _DFT_COL_CHUNK)
            ca = expand(tab[:, 0:s], e1_ref[:, cols])
            sa = expand(tab[:, s:2 * s], e1_ref[:, cols])
            cb = expand(tab[:, 2 * s:3 * s], e2_ref[:, cols])
            sb = expand(tab[:, 3 * s:4 * s], e2_ref[:, cols])
            cs_ref[:, cols] = (ca * cb - sa * sb).astype(BF16)
            ss_ref[:, cols] = (sa * cb + ca * sb).astype(BF16)

    y = _dot(cs_ref[...], pq_ref[:, :HALF]) - _dot(ss_ref[...], pq_ref[:, HALF:])
    y = y * _FOURIER_SCALE
    outs = []
    for h in range(FOURIER_HEADS):
        outs.append(_dot(y[:, FOURIER_DIM * h:FOURIER_DIM * (h + 1)].astype(BF16), fw_ref[h]))
    yb_ref[...] = jnp.concatenate(outs, axis=-1).astype(BF16)


def _fourier(tables, e1, e2, pq, fw):
    row_tiles = SEQ // FOURIER_TR
    return pl.pallas_call(
        _k_fourier,
        out_shape=jax.ShapeDtypeStruct((TOKENS, HALF), BF16),
        grid=(row_tiles, BATCH),
        in_specs=[pl.BlockSpec((FOURIER_TR, 4 * _DFT_SIDE), lambda r, b: (r, 0)),
                  _resident((_DFT_SIDE, SEQ), (0, 0)),
                  _resident((_DFT_SIDE, SEQ), (0, 0)),
                  pl.BlockSpec((SEQ, D_MODEL), lambda r, b: (b, 0)),
                  pl.BlockSpec((4, FOURIER_DIM, FOURIER_DIM), lambda r, b: (0, 0, 0))],
        out_specs=pl.BlockSpec((FOURIER_TR, HALF), lambda r, b: (b * row_tiles + r, 0)),
        scratch_shapes=[pltpu.VMEM((FOURIER_TR, SEQ), BF16), pltpu.VMEM((FOURIER_TR, SEQ), BF16)],
        compiler_params=_params("arbitrary", "arbitrary"),
        name="pf_fourier",
    )(tables, e1, e2, pq, fw)


_HG_STAGE_COLS = 512


def _k_hg_in(x_ref, w_hbm, o_ref, wbf_ref, stage_ref, sem):
    @pl.when(pl.program_id(0) == 0)
    def _():
        n_chunks = HG_N_PROJ * D_MODEL // _HG_STAGE_COLS

        def chunk_copy(c):
            cols = pl.ds(c * _HG_STAGE_COLS, _HG_STAGE_COLS)
            return pltpu.make_async_copy(w_hbm.at[:, cols], stage_ref.at[c % 2], sem.at[c % 2])

        chunk_copy(0).start()
        for c in range(n_chunks):
            if c + 1 < n_chunks:
                chunk_copy(c + 1).start()
            chunk_copy(c).wait()
            wbf_ref[:, c * _HG_STAGE_COLS:(c + 1) * _HG_STAGE_COLS] = stage_ref[c % 2].astype(BF16)

    xb = x_ref[...].astype(BF16)
    for p in range(HG_N_PROJ):
        acc = _dot(xb, wbf_ref[:, p * D_MODEL:(p + 1) * D_MODEL])
        if p == 0:
            acc = _silu(acc)
        for h in range(HG_HEADS):
            o_ref[p, h] = acc[:, HG_DIM * h:HG_DIM * (h + 1)].astype(BF16)


def _hg_in(x, w_in):
    return pl.pallas_call(
        _k_hg_in,
        out_shape=jax.ShapeDtypeStruct((HG_N_PROJ, HG_HEADS, TOKENS, HG_DIM), BF16),
        grid=(TOKENS // TM,),
        in_specs=[pl.BlockSpec((TM, D_MODEL), lambda i: (i, 0)),
                  pl.BlockSpec(memory_space=pl.ANY)],
        out_specs=pl.BlockSpec((HG_N_PROJ, HG_HEADS, TM, HG_DIM), lambda i: (0, 0, i, 0)),
        scratch_shapes=[pltpu.VMEM((D_MODEL, HG_N_PROJ * D_MODEL), BF16),
                        pltpu.VMEM((2, D_MODEL, _HG_STAGE_COLS), F32),
                        pltpu.SemaphoreType.DMA((2,))],
        compiler_params=_params("arbitrary"),
        name="hg_in",
    )(x, w_in)


def _gla_group(q, v, z, lbv, tri, st, rev):
    L, G = HG_CHUNK, GLA_GROUP
    f = lbv + (1.0 - lbv) * jax.nn.sigmoid(z.astype(F32))
    lf = jnp.log(f).reshape(G, L, HG_DIM)
    k3 = (1.0 - f).reshape(G, L, HG_DIM)
    q3 = q.astype(F32).reshape(G, L, HG_DIM)
    v3 = v.reshape(G, L, HG_DIM)
    hi32 = lax.bitcast_convert_type(
        lax.bitcast_convert_type(lf, jnp.uint32) & jnp.uint32(0xFFFF0000), F32)
    lf_hi = hi32.astype(BF16)
    lf_lo = (lf - hi32).astype(BF16)
    b2 = jnp.einsum('glm,gmk->glk', jnp.broadcast_to(tri, (G, L, L)),
                    jnp.concatenate([lf_hi, lf_lo], axis=-1), preferred_element_type=F32)
    b = b2[..., :HG_DIM] + b2[..., HG_DIM:]
    mid = L // 2 if not rev else L - 1 - L // 2
    end = L - 1 if not rev else 0
    b_ref = b[:, mid:mid + 1, :]
    b_last = b[:, end:end + 1, :]
    e_pos = jnp.exp(b - b_ref)
    qt32 = q3 * e_pos
    kt32 = k3 * (1.0 / e_pos)
    qt = qt32.astype(BF16)
    kt = kt32.astype(BF16)
    sc = jnp.einsum('glk,gmk->glm', qt, kt, preferred_element_type=F32)
    li = lax.broadcasted_iota(I32, (L, L), 0)
    mi = lax.broadcasted_iota(I32, (L, L), 1)
    causal = (li >= mi) if not rev else (li <= mi)
    sc = jnp.where(causal[None], sc, 0.0).astype(BF16)
    o_intra = jnp.einsum('glm,gmv->glv', sc, v3, preferred_element_type=F32)
    ku = (kt32 * jnp.exp(b_last - b_ref)).astype(BF16)
    u_t = jnp.einsum('glv,glk->gvk', v3, ku, preferred_element_type=F32)
    qe = (qt32 * jnp.exp(b_ref)).astype(BF16)
    dec = jnp.exp(b_last)
    outs = [None] * G
    for n in (range(G) if not rev else range(G - 1, -1, -1)):
        o_inter = lax.dot_general(qe[n], st.astype(BF16), _NT, preferred_element_type=F32)
        outs[n] = o_intra[n] + o_inter
        st = dec[n] * st + u_t[n]
    return jnp.concatenate(outs, axis=0), st


def _k_gla(q_ref, v_ref, zf_ref, zb_ref, g_ref, lb_ref, ng_ref, lt_ref, ut_ref, o_ref, of_ref, ob_ref):
    n_groups = SEQ // GLA_ROWS
    zero_state = jnp.zeros((HG_DIM, HG_DIM), F32)

    def group_rows(gi):
        return pl.ds(pl.multiple_of(gi * GLA_ROWS, GLA_ROWS), GLA_ROWS)

    def scan(t, carry):
        st_f, st_b = carry
        rf, rb = group_rows(t), group_rows(n_groups - 1 - t)
        o_f, st_f = _gla_group(q_ref[0, 0, rf, :], v_ref[0, 0, rf, :], zf_ref[0, 0, rf, :],
                               lb_ref[0, 0], lt_ref[...], st_f, rev=False)
        o_b, st_b = _gla_group(q_ref[0, 0, rb, :], v_ref[0, 0, rb, :], zb_ref[0, 0, rb, :],
                               lb_ref[1, 0], ut_ref[...], st_b, rev=True)
        of_ref[rf, :] = o_f
        ob_ref[rb, :] = o_b
        return st_f, st_b

    lax.fori_loop(0, n_groups, scan, (zero_state, zero_state))

    def finish(gi, c):
        rows = group_rows(gi)
        o = of_ref[rows, :] + ob_ref[rows, :]
        o = o * lax.rsqrt(jnp.mean(o * o, axis=-1, keepdims=True) + LN_EPS)
        o = o * ng_ref[...] * _silu(g_ref[0, 0, rows, :].astype(F32))
        o_ref[0, rows, :] = o.astype(BF16)
        return c

    lax.fori_loop(0, n_groups, finish, 0)


def _gla(proj, lb, norm_g, tril, triu):
    def pspec(p):
        return pl.BlockSpec((1, 1, SEQ, HG_DIM), lambda b, h: (p, h, b, 0))
    fixed = lambda b, h: (0, 0)
    return pl.pallas_call(
        _k_gla,
        out_shape=jax.ShapeDtypeStruct((HG_HEADS, TOKENS, HG_DIM), BF16),
        grid=(BATCH, HG_HEADS),
        in_specs=[pspec(0), pspec(1), pspec(2), pspec(3), pspec(4),
                  pl.BlockSpec((2, 1, 1, HG_DIM), lambda b, h: (0, h, 0, 0)),
                  pl.BlockSpec((1, HG_DIM), fixed),
                  pl.BlockSpec((HG_CHUNK, HG_CHUNK), fixed),
                  pl.BlockSpec((HG_CHUNK, HG_CHUNK), fixed)],
        out_specs=pl.BlockSpec((1, SEQ, HG_DIM), lambda b, h: (h, b, 0)),
        scratch_shapes=[pltpu.VMEM((SEQ, HG_DIM), F32), pltpu.VMEM((SEQ, HG_DIM), F32)],
        compiler_params=_params("parallel", "parallel"),
        name="hg_gla",
    )(proj, proj, proj, proj, proj, lb, norm_g, tril, triu)


def _k_mm(x_ref, w_ref, o_ref):
    o_ref[...] = _dot(x_ref[...].astype(BF16), w_ref[0].astype(BF16)).astype(o_ref.dtype)


def _kv_proj(mem2d, wkv_all, l):
    rows, cols = mem2d.shape[0], wkv_all.shape[2]
    return pl.pallas_call(
        _k_mm,
        out_shape=jax.ShapeDtypeStruct((rows, cols), BF16),
        grid=(cols // D_MODEL, rows // TM),
        in_specs=[pl.BlockSpec((TM, D_MODEL), lambda j, i: (i, 0)),
                  pl.BlockSpec((1, D_MODEL, D_MODEL), lambda j, i: (l, 0, j))],
        out_specs=pl.BlockSpec((TM, D_MODEL), lambda j, i: (i, j)),
        compiler_params=_params("parallel", "parallel"),
        name="xa_kv",
    )(mem2d, wkv_all)


ROW_WORDS = D_MODEL // 2
ROW_SLABS = ROW_WORDS // V7X_LANES
U32 = jnp.uint32
_HI_MASK = 0xFFFF0000


def _to_row_tiles(ref, x):
    n = x.shape[0]
    as_bits = lambda v: lax.bitcast_convert_type(v.astype(BF16).astype(F32), U32)
    words = (as_bits(x[:, :ROW_WORDS]) >> 16) | (as_bits(x[:, ROW_WORDS:]) & U32(_HI_MASK))
    for s in range(ROW_SLABS):
        ref[pl.ds(s, n, stride=ROW_SLABS), :] = words[:, V7X_LANES * s:V7X_LANES * (s + 1)]


def _from_row_tiles(ref):
    n = ref.shape[0] // ROW_SLABS
    words = jnp.concatenate([ref[pl.ds(s, n, stride=ROW_SLABS), :] for s in range(ROW_SLABS)], axis=-1)
    lo = lax.bitcast_convert_type(words << 16, F32)
    hi = lax.bitcast_convert_type(words & U32(_HI_MASK), F32)
    return jnp.concatenate([lo, hi], axis=-1)


def _first_argmax_rows(v, n_rows):
    m = jnp.max(v, axis=0, keepdims=True)
    rows = lax.broadcasted_iota(I32, v.shape, 0)
    idx = jnp.min(jnp.where(v == m, rows, n_rows), axis=0, keepdims=True)
    return m, idx


def _route(lt):
    gl = lt[0:MOE_GROUPS, :]
    gmax, gi = _first_argmax_rows(gl, MOE_GROUPS)
    p_grp = 1.0 / jnp.sum(jnp.exp(gl - gmax), axis=0, keepdims=True)
    sel = lt[MOE_GROUPS:MOE_GROUPS + MOE_PER_GROUP, :]
    for g in range(1, MOE_GROUPS):
        lo = MOE_GROUPS + MOE_PER_GROUP * g
        sel = jnp.where(gi == g, lt[lo:lo + MOE_PER_GROUP, :], sel)
    m1, i1 = _first_argmax_rows(sel, MOE_PER_GROUP)
    rows = lax.broadcasted_iota(I32, sel.shape, 0)
    m2, i2 = _first_argmax_rows(jnp.where(rows == i1, -jnp.inf, sel), MOE_PER_GROUP)
    e2 = jnp.exp(m2 - m1)
    g1 = p_grp / (1.0 + e2)
    return (gi * MOE_PER_GROUP + i1, gi * MOE_PER_GROUP + i2), (g1, g1 * e2)


def _rank_in_expert(e0, e1, strict_upper, carry_ref):
    rows = lax.broadcasted_iota(I32, (MOE_EXPERTS, e0.shape[1]), 0)
    oh0 = rows == e0
    oh1 = rows == e1
    oh = jnp.where(oh0 | oh1, 1.0, 0.0)
    before = _dot(oh.astype(BF16), strict_upper) + carry_ref[:, 0:1]
    r0 = jnp.sum(jnp.where(oh0, before, 0.0), axis=0, keepdims=True).astype(I32)
    r1 = jnp.sum(jnp.where(oh1, before, 0.0), axis=0, keepdims=True).astype(I32)
    carry_ref[...] = carry_ref[...] + jnp.sum(oh, axis=1, keepdims=True)
    return r0, r1


def _k_block(heads_major, *refs):
    n_mix = 1 if heads_major else 2
    mix_refs = refs[:n_mix]
    (wm_ref, x_ref, g0_ref, b0_ref, wq_ref, kv_ref, wo_ref, g_ref, b_ref, wr_ref, br_ref, tri_ref,
     o_ref, o3_ref, e_ref, gate_ref, rank_ref, cnt_ref,
     wmbf_ref, wqbf_ref, wobf_ref, carry_ref) = refs[n_mix:]
    _cast_weight_once(wm_ref, wmbf_ref)
    _cast_weight_once(wq_ref, wqbf_ref)
    _cast_weight_once(wo_ref, wobf_ref)

    @pl.when(pl.program_id(0) == 0)
    def _():
        carry_ref[...] = jnp.zeros_like(carry_ref)

    if heads_major:
        a = jnp.concatenate([mix_refs[0][h] for h in range(HG_HEADS)], axis=-1)
    else:
        a = jnp.concatenate([mix_refs[0][...], mix_refs[1][...]], axis=-1)
    x = _post_ln(x_ref[...], _dot(a, wmbf_ref[...]), g0_ref[...], b0_ref[...])
    q = _dot(x.astype(BF16), wqbf_ref[...]).astype(BF16)
    outs = []
    for h in range(XA_HEADS):
        cols = slice(XA_HEAD_DIM * h, XA_HEAD_DIM * (h + 1))
        kh = kv_ref[:, cols]
        vh = kv_ref[:, D_MODEL + XA_HEAD_DIM * h:D_MODEL + XA_HEAD_DIM * (h + 1)]
        s = lax.dot_general(q[:, cols], kh, _NT, preferred_element_type=F32) * (XA_HEAD_DIM ** -0.5)
        e = jnp.exp(s - jnp.max(s, axis=-1, keepdims=True))
        p = e / jnp.sum(e, axis=-1, keepdims=True)
        outs.append(_dot(p.astype(BF16), vh))
    o = jnp.concatenate(outs, axis=-1).astype(BF16)
    y = _post_ln(x, _dot(o, wobf_ref[...]), g_ref[...], b_ref[...])
    o_ref[...] = y
    _to_row_tiles(o3_ref, y)
    lt = lax.dot_general(wr_ref[...], y.astype(BF16), _NT, preferred_element_type=F32) + br_ref[...]
    (e0, e1), (gate0, gate1) = _route(lt)
    r0, r1 = _rank_in_expert(e0, e1, tri_ref[...], carry_ref)
    e_ref[0:1, :], e_ref[1:2, :] = e0, e1
    gate_ref[0:1, :], gate_ref[1:2, :] = gate0, gate1
    rank_ref[0:1, :], rank_ref[1:2, :] = r0, r1
    cnt_ref[...] = carry_ref[...]


def _block(mix, heads_major, w_mix_all, j, x, g0, b0, wq_all, kv, wo_all, l, g1, b1, wr, br, strict_upper):
    row = lambda i: (i, 0)
    fixed = lambda i: (0, 0)
    tok = pl.BlockSpec((MOE_TOPK, TM), lambda i: (0, i))
    tiles_per_seq = SEQ // TM
    if heads_major:
        mix_specs = [pl.BlockSpec((HG_HEADS, TM, HG_DIM), lambda i: (0, i, 0))]
    else:
        mix_specs = [pl.BlockSpec((TM, HALF), row), pl.BlockSpec((TM, HALF), row)]
    return pl.pallas_call(
        functools.partial(_k_block, heads_major),
        out_shape=(jax.ShapeDtypeStruct((TOKENS, D_MODEL), F32),
                   jax.ShapeDtypeStruct((TOKENS * ROW_SLABS, V7X_LANES), U32),
                   jax.ShapeDtypeStruct((MOE_TOPK, TOKENS), I32),
                   jax.ShapeDtypeStruct((MOE_TOPK, TOKENS), F32),
                   jax.ShapeDtypeStruct((MOE_TOPK, TOKENS), I32),
                   jax.ShapeDtypeStruct((MOE_EXPERTS, V7X_LANES), F32)),
        grid=(TOKENS // TM,),
        in_specs=mix_specs + [
            _resident((1, D_MODEL, D_MODEL), (j, 0, 0)),
            pl.BlockSpec((TM, D_MODEL), row),
            pl.BlockSpec((1, D_MODEL), fixed), pl.BlockSpec((1, D_MODEL), fixed),
            _resident((1, D_MODEL, D_MODEL), (l, 0, 0)),
            pl.BlockSpec((MEM_LEN, 2 * D_MODEL), lambda i: (i // tiles_per_seq, 0)),
            _resident((1, D_MODEL, D_MODEL), (l, 0, 0)),
            pl.BlockSpec((1, D_MODEL), fixed), pl.BlockSpec((1, D_MODEL), fixed),
            _resident((ROUTER_ROWS, D_MODEL), (0, 0)),
            _resident((ROUTER_ROWS, 1), (0, 0)),
            _resident((TM, TM), (0, 0))],
        out_specs=(pl.BlockSpec((TM, D_MODEL), row),
                   pl.BlockSpec((TM * ROW_SLABS, V7X_LANES), row),
                   tok, tok, tok,
                   pl.BlockSpec((MOE_EXPERTS, V7X_LANES), fixed)),
        scratch_shapes=[pltpu.VMEM((D_MODEL, D_MODEL), BF16), pltpu.VMEM((D_MODEL, D_MODEL), BF16),
                        pltpu.VMEM((D_MODEL, D_MODEL), BF16), pltpu.VMEM((MOE_EXPERTS, V7X_LANES), F32)],
        compiler_params=_params("arbitrary"),
        name="attn_block",
    )(*mix, w_mix_all, x, g0, b0, wq_all, kv, wo_all, g1, b1, wr, br, strict_upper)


def _k_pos(e_ref, rank_ref, off_ref, pos_ref):
    rows = lax.broadcasted_iota(I32, (MOE_EXPERTS, TM_ROUTE), 0)
    for k in range(MOE_TOPK):
        start = jnp.sum(jnp.where(rows == e_ref[k:k + 1, :], off_ref[...], 0), axis=0, keepdims=True)
        pos_ref[k:k + 1, :] = start + rank_ref[k:k + 1, :]


def _positions(eidx, rank, offsets):
    tok = pl.BlockSpec((MOE_TOPK, TM_ROUTE), lambda i: (0, i))
    return pl.pallas_call(
        _k_pos,
        out_shape=jax.ShapeDtypeStruct((MOE_TOPK, TOKENS), I32),
        grid=(TOKENS // TM_ROUTE,),
        in_specs=[tok, tok, pl.BlockSpec((MOE_EXPERTS, 1), lambda i: (0, 0))],
        out_specs=tok,
        compiler_params=_params("parallel"),
        name="moe_pos",
    )(eidx, rank, offsets)


_ISSUE_BATCH = 8


def _row_copy(src_ref, src_row, dst_ref, dst_row, sem):
    src = pl.ds(pl.multiple_of(src_row * ROW_SLABS, ROW_SLABS), ROW_SLABS)
    dst = pl.ds(pl.multiple_of(dst_row * ROW_SLABS, ROW_SLABS), ROW_SLABS)
    return pltpu.make_async_copy(src_ref.at[src], dst_ref.at[dst], sem)


_DISPATCH_SLOTS = 4


def _k_dispatch(pos_ref, ends_ref, x_ref, xs_ref, zero_ref, stage_ref, row_sems, load_sems, zsem):
    step = pl.program_id(0)
    n_steps = pl.num_programs(0)
    block_rows = TM_PERM * ROW_SLABS

    def load(tile):
        slot = lax.rem(tile, _DISPATCH_SLOTS)
        src = x_ref.at[pl.ds(pl.multiple_of(tile * block_rows, block_rows), block_rows)]
        return pltpu.make_async_copy(src, stage_ref.at[slot], load_sems.at[slot])

    def drain_rows(tile):
        slot = lax.rem(tile, _DISPATCH_SLOTS)
        for _ in range(MOE_TOPK):
            pltpu.make_async_copy(stage_ref.at[slot], xs_ref.at[pl.ds(0, block_rows)],
                                  row_sems.at[slot]).wait()

    @pl.when(step == 0)
    def _():
        load(0).start()
        pl.when(n_steps > 1)(load(1).start)
        zero_ref[...] = jnp.zeros_like(zero_ref)

        def last_tile(e):
            seg_start = ends_ref[e - 1] if e > 0 else 0
            start = ends_ref[e] - TE
            return ends_ref[e] > seg_start, zero_fill(start)

        def unused_tile(t):
            start = ends_ref[MOE_EXPERTS - 1] + t * TE
            return start < SORTED_ROWS, zero_fill(start)

        def zero_fill(start_row):
            rows = pl.ds(pl.multiple_of(start_row * ROW_SLABS, TE * ROW_SLABS), TE * ROW_SLABS)
            return pltpu.make_async_copy(zero_ref, xs_ref.at[rows], zsem)

        fills = [last_tile(e) for e in range(MOE_EXPERTS)] + [unused_tile(t) for t in range(MOE_EXPERTS)]
        for needed, copy in fills:
            pl.when(needed)(copy.start)
        for needed, copy in fills:
            pl.when(needed)(copy.wait)

    pl.when(step >= 2)(lambda: drain_rows(step - 2))
    pl.when(step + 2 < n_steps)(lambda: load(step + 2).start())
    load(step).wait()

    slot = lax.rem(step, _DISPATCH_SLOTS)
    base = step * TM_PERM

    def body(jb, c):
        j0 = jb * _ISSUE_BATCH
        dst = [[pos_ref[k * TOKENS + base + j0 + u] for k in range(MOE_TOPK)]
               for u in range(_ISSUE_BATCH)]
        for u in range(_ISSUE_BATCH):
            for k in range(MOE_TOPK):
                _row_copy(stage_ref.at[slot], j0 + u, xs_ref, dst[u][k],
                          row_sems.at[slot]).start(priority=k)
        return c

    lax.fori_loop(0, TM_PERM // _ISSUE_BATCH, body, 0)

    @pl.when(step == n_steps - 1)
    def _():
        pl.when(step >= 1)(lambda: drain_rows(step - 1))
        drain_rows(step)


def _dispatch(pos_flat, ends, x):
    return pl.pallas_call(
        _k_dispatch,
        out_shape=jax.ShapeDtypeStruct((SORTED_ROWS * ROW_SLABS, V7X_LANES), U32),
        grid_spec=pltpu.PrefetchScalarGridSpec(
            num_scalar_prefetch=2,
            grid=(TOKENS // TM_PERM,),
            in_specs=[pl.BlockSpec(memory_space=pl.ANY)],
            out_specs=pl.BlockSpec(memory_space=pl.ANY),
            scratch_shapes=[pltpu.VMEM((TE * ROW_SLABS, V7X_LANES), U32),
                            pltpu.VMEM((_DISPATCH_SLOTS, TM_PERM * ROW_SLABS, V7X_LANES), U32),
                            pltpu.SemaphoreType.DMA((_DISPATCH_SLOTS,)),
                            pltpu.SemaphoreType.DMA((_DISPATCH_SLOTS,)),
                            pltpu.SemaphoreType.DMA],
        ),
        compiler_params=_params("arbitrary"),
        name="moe_dispatch",
    )(pos_flat, ends, x)


_EXPERT_SUBTILES = 1


def _k_experts(te_ref, na_ref, xs_ref, wg_ref, wu_ref, wd_ref, ys_ref, wgubf_ref, wdbf_ref):
    i = pl.program_id(0)
    active = i < na_ref[0]
    new_expert = jnp.logical_or(i == 0, te_ref[jnp.maximum(i - 1, 0)] != te_ref[i])

    @pl.when(jnp.logical_and(active, new_expert))
    def _():
        wgubf_ref[:, :MOE_HIDDEN] = wg_ref[0, 0].astype(BF16)
        wgubf_ref[:, MOE_HIDDEN:] = wu_ref[0, 0].astype(BF16)
        wdbf_ref[...] = wd_ref[0, 0].astype(BF16)

    @pl.when(active)
    def _():
        sub_rows = TE // _EXPERT_SUBTILES * ROW_SLABS
        for t in range(_EXPERT_SUBTILES):
            rows = pl.ds(t * sub_rows, sub_rows)
            xb = _from_row_tiles(xs_ref.at[rows]).astype(BF16)
            gu = _dot(xb, wgubf_ref[...])
            h = _silu(gu[:, :MOE_HIDDEN]) * gu[:, MOE_HIDDEN:]
            _to_row_tiles(ys_ref.at[rows], _dot(h.astype(BF16), wdbf_ref[...]))

    @pl.when(jnp.logical_not(active))
    def _():
        ys_ref[...] = jnp.zeros_like(ys_ref)


def _experts(tile_expert, n_active, xs, wg_all, wu_all, wd_all, l):
    def xs_index(i, te, na):
        return (jnp.minimum(i, na[0] - 1), 0)

    def w_index(i, te, na):
        return (l, te[i], 0, 0)

    return pl.pallas_call(
        _k_experts,
        out_shape=jax.ShapeDtypeStruct((SORTED_ROWS * ROW_SLABS, V7X_LANES), U32),
        grid_spec=pltpu.PrefetchScalarGridSpec(
            num_scalar_prefetch=2,
            grid=(N_EXPERT_TILES,),
            in_specs=[pl.BlockSpec((TE * ROW_SLABS, V7X_LANES), xs_index),
                      pl.BlockSpec((1, 1, D_MODEL, MOE_HIDDEN), w_index),
                      pl.BlockSpec((1, 1, D_MODEL, MOE_HIDDEN), w_index),
                      pl.BlockSpec((1, 1, MOE_HIDDEN, D_MODEL), w_index)],
            out_specs=pl.BlockSpec((TE * ROW_SLABS, V7X_LANES), lambda i, te, na: (i, 0)),
            scratch_shapes=[pltpu.VMEM((D_MODEL, 2 * MOE_HIDDEN), BF16),
                            pltpu.VMEM((MOE_HIDDEN, D_MODEL), BF16)],
        ),
        compiler_params=_params("arbitrary"),
        name="moe_experts",
    )(tile_expert, n_active, xs, wg_all, wu_all, wd_all)


def _k_combine(pos_ref, ys_ref, gt_ref, x_ref, g_ref, b_ref, o_ref, buf_ref, sem):
    step = pl.program_id(0)

    def start_gather(tile, slot):
        base = tile * TM_COMB

        def body(jb, c):
            j0 = jb * _ISSUE_BATCH
            src = [[pos_ref[k * TOKENS + base + j0 + u] for k in range(MOE_TOPK)]
                   for u in range(_ISSUE_BATCH)]
            for u in range(_ISSUE_BATCH):
                for k in range(MOE_TOPK):
                    _row_copy(ys_ref, src[u][k], buf_ref.at[slot, k], j0 + u,
                              sem.at[slot]).start(priority=k)
            return c

        lax.fori_loop(0, TM_COMB // _ISSUE_BATCH, body, 0)

    slot = lax.rem(step, 2)
    pl.when(step == 0)(lambda: start_gather(0, 0))
    pl.when(step + 1 < pl.num_programs(0))(lambda: start_gather(step + 1, 1 - slot))
    for k in range(MOE_TOPK):
        pltpu.make_async_copy(ys_ref.at[pl.ds(0, TM_COMB * ROW_SLABS)], buf_ref.at[slot, k],
                              sem.at[slot]).wait()
    h = (gt_ref[:, 0:1] * _from_row_tiles(buf_ref.at[slot, 0])
         + gt_ref[:, 1:2] * _from_row_tiles(buf_ref.at[slot, 1]))
    o_ref[...] = _post_ln(x_ref[...], h, g_ref[...], b_ref[...])


def _combine(pos_flat, ys, gates_t, x, g, b):
    row = lambda i, pos: (i, 0)
    fixed = lambda i, pos: (0, 0)
    return pl.pallas_call(
        _k_combine,
        out_shape=jax.ShapeDtypeStruct((TOKENS, D_MODEL), F32),
        grid_spec=pltpu.PrefetchScalarGridSpec(
            num_scalar_prefetch=1,
            grid=(TOKENS // TM_COMB,),
            in_specs=[pl.BlockSpec(memory_space=pl.ANY),
                      pl.BlockSpec((TM_COMB, MOE_TOPK), row),
                      pl.BlockSpec((TM_COMB, D_MODEL), row),
                      pl.BlockSpec((1, D_MODEL), fixed), pl.BlockSpec((1, D_MODEL), fixed)],
            out_specs=pl.BlockSpec((TM_COMB, D_MODEL), row),
            scratch_shapes=[pltpu.VMEM((2, MOE_TOPK, TM_COMB * ROW_SLABS, V7X_LANES), U32),
                            pltpu.SemaphoreType.DMA((2,))],
        ),
        compiler_params=_params("arbitrary"),
        name="moe_combine",
    )(pos_flat, ys, gates_t, x, g, b)


def _router_params(w_group, b_group, w_expert, b_expert):
    pad = ROUTER_ROWS - MOE_GROUPS - MOE_EXPERTS
    wr = jnp.concatenate([w_group.T, w_expert.T, jnp.zeros((pad, D_MODEL), F32)], axis=0)
    br = jnp.concatenate([b_group, b_expert, jnp.zeros((pad,), F32)])[:, None]
    return wr.astype(BF16), br.astype(F32)


def _moe(x, x_tiles, eidx, gates, rank, cnt, w_gate_all, w_up_all, w_down_all, l, g, b):
    counts = cnt[:, 0].astype(I32)
    padded = ((counts + TE - 1) // TE) * TE
    ends = jnp.cumsum(padded)
    offsets = ends - padded
    tile_start = jnp.arange(N_EXPERT_TILES, dtype=I32) * TE
    tile_expert = jnp.sum((tile_start[:, None] >= ends[None, :]).astype(I32), axis=1)
    tile_expert = jnp.minimum(tile_expert, MOE_EXPERTS - 1)
    n_active = (ends[-1:] // TE).astype(I32)
    pos = _positions(eidx, rank, offsets[:, None])
    pos_flat = pos.reshape(-1)
    xs = _dispatch(pos_flat, ends.astype(I32), x_tiles)
    ys = _experts(tile_expert, n_active, xs, w_gate_all, w_up_all, w_down_all, l)
    return _combine(pos_flat, ys, gates.T, x, g, b)


def _channel_dft_matrix():
    c = np.arange(FOURIER_DIM)
    ang = 2.0 * np.pi * ((c[:, None] * c[None, :]) % FOURIER_DIM) / FOURIER_DIM
    wc = np.zeros((HALF, D_MODEL), np.float32)
    for h in range(FOURIER_HEADS):
        rows = slice(FOURIER_DIM * h, FOURIER_DIM * (h + 1))
        wc[rows, FOURIER_DIM * h:FOURIER_DIM * (h + 1)] = np.cos(ang)
        wc[rows, HALF + FOURIER_DIM * h:HALF + FOURIER_DIM * (h + 1)] = np.sin(ang)
    return jnp.asarray(wc, BF16)


def _sequence_dft_tables():
    side = _DFT_SIDE
    j = np.arange(SEQ, dtype=np.int64)[:, None]
    k = np.arange(side, dtype=np.int64)[None, :]
    ang_a = ((j * k) % side).astype(np.float64) * (2.0 * np.pi / side)
    ang_b = ((j * k) % SEQ).astype(np.float64) * (2.0 * np.pi / SEQ)
    tables = np.concatenate([np.cos(ang_a), np.sin(ang_a), np.cos(ang_b), np.sin(ang_b)], axis=1)
    col = np.arange(SEQ)
    e1 = (col[None, :] // side == np.arange(side)[:, None]).astype(np.float32)
    e2 = (col[None, :] % side == np.arange(side)[:, None]).astype(np.float32)
    return jnp.asarray(tables, F32), jnp.asarray(e1, BF16), jnp.asarray(e2, BF16)


def kernel(x, mem, pf_w_in, pf_pool_w, pf_pool_scale, pf_fourier_ln_g, pf_fourier_w, pf_w_out, hg_w_in, hg_lower_bounds, hg_norm_g, hg_w_out, xa_wq, xa_wkv, xa_wo, moe_w_group, moe_b_group, moe_w_expert, moe_b_expert, moe_w_gate, moe_w_up, moe_w_down, ln_g, ln_b):
    bf = lambda a: a.astype(BF16)
    xt = x.reshape(TOKENS, D_MODEL)
    mem2d = mem.reshape(BATCH * MEM_LEN, D_MODEL)
    lb_all = jnp.cumsum(jax.nn.softmax(hg_lower_bounds.astype(F32), axis=0), axis=0)
    lb_all = lb_all - lb_all[:1]
    tril = jnp.asarray(np.tril(np.ones((HG_CHUNK, HG_CHUNK), np.float32)), BF16)
    triu = jnp.asarray(np.triu(np.ones((HG_CHUNK, HG_CHUNK), np.float32)), BF16)
    strict_upper = jnp.asarray(np.triu(np.ones((TM, TM), np.float32), k=1), BF16)

    for l in range(DEPTH):
        j = l // 2
        lng = lambda s: ln_g[l, s][None, :]
        lnb = lambda s: ln_b[l, s][None, :]
        if l % 2 == 0:
            tables, e1, e2 = _sequence_dft_tables()
            ua, pq = _pf_in(xt, pf_w_in, j, pf_fourier_ln_g[j].reshape(1, HALF), _channel_dft_matrix())
            ya = _pool(ua, bf(pf_pool_w[j]), pf_pool_scale[j][None, :])
            yb = _fourier(tables, e1, e2, pq, bf(pf_fourier_w[j]))
            mix, heads_major, w_mix = (ya, yb), False, pf_w_out
        else:
            proj = _hg_in(xt, hg_w_in[j])
            lb = lb_all[l].reshape(2, HG_HEADS, 1, HG_DIM)
            og = _gla(proj, lb, hg_norm_g[j][None, :], tril, triu)
            mix, heads_major, w_mix = (og,), True, hg_w_out
        kv = _kv_proj(mem2d, xa_wkv, l)
        wr, br = _router_params(moe_w_group[l], moe_b_group[l], moe_w_expert[l], moe_b_expert[l])
        xt, xt_tiles, eidx, gates, rank, cnt = _block(
            mix, heads_major, w_mix, j, xt, lng(0), lnb(0), xa_wq, kv, xa_wo, l, lng(1), lnb(1),
            wr, br, strict_upper)
        xt = _moe(xt, xt_tiles, eidx, gates, rank, cnt, moe_w_gate, moe_w_up, moe_w_down, l,
                  lng(2), lnb(2))
    return xt.reshape(BATCH, SEQ, D_MODEL)
```

```python
import functools

import jax
import jax.numpy as jnp
import numpy as np
from jax import lax
from jax.experimental import pallas as pl
from jax.experimental.pallas import tpu as pltpu

F32 = jnp.float32
BF16 = jnp.bfloat16
I32 = jnp.int32

D_MODEL = 1024
BATCH = 8
SEQ = 4096
DEPTH = 2
TOKENS = BATCH * SEQ
MEM_LEN = 256
HALF = D_MODEL // 2
POOL_WINDOWS = (2, 4, 8, 16)
POOL_DIM = 128
FOURIER_HEADS = 4
FOURIER_DIM = 128
HG_HEADS = 8
HG_DIM = 128
HG_N_PROJ = 5
HG_CHUNK = 64
XA_HEADS = 4
XA_HEAD_DIM = 256
MOE_GROUPS = 4
MOE_PER_GROUP = 8
MOE_EXPERTS = 32
MOE_TOPK = 2
MOE_HIDDEN = 512
DN_ALPHA = (2.0 * DEPTH) ** 0.25
LN_EPS = 1e-5

V7X_LANES = 128
V7X_SUBLANES = 8
V7X_VMEM_LIMIT_BYTES = 52 * 1024 * 1024

TM = 512
TM_ROUTE = 1024
TM_PERM = 1024
TM_COMB = 512
TE = 512
SORTED_ROWS = TOKENS * MOE_TOPK + MOE_EXPERTS * TE
N_EXPERT_TILES = SORTED_ROWS // TE
GLA_GROUP = 16
GLA_ROWS = GLA_GROUP * HG_CHUNK
FOURIER_TR = 512
ROUTER_ROWS = 40

_NT = (((1,), (1,)), ((), ()))


def _params(*sem):
    return pltpu.CompilerParams(dimension_semantics=sem, vmem_limit_bytes=V7X_VMEM_LIMIT_BYTES)


def _dot(a, b):
    return jnp.dot(a, b, preferred_element_type=F32)


def _post_ln(x, h, g, b):
    y = DN_ALPHA * x + h
    mu = jnp.mean(y, axis=-1, keepdims=True)
    yc = y - mu
    var = jnp.mean(yc * yc, axis=-1, keepdims=True)
    return yc * lax.rsqrt(var + LN_EPS) * g + b


def _silu(x):
    return x * jax.nn.sigmoid(x)


def _resident(shape, index):
    return pl.BlockSpec(shape, lambda *_: index, pipeline_mode=pl.Buffered(1))


def _cast_weight_once(w_ref, wbf_ref):
    @pl.when(pl.program_id(0) == 0)
    def _():
        wbf_ref[...] = w_ref[0].astype(BF16)


TM_PF = 1024
_POOL_HALO = 8


def _k_pf_in(x_ref, xp_ref, xn_ref, w_ref, lng_ref, wc_ref, pw_ref, ps_ref, ya_ref, pq_ref, wbf_ref, ext_ref):
    _cast_weight_once(w_ref, wbf_ref)
    tiles_per_seq = SEQ // TM_PF
    j = lax.rem(pl.program_id(0), tiles_per_seq)
    u = _dot(x_ref[...].astype(BF16), wbf_ref[...])
    halo = jnp.concatenate([xp_ref[...], xn_ref[...]], axis=0).astype(BF16)
    uh = _dot(halo, wbf_ref[:, :HALF])
    ext_ref[_POOL_HALO:_POOL_HALO + TM_PF, :] = u[:, :HALF]
    ext_ref[0:_POOL_HALO, :] = jnp.where(j == 0, 0.0, uh[:_POOL_HALO])
    ext_ref[_POOL_HALO + TM_PF:, :] = jnp.where(j == tiles_per_seq - 1, 0.0, uh[_POOL_HALO:])
    pos = lax.broadcasted_iota(I32, (TM_PF, POOL_DIM), 0) + j * TM_PF
    outs = []
    for g, w in enumerate(POOL_WINDOWS):
        hw = w // 2
        cols = slice(g * POOL_DIM, (g + 1) * POOL_DIM)
        acc = ext_ref[_POOL_HALO - hw:_POOL_HALO - hw + TM_PF, cols]
        for d in range(-hw + 1, hw):
            acc = acc + ext_ref[_POOL_HALO + d:_POOL_HALO + d + TM_PF, cols]
        cnt = jnp.minimum(pos + hw, SEQ) - jnp.maximum(pos - hw, 0)
        pooled = acc / cnt.astype(F32) - ext_ref[_POOL_HALO:_POOL_HALO + TM_PF, cols]
        outs.append(_dot(pooled.astype(BF16), pw_ref[g]))
    ya_ref[...] = (jnp.concatenate(outs, axis=-1) * ps_ref[...]).astype(BF16)
    parts = []
    for h in range(FOURIER_HEADS):
        ub = u[:, HALF + FOURIER_DIM * h:HALF + FOURIER_DIM * (h + 1)]
        mu = jnp.mean(ub, axis=-1, keepdims=True)
        uc = ub - mu
        var = jnp.mean(uc * uc, axis=-1, keepdims=True)
        parts.append(uc * lax.rsqrt(var + LN_EPS))
    un = jnp.concatenate(parts, axis=-1) * lng_ref[...]
    pq_ref[...] = _dot(un.astype(BF16), wc_ref[...]).astype(BF16)


def _pf_in(x, w_in_all, j, ln_g, wc, pool_w, pool_scale):
    blocks_per_tile = TM_PF // _POOL_HALO
    last_block = TOKENS // _POOL_HALO - 1
    row = lambda i: (i, 0)
    fixed = lambda i: (0, 0)
    return pl.pallas_call(
        _k_pf_in,
        out_shape=(jax.ShapeDtypeStruct((TOKENS, HALF), BF16),
                   jax.ShapeDtypeStruct((TOKENS, D_MODEL), BF16)),
        grid=(TOKENS // TM_PF,),
        in_specs=[pl.BlockSpec((TM_PF, D_MODEL), row),
                  pl.BlockSpec((_POOL_HALO, D_MODEL), lambda i: (jnp.maximum(i * blocks_per_tile - 1, 0), 0)),
                  pl.BlockSpec((_POOL_HALO, D_MODEL),
                               lambda i: (jnp.minimum((i + 1) * blocks_per_tile, last_block), 0)),
                  _resident((1, D_MODEL, D_MODEL), (j, 0, 0)),
                  pl.BlockSpec((1, HALF), fixed),
                  _resident((HALF, D_MODEL), (0, 0)),
                  pl.BlockSpec((4, POOL_DIM, POOL_DIM), lambda i: (0, 0, 0)),
                  pl.BlockSpec((1, HALF), fixed)],
        out_specs=(pl.BlockSpec((TM_PF, HALF), row),
                   pl.BlockSpec((TM_PF, D_MODEL), row)),
        scratch_shapes=[pltpu.VMEM((D_MODEL, D_MODEL), BF16),
                        pltpu.VMEM((TM_PF + 2 * _POOL_HALO, HALF), F32)],
        compiler_params=_params("arbitrary"),
        name="pf_in",
    )(x, x, x, w_in_all, ln_g, wc, pool_w, pool_scale)


_FOURIER_SCALE = 1.0 / float(np.sqrt(SEQ * FOURIER_DIM))


_DFT_SIDE = 64
_DFT_COL_CHUNK = 1024


def _k_fourier(tab_ref, e1_ref, e2_ref, pq_ref, fw_ref, yb_ref, cs_ref, ss_ref):
    @pl.when(pl.program_id(1) == 0)
    def _():
        def expand(v, e):
            hi = v.astype(BF16)
            lo = (v - hi.astype(F32)).astype(BF16)
            return _dot(hi, e) + _dot(lo, e)

        tab = tab_ref[...]
        s = _DFT_SIDE
        for c in range(SEQ // _DFT_COL_CHUNK):
            cols = slice(c * _DFT_COL_CHUNK, (c + 1) * _DFT_COL_CHUNK)
            ca = expand(tab[:, 0:s], e1_ref[:, cols])
            sa = expand(tab[:, s:2 * s], e1_ref[:, cols])
            cb = expand(tab[:, 2 * s:3 * s], e2_ref[:, cols])
            sb = expand(tab[:, 3 * s:4 * s], e2_ref[:, cols])
            cs_ref[:, cols] = (ca * cb - sa * sb).astype(BF16)
            ss_ref[:, cols] = (sa * cb + ca * sb).astype(BF16)

    y = _dot(cs_ref[...], pq_ref[:, :HALF]) - _dot(ss_ref[...], pq_ref[:, HALF:])
    y = y * _FOURIER_SCALE
    outs = []
    for h in range(FOURIER_HEADS):
        outs.append(_dot(y[:, FOURIER_DIM * h:FOURIER_DIM * (h + 1)].astype(BF16), fw_ref[h]))
    yb_ref[...] = jnp.concatenate(outs, axis=-1).astype(BF16)


def _fourier(tables, e1, e2, pq, fw):
    row_tiles = SEQ // FOURIER_TR
    return pl.pallas_call(
        _k_fourier,
        out_shape=jax.ShapeDtypeStruct((TOKENS, HALF), BF16),
        grid=(row_tiles, BATCH),
        in_specs=[pl.BlockSpec((FOURIER_TR, 4 * _DFT_SIDE), lambda r, b: (r, 0)),
                  _resident((_DFT_SIDE, SEQ), (0, 0)),
                  _resident((_DFT_SIDE, SEQ), (0, 0)),
                  pl.BlockSpec((SEQ, D_MODEL), lambda r, b: (b, 0)),
                  pl.BlockSpec((4, FOURIER_DIM, FOURIER_DIM), lambda r, b: (0, 0, 0))],
        out_specs=pl.BlockSpec((FOURIER_TR, HALF), lambda r, b: (b * row_tiles + r, 0)),
        scratch_shapes=[pltpu.VMEM((FOURIER_TR, SEQ), BF16), pltpu.VMEM((FOURIER_TR, SEQ), BF16)],
        compiler_params=_params("arbitrary", "arbitrary"),
        name="pf_fourier",
    )(tables, e1, e2, pq, fw)


_HG_STAGE_COLS = 512


def _k_hg_in(x_ref, w_hbm, o_ref, wbf_ref, stage_ref, sem):
    @pl.when(pl.program_id(0) == 0)
    def _():
        n_chunks = HG_N_PROJ * D_MODEL // _HG_STAGE_COLS

        def chunk_copy(c):
            cols = pl.ds(c * _HG_STAGE_COLS, _HG_STAGE_COLS)
            return pltpu.make_async_copy(w_hbm.at[:, cols], stage_ref.at[c % 2], sem.at[c % 2])

        chunk_copy(0).start()
        for c in range(n_chunks):
            if c + 1 < n_chunks:
                chunk_copy(c + 1).start()
            chunk_copy(c).wait()
            wbf_ref[:, c * _HG_STAGE_COLS:(c + 1) * _HG_STAGE_COLS] = stage_ref[c % 2].astype(BF16)

    xb = x_ref[...].astype(BF16)
    for p in range(HG_N_PROJ):
        acc = _dot(xb, wbf_ref[:, p * D_MODEL:(p + 1) * D_MODEL])
        if p == 0:
            acc = _silu(acc)
        for h in range(HG_HEADS):
            o_ref[p, h] = acc[:, HG_DIM * h:HG_DIM * (h + 1)].astype(BF16)


def _hg_in(x, w_in):
    return pl.pallas_call(
        _k_hg_in,
        out_shape=jax.ShapeDtypeStruct((HG_N_PROJ, HG_HEADS, TOKENS, HG_DIM), BF16),
        grid=(TOKENS // TM,),
        in_specs=[pl.BlockSpec((TM, D_MODEL), lambda i: (i, 0)),
                  pl.BlockSpec(memory_space=pl.ANY)],
        out_specs=pl.BlockSpec((HG_N_PROJ, HG_HEADS, TM, HG_DIM), lambda i: (0, 0, i, 0)),
        scratch_shapes=[pltpu.VMEM((D_MODEL, HG_N_PROJ * D_MODEL), BF16),
                        pltpu.VMEM((2, D_MODEL, _HG_STAGE_COLS), F32),
                        pltpu.SemaphoreType.DMA((2,))],
        compiler_params=_params("arbitrary"),
        name="hg_in",
    )(x, w_in)


def _gla_prep(q, z, lbv, tri, rev):
    L, G = HG_CHUNK, GLA_GROUP
    f = lbv + (1.0 - lbv) * jax.nn.sigmoid(z.astype(F32))
    lf = jnp.log(f).reshape(G, L, HG_DIM)
    k3 = (1.0 - f).reshape(G, L, HG_DIM)
    q3 = q.astype(F32).reshape(G, L, HG_DIM)
    hi32 = lax.bitcast_convert_type(
        lax.bitcast_convert_type(lf, jnp.uint32) & jnp.uint32(0xFFFF0000), F32)
    lf_hi = hi32.astype(BF16)
    lf_lo = (lf - hi32).astype(BF16)
    b2 = jnp.einsum('glm,gmk->glk', jnp.broadcast_to(tri, (G, L, L)),
                    jnp.concatenate([lf_hi, lf_lo], axis=-1), preferred_element_type=F32)
    b = b2[..., :HG_DIM] + b2[..., HG_DIM:]
    mid = L // 2 if not rev else L - 1 - L // 2
    end = L - 1 if not rev else 0
    b_ref = b[:, mid:mid + 1, :]
    b_last = b[:, end:end + 1, :]
    e_pos = jnp.exp(b - b_ref)
    qt32 = q3 * e_pos
    kt32 = k3 * (1.0 / e_pos)
    ku = kt32 * jnp.exp(b_last - b_ref)
    qe = qt32 * jnp.exp(b_ref)
    dec = jnp.exp(b_last)
    return qt32.astype(BF16), kt32.astype(BF16), ku.astype(BF16), qe.astype(BF16), dec


def _gla_mix(qt, kt, ku, qe, dec, v, st, rev):
    L, G = HG_CHUNK, GLA_GROUP
    v3 = v.reshape(G, L, HG_DIM)
    sc = jnp.einsum('glk,gmk->glm', qt, kt, preferred_element_type=F32)
    li = lax.broadcasted_iota(I32, (L, L), 0)
    mi = lax.broadcasted_iota(I32, (L, L), 1)
    causal = (li >= mi) if not rev else (li <= mi)
    sc = jnp.where(causal[None], sc, 0.0).astype(BF16)
    o_intra = jnp.einsum('glm,gmv->glv', sc, v3, preferred_element_type=F32)
    u_t = jnp.einsum('glv,glk->gvk', v3, ku, preferred_element_type=F32)
    outs = [None] * G
    for n in (range(G) if not rev else range(G - 1, -1, -1)):
        o_inter = lax.dot_general(qe[n], st.astype(BF16), _NT, preferred_element_type=F32)
        outs[n] = o_intra[n] + o_inter
        st = dec[n] * st + u_t[n]
    return jnp.concatenate(outs, axis=0), st


_GLA_N_OPERANDS = 5


def _k_gla(q_ref, v_ref, zf_ref, zb_ref, g_ref, lb_ref, ng_ref, lt_ref, ut_ref, o_ref, of_ref, ob_ref,
           *operand_refs):
    n_groups = SEQ // GLA_ROWS
    per_slot = 2 * _GLA_N_OPERANDS
    slots = [[operand_refs[s * per_slot + d * _GLA_N_OPERANDS:s * per_slot + (d + 1) * _GLA_N_OPERANDS]
              for d in range(2)] for s in range(2)]
    directions = ((zf_ref, lt_ref, False), (zb_ref, ut_ref, True))

    def group_rows(t, rev):
        gi = t if not rev else n_groups - 1 - t
        return slice(gi * GLA_ROWS, (gi + 1) * GLA_ROWS)

    def prep(slot, t):
        for d, (z_ref, tri_ref, rev) in enumerate(directions):
            rows = group_rows(t, rev)
            vals = _gla_prep(q_ref[0, 0, rows, :], z_ref[0, 0, rows, :], lb_ref[d, 0], tri_ref[...], rev)
            for ref, val in zip(slots[slot][d], vals):
                ref[...] = val

    def mix(slot, t, states):
        out_refs = (of_ref, ob_ref)
        new_states = []
        for d, (_, _, rev) in enumerate(directions):
            rows = group_rows(t, rev)
            o, st = _gla_mix(*[ref[...] for ref in slots[slot][d]], v_ref[0, 0, rows, :], states[d], rev)
            out_refs[d][rows, :] = o
            new_states.append(st)
        return new_states

    states = [jnp.zeros((HG_DIM, HG_DIM), F32)] * 2
    prep(0, 0)
    for t in range(n_groups):
        if t + 1 < n_groups:
            prep((t + 1) % 2, t + 1)
        states = mix(t % 2, t, states)

    def finish(gi, c):
        rows = pl.ds(pl.multiple_of(gi * GLA_ROWS, GLA_ROWS), GLA_ROWS)
        o = of_ref[rows, :] + ob_ref[rows, :]
        o = o * lax.rsqrt(jnp.mean(o * o, axis=-1, keepdims=True) + LN_EPS)
        o = o * ng_ref[...] * _silu(g_ref[0, 0, rows, :].astype(F32))
        o_ref[0, rows, :] = o.astype(BF16)
        return c

    lax.fori_loop(0, n_groups, finish, 0)


def _gla(proj, lb, norm_g, tril, triu):
    def pspec(p):
        return pl.BlockSpec((1, 1, SEQ, HG_DIM), lambda b, h: (p, h, b, 0))
    fixed = lambda b, h: (0, 0)
    return pl.pallas_call(
        _k_gla,
        out_shape=jax.ShapeDtypeStruct((HG_HEADS, TOKENS, HG_DIM), BF16),
        grid=(BATCH, HG_HEADS),
        in_specs=[pspec(0), pspec(1), pspec(2), pspec(3), pspec(4),
                  pl.BlockSpec((2, 1, 1, HG_DIM), lambda b, h: (0, h, 0, 0)),
                  pl.BlockSpec((1, HG_DIM), fixed),
                  pl.BlockSpec((HG_CHUNK, HG_CHUNK), fixed),
                  pl.BlockSpec((HG_CHUNK, HG_CHUNK), fixed)],
        out_specs=pl.BlockSpec((1, SEQ, HG_DIM), lambda b, h: (h, b, 0)),
        scratch_shapes=[pltpu.VMEM((SEQ, HG_DIM), F32), pltpu.VMEM((SEQ, HG_DIM), F32)]
        + 4 * ([pltpu.VMEM((GLA_GROUP, HG_CHUNK, HG_DIM), BF16)] * (_GLA_N_OPERANDS - 1)
               + [pltpu.VMEM((GLA_GROUP, 1, HG_DIM), F32)]),
        compiler_params=_params("parallel", "parallel"),
        name="hg_gla",
    )(proj, proj, proj, proj, proj, lb, norm_g, tril, triu)


def _k_mm(x_ref, w_ref, o_ref):
    o_ref[...] = _dot(x_ref[...].astype(BF16), w_ref[0].astype(BF16)).astype(o_ref.dtype)


def _kv_proj(mem2d, wkv_all, l):
    rows, cols = mem2d.shape[0], wkv_all.shape[2]
    return pl.pallas_call(
        _k_mm,
        out_shape=jax.ShapeDtypeStruct((rows, cols), BF16),
        grid=(cols // D_MODEL, rows // TM),
        in_specs=[pl.BlockSpec((TM, D_MODEL), lambda j, i: (i, 0)),
                  pl.BlockSpec((1, D_MODEL, D_MODEL), lambda j, i: (l, 0, j))],
        out_specs=pl.BlockSpec((TM, D_MODEL), lambda j, i: (i, j)),
        compiler_params=_params("parallel", "parallel"),
        name="xa_kv",
    )(mem2d, wkv_all)


ROW_WORDS = D_MODEL // 2
ROW_SLABS = ROW_WORDS // V7X_LANES
U32 = jnp.uint32
_HI_MASK = 0xFFFF0000


def _to_row_tiles(ref, x):
    n = x.shape[0]
    as_bits = lambda v: lax.bitcast_convert_type(v.astype(BF16).astype(F32), U32)
    words = (as_bits(x[:, :ROW_WORDS]) >> 16) | (as_bits(x[:, ROW_WORDS:]) & U32(_HI_MASK))
    for s in range(ROW_SLABS):
        ref[pl.ds(s, n, stride=ROW_SLABS), :] = words[:, V7X_LANES * s:V7X_LANES * (s + 1)]


def _from_row_tiles(ref):
    n = ref.shape[0] // ROW_SLABS
    words = jnp.concatenate([ref[pl.ds(s, n, stride=ROW_SLABS), :] for s in range(ROW_SLABS)], axis=-1)
    lo = lax.bitcast_convert_type(words << 16, F32)
    hi = lax.bitcast_convert_type(words & U32(_HI_MASK), F32)
    return jnp.concatenate([lo, hi], axis=-1)


def _first_argmax_rows(v, n_rows):
    m = jnp.max(v, axis=0, keepdims=True)
    rows = lax.broadcasted_iota(I32, v.shape, 0)
    idx = jnp.min(jnp.where(v == m, rows, n_rows), axis=0, keepdims=True)
    return m, idx


def _route(lt):
    gl = lt[0:MOE_GROUPS, :]
    gmax, gi = _first_argmax_rows(gl, MOE_GROUPS)
    p_grp = 1.0 / jnp.sum(jnp.exp(gl - gmax), axis=0, keepdims=True)
    sel = lt[MOE_GROUPS:MOE_GROUPS + MOE_PER_GROUP, :]
    for g in range(1, MOE_GROUPS):
        lo = MOE_GROUPS + MOE_PER_GROUP * g
        sel = jnp.where(gi == g, lt[lo:lo + MOE_PER_GROUP, :], sel)
    m1, i1 = _first_argmax_rows(sel, MOE_PER_GROUP)
    rows = lax.broadcasted_iota(I32, sel.shape, 0)
    m2, i2 = _first_argmax_rows(jnp.where(rows == i1, -jnp.inf, sel), MOE_PER_GROUP)
    e2 = jnp.exp(m2 - m1)
    g1 = p_grp / (1.0 + e2)
    return (gi * MOE_PER_GROUP + i1, gi * MOE_PER_GROUP + i2), (g1, g1 * e2)


def _rank_in_expert(e0, e1, strict_upper, carry_ref):
    rows = lax.broadcasted_iota(I32, (MOE_EXPERTS, e0.shape[1]), 0)
    oh0 = rows == e0
    oh1 = rows == e1
    oh = jnp.where(oh0 | oh1, 1.0, 0.0)
    before = _dot(oh.astype(BF16), strict_upper) + carry_ref[:, 0:1]
    r0 = jnp.sum(jnp.where(oh0, before, 0.0), axis=0, keepdims=True).astype(I32)
    r1 = jnp.sum(jnp.where(oh1, before, 0.0), axis=0, keepdims=True).astype(I32)
    carry_ref[...] = carry_ref[...] + jnp.sum(oh, axis=1, keepdims=True)
    return r0, r1


_BLOCK_SUBTILES = 1


def _k_block(heads_major, *refs):
    n_mix = 1 if heads_major else 2
    mix_refs = refs[:n_mix]
    (wm_ref, x_ref, g0_ref, b0_ref, wq_ref, kv_ref, wo_ref, g_ref, b_ref, wr_ref, br_ref, tri_ref,
     o_ref, o3_ref, e_ref, gate_ref, rank_ref, cnt_ref,
     wmbf_ref, wqbf_ref, wobf_ref, carry_ref) = refs[n_mix:]
    _cast_weight_once(wm_ref, wmbf_ref)
    _cast_weight_once(wq_ref, wqbf_ref)
    _cast_weight_once(wo_ref, wobf_ref)

    @pl.when(pl.program_id(0) == 0)
    def _():
        carry_ref[...] = jnp.zeros_like(carry_ref)

    sub = TM // _BLOCK_SUBTILES
    routed = []
    for t in range(_BLOCK_SUBTILES):
        rows = slice(t * sub, (t + 1) * sub)
        if heads_major:
            a = jnp.concatenate([mix_refs[0][h, rows, :] for h in range(HG_HEADS)], axis=-1)
        else:
            a = jnp.concatenate([mix_refs[0][rows, :], mix_refs[1][rows, :]], axis=-1)
        x = _post_ln(x_ref[rows, :], _dot(a, wmbf_ref[...]), g0_ref[...], b0_ref[...])
        q = _dot(x.astype(BF16), wqbf_ref[...]).astype(BF16)
        outs = []
        for h in range(XA_HEADS):
            cols = slice(XA_HEAD_DIM * h, XA_HEAD_DIM * (h + 1))
            kh = kv_ref[:, cols]
            vh = kv_ref[:, D_MODEL + XA_HEAD_DIM * h:D_MODEL + XA_HEAD_DIM * (h + 1)]
            s = lax.dot_general(q[:, cols], kh, _NT, preferred_element_type=F32) * (XA_HEAD_DIM ** -0.5)
            e = jnp.exp(s - jnp.max(s, axis=-1, keepdims=True))
            p = e / jnp.sum(e, axis=-1, keepdims=True)
            outs.append(_dot(p.astype(BF16), vh))
        o = jnp.concatenate(outs, axis=-1).astype(BF16)
        y = _post_ln(x, _dot(o, wobf_ref[...]), g_ref[...], b_ref[...])
        o_ref[rows, :] = y
        _to_row_tiles(o3_ref.at[pl.ds(t * sub * ROW_SLABS, sub * ROW_SLABS)], y)
        lt = lax.dot_general(wr_ref[...], y.astype(BF16), _NT, preferred_element_type=F32) + br_ref[...]
        routed.append(_route(lt))
    e0, e1, gate0, gate1 = [jnp.concatenate([r[a][b] for r in routed], axis=-1)
                            for a in range(2) for b in range(2)]
    r0, r1 = _rank_in_expert(e0, e1, tri_ref[...], carry_ref)
    e_ref[0:1, :], e_ref[1:2, :] = e0, e1
    gate_ref[0:1, :], gate_ref[1:2, :] = gate0, gate1
    rank_ref[0:1, :], rank_ref[1:2, :] = r0, r1
    cnt_ref[...] = carry_ref[...]


def _block(mix, heads_major, w_mix_all, j, x, g0, b0, wq_all, kv, wo_all, l, g1, b1, wr, br, strict_upper):
    row = lambda i: (i, 0)
    fixed = lambda i: (0, 0)
    tok = pl.BlockSpec((MOE_TOPK, TM), lambda i: (0, i))
    tiles_per_seq = SEQ // TM
    if heads_major:
        mix_specs = [pl.BlockSpec((HG_HEADS, TM, HG_DIM), lambda i: (0, i, 0))]
    else:
        mix_specs = [pl.BlockSpec((TM, HALF), row), pl.BlockSpec((TM, HALF), row)]
    return pl.pallas_call(
        functools.partial(_k_block, heads_major),
        out_shape=(jax.ShapeDtypeStruct((TOKENS, D_MODEL), F32),
                   jax.ShapeDtypeStruct((TOKENS * ROW_SLABS, V7X_LANES), U32),
                   jax.ShapeDtypeStruct((MOE_TOPK, TOKENS), I32),
                   jax.ShapeDtypeStruct((MOE_TOPK, TOKENS), F32),
                   jax.ShapeDtypeStruct((MOE_TOPK, TOKENS), I32),
                   jax.ShapeDtypeStruct((MOE_EXPERTS, V7X_LANES), F32)),
        grid=(TOKENS // TM,),
        in_specs=mix_specs + [
            _resident((1, D_MODEL, D_MODEL), (j, 0, 0)),
            pl.BlockSpec((TM, D_MODEL), row),
            pl.BlockSpec((1, D_MODEL), fixed), pl.BlockSpec((1, D_MODEL), fixed),
            _resident((1, D_MODEL, D_MODEL), (l, 0, 0)),
            pl.BlockSpec((MEM_LEN, 2 * D_MODEL), lambda i: (i // tiles_per_seq, 0)),
            _resident((1, D_MODEL, D_MODEL), (l, 0, 0)),
            pl.BlockSpec((1, D_MODEL), fixed), pl.BlockSpec((1, D_MODEL), fixed),
            _resident((ROUTER_ROWS, D_MODEL), (0, 0)),
            _resident((ROUTER_ROWS, 1), (0, 0)),
            _resident((TM, TM), (0, 0))],
        out_specs=(pl.BlockSpec((TM, D_MODEL), row),
                   pl.BlockSpec((TM * ROW_SLABS, V7X_LANES), row),
                   tok, tok, tok,
                   pl.BlockSpec((MOE_EXPERTS, V7X_LANES), fixed)),
        scratch_shapes=[pltpu.VMEM((D_MODEL, D_MODEL), BF16), pltpu.VMEM((D_MODEL, D_MODEL), BF16),
                        pltpu.VMEM((D_MODEL, D_MODEL), BF16), pltpu.VMEM((MOE_EXPERTS, V7X_LANES), F32)],
        compiler_params=_params("arbitrary"),
        name="attn_block",
    )(*mix, w_mix_all, x, g0, b0, wq_all, kv, wo_all, g1, b1, wr, br, strict_upper)


def _k_pos(e_ref, rank_ref, off_ref, pos_ref):
    rows = lax.broadcasted_iota(I32, (MOE_EXPERTS, TM_ROUTE), 0)
    for k in range(MOE_TOPK):
        start = jnp.sum(jnp.where(rows == e_ref[k:k + 1, :], off_ref[...], 0), axis=0, keepdims=True)
        pos_ref[k:k + 1, :] = start + rank_ref[k:k + 1, :]


def _positions(eidx, rank, offsets):
    tok = pl.BlockSpec((MOE_TOPK, TM_ROUTE), lambda i: (0, i))
    return pl.pallas_call(
        _k_pos,
        out_shape=jax.ShapeDtypeStruct((MOE_TOPK, TOKENS), I32),
        grid=(TOKENS // TM_ROUTE,),
        in_specs=[tok, tok, pl.BlockSpec((MOE_EXPERTS, 1), lambda i: (0, 0))],
        out_specs=tok,
        compiler_params=_params("parallel"),
        name="moe_pos",
    )(eidx, rank, offsets)


_ISSUE_BATCH = 8


def _row_copy(src_ref, src_row, dst_ref, dst_row, sem):
    src = pl.ds(pl.multiple_of(src_row * ROW_SLABS, ROW_SLABS), ROW_SLABS)
    dst = pl.ds(pl.multiple_of(dst_row * ROW_SLABS, ROW_SLABS), ROW_SLABS)
    return pltpu.make_async_copy(src_ref.at[src], dst_ref.at[dst], sem)


_DISPATCH_SLOTS = 4


def _k_dispatch(pos_ref, ends_ref, x_ref, xs_ref, zero_ref, stage_ref, row_sems, load_sems, zsem):
    step = pl.program_id(0)
    n_steps = pl.num_programs(0)
    block_rows = TM_PERM * ROW_SLABS

    def load(tile):
        slot = lax.rem(tile, _DISPATCH_SLOTS)
        src = x_ref.at[pl.ds(pl.multiple_of(tile * block_rows, block_rows), block_rows)]
        return pltpu.make_async_copy(src, stage_ref.at[slot], load_sems.at[slot])

    def drain_rows(tile):
        slot = lax.rem(tile, _DISPATCH_SLOTS)
        for _ in range(MOE_TOPK):
            pltpu.make_async_copy(stage_ref.at[slot], xs_ref.at[pl.ds(0, block_rows)],
                                  row_sems.at[slot]).wait()

    @pl.when(step == 0)
    def _():
        load(0).start()
        pl.when(n_steps > 1)(load(1).start)
        zero_ref[...] = jnp.zeros_like(zero_ref)

        def last_tile(e):
            seg_start = ends_ref[e - 1] if e > 0 else 0
            start = ends_ref[e] - TE
            return ends_ref[e] > seg_start, zero_fill(start)

        def unused_tile(t):
            start = ends_ref[MOE_EXPERTS - 1] + t * TE
            return start < SORTED_ROWS, zero_fill(start)

        def zero_fill(start_row):
            rows = pl.ds(pl.multiple_of(start_row * ROW_SLABS, TE * ROW_SLABS), TE * ROW_SLABS)
            return pltpu.make_async_copy(zero_ref, xs_ref.at[rows], zsem)

        fills = [last_tile(e) for e in range(MOE_EXPERTS)] + [unused_tile(t) for t in range(MOE_EXPERTS)]
        for needed, copy in fills:
            pl.when(needed)(copy.start)
        for needed, copy in fills:
            pl.when(needed)(copy.wait)

    pl.when(step >= 2)(lambda: drain_rows(step - 2))
    pl.when(step + 2 < n_steps)(lambda: load(step + 2).start())
    load(step).wait()

    slot = lax.rem(step, _DISPATCH_SLOTS)
    base = step * TM_PERM

    def body(jb, c):
        j0 = jb * _ISSUE_BATCH
        dst = [[pos_ref[k * TOKENS + base + j0 + u] for k in range(MOE_TOPK)]
               for u in range(_ISSUE_BATCH)]
        for u in range(_ISSUE_BATCH):
            for k in range(MOE_TOPK):
                _row_copy(stage_ref.at[slot], j0 + u, xs_ref, dst[u][k],
                          row_sems.at[slot]).start(priority=k)
        return c

    lax.fori_loop(0, TM_PERM // _ISSUE_BATCH, body, 0)

    @pl.when(step == n_steps - 1)
    def _():
        pl.when(step >= 1)(lambda: drain_rows(step - 1))
        drain_rows(step)


def _dispatch(pos_flat, ends, x):
    return pl.pallas_call(
        _k_dispatch,
        out_shape=jax.ShapeDtypeStruct((SORTED_ROWS * ROW_SLABS, V7X_LANES), U32),
        grid_spec=pltpu.PrefetchScalarGridSpec(
            num_scalar_prefetch=2,
            grid=(TOKENS // TM_PERM,),
            in_specs=[pl.BlockSpec(memory_space=pl.ANY)],
            out_specs=pl.BlockSpec(memory_space=pl.ANY),
            scratch_shapes=[pltpu.VMEM((TE * ROW_SLABS, V7X_LANES), U32),
                            pltpu.VMEM((_DISPATCH_SLOTS, TM_PERM * ROW_SLABS, V7X_LANES), U32),
                            pltpu.SemaphoreType.DMA((_DISPATCH_SLOTS,)),
                            pltpu.SemaphoreType.DMA((_DISPATCH_SLOTS,)),
                            pltpu.SemaphoreType.DMA],
        ),
        compiler_params=_params("arbitrary"),
        name="moe_dispatch",
    )(pos_flat, ends, x)


_EXPERT_SUBTILES = 1


def _k_experts(te_ref, na_ref, xs_ref, wg_ref, wu_ref, wd_ref, ys_ref, wgubf_ref, wdbf_ref):
    i = pl.program_id(0)
    active = i < na_ref[0]
    new_expert = jnp.logical_or(i == 0, te_ref[jnp.maximum(i - 1, 0)] != te_ref[i])

    @pl.when(jnp.logical_and(active, new_expert))
    def _():
        wgubf_ref[:, :MOE_HIDDEN] = wg_ref[0, 0].astype(BF16)
        wgubf_ref[:, MOE_HIDDEN:] = wu_ref[0, 0].astype(BF16)
        wdbf_ref[...] = wd_ref[0, 0].astype(BF16)

    @pl.when(active)
    def _():
        sub_rows = TE // _EXPERT_SUBTILES * ROW_SLABS
        for t in range(_EXPERT_SUBTILES):
            rows = pl.ds(t * sub_rows, sub_rows)
            xb = _from_row_tiles(xs_ref.at[rows]).astype(BF16)
            gu = _dot(xb, wgubf_ref[...])
            h = _silu(gu[:, :MOE_HIDDEN]) * gu[:, MOE_HIDDEN:]
            _to_row_tiles(ys_ref.at[rows], _dot(h.astype(BF16), wdbf_ref[...]))

    @pl.when(jnp.logical_not(active))
    def _():
        ys_ref[...] = jnp.zeros_like(ys_ref)


def _experts(tile_expert, n_active, xs, wg_all, wu_all, wd_all, l):
    def xs_index(i, te, na):
        return (jnp.minimum(i, na[0] - 1), 0)

    def w_index(i, te, na):
        return (l, te[i], 0, 0)

    return pl.pallas_call(
        _k_experts,
        out_shape=jax.ShapeDtypeStruct((SORTED_ROWS * ROW_SLABS, V7X_LANES), U32),
        grid_spec=pltpu.PrefetchScalarGridSpec(
            num_scalar_prefetch=2,
            grid=(N_EXPERT_TILES,),
            in_specs=[pl.BlockSpec((TE * ROW_SLABS, V7X_LANES), xs_index),
                      pl.BlockSpec((1, 1, D_MODEL, MOE_HIDDEN), w_index),
                      pl.BlockSpec((1, 1, D_MODEL, MOE_HIDDEN), w_index),
                      pl.BlockSpec((1, 1, MOE_HIDDEN, D_MODEL), w_index)],
            out_specs=pl.BlockSpec((TE * ROW_SLABS, V7X_LANES), lambda i, te, na: (i, 0)),
            scratch_shapes=[pltpu.VMEM((D_MODEL, 2 * MOE_HIDDEN), BF16),
                            pltpu.VMEM((MOE_HIDDEN, D_MODEL), BF16)],
        ),
        compiler_params=_params("arbitrary"),
        name="moe_experts",
    )(tile_expert, n_active, xs, wg_all, wu_all, wd_all)


def _k_combine(pos_ref, ys_ref, gt_ref, x_ref, g_ref, b_ref, o_ref, buf_ref, sem):
    step = pl.program_id(0)

    def start_gather(tile, slot):
        base = tile * TM_COMB

        def body(jb, c):
            j0 = jb * _ISSUE_BATCH
            src = [[pos_ref[k * TOKENS + base + j0 + u] for k in range(MOE_TOPK)]
                   for u in range(_ISSUE_BATCH)]
            for u in range(_ISSUE_BATCH):
                for k in range(MOE_TOPK):
                    _row_copy(ys_ref, src[u][k], buf_ref.at[slot, k], j0 + u,
                              sem.at[slot]).start(priority=k)
            return c

        lax.fori_loop(0, TM_COMB // _ISSUE_BATCH, body, 0)

    slot = lax.rem(step, 2)
    pl.when(step == 0)(lambda: start_gather(0, 0))
    pl.when(step + 1 < pl.num_programs(0))(lambda: start_gather(step + 1, 1 - slot))
    for k in range(MOE_TOPK):
        pltpu.make_async_copy(ys_ref.at[pl.ds(0, TM_COMB * ROW_SLABS)], buf_ref.at[slot, k],
                              sem.at[slot]).wait()
    h = (gt_ref[:, 0:1] * _from_row_tiles(buf_ref.at[slot, 0])
         + gt_ref[:, 1:2] * _from_row_tiles(buf_ref.at[slot, 1]))
    o_ref[...] = _post_ln(x_ref[...], h, g_ref[...], b_ref[...])


def _combine(pos_flat, ys, gates_t, x, g, b):
    row = lambda i, pos: (i, 0)
    fixed = lambda i, pos: (0, 0)
    return pl.pallas_call(
        _k_combine,
        out_shape=jax.ShapeDtypeStruct((TOKENS, D_MODEL), F32),
        grid_spec=pltpu.PrefetchScalarGridSpec(
            num_scalar_prefetch=1,
            grid=(TOKENS // TM_COMB,),
            in_specs=[pl.BlockSpec(memory_space=pl.ANY),
                      pl.BlockSpec((TM_COMB, MOE_TOPK), row),
                      pl.BlockSpec((TM_COMB, D_MODEL), row),
                      pl.BlockSpec((1, D_MODEL), fixed), pl.BlockSpec((1, D_MODEL), fixed)],
            out_specs=pl.BlockSpec((TM_COMB, D_MODEL), row),
            scratch_shapes=[pltpu.VMEM((2, MOE_TOPK, TM_COMB * ROW_SLABS, V7X_LANES), U32),
                            pltpu.SemaphoreType.DMA((2,))],
        ),
        compiler_params=_params("arbitrary"),
        name="moe_combine",
    )(pos_flat, ys, gates_t, x, g, b)


def _router_params(w_group, b_group, w_expert, b_expert):
    pad = ROUTER_ROWS - MOE_GROUPS - MOE_EXPERTS
    wr = jnp.concatenate([w_group.T, w_expert.T, jnp.zeros((pad, D_MODEL), F32)], axis=0)
    br = jnp.concatenate([b_group, b_expert, jnp.zeros((pad,), F32)])[:, None]
    return wr.astype(BF16), br.astype(F32)


def _moe(x, x_tiles, eidx, gates, rank, cnt, w_gate_all, w_up_all, w_down_all, l, g, b):
    counts = cnt[:, 0].astype(I32)
    padded = ((counts + TE - 1) // TE) * TE
    ends = jnp.cumsum(padded)
    offsets = ends - padded
    tile_start = jnp.arange(N_EXPERT_TILES, dtype=I32) * TE
    tile_expert = jnp.sum((tile_start[:, None] >= ends[None, :]).astype(I32), axis=1)
    tile_expert = jnp.minimum(tile_expert, MOE_EXPERTS - 1)
    n_active = (ends[-1:] // TE).astype(I32)
    pos = _positions(eidx, rank, offsets[:, None])
    pos_flat = pos.reshape(-1)
    xs = _dispatch(pos_flat, ends.astype(I32), x_tiles)
    ys = _experts(tile_expert, n_active, xs, w_gate_all, w_up_all, w_down_all, l)
    return _combine(pos_flat, ys, gates.T, x, g, b)


def _channel_dft_matrix():
    c = np.arange(FOURIER_DIM)
    ang = 2.0 * np.pi * ((c[:, None] * c[None, :]) % FOURIER_DIM) / FOURIER_DIM
    wc = np.zeros((HALF, D_MODEL), np.float32)
    for h in range(FOURIER_HEADS):
        rows = slice(FOURIER_DIM * h, FOURIER_DIM * (h + 1))
        wc[rows, FOURIER_DIM * h:FOURIER_DIM * (h + 1)] = np.cos(ang)
        wc[rows, HALF + FOURIER_DIM * h:HALF + FOURIER_DIM * (h + 1)] = np.sin(ang)
    return jnp.asarray(wc, BF16)


def _sequence_dft_tables():
    side = _DFT_SIDE
    j = np.arange(SEQ, dtype=np.int64)[:, None]
    k = np.arange(side, dtype=np.int64)[None, :]
    ang_a = ((j * k) % side).astype(np.float64) * (2.0 * np.pi / side)
    ang_b = ((j * k) % SEQ).astype(np.float64) * (2.0 * np.pi / SEQ)
    tables = np.concatenate([np.cos(ang_a), np.sin(ang_a), np.cos(ang_b), np.sin(ang_b)], axis=1)
    col = np.arange(SEQ)
    e1 = (col[None, :] // side == np.arange(side)[:, None]).astype(np.float32)
    e2 = (col[None, :] % side == np.arange(side)[:, None]).astype(np.float32)
    return jnp.asarray(tables, F32), jnp.asarray(e1, BF16), jnp.asarray(e2, BF16)


def kernel(x, mem, pf_w_in, pf_pool_w, pf_pool_scale, pf_fourier_ln_g, pf_fourier_w, pf_w_out, hg_w_in, hg_lower_bounds, hg_norm_g, hg_w_out, xa_wq, xa_wkv, xa_wo, moe_w_group, moe_b_group, moe_w_expert, moe_b_expert, moe_w_gate, moe_w_up, moe_w_down, ln_g, ln_b):
    bf = lambda a: a.astype(BF16)
    xt = x.reshape(TOKENS, D_MODEL)
    mem2d = mem.reshape(BATCH * MEM_LEN, D_MODEL)
    lb_all = jnp.cumsum(jax.nn.softmax(hg_lower_bounds.astype(F32), axis=0), axis=0)
    lb_all = lb_all - lb_all[:1]
    tril = jnp.asarray(np.tril(np.ones((HG_CHUNK, HG_CHUNK), np.float32)), BF16)
    triu = jnp.asarray(np.triu(np.ones((HG_CHUNK, HG_CHUNK), np.float32)), BF16)
    strict_upper = jnp.asarray(np.triu(np.ones((TM, TM), np.float32), k=1), BF16)

    for l in range(DEPTH):
        j = l // 2
        lng = lambda s: ln_g[l, s][None, :]
        lnb = lambda s: ln_b[l, s][None, :]
        if l % 2 == 0:
            tables, e1, e2 = _sequence_dft_tables()
            ya, pq = _pf_in(xt, pf_w_in, j, pf_fourier_ln_g[j].reshape(1, HALF), _channel_dft_matrix(),
                            bf(pf_pool_w[j]), pf_pool_scale[j][None, :])
            yb = _fourier(tables, e1, e2, pq, bf(pf_fourier_w[j]))
            mix, heads_major, w_mix = (ya, yb), False, pf_w_out
        else:
            proj = _hg_in(xt, hg_w_in[j])
            lb = lb_all[l].reshape(2, HG_HEADS, 1, HG_DIM)
            og = _gla(proj, lb, hg_norm_g[j][None, :], tril, triu)
            mix, heads_major, w_mix = (og,), True, hg_w_out
        kv = _kv_proj(mem2d, xa_wkv, l)
        wr, br = _router_params(moe_w_group[l], moe_b_group[l], moe_w_expert[l], moe_b_expert[l])
        xt, xt_tiles, eidx, gates, rank, cnt = _block(
            mix, heads_major, w_mix, j, xt, lng(0), lnb(0), xa_wq, kv, xa_wo, l, lng(1), lnb(1),
            wr, br, strict_upper)
        xt = _moe(xt, xt_tiles, eidx, gates, rank, cnt, moe_w_gate, moe_w_up, moe_w_down, l,
                  lng(2), lnb(2))
    return xt.reshape(BATCH, SEQ, D_MODEL)
```

```python
import functools

import jax
import jax.numpy as jnp
import numpy as np
from jax import lax
from jax.experimental import pallas as pl
from jax.experimental.pallas import tpu as pltpu

F32 = jnp.float32
BF16 = jnp.bfloat16
I32 = jnp.int32

D_MODEL = 1024
BATCH = 8
SEQ = 4096
DEPTH = 2
TOKENS = BATCH * SEQ
MEM_LEN = 256
HALF = D_MODEL // 2
POOL_WINDOWS = (2, 4, 8, 16)
POOL_DIM = 128
FOURIER_HEADS = 4
FOURIER_DIM = 128
HG_HEADS = 8
HG_DIM = 128
HG_N_PROJ = 5
HG_CHUNK = 64
XA_HEADS = 4
XA_HEAD_DIM = 256
MOE_GROUPS = 4
MOE_PER_GROUP = 8
MOE_EXPERTS = 32
MOE_TOPK = 2
MOE_HIDDEN = 512
DN_ALPHA = (2.0 * DEPTH) ** 0.25
LN_EPS = 1e-5

V7X_LANES = 128
V7X_SUBLANES = 8
V7X_VMEM_LIMIT_BYTES = 52 * 1024 * 1024

TM = 512
TM_ROUTE = 1024
TM_PERM = 2048
TM_COMB = 1024
TE = 512
SORTED_ROWS = TOKENS * MOE_TOPK + MOE_EXPERTS * TE
N_EXPERT_TILES = SORTED_ROWS // TE
GLA_GROUP = 16
GLA_ROWS = GLA_GROUP * HG_CHUNK
FOURIER_TR = 512
ROUTER_ROWS = 40

_NT = (((1,), (1,)), ((), ()))


def _params(*sem):
    return pltpu.CompilerParams(dimension_semantics=sem, vmem_limit_bytes=V7X_VMEM_LIMIT_BYTES)


def _dot(a, b):
    return jnp.dot(a, b, preferred_element_type=F32)


def _post_ln(x, h, g, b):
    y = DN_ALPHA * x + h
    mu = jnp.mean(y, axis=-1, keepdims=True)
    yc = y - mu
    var = jnp.mean(yc * yc, axis=-1, keepdims=True)
    return yc * lax.rsqrt(var + LN_EPS) * g + b


def _silu(x):
    return x * jax.nn.sigmoid(x)


def _resident(shape, index):
    return pl.BlockSpec(shape, lambda *_: index, pipeline_mode=pl.Buffered(1))


def _cast_weight_once(w_ref, wbf_ref):
    @pl.when(pl.program_id(0) == 0)
    def _():
        wbf_ref[...] = w_ref[0].astype(BF16)


TM_PF = 1024
_POOL_HALO = 8


def _k_pf_in(x_ref, xp_ref, xn_ref, w_ref, lng_ref, wc_ref, pw_ref, ps_ref, ya_ref, pq_ref, wbf_ref, ext_ref):
    _cast_weight_once(w_ref, wbf_ref)
    tiles_per_seq = SEQ // TM_PF
    j = lax.rem(pl.program_id(0), tiles_per_seq)
    u = _dot(x_ref[...].astype(BF16), wbf_ref[...])
    halo = jnp.concatenate([xp_ref[...], xn_ref[...]], axis=0).astype(BF16)
    uh = _dot(halo, wbf_ref[:, :HALF])
    ext_ref[_POOL_HALO:_POOL_HALO + TM_PF, :] = u[:, :HALF]
    ext_ref[0:_POOL_HALO, :] = jnp.where(j == 0, 0.0, uh[:_POOL_HALO])
    ext_ref[_POOL_HALO + TM_PF:, :] = jnp.where(j == tiles_per_seq - 1, 0.0, uh[_POOL_HALO:])
    pos = lax.broadcasted_iota(I32, (TM_PF, POOL_DIM), 0) + j * TM_PF
    outs = []
    for g, w in enumerate(POOL_WINDOWS):
        hw = w // 2
        cols = slice(g * POOL_DIM, (g + 1) * POOL_DIM)
        acc = ext_ref[_POOL_HALO - hw:_POOL_HALO - hw + TM_PF, cols]
        for d in range(-hw + 1, hw):
            acc = acc + ext_ref[_POOL_HALO + d:_POOL_HALO + d + TM_PF, cols]
        cnt = jnp.minimum(pos + hw, SEQ) - jnp.maximum(pos - hw, 0)
        pooled = acc / cnt.astype(F32) - ext_ref[_POOL_HALO:_POOL_HALO + TM_PF, cols]
        outs.append(_dot(pooled.astype(BF16), pw_ref[g]))
    ya_ref[...] = (jnp.concatenate(outs, axis=-1) * ps_ref[...]).astype(BF16)
    parts = []
    for h in range(FOURIER_HEADS):
        ub = u[:, HALF + FOURIER_DIM * h:HALF + FOURIER_DIM * (h + 1)]
        mu = jnp.mean(ub, axis=-1, keepdims=True)
        uc = ub - mu
        var = jnp.mean(uc * uc, axis=-1, keepdims=True)
        parts.append(uc * lax.rsqrt(var + LN_EPS))
    un = jnp.concatenate(parts, axis=-1) * lng_ref[...]
    pq_ref[...] = _dot(un.astype(BF16), wc_ref[...]).astype(BF16)


def _pf_in(x, w_in_all, j, ln_g, wc, pool_w, pool_scale):
    blocks_per_tile = TM_PF // _POOL_HALO
    last_block = TOKENS // _POOL_HALO - 1
    row = lambda i: (i, 0)
    fixed = lambda i: (0, 0)
    return pl.pallas_call(
        _k_pf_in,
        out_shape=(jax.ShapeDtypeStruct((TOKENS, HALF), BF16),
                   jax.ShapeDtypeStruct((TOKENS, D_MODEL), BF16)),
        grid=(TOKENS // TM_PF,),
        in_specs=[pl.BlockSpec((TM_PF, D_MODEL), row),
                  pl.BlockSpec((_POOL_HALO, D_MODEL), lambda i: (jnp.maximum(i * blocks_per_tile - 1, 0), 0)),
                  pl.BlockSpec((_POOL_HALO, D_MODEL),
                               lambda i: (jnp.minimum((i + 1) * blocks_per_tile, last_block), 0)),
                  _resident((1, D_MODEL, D_MODEL), (j, 0, 0)),
                  pl.BlockSpec((1, HALF), fixed),
                  _resident((HALF, D_MODEL), (0, 0)),
                  pl.BlockSpec((4, POOL_DIM, POOL_DIM), lambda i: (0, 0, 0)),
                  pl.BlockSpec((1, HALF), fixed)],
        out_specs=(pl.BlockSpec((TM_PF, HALF), row),
                   pl.BlockSpec((TM_PF, D_MODEL), row)),
        scratch_shapes=[pltpu.VMEM((D_MODEL, D_MODEL), BF16),
                        pltpu.VMEM((TM_PF + 2 * _POOL_HALO, HALF), F32)],
        compiler_params=_params("arbitrary"),
        name="pf_in",
    )(x, x, x, w_in_all, ln_g, wc, pool_w, pool_scale)


_FOURIER_SCALE = 1.0 / float(np.sqrt(SEQ * FOURIER_DIM))


_DFT_SIDE = 64
_DFT_COL_CHUNK = 1024


_DFT_HALF = SEQ // 2
_REV_BLOCK = 256


def _k_fourier(tab_ref, e1_ref, e2_ref, rw_ref, sg_ref, pq_ref, fw_ref, yb_ref,
               cs_ref, ss_ref, pf_ref, qf_ref, z_ref):
    h_rows = _DFT_HALF

    @pl.when(pl.program_id(0) == 0)
    def _():
        def expand(v, e):
            hi = v.astype(BF16)
            lo = (v - hi.astype(F32)).astype(BF16)
            return _dot(hi, e) + _dot(lo, e)

        s = _DFT_SIDE
        for rb in range(h_rows // FOURIER_TR):
            rows = slice(rb * FOURIER_TR, (rb + 1) * FOURIER_TR)
            tab = tab_ref[rows, :]
            for c in range(h_rows // _DFT_COL_CHUNK):
                cols = slice(c * _DFT_COL_CHUNK, (c + 1) * _DFT_COL_CHUNK)
                ca = expand(tab[:, 0:s], e1_ref[:, cols])
                sa = expand(tab[:, s:2 * s], e1_ref[:, cols])
                cb = expand(tab[:, 2 * s:3 * s], e2_ref[:, cols])
                sb = expand(tab[:, 3 * s:4 * s], e2_ref[:, cols])
                cs_ref[rows, cols] = (ca * cb - sa * sb).astype(BF16)
                ss_ref[rows, cols] = (sa * cb + ca * sb).astype(BF16)

    def head_maps(y):
        return jnp.concatenate(
            [_dot(y[:, FOURIER_DIM * h:FOURIER_DIM * (h + 1)].astype(BF16), fw_ref[h])
             for h in range(FOURIER_HEADS)], axis=-1)

    def reversed_block(src_ref, blk):
        lo = _REV_BLOCK * (h_rows // _REV_BLOCK - 1 - blk)
        if blk == 0:
            tail = src_ref[lo:lo + _REV_BLOCK, :]
            win = jnp.concatenate([tail, jnp.zeros_like(tail)], axis=0)
        else:
            win = src_ref[lo:lo + 2 * _REV_BLOCK, :]
        return _dot(rw_ref[...], win)

    n_blocks = h_rows // _REV_BLOCK
    upper = pq_ref.at[h_rows:SEQ]
    for blk in range(n_blocks):
        rows = slice(blk * _REV_BLOCK, (blk + 1) * _REV_BLOCK)
        rev = reversed_block(upper, blk)
        low = pq_ref[rows, :].astype(F32)
        pf_ref[rows, :] = (low[:, :HALF] + rev[:, :HALF]).astype(BF16)
        qf_ref[rows, :] = (low[:, HALF:] - rev[:, HALF:]).astype(BF16)

    p_mid = pq_ref[h_rows:h_rows + 1, :HALF].astype(F32)
    for rb in range(h_rows // FOURIER_TR):
        rows = slice(rb * FOURIER_TR, (rb + 1) * FOURIER_TR)
        a = _dot(cs_ref[rows, :], pf_ref[...])
        bm = _dot(ss_ref[rows, :], qf_ref[...])
        j = lax.broadcasted_iota(I32, (FOURIER_TR, 1), 0) + rb * FOURIER_TR
        t = jnp.where((j & 1) == 0, 1.0, -1.0) * p_mid
        yb_ref[rows, :] = head_maps((a - bm + t) * _FOURIER_SCALE).astype(BF16)
        z_ref[rows, :] = head_maps((a + bm + t) * _FOURIER_SCALE).astype(BF16)

    y_mid = head_maps(_dot(sg_ref[...], pq_ref[:, :HALF]) * _FOURIER_SCALE)[0:1]
    for blk in range(n_blocks):
        up = reversed_block(z_ref, blk)
        if blk == 0:
            r = lax.broadcasted_iota(I32, (_REV_BLOCK, 1), 0)
            up = jnp.where(r == 0, y_mid, up)
        yb_ref[h_rows + blk * _REV_BLOCK:h_rows + (blk + 1) * _REV_BLOCK, :] = up.astype(BF16)


def _fourier(tables, e1, e2, rev_window, signs, pq, fw):
    h_rows = _DFT_HALF
    return pl.pallas_call(
        _k_fourier,
        out_shape=jax.ShapeDtypeStruct((TOKENS, HALF), BF16),
        grid=(BATCH,),
        in_specs=[_resident((h_rows, 4 * _DFT_SIDE), (0, 0)),
                  _resident((_DFT_SIDE, h_rows), (0, 0)),
                  _resident((_DFT_SIDE, h_rows), (0, 0)),
                  _resident((_REV_BLOCK, 2 * _REV_BLOCK), (0, 0)),
                  _resident((V7X_SUBLANES, SEQ), (0, 0)),
                  pl.BlockSpec((SEQ, D_MODEL), lambda b: (b, 0), pipeline_mode=pl.Buffered(1)),
                  pl.BlockSpec((4, FOURIER_DIM, FOURIER_DIM), lambda b: (0, 0, 0))],
        out_specs=pl.BlockSpec((SEQ, HALF), lambda b: (b, 0)),
        scratch_shapes=[pltpu.VMEM((h_rows, h_rows), BF16), pltpu.VMEM((h_rows, h_rows), BF16),
                        pltpu.VMEM((h_rows, HALF), BF16), pltpu.VMEM((h_rows, HALF), BF16),
                        pltpu.VMEM((h_rows, HALF), BF16)],
        compiler_params=_params("arbitrary"),
        name="pf_fourier",
    )(tables, e1, e2, rev_window, signs, pq, fw)


_HG_STAGE_COLS = 512


def _k_hg_in(x_ref, w_hbm, o_ref, wbf_ref, stage_ref, sem):
    @pl.when(pl.program_id(0) == 0)
    def _():
        n_chunks = HG_N_PROJ * D_MODEL // _HG_STAGE_COLS

        def chunk_copy(c):
            cols = pl.ds(c * _HG_STAGE_COLS, _HG_STAGE_COLS)
            return pltpu.make_async_copy(w_hbm.at[:, cols], stage_ref.at[c % 2], sem.at[c % 2])

        chunk_copy(0).start()
        for c in range(n_chunks):
            if c + 1 < n_chunks:
                chunk_copy(c + 1).start()
            chunk_copy(c).wait()
            wbf_ref[:, c * _HG_STAGE_COLS:(c + 1) * _HG_STAGE_COLS] = stage_ref[c % 2].astype(BF16)

    xb = x_ref[...].astype(BF16)
    for p in range(HG_N_PROJ):
        acc = _dot(xb, wbf_ref[:, p * D_MODEL:(p + 1) * D_MODEL])
        if p == 0:
            acc = _silu(acc)
        for h in range(HG_HEADS):
            o_ref[p, h] = acc[:, HG_DIM * h:HG_DIM * (h + 1)].astype(BF16)


def _hg_in(x, w_in):
    return pl.pallas_call(
        _k_hg_in,
        out_shape=jax.ShapeDtypeStruct((HG_N_PROJ, HG_HEADS, TOKENS, HG_DIM), BF16),
        grid=(TOKENS // TM,),
        in_specs=[pl.BlockSpec((TM, D_MODEL), lambda i: (i, 0)),
                  pl.BlockSpec(memory_space=pl.ANY)],
        out_specs=pl.BlockSpec((HG_N_PROJ, HG_HEADS, TM, HG_DIM), lambda i: (0, 0, i, 0)),
        scratch_shapes=[pltpu.VMEM((D_MODEL, HG_N_PROJ * D_MODEL), BF16),
                        pltpu.VMEM((2, D_MODEL, _HG_STAGE_COLS), F32),
                        pltpu.SemaphoreType.DMA((2,))],
        compiler_params=_params("arbitrary"),
        name="hg_in",
    )(x, w_in)


def _gla_prep(q, z, lbv, tri, rev):
    L, G = HG_CHUNK, GLA_GROUP
    f = lbv + (1.0 - lbv) * jax.nn.sigmoid(z.astype(F32))
    lf = jnp.log(f).reshape(G, L, HG_DIM)
    k3 = (1.0 - f).reshape(G, L, HG_DIM)
    q3 = q.astype(F32).reshape(G, L, HG_DIM)
    hi32 = lax.bitcast_convert_type(
        lax.bitcast_convert_type(lf, jnp.uint32) & jnp.uint32(0xFFFF0000), F32)
    lf_hi = hi32.astype(BF16)
    lf_lo = (lf - hi32).astype(BF16)
    b2 = jnp.einsum('glm,gmk->glk', jnp.broadcast_to(tri, (G, L, L)),
                    jnp.concatenate([lf_hi, lf_lo], axis=-1), preferred_element_type=F32)
    b = b2[..., :HG_DIM] + b2[..., HG_DIM:]
    mid = L // 2 if not rev else L - 1 - L // 2
    end = L - 1 if not rev else 0
    b_ref = b[:, mid:mid + 1, :]
    b_last = b[:, end:end + 1, :]
    e_pos = jnp.exp(b - b_ref)
    qt32 = q3 * e_pos
    kt32 = k3 * (1.0 / e_pos)
    ku = kt32 * jnp.exp(b_last - b_ref)
    qe = qt32 * jnp.exp(b_ref)
    dec = jnp.exp(b_last)
    return qt32.astype(BF16), kt32.astype(BF16), ku.astype(BF16), qe.astype(BF16), dec


def _gla_mix(qt, kt, ku, qe, dec, v, st, rev):
    L, G = HG_CHUNK, GLA_GROUP
    v3 = v.reshape(G, L, HG_DIM)
    sc = jnp.einsum('glk,gmk->glm', qt, kt, preferred_element_type=F32)
    li = lax.broadcasted_iota(I32, (L, L), 0)
    mi = lax.broadcasted_iota(I32, (L, L), 1)
    causal = (li >= mi) if not rev else (li <= mi)
    sc = jnp.where(causal[None], sc, 0.0).astype(BF16)
    o_intra = jnp.einsum('glm,gmv->glv', sc, v3, preferred_element_type=F32)
    u_t = jnp.einsum('glv,glk->gvk', v3, ku, preferred_element_type=F32)
    outs = [None] * G
    for n in (range(G) if not rev else range(G - 1, -1, -1)):
        o_inter = lax.dot_general(qe[n], st.astype(BF16), _NT, preferred_element_type=F32)
        outs[n] = o_intra[n] + o_inter
        st = dec[n] * st + u_t[n]
    return jnp.concatenate(outs, axis=0), st


_GLA_N_OPERANDS = 5


def _k_gla(q_ref, v_ref, zf_ref, zb_ref, g_ref, lb_ref, ng_ref, lt_ref, ut_ref, o_ref, of_ref, ob_ref,
           *operand_refs):
    n_groups = SEQ // GLA_ROWS
    per_slot = 2 * _GLA_N_OPERANDS
    slots = [[operand_refs[s * per_slot + d * _GLA_N_OPERANDS:s * per_slot + (d + 1) * _GLA_N_OPERANDS]
              for d in range(2)] for s in range(2)]
    directions = ((zf_ref, lt_ref, False), (zb_ref, ut_ref, True))

    def group_rows(t, rev):
        gi = t if not rev else n_groups - 1 - t
        return slice(gi * GLA_ROWS, (gi + 1) * GLA_ROWS)

    def prep(slot, t):
        for d, (z_ref, tri_ref, rev) in enumerate(directions):
            rows = group_rows(t, rev)
            vals = _gla_prep(q_ref[0, 0, rows, :], z_ref[0, 0, rows, :], lb_ref[d, 0], tri_ref[...], rev)
            for ref, val in zip(slots[slot][d], vals):
                ref[...] = val

    def mix(slot, t, states):
        out_refs = (of_ref, ob_ref)
        new_states = []
        for d, (_, _, rev) in enumerate(directions):
            rows = group_rows(t, rev)
            o, st = _gla_mix(*[ref[...] for ref in slots[slot][d]], v_ref[0, 0, rows, :], states[d], rev)
            out_refs[d][rows, :] = o
            new_states.append(st)
        return new_states

    states = [jnp.zeros((HG_DIM, HG_DIM), F32)] * 2
    prep(0, 0)
    for t in range(n_groups):
        if t + 1 < n_groups:
            prep((t + 1) % 2, t + 1)
        states = mix(t % 2, t, states)

    def finish(gi, c):
        rows = pl.ds(pl.multiple_of(gi * GLA_ROWS, GLA_ROWS), GLA_ROWS)
        o = of_ref[rows, :] + ob_ref[rows, :]
        o = o * lax.rsqrt(jnp.mean(o * o, axis=-1, keepdims=True) + LN_EPS)
        o = o * ng_ref[...] * _silu(g_ref[0, 0, rows, :].astype(F32))
        o_ref[0, rows, :] = o.astype(BF16)
        return c

    lax.fori_loop(0, n_groups, finish, 0)


def _gla(proj, lb, norm_g, tril, triu):
    def pspec(p):
        return pl.BlockSpec((1, 1, SEQ, HG_DIM), lambda b, h: (p, h, b, 0))
    fixed = lambda b, h: (0, 0)
    return pl.pallas_call(
        _k_gla,
        out_shape=jax.ShapeDtypeStruct((HG_HEADS, TOKENS, HG_DIM), BF16),
        grid=(BATCH, HG_HEADS),
        in_specs=[pspec(0), pspec(1), pspec(2), pspec(3), pspec(4),
                  pl.BlockSpec((2, 1, 1, HG_DIM), lambda b, h: (0, h, 0, 0)),
                  pl.BlockSpec((1, HG_DIM), fixed),
                  pl.BlockSpec((HG_CHUNK, HG_CHUNK), fixed),
                  pl.BlockSpec((HG_CHUNK, HG_CHUNK), fixed)],
        out_specs=pl.BlockSpec((1, SEQ, HG_DIM), lambda b, h: (h, b, 0)),
        scratch_shapes=[pltpu.VMEM((SEQ, HG_DIM), F32), pltpu.VMEM((SEQ, HG_DIM), F32)]
        + 4 * ([pltpu.VMEM((GLA_GROUP, HG_CHUNK, HG_DIM), BF16)] * (_GLA_N_OPERANDS - 1)
               + [pltpu.VMEM((GLA_GROUP, 1, HG_DIM), F32)]),
        compiler_params=_params("parallel", "parallel"),
        name="hg_gla",
    )(proj, proj, proj, proj, proj, lb, norm_g, tril, triu)


def _k_mm(x_ref, w_ref, o_ref):
    o_ref[...] = _dot(x_ref[...].astype(BF16), w_ref[0].astype(BF16)).astype(o_ref.dtype)


def _kv_proj(mem2d, wkv_all, l):
    rows, cols = mem2d.shape[0], wkv_all.shape[2]
    return pl.pallas_call(
        _k_mm,
        out_shape=jax.ShapeDtypeStruct((rows, cols), BF16),
        grid=(cols // D_MODEL, rows // TM),
        in_specs=[pl.BlockSpec((TM, D_MODEL), lambda j, i: (i, 0)),
                  pl.BlockSpec((1, D_MODEL, D_MODEL), lambda j, i: (l, 0, j))],
        out_specs=pl.BlockSpec((TM, D_MODEL), lambda j, i: (i, j)),
        compiler_params=_params("parallel", "parallel"),
        name="xa_kv",
    )(mem2d, wkv_all)


ROW_WORDS = D_MODEL // 2
ROW_SLABS = ROW_WORDS // V7X_LANES
U32 = jnp.uint32
_HI_MASK = 0xFFFF0000


def _to_row_tiles(ref, x):
    n = x.shape[0]
    as_bits = lambda v: lax.bitcast_convert_type(v.astype(BF16).astype(F32), U32)
    words = (as_bits(x[:, :ROW_WORDS]) >> 16) | (as_bits(x[:, ROW_WORDS:]) & U32(_HI_MASK))
    for s in range(ROW_SLABS):
        ref[pl.ds(s, n, stride=ROW_SLABS), :] = words[:, V7X_LANES * s:V7X_LANES * (s + 1)]


def _from_row_tiles(ref):
    n = ref.shape[0] // ROW_SLABS
    words = jnp.concatenate([ref[pl.ds(s, n, stride=ROW_SLABS), :] for s in range(ROW_SLABS)], axis=-1)
    lo = lax.bitcast_convert_type(words << 16, F32)
    hi = lax.bitcast_convert_type(words & U32(_HI_MASK), F32)
    return jnp.concatenate([lo, hi], axis=-1)


def _first_argmax_rows(v, n_rows):
    m = jnp.max(v, axis=0, keepdims=True)
    rows = lax.broadcasted_iota(I32, v.shape, 0)
    idx = jnp.min(jnp.where(v == m, rows, n_rows), axis=0, keepdims=True)
    return m, idx


def _route(lt):
    gl = lt[0:MOE_GROUPS, :]
    gmax, gi = _first_argmax_rows(gl, MOE_GROUPS)
    p_grp = 1.0 / jnp.sum(jnp.exp(gl - gmax), axis=0, keepdims=True)
    sel = lt[MOE_GROUPS:MOE_GROUPS + MOE_PER_GROUP, :]
    for g in range(1, MOE_GROUPS):
        lo = MOE_GROUPS + MOE_PER_GROUP * g
        sel = jnp.where(gi == g, lt[lo:lo + MOE_PER_GROUP, :], sel)
    m1, i1 = _first_argmax_rows(sel, MOE_PER_GROUP)
    rows = lax.broadcasted_iota(I32, sel.shape, 0)
    m2, i2 = _first_argmax_rows(jnp.where(rows == i1, -jnp.inf, sel), MOE_PER_GROUP)
    e2 = jnp.exp(m2 - m1)
    g1 = p_grp / (1.0 + e2)
    return (gi * MOE_PER_GROUP + i1, gi * MOE_PER_GROUP + i2), (g1, g1 * e2)


def _rank_in_expert(e0, e1, strict_upper, carry_ref):
    rows = lax.broadcasted_iota(I32, (MOE_EXPERTS, e0.shape[1]), 0)
    oh0 = rows == e0
    oh1 = rows == e1
    oh = jnp.where(oh0 | oh1, 1.0, 0.0)
    before = _dot(oh.astype(BF16), strict_upper) + carry_ref[:, 0:1]
    r0 = jnp.sum(jnp.where(oh0, before, 0.0), axis=0, keepdims=True).astype(I32)
    r1 = jnp.sum(jnp.where(oh1, before, 0.0), axis=0, keepdims=True).astype(I32)
    carry_ref[...] = carry_ref[...] + jnp.sum(oh, axis=1, keepdims=True)
    return r0, r1


_BLOCK_SUBTILES = 1


def _k_block(heads_major, *refs):
    n_mix = 1 if heads_major else 2
    mix_refs = refs[:n_mix]
    (wm_ref, x_ref, g0_ref, b0_ref, wq_ref, kv_ref, wo_ref, g_ref, b_ref, wr_ref, br_ref, tri_ref,
     o_ref, o3_ref, e_ref, gate_ref, rank_ref, cnt_ref,
     wmbf_ref, wqbf_ref, wobf_ref, carry_ref) = refs[n_mix:]
    _cast_weight_once(wm_ref, wmbf_ref)
    _cast_weight_once(wq_ref, wqbf_ref)
    _cast_weight_once(wo_ref, wobf_ref)

    @pl.when(pl.program_id(0) == 0)
    def _():
        carry_ref[...] = jnp.zeros_like(carry_ref)

    sub = TM // _BLOCK_SUBTILES
    routed = []
    for t in range(_BLOCK_SUBTILES):
        rows = slice(t * sub, (t + 1) * sub)
        if heads_major:
            a = jnp.concatenate([mix_refs[0][h, rows, :] for h in range(HG_HEADS)], axis=-1)
        else:
            a = jnp.concatenate([mix_refs[0][rows, :], mix_refs[1][rows, :]], axis=-1)
        x = _post_ln(x_ref[rows, :], _dot(a, wmbf_ref[...]), g0_ref[...], b0_ref[...])
        q = _dot(x.astype(BF16), wqbf_ref[...]).astype(BF16)
        outs = []
        for h in range(XA_HEADS):
            cols = slice(XA_HEAD_DIM * h, XA_HEAD_DIM * (h + 1))
            kh = kv_ref[:, cols]
            vh = kv_ref[:, D_MODEL + XA_HEAD_DIM * h:D_MODEL + XA_HEAD_DIM * (h + 1)]
            s = lax.dot_general(q[:, cols], kh, _NT, preferred_element_type=F32) * (XA_HEAD_DIM ** -0.5)
            e = jnp.exp(s - jnp.max(s, axis=-1, keepdims=True))
            p = e / jnp.sum(e, axis=-1, keepdims=True)
            outs.append(_dot(p.astype(BF16), vh))
        o = jnp.concatenate(outs, axis=-1).astype(BF16)
        y = _post_ln(x, _dot(o, wobf_ref[...]), g_ref[...], b_ref[...])
        o_ref[rows, :] = y
        _to_row_tiles(o3_ref.at[pl.ds(t * sub * ROW_SLABS, sub * ROW_SLABS)], y)
        lt = lax.dot_general(wr_ref[...], y.astype(BF16), _NT, preferred_element_type=F32) + br_ref[...]
        routed.append(_route(lt))
    e0, e1, gate0, gate1 = [jnp.concatenate([r[a][b] for r in routed], axis=-1)
                            for a in range(2) for b in range(2)]
    r0, r1 = _rank_in_expert(e0, e1, tri_ref[...], carry_ref)
    e_ref[0:1, :], e_ref[1:2, :] = e0, e1
    gate_ref[0:1, :], gate_ref[1:2, :] = gate0, gate1
    rank_ref[0:1, :], rank_ref[1:2, :] = r0, r1
    cnt_ref[...] = carry_ref[...]


def _block(mix, heads_major, w_mix_all, j, x, g0, b0, wq_all, kv, wo_all, l, g1, b1, wr, br, strict_upper):
    row = lambda i: (i, 0)
    fixed = lambda i: (0, 0)
    tok = pl.BlockSpec((MOE_TOPK, TM), lambda i: (0, i))
    tiles_per_seq = SEQ // TM
    if heads_major:
        mix_specs = [pl.BlockSpec((HG_HEADS, TM, HG_DIM), lambda i: (0, i, 0))]
    else:
        mix_specs = [pl.BlockSpec((TM, HALF), row), pl.BlockSpec((TM, HALF), row)]
    return pl.pallas_call(
        functools.partial(_k_block, heads_major),
        out_shape=(jax.ShapeDtypeStruct((TOKENS, D_MODEL), F32),
                   jax.ShapeDtypeStruct((TOKENS * ROW_SLABS, V7X_LANES), U32),
                   jax.ShapeDtypeStruct((MOE_TOPK, TOKENS), I32),
                   jax.ShapeDtypeStruct((MOE_TOPK, TOKENS), F32),
                   jax.ShapeDtypeStruct((MOE_TOPK, TOKENS), I32),
                   jax.ShapeDtypeStruct((MOE_EXPERTS, V7X_LANES), F32)),
        grid=(TOKENS // TM,),
        in_specs=mix_specs + [
            _resident((1, D_MODEL, D_MODEL), (j, 0, 0)),
            pl.BlockSpec((TM, D_MODEL), row),
            pl.BlockSpec((1, D_MODEL), fixed), pl.BlockSpec((1, D_MODEL), fixed),
            _resident((1, D_MODEL, D_MODEL), (l, 0, 0)),
            pl.BlockSpec((MEM_LEN, 2 * D_MODEL), lambda i: (i // tiles_per_seq, 0)),
            _resident((1, D_MODEL, D_MODEL), (l, 0, 0)),
            pl.BlockSpec((1, D_MODEL), fixed), pl.BlockSpec((1, D_MODEL), fixed),
            _resident((ROUTER_ROWS, D_MODEL), (0, 0)),
            _resident((ROUTER_ROWS, 1), (0, 0)),
            _resident((TM, TM), (0, 0))],
        out_specs=(pl.BlockSpec((TM, D_MODEL), row),
                   pl.BlockSpec((TM * ROW_SLABS, V7X_LANES), row),
                   tok, tok, tok,
                   pl.BlockSpec((MOE_EXPERTS, V7X_LANES), fixed)),
        scratch_shapes=[pltpu.VMEM((D_MODEL, D_MODEL), BF16), pltpu.VMEM((D_MODEL, D_MODEL), BF16),
                        pltpu.VMEM((D_MODEL, D_MODEL), BF16), pltpu.VMEM((MOE_EXPERTS, V7X_LANES), F32)],
        compiler_params=_params("arbitrary"),
        name="attn_block",
    )(*mix, w_mix_all, x, g0, b0, wq_all, kv, wo_all, g1, b1, wr, br, strict_upper)


def _k_pos(e_ref, rank_ref, off_ref, pos_ref):
    rows = lax.broadcasted_iota(I32, (MOE_EXPERTS, TM_ROUTE), 0)
    for k in range(MOE_TOPK):
        start = jnp.sum(jnp.where(rows == e_ref[k:k + 1, :], off_ref[...], 0), axis=0, keepdims=True)
        pos_ref[k:k + 1, :] = start + rank_ref[k:k + 1, :]


def _positions(eidx, rank, offsets):
    tok = pl.BlockSpec((MOE_TOPK, TM_ROUTE), lambda i: (0, i))
    return pl.pallas_call(
        _k_pos,
        out_shape=jax.ShapeDtypeStruct((MOE_TOPK, TOKENS), I32),
        grid=(TOKENS // TM_ROUTE,),
        in_specs=[tok, tok, pl.BlockSpec((MOE_EXPERTS, 1), lambda i: (0, 0))],
        out_specs=tok,
        compiler_params=_params("parallel"),
        name="moe_pos",
    )(eidx, rank, offsets)


_ISSUE_BATCH = 8


def _row_copy(src_ref, src_row, dst_ref, dst_row, sem):
    src = pl.ds(pl.multiple_of(src_row * ROW_SLABS, ROW_SLABS), ROW_SLABS)
    dst = pl.ds(pl.multiple_of(dst_row * ROW_SLABS, ROW_SLABS), ROW_SLABS)
    return pltpu.make_async_copy(src_ref.at[src], dst_ref.at[dst], sem)


_DISPATCH_SLOTS = 4


def _k_dispatch(pos_ref, ends_ref, x_ref, xs_ref, zero_ref, stage_ref, row_sems, load_sems, zsem):
    step = pl.program_id(0)
    n_steps = pl.num_programs(0)
    block_rows = TM_PERM * ROW_SLABS

    def load(tile):
        slot = lax.rem(tile, _DISPATCH_SLOTS)
        src = x_ref.at[pl.ds(pl.multiple_of(tile * block_rows, block_rows), block_rows)]
        return pltpu.make_async_copy(src, stage_ref.at[slot], load_sems.at[slot])

    def drain_rows(tile):
        slot = lax.rem(tile, _DISPATCH_SLOTS)
        for _ in range(MOE_TOPK):
            pltpu.make_async_copy(stage_ref.at[slot], xs_ref.at[pl.ds(0, block_rows)],
                                  row_sems.at[slot]).wait()

    @pl.when(step == 0)
    def _():
        load(0).start()
        pl.when(n_steps > 1)(load(1).start)
        zero_ref[...] = jnp.zeros_like(zero_ref)

        def last_tile(e):
            seg_start = ends_ref[e - 1] if e > 0 else 0
            start = ends_ref[e] - TE
            return ends_ref[e] > seg_start, zero_fill(start)

        def unused_tile(t):
            start = ends_ref[MOE_EXPERTS - 1] + t * TE
            return start < SORTED_ROWS, zero_fill(start)

        def zero_fill(start_row):
            rows = pl.ds(pl.multiple_of(start_row * ROW_SLABS, TE * ROW_SLABS), TE * ROW_SLABS)
            return pltpu.make_async_copy(zero_ref, xs_ref.at[rows], zsem)

        fills = [last_tile(e) for e in range(MOE_EXPERTS)] + [unused_tile(t) for t in range(MOE_EXPERTS)]
        for needed, copy in fills:
            pl.when(needed)(copy.start)
        for needed, copy in fills:
            pl.when(needed)(copy.wait)

    pl.when(step >= 2)(lambda: drain_rows(step - 2))
    pl.when(step + 2 < n_steps)(lambda: load(step + 2).start())
    load(step).wait()

    slot = lax.rem(step, _DISPATCH_SLOTS)
    base = step * TM_PERM

    def body(jb, c):
        j0 = jb * _ISSUE_BATCH
        dst = [[pos_ref[k * TOKENS + base + j0 + u] for k in range(MOE_TOPK)]
               for u in range(_ISSUE_BATCH)]
        for u in range(_ISSUE_BATCH):
            for k in range(MOE_TOPK):
                _row_copy(stage_ref.at[slot], j0 + u, xs_ref, dst[u][k],
                          row_sems.at[slot]).start(priority=k)
        return c

    lax.fori_loop(0, TM_PERM // _ISSUE_BATCH, body, 0)

    @pl.when(step == n_steps - 1)
    def _():
        pl.when(step >= 1)(lambda: drain_rows(step - 1))
        drain_rows(step)


def _dispatch(pos_flat, ends, x):
    return pl.pallas_call(
        _k_dispatch,
        out_shape=jax.ShapeDtypeStruct((SORTED_ROWS * ROW_SLABS, V7X_LANES), U32),
        grid_spec=pltpu.PrefetchScalarGridSpec(
            num_scalar_prefetch=2,
            grid=(TOKENS // TM_PERM,),
            in_specs=[pl.BlockSpec(memory_space=pl.ANY)],
            out_specs=pl.BlockSpec(memory_space=pl.ANY),
            scratch_shapes=[pltpu.VMEM((TE * ROW_SLABS, V7X_LANES), U32),
                            pltpu.VMEM((_DISPATCH_SLOTS, TM_PERM * ROW_SLABS, V7X_LANES), U32),
                            pltpu.SemaphoreType.DMA((_DISPATCH_SLOTS,)),
                            pltpu.SemaphoreType.DMA((_DISPATCH_SLOTS,)),
                            pltpu.SemaphoreType.DMA],
        ),
        compiler_params=_params("arbitrary"),
        name="moe_dispatch",
    )(pos_flat, ends, x)


_EXPERT_SUBTILES = 1


def _k_experts(te_ref, na_ref, xs_ref, wg_ref, wu_ref, wd_ref, ys_ref, wgubf_ref, wdbf_ref):
    i = pl.program_id(0)
    active = i < na_ref[0]
    new_expert = jnp.logical_or(i == 0, te_ref[jnp.maximum(i - 1, 0)] != te_ref[i])

    @pl.when(jnp.logical_and(active, new_expert))
    def _():
        wgubf_ref[:, :MOE_HIDDEN] = wg_ref[0, 0].astype(BF16)
        wgubf_ref[:, MOE_HIDDEN:] = wu_ref[0, 0].astype(BF16)
        wdbf_ref[...] = wd_ref[0, 0].astype(BF16)

    @pl.when(active)
    def _():
        sub_rows = TE // _EXPERT_SUBTILES * ROW_SLABS
        for t in range(_EXPERT_SUBTILES):
            rows = pl.ds(t * sub_rows, sub_rows)
            xb = _from_row_tiles(xs_ref.at[rows]).astype(BF16)
            gu = _dot(xb, wgubf_ref[...])
            h = _silu(gu[:, :MOE_HIDDEN]) * gu[:, MOE_HIDDEN:]
            _to_row_tiles(ys_ref.at[rows], _dot(h.astype(BF16), wdbf_ref[...]))

    @pl.when(jnp.logical_not(active))
    def _():
        ys_ref[...] = jnp.zeros_like(ys_ref)


def _experts(tile_expert, n_active, xs, wg_all, wu_all, wd_all, l):
    def xs_index(i, te, na):
        return (jnp.minimum(i, na[0] - 1), 0)

    def w_index(i, te, na):
        return (l, te[i], 0, 0)

    return pl.pallas_call(
        _k_experts,
        out_shape=jax.ShapeDtypeStruct((SORTED_ROWS * ROW_SLABS, V7X_LANES), U32),
        grid_spec=pltpu.PrefetchScalarGridSpec(
            num_scalar_prefetch=2,
            grid=(N_EXPERT_TILES,),
            in_specs=[pl.BlockSpec((TE * ROW_SLABS, V7X_LANES), xs_index),
                      pl.BlockSpec((1, 1, D_MODEL, MOE_HIDDEN), w_index),
                      pl.BlockSpec((1, 1, D_MODEL, MOE_HIDDEN), w_index),
                      pl.BlockSpec((1, 1, MOE_HIDDEN, D_MODEL), w_index)],
            out_specs=pl.BlockSpec((TE * ROW_SLABS, V7X_LANES), lambda i, te, na: (i, 0)),
            scratch_shapes=[pltpu.VMEM((D_MODEL, 2 * MOE_HIDDEN), BF16),
                            pltpu.VMEM((MOE_HIDDEN, D_MODEL), BF16)],
        ),
        compiler_params=_params("arbitrary"),
        name="moe_experts",
    )(tile_expert, n_active, xs, wg_all, wu_all, wd_all)


def _k_combine(pos_ref, ys_ref, gt_ref, x_ref, g_ref, b_ref, o_ref, buf_ref, sem):
    step = pl.program_id(0)

    def start_gather(tile, slot):
        base = tile * TM_COMB

        def body(jb, c):
            j0 = jb * _ISSUE_BATCH
            src = [[pos_ref[k * TOKENS + base + j0 + u] for k in range(MOE_TOPK)]
                   for u in range(_ISSUE_BATCH)]
            for u in range(_ISSUE_BATCH):
                for k in range(MOE_TOPK):
                    _row_copy(ys_ref, src[u][k], buf_ref.at[slot, k], j0 + u,
                              sem.at[slot]).start(priority=k)
            return c

        lax.fori_loop(0, TM_COMB // _ISSUE_BATCH, body, 0)

    slot = lax.rem(step, 2)
    pl.when(step == 0)(lambda: start_gather(0, 0))
    pl.when(step + 1 < pl.num_programs(0))(lambda: start_gather(step + 1, 1 - slot))
    for k in range(MOE_TOPK):
        pltpu.make_async_copy(ys_ref.at[pl.ds(0, TM_COMB * ROW_SLABS)], buf_ref.at[slot, k],
                              sem.at[slot]).wait()
    h = (gt_ref[:, 0:1] * _from_row_tiles(buf_ref.at[slot, 0])
         + gt_ref[:, 1:2] * _from_row_tiles(buf_ref.at[slot, 1]))
    o_ref[...] = _post_ln(x_ref[...], h, g_ref[...], b_ref[...])


def _combine(pos_flat, ys, gates_t, x, g, b):
    row = lambda i, pos: (i, 0)
    fixed = lambda i, pos: (0, 0)
    return pl.pallas_call(
        _k_combine,
        out_shape=jax.ShapeDtypeStruct((TOKENS, D_MODEL), F32),
        grid_spec=pltpu.PrefetchScalarGridSpec(
            num_scalar_prefetch=1,
            grid=(TOKENS // TM_COMB,),
            in_specs=[pl.BlockSpec(memory_space=pl.ANY),
                      pl.BlockSpec((TM_COMB, MOE_TOPK), row),
                      pl.BlockSpec((TM_COMB, D_MODEL), row),
                      pl.BlockSpec((1, D_MODEL), fixed), pl.BlockSpec((1, D_MODEL), fixed)],
            out_specs=pl.BlockSpec((TM_COMB, D_MODEL), row),
            scratch_shapes=[pltpu.VMEM((2, MOE_TOPK, TM_COMB * ROW_SLABS, V7X_LANES), U32),
                            pltpu.SemaphoreType.DMA((2,))],
        ),
        compiler_params=_params("arbitrary"),
        name="moe_combine",
    )(pos_flat, ys, gates_t, x, g, b)


def _router_params(w_group, b_group, w_expert, b_expert):
    pad = ROUTER_ROWS - MOE_GROUPS - MOE_EXPERTS
    wr = jnp.concatenate([w_group.T, w_expert.T, jnp.zeros((pad, D_MODEL), F32)], axis=0)
    br = jnp.concatenate([b_group, b_expert, jnp.zeros((pad,), F32)])[:, None]
    return wr.astype(BF16), br.astype(F32)


def _moe(x, x_tiles, eidx, gates, rank, cnt, w_gate_all, w_up_all, w_down_all, l, g, b):
    counts = cnt[:, 0].astype(I32)
    padded = ((counts + TE - 1) // TE) * TE
    ends = jnp.cumsum(padded)
    offsets = ends - padded
    tile_start = jnp.arange(N_EXPERT_TILES, dtype=I32) * TE
    tile_expert = jnp.sum((tile_start[:, None] >= ends[None, :]).astype(I32), axis=1)
    tile_expert = jnp.minimum(tile_expert, MOE_EXPERTS - 1)
    n_active = (ends[-1:] // TE).astype(I32)
    pos = _positions(eidx, rank, offsets[:, None])
    pos_flat = pos.reshape(-1)
    xs = _dispatch(pos_flat, ends.astype(I32), x_tiles)
    ys = _experts(tile_expert, n_active, xs, w_gate_all, w_up_all, w_down_all, l)
    return _combine(pos_flat, ys, gates.T, x, g, b)


def _channel_dft_matrix():
    c = np.arange(FOURIER_DIM)
    ang = 2.0 * np.pi * ((c[:, None] * c[None, :]) % FOURIER_DIM) / FOURIER_DIM
    wc = np.zeros((HALF, D_MODEL), np.float32)
    for h in range(FOURIER_HEADS):
        rows = slice(FOURIER_DIM * h, FOURIER_DIM * (h + 1))
        wc[rows, FOURIER_DIM * h:FOURIER_DIM * (h + 1)] = np.cos(ang)
        wc[rows, HALF + FOURIER_DIM * h:HALF + FOURIER_DIM * (h + 1)] = np.sin(ang)
    return jnp.asarray(wc, BF16)


def _sequence_dft_tables():
    side = _DFT_SIDE
    j = np.arange(_DFT_HALF, dtype=np.int64)[:, None]
    k = np.arange(side, dtype=np.int64)[None, :]
    ang_a = ((j * k) % side).astype(np.float64) * (2.0 * np.pi / side)
    ang_b = ((j * k) % SEQ).astype(np.float64) * (2.0 * np.pi / SEQ)
    tables = np.concatenate([np.cos(ang_a), np.sin(ang_a), np.cos(ang_b), np.sin(ang_b)], axis=1)
    col = np.arange(_DFT_HALF)
    e1 = (col[None, :] // side == np.arange(side)[:, None]).astype(np.float32)
    e2 = (col[None, :] % side == np.arange(side)[:, None]).astype(np.float32)
    r = np.arange(_REV_BLOCK)
    rev_window = (np.arange(2 * _REV_BLOCK)[None, :] == (_REV_BLOCK - r)[:, None]).astype(np.float32)
    signs = np.zeros((V7X_SUBLANES, SEQ), np.float32)
    signs[0] = 1.0 - 2.0 * (np.arange(SEQ) % 2)
    return (jnp.asarray(tables, F32), jnp.asarray(e1, BF16), jnp.asarray(e2, BF16),
            jnp.asarray(rev_window, BF16), jnp.asarray(signs, BF16))


def kernel(x, mem, pf_w_in, pf_pool_w, pf_pool_scale, pf_fourier_ln_g, pf_fourier_w, pf_w_out, hg_w_in, hg_lower_bounds, hg_norm_g, hg_w_out, xa_wq, xa_wkv, xa_wo, moe_w_group, moe_b_group, moe_w_expert, moe_b_expert, moe_w_gate, moe_w_up, moe_w_down, ln_g, ln_b):
    bf = lambda a: a.astype(BF16)
    xt = x.reshape(TOKENS, D_MODEL)
    mem2d = mem.reshape(BATCH * MEM_LEN, D_MODEL)
    lb_all = jnp.cumsum(jax.nn.softmax(hg_lower_bounds.astype(F32), axis=0), axis=0)
    lb_all = lb_all - lb_all[:1]
    tril = jnp.asarray(np.tril(np.ones((HG_CHUNK, HG_CHUNK), np.float32)), BF16)
    triu = jnp.asarray(np.triu(np.ones((HG_CHUNK, HG_CHUNK), np.float32)), BF16)
    strict_upper = jnp.asarray(np.triu(np.ones((TM, TM), np.float32), k=1), BF16)

    for l in range(DEPTH):
        j = l // 2
        lng = lambda s: ln_g[l, s][None, :]
        lnb = lambda s: ln_b[l, s][None, :]
        if l % 2 == 0:
            dft_consts = _sequence_dft_tables()
            ya, pq = _pf_in(xt, pf_w_in, j, pf_fourier_ln_g[j].reshape(1, HALF), _channel_dft_matrix(),
                            bf(pf_pool_w[j]), pf_pool_scale[j][None, :])
            yb = _fourier(*dft_consts, pq, bf(pf_fourier_w[j]))
            mix, heads_major, w_mix = (ya, yb), False, pf_w_out
        else:
            proj = _hg_in(xt, hg_w_in[j])
            lb = lb_all[l].reshape(2, HG_HEADS, 1, HG_DIM)
            og = _gla(proj, lb, hg_norm_g[j][None, :], tril, triu)
            mix, heads_major, w_mix = (og,), True, hg_w_out
        kv = _kv_proj(mem2d, xa_wkv, l)
        wr, br = _router_params(moe_w_group[l], moe_b_group[l], moe_w_expert[l], moe_b_expert[l])
        xt, xt_tiles, eidx, gates, rank, cnt = _block(
            mix, heads_major, w_mix, j, xt, lng(0), lnb(0), xa_wq, kv, xa_wo, l, lng(1), lnb(1),
            wr, br, strict_upper)
        xt = _moe(xt, xt_tiles, eidx, gates, rank, cnt, moe_w_gate, moe_w_up, moe_w_down, l,
                  lng(2), lnb(2))
    return xt.reshape(BATCH, SEQ, D_MODEL)
```

```python
import functools

import jax
import jax.numpy as jnp
import numpy as np
from jax import lax
from jax.experimental import pallas as pl
from jax.experimental.pallas import tpu as pltpu

F32 = jnp.float32
BF16 = jnp.bfloat16
I32 = jnp.int32

D_MODEL = 1024
BATCH = 8
SEQ = 4096
DEPTH = 2
TOKENS = BATCH * SEQ
MEM_LEN = 256
HALF = D_MODEL // 2
POOL_WINDOWS = (2, 4, 8, 16)
POOL_DIM = 128
FOURIER_HEADS = 4
FOURIER_DIM = 128
HG_HEADS = 8
HG_DIM = 128
HG_N_PROJ = 5
HG_CHUNK = 64
XA_HEADS = 4
XA_HEAD_DIM = 256
MOE_GROUPS = 4
MOE_PER_GROUP = 8
MOE_EXPERTS = 32
MOE_TOPK = 2
MOE_HIDDEN = 512
DN_ALPHA = (2.0 * DEPTH) ** 0.25
LN_EPS = 1e-5

V7X_LANES = 128
V7X_SUBLANES = 8
V7X_VMEM_LIMIT_BYTES = 52 * 1024 * 1024

TM = 512
TM_ROUTE = 1024
TM_PERM = 2048
TM_COMB = 512
TE = 512
SORTED_ROWS = TOKENS * MOE_TOPK + MOE_EXPERTS * TE
N_EXPERT_TILES = SORTED_ROWS // TE
GLA_GROUP = 16
GLA_ROWS = GLA_GROUP * HG_CHUNK
FOURIER_TR = 512
ROUTER_ROWS = 40

_NT = (((1,), (1,)), ((), ()))


def _params(*sem):
    return pltpu.CompilerParams(dimension_semantics=sem, vmem_limit_bytes=V7X_VMEM_LIMIT_BYTES)


def _dot(a, b):
    return jnp.dot(a, b, preferred_element_type=F32)


def _layer_norm(y, g, b):
    mu = jnp.mean(y, axis=-1, keepdims=True)
    yc = y - mu
    var = jnp.mean(yc * yc, axis=-1, keepdims=True)
    return yc * lax.rsqrt(var + LN_EPS) * g + b


def _post_ln(x, h, g, b):
    return _layer_norm(DN_ALPHA * x + h, g, b)


def _silu(x):
    return x * jax.nn.sigmoid(x)


def _resident(shape, index):
    return pl.BlockSpec(shape, lambda *_: index, pipeline_mode=pl.Buffered(1))


def _cast_weight_once(w_ref, wbf_ref):
    @pl.when(pl.program_id(0) == 0)
    def _():
        wbf_ref[...] = w_ref[0].astype(BF16)


TM_PF = 1024
_POOL_HALO = 8


def _k_pf_in(x_ref, xp_ref, xn_ref, w_ref, lng_ref, wc_ref, pw_ref, ps_ref, ya_ref, pq_ref, wbf_ref, ext_ref):
    _cast_weight_once(w_ref, wbf_ref)
    tiles_per_seq = SEQ // TM_PF
    j = lax.rem(pl.program_id(0), tiles_per_seq)
    u = _dot(x_ref[...].astype(BF16), wbf_ref[...])
    halo = jnp.concatenate([xp_ref[...], xn_ref[...]], axis=0).astype(BF16)
    uh = _dot(halo, wbf_ref[:, :HALF])
    ext_ref[_POOL_HALO:_POOL_HALO + TM_PF, :] = u[:, :HALF]
    ext_ref[0:_POOL_HALO, :] = jnp.where(j == 0, 0.0, uh[:_POOL_HALO])
    ext_ref[_POOL_HALO + TM_PF:, :] = jnp.where(j == tiles_per_seq - 1, 0.0, uh[_POOL_HALO:])
    pos = lax.broadcasted_iota(I32, (TM_PF, POOL_DIM), 0) + j * TM_PF
    outs = []
    for g, w in enumerate(POOL_WINDOWS):
        hw = w // 2
        cols = slice(g * POOL_DIM, (g + 1) * POOL_DIM)
        acc = ext_ref[_POOL_HALO - hw:_POOL_HALO - hw + TM_PF, cols]
        for d in range(-hw + 1, hw):
            acc = acc + ext_ref[_POOL_HALO + d:_POOL_HALO + d + TM_PF, cols]
        cnt = jnp.minimum(pos + hw, SEQ) - jnp.maximum(pos - hw, 0)
        pooled = acc / cnt.astype(F32) - ext_ref[_POOL_HALO:_POOL_HALO + TM_PF, cols]
        outs.append(_dot(pooled.astype(BF16), pw_ref[g]))
    ya_ref[...] = (jnp.concatenate(outs, axis=-1) * ps_ref[...]).astype(BF16)
    parts = []
    for h in range(FOURIER_HEADS):
        ub = u[:, HALF + FOURIER_DIM * h:HALF + FOURIER_DIM * (h + 1)]
        mu = jnp.mean(ub, axis=-1, keepdims=True)
        uc = ub - mu
        var = jnp.mean(uc * uc, axis=-1, keepdims=True)
        parts.append(uc * lax.rsqrt(var + LN_EPS))
    un = jnp.concatenate(parts, axis=-1) * lng_ref[...]
    pq_ref[...] = _dot(un.astype(BF16), wc_ref[...]).astype(BF16)


def _pf_in(x, w_in_all, j, ln_g, wc, pool_w, pool_scale):
    blocks_per_tile = TM_PF // _POOL_HALO
    last_block = TOKENS // _POOL_HALO - 1
    row = lambda i: (i, 0)
    fixed = lambda i: (0, 0)
    return pl.pallas_call(
        _k_pf_in,
        out_shape=(jax.ShapeDtypeStruct((TOKENS, HALF), BF16),
                   jax.ShapeDtypeStruct((TOKENS, D_MODEL), BF16)),
        grid=(TOKENS // TM_PF,),
        in_specs=[pl.BlockSpec((TM_PF, D_MODEL), row),
                  pl.BlockSpec((_POOL_HALO, D_MODEL), lambda i: (jnp.maximum(i * blocks_per_tile - 1, 0), 0)),
                  pl.BlockSpec((_POOL_HALO, D_MODEL),
                               lambda i: (jnp.minimum((i + 1) * blocks_per_tile, last_block), 0)),
                  _resident((1, D_MODEL, D_MODEL), (j, 0, 0)),
                  pl.BlockSpec((1, HALF), fixed),
                  _resident((HALF, D_MODEL), (0, 0)),
                  pl.BlockSpec((4, POOL_DIM, POOL_DIM), lambda i: (0, 0, 0)),
                  pl.BlockSpec((1, HALF), fixed)],
        out_specs=(pl.BlockSpec((TM_PF, HALF), row),
                   pl.BlockSpec((TM_PF, D_MODEL), row)),
        scratch_shapes=[pltpu.VMEM((D_MODEL, D_MODEL), BF16),
                        pltpu.VMEM((TM_PF + 2 * _POOL_HALO, HALF), F32)],
        compiler_params=_params("arbitrary"),
        name="pf_in",
    )(x, x, x, w_in_all, ln_g, wc, pool_w, pool_scale)


_FOURIER_SCALE = 1.0 / float(np.sqrt(SEQ * FOURIER_DIM))


_DFT_SIDE = 64
_DFT_COL_CHUNK = 1024


_DFT_HALF = SEQ // 2
_REV_BLOCK = 256


def _k_fourier(tab_ref, e1_ref, e2_ref, rw_ref, sg_ref, pq_ref, fw_ref, yb_ref,
               cs_ref, ss_ref, pf_ref, qf_ref, z_ref):
    h_rows = _DFT_HALF

    @pl.when(pl.program_id(0) == 0)
    def _():
        def expand(v, e):
            hi = v.astype(BF16)
            lo = (v - hi.astype(F32)).astype(BF16)
            return _dot(hi, e) + _dot(lo, e)

        s = _DFT_SIDE
        for rb in range(h_rows // FOURIER_TR):
            rows = slice(rb * FOURIER_TR, (rb + 1) * FOURIER_TR)
            tab = tab_ref[rows, :]
            for c in range(h_rows // _DFT_COL_CHUNK):
                cols = slice(c * _DFT_COL_CHUNK, (c + 1) * _DFT_COL_CHUNK)
                ca = expand(tab[:, 0:s], e1_ref[:, cols])
                sa = expand(tab[:, s:2 * s], e1_ref[:, cols])
                cb = expand(tab[:, 2 * s:3 * s], e2_ref[:, cols])
                sb = expand(tab[:, 3 * s:4 * s], e2_ref[:, cols])
                cs_ref[rows, cols] = (ca * cb - sa * sb).astype(BF16)
                ss_ref[rows, cols] = (sa * cb + ca * sb).astype(BF16)

    def head_maps(y):
        return jnp.concatenate(
            [_dot(y[:, FOURIER_DIM * h:FOURIER_DIM * (h + 1)].astype(BF16), fw_ref[h])
             for h in range(FOURIER_HEADS)], axis=-1)

    def reversed_block(src_ref, blk):
        lo = _REV_BLOCK * (h_rows // _REV_BLOCK - 1 - blk)
        if blk == 0:
            tail = src_ref[lo:lo + _REV_BLOCK, :]
            win = jnp.concatenate([tail, jnp.zeros_like(tail)], axis=0)
        else:
            win = src_ref[lo:lo + 2 * _REV_BLOCK, :]
        return _dot(rw_ref[...], win)

    n_blocks = h_rows // _REV_BLOCK
    upper = pq_ref.at[h_rows:SEQ]
    for blk in range(n_blocks):
        rows = slice(blk * _REV_BLOCK, (blk + 1) * _REV_BLOCK)
        rev = reversed_block(upper, blk)
        low = pq_ref[rows, :].astype(F32)
        pf_ref[rows, :] = (low[:, :HALF] + rev[:, :HALF]).astype(BF16)
        qf_ref[rows, :] = (low[:, HALF:] - rev[:, HALF:]).astype(BF16)

    p_mid = pq_ref[h_rows:h_rows + 1, :HALF].astype(F32)
    for rb in range(h_rows // FOURIER_TR):
        rows = slice(rb * FOURIER_TR, (rb + 1) * FOURIER_TR)
        a = _dot(cs_ref[rows, :], pf_ref[...])
        bm = _dot(ss_ref[rows, :], qf_ref[...])
        j = lax.broadcasted_iota(I32, (FOURIER_TR, 1), 0) + rb * FOURIER_TR
        t = jnp.where((j & 1) == 0, 1.0, -1.0) * p_mid
        yb_ref[rows, :] = head_maps((a - bm + t) * _FOURIER_SCALE).astype(BF16)
        z_ref[rows, :] = head_maps((a + bm + t) * _FOURIER_SCALE).astype(BF16)

    y_mid = head_maps(_dot(sg_ref[...], pq_ref[:, :HALF]) * _FOURIER_SCALE)[0:1]
    for blk in range(n_blocks):
        up = reversed_block(z_ref, blk)
        if blk == 0:
            r = lax.broadcasted_iota(I32, (_REV_BLOCK, 1), 0)
            up = jnp.where(r == 0, y_mid, up)
        yb_ref[h_rows + blk * _REV_BLOCK:h_rows + (blk + 1) * _REV_BLOCK, :] = up.astype(BF16)


def _fourier(tables, e1, e2, rev_window, signs, pq, fw):
    h_rows = _DFT_HALF
    return pl.pallas_call(
        _k_fourier,
        out_shape=jax.ShapeDtypeStruct((TOKENS, HALF), BF16),
        grid=(BATCH,),
        in_specs=[_resident((h_rows, 4 * _DFT_SIDE), (0, 0)),
                  _resident((_DFT_SIDE, h_rows), (0, 0)),
                  _resident((_DFT_SIDE, h_rows), (0, 0)),
                  _resident((_REV_BLOCK, 2 * _REV_BLOCK), (0, 0)),
                  _resident((V7X_SUBLANES, SEQ), (0, 0)),
                  pl.BlockSpec((SEQ, D_MODEL), lambda b: (b, 0), pipeline_mode=pl.Buffered(1)),
                  pl.BlockSpec((4, FOURIER_DIM, FOURIER_DIM), lambda b: (0, 0, 0))],
        out_specs=pl.BlockSpec((SEQ, HALF), lambda b: (b, 0)),
        scratch_shapes=[pltpu.VMEM((h_rows, h_rows), BF16), pltpu.VMEM((h_rows, h_rows), BF16),
                        pltpu.VMEM((h_rows, HALF), BF16), pltpu.VMEM((h_rows, HALF), BF16),
                        pltpu.VMEM((h_rows, HALF), BF16)],
        compiler_params=_params("arbitrary"),
        name="pf_fourier",
    )(tables, e1, e2, rev_window, signs, pq, fw)


_HG_STAGE_COLS = 512


def _k_hg_in(x_ref, w_hbm, o_ref, wbf_ref, stage_ref, sem):
    @pl.when(pl.program_id(0) == 0)
    def _():
        n_chunks = HG_N_PROJ * D_MODEL // _HG_STAGE_COLS

        def chunk_copy(c):
            cols = pl.ds(c * _HG_STAGE_COLS, _HG_STAGE_COLS)
            return pltpu.make_async_copy(w_hbm.at[:, cols], stage_ref.at[c % 2], sem.at[c % 2])

        chunk_copy(0).start()
        for c in range(n_chunks):
            if c + 1 < n_chunks:
                chunk_copy(c + 1).start()
            chunk_copy(c).wait()
            wbf_ref[:, c * _HG_STAGE_COLS:(c + 1) * _HG_STAGE_COLS] = stage_ref[c % 2].astype(BF16)

    xb = x_ref[...].astype(BF16)
    for p in range(HG_N_PROJ):
        acc = _dot(xb, wbf_ref[:, p * D_MODEL:(p + 1) * D_MODEL])
        if p == 0:
            acc = _silu(acc)
        for h in range(HG_HEADS):
            o_ref[p, h] = acc[:, HG_DIM * h:HG_DIM * (h + 1)].astype(BF16)


def _hg_in(x, w_in):
    return pl.pallas_call(
        _k_hg_in,
        out_shape=jax.ShapeDtypeStruct((HG_N_PROJ, HG_HEADS, TOKENS, HG_DIM), BF16),
        grid=(TOKENS // TM,),
        in_specs=[pl.BlockSpec((TM, D_MODEL), lambda i: (i, 0)),
                  pl.BlockSpec(memory_space=pl.ANY)],
        out_specs=pl.BlockSpec((HG_N_PROJ, HG_HEADS, TM, HG_DIM), lambda i: (0, 0, i, 0)),
        scratch_shapes=[pltpu.VMEM((D_MODEL, HG_N_PROJ * D_MODEL), BF16),
                        pltpu.VMEM((2, D_MODEL, _HG_STAGE_COLS), F32),
                        pltpu.SemaphoreType.DMA((2,))],
        compiler_params=_params("arbitrary"),
        name="hg_in",
    )(x, w_in)


def _gla_prep(q, z, lbv, tri, rev):
    L, G = HG_CHUNK, GLA_GROUP
    f = lbv + (1.0 - lbv) * jax.nn.sigmoid(z.astype(F32))
    lf = jnp.log(f).reshape(G, L, HG_DIM)
    k3 = (1.0 - f).reshape(G, L, HG_DIM)
    q3 = q.astype(F32).reshape(G, L, HG_DIM)
    hi32 = lax.bitcast_convert_type(
        lax.bitcast_convert_type(lf, jnp.uint32) & jnp.uint32(0xFFFF0000), F32)
    lf_hi = hi32.astype(BF16)
    lf_lo = (lf - hi32).astype(BF16)
    b2 = jnp.einsum('glm,gmk->glk', jnp.broadcast_to(tri, (G, L, L)),
                    jnp.concatenate([lf_hi, lf_lo], axis=-1), preferred_element_type=F32)
    b = b2[..., :HG_DIM] + b2[..., HG_DIM:]
    mid = L // 2 if not rev else L - 1 - L // 2
    end = L - 1 if not rev else 0
    b_ref = b[:, mid:mid + 1, :]
    b_last = b[:, end:end + 1, :]
    e_pos = jnp.exp(b - b_ref)
    qt32 = q3 * e_pos
    kt32 = k3 * (1.0 / e_pos)
    ku = kt32 * jnp.exp(b_last - b_ref)
    qe = qt32 * jnp.exp(b_ref)
    dec = jnp.exp(b_last)
    return qt32.astype(BF16), kt32.astype(BF16), ku.astype(BF16), qe.astype(BF16), dec


def _gla_mix(qt, kt, ku, qe, dec, v, st, rev):
    L, G = HG_CHUNK, GLA_GROUP
    v3 = v.reshape(G, L, HG_DIM)
    sc = jnp.einsum('glk,gmk->glm', qt, kt, preferred_element_type=F32)
    li = lax.broadcasted_iota(I32, (L, L), 0)
    mi = lax.broadcasted_iota(I32, (L, L), 1)
    causal = (li >= mi) if not rev else (li <= mi)
    sc = jnp.where(causal[None], sc, 0.0).astype(BF16)
    o_intra = jnp.einsum('glm,gmv->glv', sc, v3, preferred_element_type=F32)
    u_t = jnp.einsum('glv,glk->gvk', v3, ku, preferred_element_type=F32)
    outs = [None] * G
    for n in (range(G) if not rev else range(G - 1, -1, -1)):
        o_inter = lax.dot_general(qe[n], st.astype(BF16), _NT, preferred_element_type=F32)
        outs[n] = o_intra[n] + o_inter
        st = dec[n] * st + u_t[n]
    return jnp.concatenate(outs, axis=0), st


_GLA_N_OPERANDS = 5


def _k_gla(q_ref, v_ref, zf_ref, zb_ref, g_ref, lb_ref, ng_ref, lt_ref, ut_ref, o_ref, of_ref, ob_ref,
           *operand_refs):
    n_groups = SEQ // GLA_ROWS
    per_slot = 2 * _GLA_N_OPERANDS
    slots = [[operand_refs[s * per_slot + d * _GLA_N_OPERANDS:s * per_slot + (d + 1) * _GLA_N_OPERANDS]
              for d in range(2)] for s in range(2)]
    directions = ((zf_ref, lt_ref, False), (zb_ref, ut_ref, True))

    def group_rows(t, rev):
        gi = t if not rev else n_groups - 1 - t
        return slice(gi * GLA_ROWS, (gi + 1) * GLA_ROWS)

    def prep(slot, t):
        for d, (z_ref, tri_ref, rev) in enumerate(directions):
            rows = group_rows(t, rev)
            vals = _gla_prep(q_ref[0, 0, rows, :], z_ref[0, 0, rows, :], lb_ref[d, 0], tri_ref[...], rev)
            for ref, val in zip(slots[slot][d], vals):
                ref[...] = val

    def mix(slot, t, states):
        out_refs = (of_ref, ob_ref)
        new_states = []
        for d, (_, _, rev) in enumerate(directions):
            rows = group_rows(t, rev)
            o, st = _gla_mix(*[ref[...] for ref in slots[slot][d]], v_ref[0, 0, rows, :], states[d], rev)
            out_refs[d][rows, :] = o
            new_states.append(st)
        return new_states

    def finish(gi):
        rows = slice(gi * GLA_ROWS, (gi + 1) * GLA_ROWS)
        o = of_ref[rows, :] + ob_ref[rows, :]
        o = o * lax.rsqrt(jnp.mean(o * o, axis=-1, keepdims=True) + LN_EPS)
        o = o * ng_ref[...] * _silu(g_ref[0, 0, rows, :].astype(F32))
        o_ref[0, rows, :] = o.astype(BF16)

    states = [jnp.zeros((HG_DIM, HG_DIM), F32)] * 2
    prep(0, 0)
    for t in range(n_groups):
        if t + 1 < n_groups:
            prep((t + 1) % 2, t + 1)
        states = mix(t % 2, t, states)
        for gi in {t, n_groups - 1 - t}:
            if max(gi, n_groups - 1 - gi) == t:
                finish(gi)


def _gla(proj, lb, norm_g, tril, triu):
    def pspec(p):
        return pl.BlockSpec((1, 1, SEQ, HG_DIM), lambda b, h: (p, h, b, 0))
    fixed = lambda b, h: (0, 0)
    return pl.pallas_call(
        _k_gla,
        out_shape=jax.ShapeDtypeStruct((HG_HEADS, TOKENS, HG_DIM), BF16),
        grid=(BATCH, HG_HEADS),
        in_specs=[pspec(0), pspec(1), pspec(2), pspec(3), pspec(4),
                  pl.BlockSpec((2, 1, 1, HG_DIM), lambda b, h: (0, h, 0, 0)),
                  pl.BlockSpec((1, HG_DIM), fixed),
                  pl.BlockSpec((HG_CHUNK, HG_CHUNK), fixed),
                  pl.BlockSpec((HG_CHUNK, HG_CHUNK), fixed)],
        out_specs=pl.BlockSpec((1, SEQ, HG_DIM), lambda b, h: (h, b, 0)),
        scratch_shapes=[pltpu.VMEM((SEQ, HG_DIM), F32), pltpu.VMEM((SEQ, HG_DIM), F32)]
        + 4 * ([pltpu.VMEM((GLA_GROUP, HG_CHUNK, HG_DIM), BF16)] * (_GLA_N_OPERANDS - 1)
               + [pltpu.VMEM((GLA_GROUP, 1, HG_DIM), F32)]),
        compiler_params=_params("parallel", "parallel"),
        name="hg_gla",
    )(proj, proj, proj, proj, proj, lb, norm_g, tril, triu)


def _k_mm(x_ref, w_ref, o_ref):
    o_ref[...] = _dot(x_ref[...].astype(BF16), w_ref[0].astype(BF16)).astype(o_ref.dtype)


def _kv_proj(mem2d, wkv_all, l):
    rows, cols = mem2d.shape[0], wkv_all.shape[2]
    return pl.pallas_call(
        _k_mm,
        out_shape=jax.ShapeDtypeStruct((rows, cols), BF16),
        grid=(cols // D_MODEL, rows // TM),
        in_specs=[pl.BlockSpec((TM, D_MODEL), lambda j, i: (i, 0)),
                  pl.BlockSpec((1, D_MODEL, D_MODEL), lambda j, i: (l, 0, j))],
        out_specs=pl.BlockSpec((TM, D_MODEL), lambda j, i: (i, j)),
        compiler_params=_params("parallel", "parallel"),
        name="xa_kv",
    )(mem2d, wkv_all)


ROW_WORDS = D_MODEL // 2
ROW_SLABS = ROW_WORDS // V7X_LANES
U32 = jnp.uint32
_HI_MASK = 0xFFFF0000


def _to_row_tiles(ref, x):
    n = x.shape[0]
    as_bits = lambda v: lax.bitcast_convert_type(v.astype(BF16).astype(F32), U32)
    words = (as_bits(x[:, :ROW_WORDS]) >> 16) | (as_bits(x[:, ROW_WORDS:]) & U32(_HI_MASK))
    for s in range(ROW_SLABS):
        ref[pl.ds(s, n, stride=ROW_SLABS), :] = words[:, V7X_LANES * s:V7X_LANES * (s + 1)]


def _from_row_tiles(ref):
    n = ref.shape[0] // ROW_SLABS
    words = jnp.concatenate([ref[pl.ds(s, n, stride=ROW_SLABS), :] for s in range(ROW_SLABS)], axis=-1)
    lo = lax.bitcast_convert_type(words << 16, F32)
    hi = lax.bitcast_convert_type(words & U32(_HI_MASK), F32)
    return jnp.concatenate([lo, hi], axis=-1)


def _first_argmax_rows(v, n_rows):
    m = jnp.max(v, axis=0, keepdims=True)
    rows = lax.broadcasted_iota(I32, v.shape, 0)
    idx = jnp.min(jnp.where(v == m, rows, n_rows), axis=0, keepdims=True)
    return m, idx


def _route(lt):
    gl = lt[0:MOE_GROUPS, :]
    gmax, gi = _first_argmax_rows(gl, MOE_GROUPS)
    p_grp = 1.0 / jnp.sum(jnp.exp(gl - gmax), axis=0, keepdims=True)
    sel = lt[MOE_GROUPS:MOE_GROUPS + MOE_PER_GROUP, :]
    for g in range(1, MOE_GROUPS):
        lo = MOE_GROUPS + MOE_PER_GROUP * g
        sel = jnp.where(gi == g, lt[lo:lo + MOE_PER_GROUP, :], sel)
    m1, i1 = _first_argmax_rows(sel, MOE_PER_GROUP)
    rows = lax.broadcasted_iota(I32, sel.shape, 0)
    m2, i2 = _first_argmax_rows(jnp.where(rows == i1, -jnp.inf, sel), MOE_PER_GROUP)
    e2 = jnp.exp(m2 - m1)
    g1 = p_grp / (1.0 + e2)
    return (gi * MOE_PER_GROUP + i1, gi * MOE_PER_GROUP + i2), (g1, g1 * e2)


def _rank_in_expert(e0, e1, strict_upper, carry_ref, counted):
    rows = lax.broadcasted_iota(I32, (MOE_EXPERTS, e0.shape[1]), 0)
    oh0 = rows == e0
    oh1 = rows == e1
    oh = jnp.where(oh0 | oh1, 1.0, 0.0)
    before = _dot(oh.astype(BF16), strict_upper) + carry_ref[:, 0:1]
    r0 = jnp.sum(jnp.where(oh0, before, 0.0), axis=0, keepdims=True).astype(I32)
    r1 = jnp.sum(jnp.where(oh1, before, 0.0), axis=0, keepdims=True).astype(I32)
    carry_ref[...] = carry_ref[...] + counted * jnp.sum(oh, axis=1, keepdims=True)
    return r0, r1


def _k_block(heads_major, *refs):
    n_mix = 1 if heads_major else 2
    mix_refs = refs[:n_mix]
    (wm_ref, x_ref, g0_ref, b0_ref, wq_ref, kv_ref, wo_ref, g_ref, b_ref, wr_ref, br_ref, tri_ref,
     o_ref, o3_ref, e_ref, gate_ref, rank_ref, cnt_ref,
     wmbf_ref, wqbf_ref, wobf_ref, carry_ref, pre_ref) = refs[n_mix:]
    step = pl.program_id(0)
    _cast_weight_once(wm_ref, wmbf_ref)
    _cast_weight_once(wq_ref, wqbf_ref)
    _cast_weight_once(wo_ref, wobf_ref)

    @pl.when(step == 0)
    def _():
        carry_ref[...] = jnp.zeros_like(carry_ref)
        pre_ref[...] = jnp.zeros_like(pre_ref)

    quarter = TM // 4
    routed = []

    def stage2(c):
        rows = slice(c * quarter, (c + 1) * quarter)
        y = _layer_norm(pre_ref[rows, :], g_ref[...], b_ref[...])
        o_ref[rows, :] = y
        _to_row_tiles(o3_ref.at[pl.ds(c * quarter * ROW_SLABS, quarter * ROW_SLABS)], y)
        lt = lax.dot_general(wr_ref[...], y.astype(BF16), _NT, preferred_element_type=F32) + br_ref[...]
        routed.append(_route(lt))

    stage2(0)
    if heads_major:
        a = jnp.concatenate([mix_refs[0][h] for h in range(HG_HEADS)], axis=-1)
    else:
        a = jnp.concatenate([mix_refs[0][...], mix_refs[1][...]], axis=-1)
    h1 = _dot(a, wmbf_ref[...])
    stage2(1)
    x = _post_ln(x_ref[...], h1, g0_ref[...], b0_ref[...])
    q = _dot(x.astype(BF16), wqbf_ref[...]).astype(BF16)
    stage2(2)
    outs = []
    for h in range(XA_HEADS):
        cols = slice(XA_HEAD_DIM * h, XA_HEAD_DIM * (h + 1))
        kh = kv_ref[:, cols]
        vh = kv_ref[:, D_MODEL + XA_HEAD_DIM * h:D_MODEL + XA_HEAD_DIM * (h + 1)]
        s = lax.dot_general(q[:, cols], kh, _NT, preferred_element_type=F32) * (XA_HEAD_DIM ** -0.5)
        e = jnp.exp(s - jnp.max(s, axis=-1, keepdims=True))
        p = e / jnp.sum(e, axis=-1, keepdims=True)
        outs.append(_dot(p.astype(BF16), vh))
    o = jnp.concatenate(outs, axis=-1).astype(BF16)
    h2 = _dot(o, wobf_ref[...])
    stage2(3)
    e0, e1, gate0, gate1 = [jnp.concatenate([r[a_][b_] for r in routed], axis=-1)
                            for a_ in range(2) for b_ in range(2)]
    r0, r1 = _rank_in_expert(e0, e1, tri_ref[...], carry_ref, jnp.where(step > 0, 1.0, 0.0))
    e_ref[0:1, :], e_ref[1:2, :] = e0, e1
    gate_ref[0:1, :], gate_ref[1:2, :] = gate0, gate1
    rank_ref[0:1, :], rank_ref[1:2, :] = r0, r1
    cnt_ref[...] = carry_ref[...]
    pre_ref[...] = DN_ALPHA * x + h2


def _block(mix, heads_major, w_mix_all, j, x, g0, b0, wq_all, kv, wo_all, l, g1, b1, wr, br, strict_upper):
    n_tiles = TOKENS // TM
    tiles_per_seq = SEQ // TM
    cur = lambda i: jnp.minimum(i, n_tiles - 1)
    prev = lambda i: jnp.maximum(i - 1, 0)
    row_in = lambda i: (cur(i), 0)
    row_out = lambda i: (prev(i), 0)
    fixed = lambda i: (0, 0)
    tok = pl.BlockSpec((MOE_TOPK, TM), lambda i: (0, prev(i)))
    if heads_major:
        mix_specs = [pl.BlockSpec((HG_HEADS, TM, HG_DIM), lambda i: (0, cur(i), 0))]
    else:
        mix_specs = [pl.BlockSpec((TM, HALF), row_in), pl.BlockSpec((TM, HALF), row_in)]
    return pl.pallas_call(
        functools.partial(_k_block, heads_major),
        out_shape=(jax.ShapeDtypeStruct((TOKENS, D_MODEL), F32),
                   jax.ShapeDtypeStruct((TOKENS * ROW_SLABS, V7X_LANES), U32),
                   jax.ShapeDtypeStruct((MOE_TOPK, TOKENS), I32),
                   jax.ShapeDtypeStruct((MOE_TOPK, TOKENS), F32),
                   jax.ShapeDtypeStruct((MOE_TOPK, TOKENS), I32),
                   jax.ShapeDtypeStruct((MOE_EXPERTS, V7X_LANES), F32)),
        grid=(n_tiles + 1,),
        in_specs=mix_specs + [
            _resident((1, D_MODEL, D_MODEL), (j, 0, 0)),
            pl.BlockSpec((TM, D_MODEL), row_in),
            pl.BlockSpec((1, D_MODEL), fixed), pl.BlockSpec((1, D_MODEL), fixed),
            _resident((1, D_MODEL, D_MODEL), (l, 0, 0)),
            pl.BlockSpec((MEM_LEN, 2 * D_MODEL), lambda i: (cur(i) // tiles_per_seq, 0)),
            _resident((1, D_MODEL, D_MODEL), (l, 0, 0)),
            pl.BlockSpec((1, D_MODEL), fixed), pl.BlockSpec((1, D_MODEL), fixed),
            _resident((ROUTER_ROWS, D_MODEL), (0, 0)),
            _resident((ROUTER_ROWS, 1), (0, 0)),
            _resident((TM, TM), (0, 0))],
        out_specs=(pl.BlockSpec((TM, D_MODEL), row_out),
                   pl.BlockSpec((TM * ROW_SLABS, V7X_LANES), row_out),
                   tok, tok, tok,
                   pl.BlockSpec((MOE_EXPERTS, V7X_LANES), fixed)),
        scratch_shapes=[pltpu.VMEM((D_MODEL, D_MODEL), BF16), pltpu.VMEM((D_MODEL, D_MODEL), BF16),
                        pltpu.VMEM((D_MODEL, D_MODEL), BF16), pltpu.VMEM((MOE_EXPERTS, V7X_LANES), F32),
                        pltpu.VMEM((TM, D_MODEL), F32)],
        compiler_params=_params("arbitrary"),
        name="attn_block",
    )(*mix, w_mix_all, x, g0, b0, wq_all, kv, wo_all, g1, b1, wr, br, strict_upper)


def _k_pos(e_ref, rank_ref, off_ref, pos_ref):
    rows = lax.broadcasted_iota(I32, (MOE_EXPERTS, TM_ROUTE), 0)
    for k in range(MOE_TOPK):
        start = jnp.sum(jnp.where(rows == e_ref[k:k + 1, :], off_ref[...], 0), axis=0, keepdims=True)
        pos_ref[k:k + 1, :] = start + rank_ref[k:k + 1, :]


def _positions(eidx, rank, offsets):
    tok = pl.BlockSpec((MOE_TOPK, TM_ROUTE), lambda i: (0, i))
    return pl.pallas_call(
        _k_pos,
        out_shape=jax.ShapeDtypeStruct((MOE_TOPK, TOKENS), I32),
        grid=(TOKENS // TM_ROUTE,),
        in_specs=[tok, tok, pl.BlockSpec((MOE_EXPERTS, 1), lambda i: (0, 0))],
        out_specs=tok,
        compiler_params=_params("parallel"),
        name="moe_pos",
    )(eidx, rank, offsets)


_ISSUE_BATCH = 8


def _row_copy(src_ref, src_row, dst_ref, dst_row, sem):
    src = pl.ds(pl.multiple_of(src_row * ROW_SLABS, ROW_SLABS), ROW_SLABS)
    dst = pl.ds(pl.multiple_of(dst_row * ROW_SLABS, ROW_SLABS), ROW_SLABS)
    return pltpu.make_async_copy(src_ref.at[src], dst_ref.at[dst], sem)


_DISPATCH_SLOTS = 4


def _k_dispatch(pos_ref, ends_ref, x_ref, xs_ref, zero_ref, stage_ref, row_sems, load_sems, zsem):
    step = pl.program_id(0)
    n_steps = pl.num_programs(0)
    block_rows = TM_PERM * ROW_SLABS

    def load(tile):
        slot = lax.rem(tile, _DISPATCH_SLOTS)
        src = x_ref.at[pl.ds(pl.multiple_of(tile * block_rows, block_rows), block_rows)]
        return pltpu.make_async_copy(src, stage_ref.at[slot], load_sems.at[slot])

    def drain_rows(tile):
        slot = lax.rem(tile, _DISPATCH_SLOTS)
        for _ in range(MOE_TOPK):
            pltpu.make_async_copy(stage_ref.at[slot], xs_ref.at[pl.ds(0, block_rows)],
                                  row_sems.at[slot]).wait()

    @pl.when(step == 0)
    def _():
        load(0).start()
        pl.when(n_steps > 1)(load(1).start)
        zero_ref[...] = jnp.zeros_like(zero_ref)

        def last_tile(e):
            seg_start = ends_ref[e - 1] if e > 0 else 0
            start = ends_ref[e] - TE
            return ends_ref[e] > seg_start, zero_fill(start)

        def unused_tile(t):
            start = ends_ref[MOE_EXPERTS - 1] + t * TE
            return start < SORTED_ROWS, zero_fill(start)

        def zero_fill(start_row):
            rows = pl.ds(pl.multiple_of(start_row * ROW_SLABS, TE * ROW_SLABS), TE * ROW_SLABS)
            return pltpu.make_async_copy(zero_ref, xs_ref.at[rows], zsem)

        fills = [last_tile(e) for e in range(MOE_EXPERTS)] + [unused_tile(t) for t in range(MOE_EXPERTS)]
        for needed, copy in fills:
            pl.when(needed)(copy.start)
        for needed, copy in fills:
            pl.when(needed)(copy.wait)

    pl.when(step >= 2)(lambda: drain_rows(step - 2))
    pl.when(step + 2 < n_steps)(lambda: load(step + 2).start())
    load(step).wait()

    slot = lax.rem(step, _DISPATCH_SLOTS)
    base = step * TM_PERM

    def body(jb, c):
        j0 = jb * _ISSUE_BATCH
        dst = [[pos_ref[k * TOKENS + base + j0 + u] for k in range(MOE_TOPK)]
               for u in range(_ISSUE_BATCH)]
        for u in range(_ISSUE_BATCH):
            for k in range(MOE_TOPK):
                _row_copy(stage_ref.at[slot], j0 + u, xs_ref, dst[u][k],
                          row_sems.at[slot]).start(priority=k)
        return c

    lax.fori_loop(0, TM_PERM // _ISSUE_BATCH, body, 0)

    @pl.when(step == n_steps - 1)
    def _():
        pl.when(step >= 1)(lambda: drain_rows(step - 1))
        drain_rows(step)


def _dispatch(pos_flat, ends, x):
    return pl.pallas_call(
        _k_dispatch,
        out_shape=jax.ShapeDtypeStruct((SORTED_ROWS * ROW_SLABS, V7X_LANES), U32),
        grid_spec=pltpu.PrefetchScalarGridSpec(
            num_scalar_prefetch=2,
            grid=(TOKENS // TM_PERM,),
            in_specs=[pl.BlockSpec(memory_space=pl.ANY)],
            out_specs=pl.BlockSpec(memory_space=pl.ANY),
            scratch_shapes=[pltpu.VMEM((TE * ROW_SLABS, V7X_LANES), U32),
                            pltpu.VMEM((_DISPATCH_SLOTS, TM_PERM * ROW_SLABS, V7X_LANES), U32),
                            pltpu.SemaphoreType.DMA((_DISPATCH_SLOTS,)),
                            pltpu.SemaphoreType.DMA((_DISPATCH_SLOTS,)),
                            pltpu.SemaphoreType.DMA],
        ),
        compiler_params=_params("arbitrary"),
        name="moe_dispatch",
    )(pos_flat, ends, x)


_EXPERT_SUBTILES = 1


def _k_experts(te_ref, na_ref, xs_ref, wg_ref, wu_ref, wd_ref, ys_ref, wgubf_ref, wdbf_ref):
    i = pl.program_id(0)
    active = i < na_ref[0]
    new_expert = jnp.logical_or(i == 0, te_ref[jnp.maximum(i - 1, 0)] != te_ref[i])

    @pl.when(jnp.logical_and(active, new_expert))
    def _():
        wgubf_ref[:, :MOE_HIDDEN] = wg_ref[0, 0].astype(BF16)
        wgubf_ref[:, MOE_HIDDEN:] = wu_ref[0, 0].astype(BF16)
        wdbf_ref[...] = wd_ref[0, 0].astype(BF16)

    @pl.when(active)
    def _():
        sub_rows = TE // _EXPERT_SUBTILES * ROW_SLABS
        for t in range(_EXPERT_SUBTILES):
            rows = pl.ds(t * sub_rows, sub_rows)
            xb = _from_row_tiles(xs_ref.at[rows]).astype(BF16)
            gu = _dot(xb, wgubf_ref[...])
            h = _silu(gu[:, :MOE_HIDDEN]) * gu[:, MOE_HIDDEN:]
            _to_row_tiles(ys_ref.at[rows], _dot(h.astype(BF16), wdbf_ref[...]))

    @pl.when(jnp.logical_not(active))
    def _():
        ys_ref[...] = jnp.zeros_like(ys_ref)


def _experts(tile_expert, n_active, xs, wg_all, wu_all, wd_all, l):
    def xs_index(i, te, na):
        return (jnp.minimum(i, na[0] - 1), 0)

    def w_index(i, te, na):
        return (l, te[i], 0, 0)

    return pl.pallas_call(
        _k_experts,
        out_shape=jax.ShapeDtypeStruct((SORTED_ROWS * ROW_SLABS, V7X_LANES), U32),
        grid_spec=pltpu.PrefetchScalarGridSpec(
            num_scalar_prefetch=2,
            grid=(N_EXPERT_TILES,),
            in_specs=[pl.BlockSpec((TE * ROW_SLABS, V7X_LANES), xs_index),
                      pl.BlockSpec((1, 1, D_MODEL, MOE_HIDDEN), w_index),
                      pl.BlockSpec((1, 1, D_MODEL, MOE_HIDDEN), w_index),
                      pl.BlockSpec((1, 1, MOE_HIDDEN, D_MODEL), w_index)],
            out_specs=pl.BlockSpec((TE * ROW_SLABS, V7X_LANES), lambda i, te, na: (i, 0)),
            scratch_shapes=[pltpu.VMEM((D_MODEL, 2 * MOE_HIDDEN), BF16),
                            pltpu.VMEM((MOE_HIDDEN, D_MODEL), BF16)],
        ),
        compiler_params=_params("arbitrary"),
        name="moe_experts",
    )(tile_expert, n_active, xs, wg_all, wu_all, wd_all)


def _k_combine(pos_ref, ys_ref, gt_ref, x_ref, g_ref, b_ref, o_ref, buf_ref, sem):
    step = pl.program_id(0)

    def start_gather(tile, slot):
        base = tile * TM_COMB

        def body(jb, c):
            j0 = jb * _ISSUE_BATCH
            src = [[pos_ref[k * TOKENS + base + j0 + u] for k in range(MOE_TOPK)]
                   for u in range(_ISSUE_BATCH)]
            for u in range(_ISSUE_BATCH):
                for k in range(MOE_TOPK):
                    _row_copy(ys_ref, src[u][k], buf_ref.at[slot, k], j0 + u,
                              sem.at[slot]).start(priority=k)
            return c

        lax.fori_loop(0, TM_COMB // _ISSUE_BATCH, body, 0)

    slot = lax.rem(step, 2)
    pl.when(step == 0)(lambda: start_gather(0, 0))
    pl.when(step + 1 < pl.num_programs(0))(lambda: start_gather(step + 1, 1 - slot))
    for k in range(MOE_TOPK):
        pltpu.make_async_copy(ys_ref.at[pl.ds(0, TM_COMB * ROW_SLABS)], buf_ref.at[slot, k],
                              sem.at[slot]).wait()
    h = (gt_ref[:, 0:1] * _from_row_tiles(buf_ref.at[slot, 0])
         + gt_ref[:, 1:2] * _from_row_tiles(buf_ref.at[slot, 1]))
    o_ref[...] = _post_ln(x_ref[...], h, g_ref[...], b_ref[...])


def _combine(pos_flat, ys, gates_t, x, g, b):
    row = lambda i, pos: (i, 0)
    fixed = lambda i, pos: (0, 0)
    return pl.pallas_call(
        _k_combine,
        out_shape=jax.ShapeDtypeStruct((TOKENS, D_MODEL), F32),
        grid_spec=pltpu.PrefetchScalarGridSpec(
            num_scalar_prefetch=1,
            grid=(TOKENS // TM_COMB,),
            in_specs=[pl.BlockSpec(memory_space=pl.ANY),
                      pl.BlockSpec((TM_COMB, MOE_TOPK), row),
                      pl.BlockSpec((TM_COMB, D_MODEL), row),
                      pl.BlockSpec((1, D_MODEL), fixed), pl.BlockSpec((1, D_MODEL), fixed)],
            out_specs=pl.BlockSpec((TM_COMB, D_MODEL), row),
            scratch_shapes=[pltpu.VMEM((2, MOE_TOPK, TM_COMB * ROW_SLABS, V7X_LANES), U32),
                            pltpu.SemaphoreType.DMA((2,))],
        ),
        compiler_params=_params("arbitrary"),
        name="moe_combine",
    )(pos_flat, ys, gates_t, x, g, b)


def _router_params(w_group, b_group, w_expert, b_expert):
    pad = ROUTER_ROWS - MOE_GROUPS - MOE_EXPERTS
    wr = jnp.concatenate([w_group.T, w_expert.T, jnp.zeros((pad, D_MODEL), F32)], axis=0)
    br = jnp.concatenate([b_group, b_expert, jnp.zeros((pad,), F32)])[:, None]
    return wr.astype(BF16), br.astype(F32)


def _moe(x, x_tiles, eidx, gates, rank, cnt, w_gate_all, w_up_all, w_down_all, l, g, b):
    counts = cnt[:, 0].astype(I32)
    padded = ((counts + TE - 1) // TE) * TE
    ends = jnp.cumsum(padded)
    offsets = ends - padded
    tile_start = jnp.arange(N_EXPERT_TILES, dtype=I32) * TE
    tile_expert = jnp.sum((tile_start[:, None] >= ends[None, :]).astype(I32), axis=1)
    tile_expert = jnp.minimum(tile_expert, MOE_EXPERTS - 1)
    n_active = (ends[-1:] // TE).astype(I32)
    pos = _positions(eidx, rank, offsets[:, None])
    pos_flat = pos.reshape(-1)
    xs = _dispatch(pos_flat, ends.astype(I32), x_tiles)
    ys = _experts(tile_expert, n_active, xs, w_gate_all, w_up_all, w_down_all, l)
    return _combine(pos_flat, ys, gates.T, x, g, b)


def _channel_dft_matrix():
    c = np.arange(FOURIER_DIM)
    ang = 2.0 * np.pi * ((c[:, None] * c[None, :]) % FOURIER_DIM) / FOURIER_DIM
    wc = np.zeros((HALF, D_MODEL), np.float32)
    for h in range(FOURIER_HEADS):
        rows = slice(FOURIER_DIM * h, FOURIER_DIM * (h + 1))
        wc[rows, FOURIER_DIM * h:FOURIER_DIM * (h + 1)] = np.cos(ang)
        wc[rows, HALF + FOURIER_DIM * h:HALF + FOURIER_DIM * (h + 1)] = np.sin(ang)
    return jnp.asarray(wc, BF16)


def _sequence_dft_tables():
    side = _DFT_SIDE
    j = np.arange(_DFT_HALF, dtype=np.int64)[:, None]
    k = np.arange(side, dtype=np.int64)[None, :]
    ang_a = ((j * k) % side).astype(np.float64) * (2.0 * np.pi / side)
    ang_b = ((j * k) % SEQ).astype(np.float64) * (2.0 * np.pi / SEQ)
    tables = np.concatenate([np.cos(ang_a), np.sin(ang_a), np.cos(ang_b), np.sin(ang_b)], axis=1)
    col = np.arange(_DFT_HALF)
    e1 = (col[None, :] // side == np.arange(side)[:, None]).astype(np.float32)
    e2 = (col[None, :] % side == np.arange(side)[:, None]).astype(np.float32)
    r = np.arange(_REV_BLOCK)
    rev_window = (np.arange(2 * _REV_BLOCK)[None, :] == (_REV_BLOCK - r)[:, None]).astype(np.float32)
    signs = np.zeros((V7X_SUBLANES, SEQ), np.float32)
    signs[0] = 1.0 - 2.0 * (np.arange(SEQ) % 2)
    return (jnp.asarray(tables, F32), jnp.asarray(e1, BF16), jnp.asarray(e2, BF16),
            jnp.asarray(rev_window, BF16), jnp.asarray(signs, BF16))


def kernel(x, mem, pf_w_in, pf_pool_w, pf_pool_scale, pf_fourier_ln_g, pf_fourier_w, pf_w_out, hg_w_in, hg_lower_bounds, hg_norm_g, hg_w_out, xa_wq, xa_wkv, xa_wo, moe_w_group, moe_b_group, moe_w_expert, moe_b_expert, moe_w_gate, moe_w_up, moe_w_down, ln_g, ln_b):
    bf = lambda a: a.astype(BF16)
    xt = x.reshape(TOKENS, D_MODEL)
    mem2d = mem.reshape(BATCH * MEM_LEN, D_MODEL)
    lb_all = jnp.cumsum(jax.nn.softmax(hg_lower_bounds.astype(F32), axis=0), axis=0)
    lb_all = lb_all - lb_all[:1]
    tril = jnp.asarray(np.tril(np.ones((HG_CHUNK, HG_CHUNK), np.float32)), BF16)
    triu = jnp.asarray(np.triu(np.ones((HG_CHUNK, HG_CHUNK), np.float32)), BF16)
    strict_upper = jnp.asarray(np.triu(np.ones((TM, TM), np.float32), k=1), BF16)

    for l in range(DEPTH):
        j = l // 2
        lng = lambda s: ln_g[l, s][None, :]
        lnb = lambda s: ln_b[l, s][None, :]
        if l % 2 == 0:
            dft_consts = _sequence_dft_tables()
            ya, pq = _pf_in(xt, pf_w_in, j, pf_fourier_ln_g[j].reshape(1, HALF), _channel_dft_matrix(),
                            bf(pf_pool_w[j]), pf_pool_scale[j][None, :])
            yb = _fourier(*dft_consts, pq, bf(pf_fourier_w[j]))
            mix, heads_major, w_mix = (ya, yb), False, pf_w_out
        else:
            proj = _hg_in(xt, hg_w_in[j])
            lb = lb_all[l].reshape(2, HG_HEADS, 1, HG_DIM)
            og = _gla(proj, lb, hg_norm_g[j][None, :], tril, triu)
            mix, heads_major, w_mix = (og,), True, hg_w_out
        kv = _kv_proj(mem2d, xa_wkv, l)
        wr, br = _router_params(moe_w_group[l], moe_b_group[l], moe_w_expert[l], moe_b_expert[l])
        xt, xt_tiles, eidx, gates, rank, cnt = _block(
            mix, heads_major, w_mix, j, xt, lng(0), lnb(0), xa_wq, kv, xa_wo, l, lng(1), lnb(1),
            wr, br, strict_upper)
        xt = _moe(xt, xt_tiles, eidx, gates, rank, cnt, moe_w_gate, moe_w_up, moe_w_down, l,
                  lng(2), lnb(2))
    return xt.reshape(BATCH, SEQ, D_MODEL)
```

```python
import functools

import jax
import jax.numpy as jnp
import numpy as np
from jax import lax
from jax.experimental import pallas as pl
from jax.experimental.pallas import tpu as pltpu

F32 = jnp.float32
BF16 = jnp.bfloat16
I32 = jnp.int32

D_MODEL = 1024
BATCH = 8
SEQ = 4096
DEPTH = 2
TOKENS = BATCH * SEQ
MEM_LEN = 256
HALF = D_MODEL // 2
POOL_WINDOWS = (2, 4, 8, 16)
POOL_DIM = 128
FOURIER_HEADS = 4
FOURIER_DIM = 128
HG_HEADS = 8
HG_DIM = 128
HG_N_PROJ = 5
HG_CHUNK = 64
XA_HEADS = 4
XA_HEAD_DIM = 256
MOE_GROUPS = 4
MOE_PER_GROUP = 8
MOE_EXPERTS = 32
MOE_TOPK = 2
MOE_HIDDEN = 512
DN_ALPHA = (2.0 * DEPTH) ** 0.25
LN_EPS = 1e-5

V7X_LANES = 128
V7X_SUBLANES = 8
V7X_VMEM_LIMIT_BYTES = 52 * 1024 * 1024

TM = 512
TM_ROUTE = 1024
TM_PERM = 2048
TM_COMB = 512
TE = 512
SORTED_ROWS = TOKENS * MOE_TOPK + MOE_EXPERTS * TE
N_EXPERT_TILES = SORTED_ROWS // TE
GLA_GROUP = 16
GLA_ROWS = GLA_GROUP * HG_CHUNK
FOURIER_TR = 512
ROUTER_ROWS = 40

_NT = (((1,), (1,)), ((), ()))


def _params(*sem):
    return pltpu.CompilerParams(dimension_semantics=sem, vmem_limit_bytes=V7X_VMEM_LIMIT_BYTES)


def _dot(a, b):
    return jnp.dot(a, b, preferred_element_type=F32)


def _layer_norm(y, g, b):
    mu = jnp.mean(y, axis=-1, keepdims=True)
    yc = y - mu
    var = jnp.mean(yc * yc, axis=-1, keepdims=True)
    return yc * lax.rsqrt(var + LN_EPS) * g + b


def _post_ln(x, h, g, b):
    return _layer_norm(DN_ALPHA * x + h, g, b)


def _silu(x):
    return x * jax.nn.sigmoid(x)


def _resident(shape, index):
    return pl.BlockSpec(shape, lambda *_: index, pipeline_mode=pl.Buffered(1))


def _cast_weight_once(w_ref, wbf_ref):
    @pl.when(pl.program_id(0) == 0)
    def _():
        wbf_ref[...] = w_ref[0].astype(BF16)


TM_PF = 1024
_POOL_HALO = 8


def _k_pf_in(x_ref, xp_ref, xn_ref, w_ref, lng_ref, wc_ref, pw_ref, ps_ref, ya_ref, pq_ref, wbf_ref, ext_ref):
    _cast_weight_once(w_ref, wbf_ref)
    tiles_per_seq = SEQ // TM_PF
    j = lax.rem(pl.program_id(0), tiles_per_seq)
    u = _dot(x_ref[...].astype(BF16), wbf_ref[...])
    halo = jnp.concatenate([xp_ref[...], xn_ref[...]], axis=0).astype(BF16)
    uh = _dot(halo, wbf_ref[:, :HALF])
    ext_ref[_POOL_HALO:_POOL_HALO + TM_PF, :] = u[:, :HALF]
    ext_ref[0:_POOL_HALO, :] = jnp.where(j == 0, 0.0, uh[:_POOL_HALO])
    ext_ref[_POOL_HALO + TM_PF:, :] = jnp.where(j == tiles_per_seq - 1, 0.0, uh[_POOL_HALO:])
    pos = lax.broadcasted_iota(I32, (TM_PF, POOL_DIM), 0) + j * TM_PF
    outs = []
    for g, w in enumerate(POOL_WINDOWS):
        hw = w // 2
        cols = slice(g * POOL_DIM, (g + 1) * POOL_DIM)
        acc = ext_ref[_POOL_HALO - hw:_POOL_HALO - hw + TM_PF, cols]
        for d in range(-hw + 1, hw):
            acc = acc + ext_ref[_POOL_HALO + d:_POOL_HALO + d + TM_PF, cols]
        cnt = jnp.minimum(pos + hw, SEQ) - jnp.maximum(pos - hw, 0)
        pooled = acc / cnt.astype(F32) - ext_ref[_POOL_HALO:_POOL_HALO + TM_PF, cols]
        outs.append(_dot(pooled.astype(BF16), pw_ref[g]))
    ya_ref[...] = (jnp.concatenate(outs, axis=-1) * ps_ref[...]).astype(BF16)
    parts = []
    for h in range(FOURIER_HEADS):
        ub = u[:, HALF + FOURIER_DIM * h:HALF + FOURIER_DIM * (h + 1)]
        mu = jnp.mean(ub, axis=-1, keepdims=True)
        uc = ub - mu
        var = jnp.mean(uc * uc, axis=-1, keepdims=True)
        parts.append(uc * lax.rsqrt(var + LN_EPS))
    un = jnp.concatenate(parts, axis=-1) * lng_ref[...]
    pq_ref[...] = _dot(un.astype(BF16), wc_ref[...]).astype(BF16)


def _pf_in(x, w_in_all, j, ln_g, wc, pool_w, pool_scale):
    blocks_per_tile = TM_PF // _POOL_HALO
    last_block = TOKENS // _POOL_HALO - 1
    row = lambda i: (i, 0)
    fixed = lambda i: (0, 0)
    return pl.pallas_call(
        _k_pf_in,
        out_shape=(jax.ShapeDtypeStruct((TOKENS, HALF), BF16),
                   jax.ShapeDtypeStruct((TOKENS, D_MODEL), BF16)),
        grid=(TOKENS // TM_PF,),
        in_specs=[pl.BlockSpec((TM_PF, D_MODEL), row),
                  pl.BlockSpec((_POOL_HALO, D_MODEL), lambda i: (jnp.maximum(i * blocks_per_tile - 1, 0), 0)),
                  pl.BlockSpec((_POOL_HALO, D_MODEL),
                               lambda i: (jnp.minimum((i + 1) * blocks_per_tile, last_block), 0)),
                  _resident((1, D_MODEL, D_MODEL), (j, 0, 0)),
                  pl.BlockSpec((1, HALF), fixed),
                  _resident((HALF, D_MODEL), (0, 0)),
                  pl.BlockSpec((4, POOL_DIM, POOL_DIM), lambda i: (0, 0, 0)),
                  pl.BlockSpec((1, HALF), fixed)],
        out_specs=(pl.BlockSpec((TM_PF, HALF), row),
                   pl.BlockSpec((TM_PF, D_MODEL), row)),
        scratch_shapes=[pltpu.VMEM((D_MODEL, D_MODEL), BF16),
                        pltpu.VMEM((TM_PF + 2 * _POOL_HALO, HALF), F32)],
        compiler_params=_params("arbitrary"),
        name="pf_in",
    )(x, x, x, w_in_all, ln_g, wc, pool_w, pool_scale)


_FOURIER_SCALE = 1.0 / float(np.sqrt(SEQ * FOURIER_DIM))


_DFT_SIDE = 64
_DFT_COL_CHUNK = 1024


_DFT_HALF = SEQ // 2
_REV_BLOCK = 256


def _k_fourier(tab_ref, e1_ref, e2_ref, rw_ref, sg_ref, pq_ref, fw_ref, yb_ref,
               cs_ref, ss_ref, pf_ref, qf_ref, z_ref):
    h_rows = _DFT_HALF

    @pl.when(pl.program_id(0) == 0)
    def _():
        def expand(v, e):
            hi = v.astype(BF16)
            lo = (v - hi.astype(F32)).astype(BF16)
            return _dot(hi, e) + _dot(lo, e)

        s = _DFT_SIDE
        for rb in range(h_rows // FOURIER_TR):
            rows = slice(rb * FOURIER_TR, (rb + 1) * FOURIER_TR)
            tab = tab_ref[rows, :]
            for c in range(h_rows // _DFT_COL_CHUNK):
                cols = slice(c * _DFT_COL_CHUNK, (c + 1) * _DFT_COL_CHUNK)
                ca = expand(tab[:, 0:s], e1_ref[:, cols])
                sa = expand(tab[:, s:2 * s], e1_ref[:, cols])
                cb = expand(tab[:, 2 * s:3 * s], e2_ref[:, cols])
                sb = expand(tab[:, 3 * s:4 * s], e2_ref[:, cols])
                cs_ref[rows, cols] = (ca * cb - sa * sb).astype(BF16)
                ss_ref[rows, cols] = (sa * cb + ca * sb).astype(BF16)

    def head_maps(y):
        return jnp.concatenate(
            [_dot(y[:, FOURIER_DIM * h:FOURIER_DIM * (h + 1)].astype(BF16), fw_ref[h])
             for h in range(FOURIER_HEADS)], axis=-1)

    def reversed_block(src_ref, blk):
        lo = _REV_BLOCK * (h_rows // _REV_BLOCK - 1 - blk)
        if blk == 0:
            tail = src_ref[lo:lo + _REV_BLOCK, :]
            win = jnp.concatenate([tail, jnp.zeros_like(tail)], axis=0)
        else:
            win = src_ref[lo:lo + 2 * _REV_BLOCK, :]
        return _dot(rw_ref[...], win)

    n_blocks = h_rows // _REV_BLOCK
    upper = pq_ref.at[h_rows:SEQ]
    for blk in range(n_blocks):
        rows = slice(blk * _REV_BLOCK, (blk + 1) * _REV_BLOCK)
        rev = reversed_block(upper, blk)
        low = pq_ref[rows, :].astype(F32)
        pf_ref[rows, :] = (low[:, :HALF] + rev[:, :HALF]).astype(BF16)
        qf_ref[rows, :] = (low[:, HALF:] - rev[:, HALF:]).astype(BF16)

    p_mid = pq_ref[h_rows:h_rows + 1, :HALF].astype(F32)
    for rb in range(h_rows // FOURIER_TR):
        rows = slice(rb * FOURIER_TR, (rb + 1) * FOURIER_TR)
        a = _dot(cs_ref[rows, :], pf_ref[...])
        bm = _dot(ss_ref[rows, :], qf_ref[...])
        j = lax.broadcasted_iota(I32, (FOURIER_TR, 1), 0) + rb * FOURIER_TR
        t = jnp.where((j & 1) == 0, 1.0, -1.0) * p_mid
        yb_ref[rows, :] = head_maps((a - bm + t) * _FOURIER_SCALE).astype(BF16)
        z_ref[rows, :] = head_maps((a + bm + t) * _FOURIER_SCALE).astype(BF16)

    y_mid = head_maps(_dot(sg_ref[...], pq_ref[:, :HALF]) * _FOURIER_SCALE)[0:1]
    for blk in range(n_blocks):
        up = reversed_block(z_ref, blk)
        if blk == 0:
            r = lax.broadcasted_iota(I32, (_REV_BLOCK, 1), 0)
            up = jnp.where(r == 0, y_mid, up)
        yb_ref[h_rows + blk * _REV_BLOCK:h_rows + (blk + 1) * _REV_BLOCK, :] = up.astype(BF16)


def _fourier(tables, e1, e2, rev_window, signs, pq, fw):
    h_rows = _DFT_HALF
    return pl.pallas_call(
        _k_fourier,
        out_shape=jax.ShapeDtypeStruct((TOKENS, HALF), BF16),
        grid=(BATCH,),
        in_specs=[_resident((h_rows, 4 * _DFT_SIDE), (0, 0)),
                  _resident((_DFT_SIDE, h_rows), (0, 0)),
                  _resident((_DFT_SIDE, h_rows), (0, 0)),
                  _resident((_REV_BLOCK, 2 * _REV_BLOCK), (0, 0)),
                  _resident((V7X_SUBLANES, SEQ), (0, 0)),
                  pl.BlockSpec((SEQ, D_MODEL), lambda b: (b, 0), pipeline_mode=pl.Buffered(1)),
                  pl.BlockSpec((4, FOURIER_DIM, FOURIER_DIM), lambda b: (0, 0, 0))],
        out_specs=pl.BlockSpec((SEQ, HALF), lambda b: (b, 0)),
        scratch_shapes=[pltpu.VMEM((h_rows, h_rows), BF16), pltpu.VMEM((h_rows, h_rows), BF16),
                        pltpu.VMEM((h_rows, HALF), BF16), pltpu.VMEM((h_rows, HALF), BF16),
                        pltpu.VMEM((h_rows, HALF), BF16)],
        compiler_params=_params("arbitrary"),
        name="pf_fourier",
    )(tables, e1, e2, rev_window, signs, pq, fw)


_HG_STAGE_COLS = 512


def _k_hg_in(x_ref, w_hbm, o_ref, wbf_ref, stage_ref, sem):
    @pl.when(pl.program_id(0) == 0)
    def _():
        n_chunks = HG_N_PROJ * D_MODEL // _HG_STAGE_COLS

        def chunk_copy(c):
            cols = pl.ds(c * _HG_STAGE_COLS, _HG_STAGE_COLS)
            return pltpu.make_async_copy(w_hbm.at[:, cols], stage_ref.at[c % 2], sem.at[c % 2])

        chunk_copy(0).start()
        for c in range(n_chunks):
            if c + 1 < n_chunks:
                chunk_copy(c + 1).start()
            chunk_copy(c).wait()
            wbf_ref[:, c * _HG_STAGE_COLS:(c + 1) * _HG_STAGE_COLS] = stage_ref[c % 2].astype(BF16)

    xb = x_ref[...].astype(BF16)
    for p in range(HG_N_PROJ):
        acc = _dot(xb, wbf_ref[:, p * D_MODEL:(p + 1) * D_MODEL])
        if p == 0:
            acc = _silu(acc)
        for h in range(HG_HEADS):
            o_ref[p, h] = acc[:, HG_DIM * h:HG_DIM * (h + 1)].astype(BF16)


def _hg_in(x, w_in):
    return pl.pallas_call(
        _k_hg_in,
        out_shape=jax.ShapeDtypeStruct((HG_N_PROJ, HG_HEADS, TOKENS, HG_DIM), BF16),
        grid=(TOKENS // TM,),
        in_specs=[pl.BlockSpec((TM, D_MODEL), lambda i: (i, 0)),
                  pl.BlockSpec(memory_space=pl.ANY)],
        out_specs=pl.BlockSpec((HG_N_PROJ, HG_HEADS, TM, HG_DIM), lambda i: (0, 0, i, 0)),
        scratch_shapes=[pltpu.VMEM((D_MODEL, HG_N_PROJ * D_MODEL), BF16),
                        pltpu.VMEM((2, D_MODEL, _HG_STAGE_COLS), F32),
                        pltpu.SemaphoreType.DMA((2,))],
        compiler_params=_params("arbitrary"),
        name="hg_in",
    )(x, w_in)


def _gla_prep(q, z, lbv, tri, rev):
    L = HG_CHUNK
    G = q.shape[0] // L
    f = lbv + (1.0 - lbv) * jax.nn.sigmoid(z.astype(F32))
    lf = jnp.log(f).reshape(G, L, HG_DIM)
    k3 = (1.0 - f).reshape(G, L, HG_DIM)
    q3 = q.astype(F32).reshape(G, L, HG_DIM)
    hi32 = lax.bitcast_convert_type(
        lax.bitcast_convert_type(lf, jnp.uint32) & jnp.uint32(0xFFFF0000), F32)
    lf_hi = hi32.astype(BF16)
    lf_lo = (lf - hi32).astype(BF16)
    b2 = jnp.einsum('glm,gmk->glk', jnp.broadcast_to(tri, (G, L, L)),
                    jnp.concatenate([lf_hi, lf_lo], axis=-1), preferred_element_type=F32)
    b = b2[..., :HG_DIM] + b2[..., HG_DIM:]
    mid = L // 2 if not rev else L - 1 - L // 2
    end = L - 1 if not rev else 0
    b_ref = b[:, mid:mid + 1, :]
    b_last = b[:, end:end + 1, :]
    e_pos = jnp.exp(b - b_ref)
    qt32 = q3 * e_pos
    kt32 = k3 * (1.0 / e_pos)
    ku = kt32 * jnp.exp(b_last - b_ref)
    qe = qt32 * jnp.exp(b_ref)
    dec = jnp.exp(b_last)
    return qt32.astype(BF16), kt32.astype(BF16), ku.astype(BF16), qe.astype(BF16), dec


def _gla_intra(qt, kt, ku, v, rev):
    L = HG_CHUNK
    v3 = v.reshape(qt.shape[0], L, HG_DIM)
    sc = jnp.einsum('glk,gmk->glm', qt, kt, preferred_element_type=F32)
    li = lax.broadcasted_iota(I32, (L, L), 0)
    mi = lax.broadcasted_iota(I32, (L, L), 1)
    causal = (li >= mi) if not rev else (li <= mi)
    sc = jnp.where(causal[None], sc, 0.0).astype(BF16)
    o_intra = jnp.einsum('glm,gmv->glv', sc, v3, preferred_element_type=F32)
    u_t = jnp.einsum('glv,glk->gvk', v3, ku, preferred_element_type=F32)
    return o_intra, u_t


def _gla_inter(qe, dec, o_intra, u_t, st, rev):
    G = qe.shape[0]
    outs = [None] * G
    for n in (range(G) if not rev else range(G - 1, -1, -1)):
        o_inter = lax.dot_general(qe[n], st.astype(BF16), _NT, preferred_element_type=F32)
        outs[n] = o_intra[n] + o_inter
        st = dec[n] * st + u_t[n]
    return jnp.concatenate(outs, axis=0), st


_GLA_N_OPERANDS = 5


def _k_gla(q_ref, v_ref, zf_ref, zb_ref, g_ref, lb_ref, ng_ref, lt_ref, ut_ref, o_ref, of_ref, ob_ref,
           *operand_refs):
    n_groups = SEQ // GLA_ROWS
    per_slot = 2 * _GLA_N_OPERANDS
    slots = [[operand_refs[s * per_slot + d * _GLA_N_OPERANDS:s * per_slot + (d + 1) * _GLA_N_OPERANDS]
              for d in range(2)] for s in range(2)]
    directions = ((zf_ref, lt_ref, False), (zb_ref, ut_ref, True))

    def group_rows(t, rev):
        gi = t if not rev else n_groups - 1 - t
        return slice(gi * GLA_ROWS, (gi + 1) * GLA_ROWS)

    def prep(slot, t):
        for d, (z_ref, tri_ref, rev) in enumerate(directions):
            rows = group_rows(t, rev)
            vals = _gla_prep(q_ref[0, 0, rows, :], z_ref[0, 0, rows, :], lb_ref[d, 0], tri_ref[...], rev)
            for ref, val in zip(slots[slot][d], vals):
                ref[...] = val

    def mix(slot, t, states):
        out_refs = (of_ref, ob_ref)
        intra = []
        for d, (_, _, rev) in enumerate(directions):
            qt_ref, kt_ref, ku_ref, _, _ = slots[slot][d]
            intra.append(_gla_intra(qt_ref[...], kt_ref[...], ku_ref[...],
                                    v_ref[0, 0, group_rows(t, rev), :], rev))
        new_states = []
        for d, (_, _, rev) in enumerate(directions):
            _, _, _, qe_ref, dec_ref = slots[slot][d]
            o, st = _gla_inter(qe_ref[...], dec_ref[...], *intra[d], states[d], rev)
            out_refs[d][group_rows(t, rev), :] = o
            new_states.append(st)
        return new_states

    def finish(gi):
        rows = slice(gi * GLA_ROWS, (gi + 1) * GLA_ROWS)
        o = of_ref[rows, :] + ob_ref[rows, :]
        o = o * lax.rsqrt(jnp.mean(o * o, axis=-1, keepdims=True) + LN_EPS)
        o = o * ng_ref[...] * _silu(g_ref[0, 0, rows, :].astype(F32))
        o_ref[0, rows, :] = o.astype(BF16)

    states = [jnp.zeros((HG_DIM, HG_DIM), F32)] * 2
    prep(0, 0)
    for t in range(n_groups):
        if t + 1 < n_groups:
            prep((t + 1) % 2, t + 1)
        states = mix(t % 2, t, states)
        for gi in {t, n_groups - 1 - t}:
            if max(gi, n_groups - 1 - gi) == t:
                finish(gi)


def _gla(proj, lb, norm_g, tril, triu):
    def pspec(p):
        return pl.BlockSpec((1, 1, SEQ, HG_DIM), lambda b, h: (p, h, b, 0))
    fixed = lambda b, h: (0, 0)
    return pl.pallas_call(
        _k_gla,
        out_shape=jax.ShapeDtypeStruct((HG_HEADS, TOKENS, HG_DIM), BF16),
        grid=(BATCH, HG_HEADS),
        in_specs=[pspec(0), pspec(1), pspec(2), pspec(3), pspec(4),
                  pl.BlockSpec((2, 1, 1, HG_DIM), lambda b, h: (0, h, 0, 0)),
                  pl.BlockSpec((1, HG_DIM), fixed),
                  pl.BlockSpec((HG_CHUNK, HG_CHUNK), fixed),
                  pl.BlockSpec((HG_CHUNK, HG_CHUNK), fixed)],
        out_specs=pl.BlockSpec((1, SEQ, HG_DIM), lambda b, h: (h, b, 0)),
        scratch_shapes=[pltpu.VMEM((SEQ, HG_DIM), F32), pltpu.VMEM((SEQ, HG_DIM), F32)]
        + 4 * ([pltpu.VMEM((GLA_GROUP, HG_CHUNK, HG_DIM), BF16)] * (_GLA_N_OPERANDS - 1)
               + [pltpu.VMEM((GLA_GROUP, 1, HG_DIM), F32)]),
        compiler_params=_params("parallel", "parallel"),
        name="hg_gla",
    )(proj, proj, proj, proj, proj, lb, norm_g, tril, triu)


def _k_mm(x_ref, w_ref, o_ref):
    o_ref[...] = _dot(x_ref[...].astype(BF16), w_ref[0].astype(BF16)).astype(o_ref.dtype)


def _kv_proj(mem2d, wkv_all, l):
    rows, cols = mem2d.shape[0], wkv_all.shape[2]
    return pl.pallas_call(
        _k_mm,
        out_shape=jax.ShapeDtypeStruct((rows, cols), BF16),
        grid=(cols // D_MODEL, rows // TM),
        in_specs=[pl.BlockSpec((TM, D_MODEL), lambda j, i: (i, 0)),
                  pl.BlockSpec((1, D_MODEL, D_MODEL), lambda j, i: (l, 0, j))],
        out_specs=pl.BlockSpec((TM, D_MODEL), lambda j, i: (i, j)),
        compiler_params=_params("parallel", "parallel"),
        name="xa_kv",
    )(mem2d, wkv_all)


ROW_WORDS = D_MODEL // 2
ROW_SLABS = ROW_WORDS // V7X_LANES
U32 = jnp.uint32
_HI_MASK = 0xFFFF0000


def _to_row_tiles(ref, x):
    n = x.shape[0]
    as_bits = lambda v: lax.bitcast_convert_type(v.astype(BF16).astype(F32), U32)
    words = (as_bits(x[:, :ROW_WORDS]) >> 16) | (as_bits(x[:, ROW_WORDS:]) & U32(_HI_MASK))
    for s in range(ROW_SLABS):
        ref[pl.ds(s, n, stride=ROW_SLABS), :] = words[:, V7X_LANES * s:V7X_LANES * (s + 1)]


def _from_row_tiles(ref):
    n = ref.shape[0] // ROW_SLABS
    words = jnp.concatenate([ref[pl.ds(s, n, stride=ROW_SLABS), :] for s in range(ROW_SLABS)], axis=-1)
    lo = lax.bitcast_convert_type(words << 16, F32)
    hi = lax.bitcast_convert_type(words & U32(_HI_MASK), F32)
    return jnp.concatenate([lo, hi], axis=-1)


def _first_argmax_rows(v, n_rows):
    m = jnp.max(v, axis=0, keepdims=True)
    rows = lax.broadcasted_iota(I32, v.shape, 0)
    idx = jnp.min(jnp.where(v == m, rows, n_rows), axis=0, keepdims=True)
    return m, idx


def _route(lt):
    gl = lt[0:MOE_GROUPS, :]
    gmax, gi = _first_argmax_rows(gl, MOE_GROUPS)
    p_grp = 1.0 / jnp.sum(jnp.exp(gl - gmax), axis=0, keepdims=True)
    sel = lt[MOE_GROUPS:MOE_GROUPS + MOE_PER_GROUP, :]
    for g in range(1, MOE_GROUPS):
        lo = MOE_GROUPS + MOE_PER_GROUP * g
        sel = jnp.where(gi == g, lt[lo:lo + MOE_PER_GROUP, :], sel)
    m1, i1 = _first_argmax_rows(sel, MOE_PER_GROUP)
    rows = lax.broadcasted_iota(I32, sel.shape, 0)
    m2, i2 = _first_argmax_rows(jnp.where(rows == i1, -jnp.inf, sel), MOE_PER_GROUP)
    e2 = jnp.exp(m2 - m1)
    g1 = p_grp / (1.0 + e2)
    return (gi * MOE_PER_GROUP + i1, gi * MOE_PER_GROUP + i2), (g1, g1 * e2)


def _rank_in_expert(e0, e1, strict_upper, carry_ref, counted):
    rows = lax.broadcasted_iota(I32, (MOE_EXPERTS, e0.shape[1]), 0)
    oh0 = rows == e0
    oh1 = rows == e1
    oh = jnp.where(oh0 | oh1, 1.0, 0.0)
    before = _dot(oh.astype(BF16), strict_upper) + carry_ref[:, 0:1]
    r0 = jnp.sum(jnp.where(oh0, before, 0.0), axis=0, keepdims=True).astype(I32)
    r1 = jnp.sum(jnp.where(oh1, before, 0.0), axis=0, keepdims=True).astype(I32)
    carry_ref[...] = carry_ref[...] + counted * jnp.sum(oh, axis=1, keepdims=True)
    return r0, r1


def _k_block(heads_major, *refs):
    n_mix = 1 if heads_major else 2
    mix_refs = refs[:n_mix]
    (wm_ref, x_ref, g0_ref, b0_ref, wq_ref, kv_ref, wo_ref, g_ref, b_ref, wr_ref, br_ref, tri_ref,
     o_ref, o3_ref, e_ref, gate_ref, rank_ref, cnt_ref,
     wmbf_ref, wqbf_ref, wobf_ref, carry_ref, pre_ref) = refs[n_mix:]
    step = pl.program_id(0)
    _cast_weight_once(wm_ref, wmbf_ref)
    _cast_weight_once(wq_ref, wqbf_ref)
    _cast_weight_once(wo_ref, wobf_ref)

    @pl.when(step == 0)
    def _():
        carry_ref[...] = jnp.zeros_like(carry_ref)
        pre_ref[...] = jnp.zeros_like(pre_ref)

    quarter = TM // 4
    routed = []

    def stage2(c):
        rows = slice(c * quarter, (c + 1) * quarter)
        y = _layer_norm(pre_ref[rows, :], g_ref[...], b_ref[...])
        o_ref[rows, :] = y
        _to_row_tiles(o3_ref.at[pl.ds(c * quarter * ROW_SLABS, quarter * ROW_SLABS)], y)
        lt = lax.dot_general(wr_ref[...], y.astype(BF16), _NT, preferred_element_type=F32) + br_ref[...]
        routed.append(_route(lt))

    stage2(0)
    if heads_major:
        a = jnp.concatenate([mix_refs[0][h] for h in range(HG_HEADS)], axis=-1)
    else:
        a = jnp.concatenate([mix_refs[0][...], mix_refs[1][...]], axis=-1)
    h1 = _dot(a, wmbf_ref[...])
    stage2(1)
    x = _post_ln(x_ref[...], h1, g0_ref[...], b0_ref[...])
    q = _dot(x.astype(BF16), wqbf_ref[...]).astype(BF16)
    stage2(2)
    outs = []
    for h in range(XA_HEADS):
        cols = slice(XA_HEAD_DIM * h, XA_HEAD_DIM * (h + 1))
        kh = kv_ref[:, cols]
        vh = kv_ref[:, D_MODEL + XA_HEAD_DIM * h:D_MODEL + XA_HEAD_DIM * (h + 1)]
        s = lax.dot_general(q[:, cols], kh, _NT, preferred_element_type=F32) * (XA_HEAD_DIM ** -0.5)
        e = jnp.exp(s - jnp.max(s, axis=-1, keepdims=True))
        p = e / jnp.sum(e, axis=-1, keepdims=True)
        outs.append(_dot(p.astype(BF16), vh))
    o = jnp.concatenate(outs, axis=-1).astype(BF16)
    h2 = _dot(o, wobf_ref[...])
    stage2(3)
    e0, e1, gate0, gate1 = [jnp.concatenate([r[a_][b_] for r in routed], axis=-1)
                            for a_ in range(2) for b_ in range(2)]
    r0, r1 = _rank_in_expert(e0, e1, tri_ref[...], carry_ref, jnp.where(step > 0, 1.0, 0.0))
    e_ref[0:1, :], e_ref[1:2, :] = e0, e1
    gate_ref[0:1, :], gate_ref[1:2, :] = gate0, gate1
    rank_ref[0:1, :], rank_ref[1:2, :] = r0, r1
    cnt_ref[...] = carry_ref[...]
    pre_ref[...] = DN_ALPHA * x + h2


def _block(mix, heads_major, w_mix_all, j, x, g0, b0, wq_all, kv, wo_all, l, g1, b1, wr, br, strict_upper):
    n_tiles = TOKENS // TM
    tiles_per_seq = SEQ // TM
    cur = lambda i: jnp.minimum(i, n_tiles - 1)
    prev = lambda i: jnp.maximum(i - 1, 0)
    row_in = lambda i: (cur(i), 0)
    row_out = lambda i: (prev(i), 0)
    fixed = lambda i: (0, 0)
    tok = pl.BlockSpec((MOE_TOPK, TM), lambda i: (0, prev(i)))
    if heads_major:
        mix_specs = [pl.BlockSpec((HG_HEADS, TM, HG_DIM), lambda i: (0, cur(i), 0))]
    else:
        mix_specs = [pl.BlockSpec((TM, HALF), row_in), pl.BlockSpec((TM, HALF), row_in)]
    return pl.pallas_call(
        functools.partial(_k_block, heads_major),
        out_shape=(jax.ShapeDtypeStruct((TOKENS, D_MODEL), F32),
                   jax.ShapeDtypeStruct((TOKENS * ROW_SLABS, V7X_LANES), U32),
                   jax.ShapeDtypeStruct((MOE_TOPK, TOKENS), I32),
                   jax.ShapeDtypeStruct((MOE_TOPK, TOKENS), F32),
                   jax.ShapeDtypeStruct((MOE_TOPK, TOKENS), I32),
                   jax.ShapeDtypeStruct((MOE_EXPERTS, V7X_LANES), F32)),
        grid=(n_tiles + 1,),
        in_specs=mix_specs + [
            _resident((1, D_MODEL, D_MODEL), (j, 0, 0)),
            pl.BlockSpec((TM, D_MODEL), row_in),
            pl.BlockSpec((1, D_MODEL), fixed), pl.BlockSpec((1, D_MODEL), fixed),
            _resident((1, D_MODEL, D_MODEL), (l, 0, 0)),
            pl.BlockSpec((MEM_LEN, 2 * D_MODEL), lambda i: (cur(i) // tiles_per_seq, 0)),
            _resident((1, D_MODEL, D_MODEL), (l, 0, 0)),
            pl.BlockSpec((1, D_MODEL), fixed), pl.BlockSpec((1, D_MODEL), fixed),
            _resident((ROUTER_ROWS, D_MODEL), (0, 0)),
            _resident((ROUTER_ROWS, 1), (0, 0)),
            _resident((TM, TM), (0, 0))],
        out_specs=(pl.BlockSpec((TM, D_MODEL), row_out),
                   pl.BlockSpec((TM * ROW_SLABS, V7X_LANES), row_out),
                   tok, tok, tok,
                   pl.BlockSpec((MOE_EXPERTS, V7X_LANES), fixed)),
        scratch_shapes=[pltpu.VMEM((D_MODEL, D_MODEL), BF16), pltpu.VMEM((D_MODEL, D_MODEL), BF16),
                        pltpu.VMEM((D_MODEL, D_MODEL), BF16), pltpu.VMEM((MOE_EXPERTS, V7X_LANES), F32),
                        pltpu.VMEM((TM, D_MODEL), F32)],
        compiler_params=_params("arbitrary"),
        name="attn_block",
    )(*mix, w_mix_all, x, g0, b0, wq_all, kv, wo_all, g1, b1, wr, br, strict_upper)


def _k_pos(e_ref, rank_ref, off_ref, pos_ref):
    rows = lax.broadcasted_iota(I32, (MOE_EXPERTS, TM_ROUTE), 0)
    for k in range(MOE_TOPK):
        start = jnp.sum(jnp.where(rows == e_ref[k:k + 1, :], off_ref[...], 0), axis=0, keepdims=True)
        pos_ref[k:k + 1, :] = start + rank_ref[k:k + 1, :]


def _positions(eidx, rank, offsets):
    tok = pl.BlockSpec((MOE_TOPK, TM_ROUTE), lambda i: (0, i))
    return pl.pallas_call(
        _k_pos,
        out_shape=jax.ShapeDtypeStruct((MOE_TOPK, TOKENS), I32),
        grid=(TOKENS // TM_ROUTE,),
        in_specs=[tok, tok, pl.BlockSpec((MOE_EXPERTS, 1), lambda i: (0, 0))],
        out_specs=tok,
        compiler_params=_params("parallel"),
        name="moe_pos",
    )(eidx, rank, offsets)


_ISSUE_BATCH = 8


def _row_copy(src_ref, src_row, dst_ref, dst_row, sem):
    src = pl.ds(pl.multiple_of(src_row * ROW_SLABS, ROW_SLABS), ROW_SLABS)
    dst = pl.ds(pl.multiple_of(dst_row * ROW_SLABS, ROW_SLABS), ROW_SLABS)
    return pltpu.make_async_copy(src_ref.at[src], dst_ref.at[dst], sem)


_DISPATCH_SLOTS = 4


def _k_dispatch(pos_ref, ends_ref, x_ref, xs_ref, zero_ref, stage_ref, row_sems, load_sems, zsem):
    step = pl.program_id(0)
    n_steps = pl.num_programs(0)
    block_rows = TM_PERM * ROW_SLABS

    def load(tile):
        slot = lax.rem(tile, _DISPATCH_SLOTS)
        src = x_ref.at[pl.ds(pl.multiple_of(tile * block_rows, block_rows), block_rows)]
        return pltpu.make_async_copy(src, stage_ref.at[slot], load_sems.at[slot])

    def drain_rows(tile):
        slot = lax.rem(tile, _DISPATCH_SLOTS)
        for _ in range(MOE_TOPK):
            pltpu.make_async_copy(stage_ref.at[slot], xs_ref.at[pl.ds(0, block_rows)],
                                  row_sems.at[slot]).wait()

    @pl.when(step == 0)
    def _():
        load(0).start()
        pl.when(n_steps > 1)(load(1).start)
        zero_ref[...] = jnp.zeros_like(zero_ref)

        def last_tile(e):
            seg_start = ends_ref[e - 1] if e > 0 else 0
            start = ends_ref[e] - TE
            return ends_ref[e] > seg_start, zero_fill(start)

        def unused_tile(t):
            start = ends_ref[MOE_EXPERTS - 1] + t * TE
            return start < SORTED_ROWS, zero_fill(start)

        def zero_fill(start_row):
            rows = pl.ds(pl.multiple_of(start_row * ROW_SLABS, TE * ROW_SLABS), TE * ROW_SLABS)
            return pltpu.make_async_copy(zero_ref, xs_ref.at[rows], zsem)

        fills = [last_tile(e) for e in range(MOE_EXPERTS)] + [unused_tile(t) for t in range(MOE_EXPERTS)]
        for needed, copy in fills:
            pl.when(needed)(copy.start)
        for needed, copy in fills:
            pl.when(needed)(copy.wait)

    pl.when(step >= 2)(lambda: drain_rows(step - 2))
    pl.when(step + 2 < n_steps)(lambda: load(step + 2).start())
    load(step).wait()

    slot = lax.rem(step, _DISPATCH_SLOTS)
    base = step * TM_PERM

    def body(jb, c):
        j0 = jb * _ISSUE_BATCH
        dst = [[pos_ref[k * TOKENS + base + j0 + u] for k in range(MOE_TOPK)]
               for u in range(_ISSUE_BATCH)]
        for u in range(_ISSUE_BATCH):
            for k in range(MOE_TOPK):
                _row_copy(stage_ref.at[slot], j0 + u, xs_ref, dst[u][k],
                          row_sems.at[slot]).start(priority=k)
        return c

    lax.fori_loop(0, TM_PERM // _ISSUE_BATCH, body, 0)

    @pl.when(step == n_steps - 1)
    def _():
        pl.when(step >= 1)(lambda: drain_rows(step - 1))
        drain_rows(step)


def _dispatch(pos_flat, ends, x):
    return pl.pallas_call(
        _k_dispatch,
        out_shape=jax.ShapeDtypeStruct((SORTED_ROWS * ROW_SLABS, V7X_LANES), U32),
        grid_spec=pltpu.PrefetchScalarGridSpec(
            num_scalar_prefetch=2,
            grid=(TOKENS // TM_PERM,),
            in_specs=[pl.BlockSpec(memory_space=pl.ANY)],
            out_specs=pl.BlockSpec(memory_space=pl.ANY),
            scratch_shapes=[pltpu.VMEM((TE * ROW_SLABS, V7X_LANES), U32),
                            pltpu.VMEM((_DISPATCH_SLOTS, TM_PERM * ROW_SLABS, V7X_LANES), U32),
                            pltpu.SemaphoreType.DMA((_DISPATCH_SLOTS,)),
                            pltpu.SemaphoreType.DMA((_DISPATCH_SLOTS,)),
                            pltpu.SemaphoreType.DMA],
        ),
        compiler_params=_params("arbitrary"),
        name="moe_dispatch",
    )(pos_flat, ends, x)


_EXPERT_SUBTILES = 1


def _k_experts(te_ref, na_ref, xs_ref, wg_ref, wu_ref, wd_ref, ys_ref, wgubf_ref, wdbf_ref):
    i = pl.program_id(0)
    active = i < na_ref[0]
    new_expert = jnp.logical_or(i == 0, te_ref[jnp.maximum(i - 1, 0)] != te_ref[i])

    @pl.when(jnp.logical_and(active, new_expert))
    def _():
        wgubf_ref[:, :MOE_HIDDEN] = wg_ref[0, 0].astype(BF16)
        wgubf_ref[:, MOE_HIDDEN:] = wu_ref[0, 0].astype(BF16)
        wdbf_ref[...] = wd_ref[0, 0].astype(BF16)

    @pl.when(active)
    def _():
        sub_rows = TE // _EXPERT_SUBTILES * ROW_SLABS
        for t in range(_EXPERT_SUBTILES):
            rows = pl.ds(t * sub_rows, sub_rows)
            xb = _from_row_tiles(xs_ref.at[rows]).astype(BF16)
            gu = _dot(xb, wgubf_ref[...])
            h = _silu(gu[:, :MOE_HIDDEN]) * gu[:, MOE_HIDDEN:]
            _to_row_tiles(ys_ref.at[rows], _dot(h.astype(BF16), wdbf_ref[...]))

    @pl.when(jnp.logical_not(active))
    def _():
        ys_ref[...] = jnp.zeros_like(ys_ref)


def _experts(tile_expert, n_active, xs, wg_all, wu_all, wd_all, l):
    def xs_index(i, te, na):
        return (jnp.minimum(i, na[0] - 1), 0)

    def w_index(i, te, na):
        return (l, te[i], 0, 0)

    return pl.pallas_call(
        _k_experts,
        out_shape=jax.ShapeDtypeStruct((SORTED_ROWS * ROW_SLABS, V7X_LANES), U32),
        grid_spec=pltpu.PrefetchScalarGridSpec(
            num_scalar_prefetch=2,
            grid=(N_EXPERT_TILES,),
            in_specs=[pl.BlockSpec((TE * ROW_SLABS, V7X_LANES), xs_index),
                      pl.BlockSpec((1, 1, D_MODEL, MOE_HIDDEN), w_index),
                      pl.BlockSpec((1, 1, D_MODEL, MOE_HIDDEN), w_index),
                      pl.BlockSpec((1, 1, MOE_HIDDEN, D_MODEL), w_index)],
            out_specs=pl.BlockSpec((TE * ROW_SLABS, V7X_LANES), lambda i, te, na: (i, 0)),
            scratch_shapes=[pltpu.VMEM((D_MODEL, 2 * MOE_HIDDEN), BF16),
                            pltpu.VMEM((MOE_HIDDEN, D_MODEL), BF16)],
        ),
        compiler_params=_params("arbitrary"),
        name="moe_experts",
    )(tile_expert, n_active, xs, wg_all, wu_all, wd_all)


_COMBINE_CHUNKS = 8


def _k_combine(pos_ref, ys_ref, gt_ref, x_ref, g_ref, b_ref, o_ref, buf_ref, sem):
    step = pl.program_id(0)
    last = pl.num_programs(0) - 1

    def issue_batch(base, j0, slot):
        src = [[pos_ref[k * TOKENS + base + j0 + u] for k in range(MOE_TOPK)]
               for u in range(_ISSUE_BATCH)]
        for u in range(_ISSUE_BATCH):
            for k in range(MOE_TOPK):
                _row_copy(ys_ref, src[u][k], buf_ref.at[slot, k], j0 + u,
                          sem.at[slot]).start(priority=k)

    def drain(slot):
        for k in range(MOE_TOPK):
            pltpu.make_async_copy(ys_ref.at[pl.ds(0, TM_COMB * ROW_SLABS)], buf_ref.at[slot, k],
                                  sem.at[slot]).wait()

    @pl.when(step == 0)
    def _():
        def body(jb, c):
            issue_batch(0, jb * _ISSUE_BATCH, 0)
            return c
        lax.fori_loop(0, TM_COMB // _ISSUE_BATCH, body, 0)

    slot = lax.rem(step, 2)
    drain(slot)
    next_base = jnp.minimum(step + 1, last) * TM_COMB
    chunk = TM_COMB // _COMBINE_CHUNKS
    for c in range(_COMBINE_CHUNKS):
        for jb in range(chunk // _ISSUE_BATCH):
            issue_batch(next_base, c * chunk + jb * _ISSUE_BATCH, 1 - slot)
        rows = slice(c * chunk, (c + 1) * chunk)
        tiles = pl.ds(c * chunk * ROW_SLABS, chunk * ROW_SLABS)
        h = (gt_ref[rows, 0:1] * _from_row_tiles(buf_ref.at[slot, 0, tiles])
             + gt_ref[rows, 1:2] * _from_row_tiles(buf_ref.at[slot, 1, tiles]))
        o_ref[rows, :] = _post_ln(x_ref[rows, :], h, g_ref[...], b_ref[...])

    pl.when(step == last)(lambda: drain(1 - slot))


def _combine(pos_flat, ys, gates_t, x, g, b):
    row = lambda i, pos: (i, 0)
    fixed = lambda i, pos: (0, 0)
    return pl.pallas_call(
        _k_combine,
        out_shape=jax.ShapeDtypeStruct((TOKENS, D_MODEL), F32),
        grid_spec=pltpu.PrefetchScalarGridSpec(
            num_scalar_prefetch=1,
            grid=(TOKENS // TM_COMB,),
            in_specs=[pl.BlockSpec(memory_space=pl.ANY),
                      pl.BlockSpec((TM_COMB, MOE_TOPK), row),
                      pl.BlockSpec((TM_COMB, D_MODEL), row),
                      pl.BlockSpec((1, D_MODEL), fixed), pl.BlockSpec((1, D_MODEL), fixed)],
            out_specs=pl.BlockSpec((TM_COMB, D_MODEL), row),
            scratch_shapes=[pltpu.VMEM((2, MOE_TOPK, TM_COMB * ROW_SLABS, V7X_LANES), U32),
                            pltpu.SemaphoreType.DMA((2,))],
        ),
        compiler_params=_params("arbitrary"),
        name="moe_combine",
    )(pos_flat, ys, gates_t, x, g, b)


def _router_params(w_group, b_group, w_expert, b_expert):
    pad = ROUTER_ROWS - MOE_GROUPS - MOE_EXPERTS
    wr = jnp.concatenate([w_group.T, w_expert.T, jnp.zeros((pad, D_MODEL), F32)], axis=0)
    br = jnp.concatenate([b_group, b_expert, jnp.zeros((pad,), F32)])[:, None]
    return wr.astype(BF16), br.astype(F32)


def _moe(x, x_tiles, eidx, gates, rank, cnt, w_gate_all, w_up_all, w_down_all, l, g, b):
    counts = cnt[:, 0].astype(I32)
    padded = ((counts + TE - 1) // TE) * TE
    ends = jnp.cumsum(padded)
    offsets = ends - padded
    tile_start = jnp.arange(N_EXPERT_TILES, dtype=I32) * TE
    tile_expert = jnp.sum((tile_start[:, None] >= ends[None, :]).astype(I32), axis=1)
    tile_expert = jnp.minimum(tile_expert, MOE_EXPERTS - 1)
    n_active = (ends[-1:] // TE).astype(I32)
    pos = _positions(eidx, rank, offsets[:, None])
    pos_flat = pos.reshape(-1)
    xs = _dispatch(pos_flat, ends.astype(I32), x_tiles)
    ys = _experts(tile_expert, n_active, xs, w_gate_all, w_up_all, w_down_all, l)
    return _combine(pos_flat, ys, gates.T, x, g, b)


def _channel_dft_matrix():
    c = np.arange(FOURIER_DIM)
    ang = 2.0 * np.pi * ((c[:, None] * c[None, :]) % FOURIER_DIM) / FOURIER_DIM
    wc = np.zeros((HALF, D_MODEL), np.float32)
    for h in range(FOURIER_HEADS):
        rows = slice(FOURIER_DIM * h, FOURIER_DIM * (h + 1))
        wc[rows, FOURIER_DIM * h:FOURIER_DIM * (h + 1)] = np.cos(ang)
        wc[rows, HALF + FOURIER_DIM * h:HALF + FOURIER_DIM * (h + 1)] = np.sin(ang)
    return jnp.asarray(wc, BF16)


def _sequence_dft_tables():
    side = _DFT_SIDE
    j = np.arange(_DFT_HALF, dtype=np.int64)[:, None]
    k = np.arange(side, dtype=np.int64)[None, :]
    ang_a = ((j * k) % side).astype(np.float64) * (2.0 * np.pi / side)
    ang_b = ((j * k) % SEQ).astype(np.float64) * (2.0 * np.pi / SEQ)
    tables = np.concatenate([np.cos(ang_a), np.sin(ang_a), np.cos(ang_b), np.sin(ang_b)], axis=1)
    col = np.arange(_DFT_HALF)
    e1 = (col[None, :] // side == np.arange(side)[:, None]).astype(np.float32)
    e2 = (col[None, :] % side == np.arange(side)[:, None]).astype(np.float32)
    r = np.arange(_REV_BLOCK)
    rev_window = (np.arange(2 * _REV_BLOCK)[None, :] == (_REV_BLOCK - r)[:, None]).astype(np.float32)
    signs = np.zeros((V7X_SUBLANES, SEQ), np.float32)
    signs[0] = 1.0 - 2.0 * (np.arange(SEQ) % 2)
    return (jnp.asarray(tables, F32), jnp.asarray(e1, BF16), jnp.asarray(e2, BF16),
            jnp.asarray(rev_window, BF16), jnp.asarray(signs, BF16))


def kernel(x, mem, pf_w_in, pf_pool_w, pf_pool_scale, pf_fourier_ln_g, pf_fourier_w, pf_w_out, hg_w_in, hg_lower_bounds, hg_norm_g, hg_w_out, xa_wq, xa_wkv, xa_wo, moe_w_group, moe_b_group, moe_w_expert, moe_b_expert, moe_w_gate, moe_w_up, moe_w_down, ln_g, ln_b):
    bf = lambda a: a.astype(BF16)
    xt = x.reshape(TOKENS, D_MODEL)
    mem2d = mem.reshape(BATCH * MEM_LEN, D_MODEL)
    lb_all = jnp.cumsum(jax.nn.softmax(hg_lower_bounds.astype(F32), axis=0), axis=0)
    lb_all = lb_all - lb_all[:1]
    tril = jnp.asarray(np.tril(np.ones((HG_CHUNK, HG_CHUNK), np.float32)), BF16)
    triu = jnp.asarray(np.triu(np.ones((HG_CHUNK, HG_CHUNK), np.float32)), BF16)
    strict_upper = jnp.asarray(np.triu(np.ones((TM, TM), np.float32), k=1), BF16)

    for l in range(DEPTH):
        j = l // 2
        lng = lambda s: ln_g[l, s][None, :]
        lnb = lambda s: ln_b[l, s][None, :]
        if l % 2 == 0:
            dft_consts = _sequence_dft_tables()
            ya, pq = _pf_in(xt, pf_w_in, j, pf_fourier_ln_g[j].reshape(1, HALF), _channel_dft_matrix(),
                            bf(pf_pool_w[j]), pf_pool_scale[j][None, :])
            yb = _fourier(*dft_consts, pq, bf(pf_fourier_w[j]))
            mix, heads_major, w_mix = (ya, yb), False, pf_w_out
        else:
            proj = _hg_in(xt, hg_w_in[j])
            lb = lb_all[l].reshape(2, HG_HEADS, 1, HG_DIM)
            og = _gla(proj, lb, hg_norm_g[j][None, :], tril, triu)
            mix, heads_major, w_mix = (og,), True, hg_w_out
        kv = _kv_proj(mem2d, xa_wkv, l)
        wr, br = _router_params(moe_w_group[l], moe_b_group[l], moe_w_expert[l], moe_b_expert[l])
        xt, xt_tiles, eidx, gates, rank, cnt = _block(
            mix, heads_major, w_mix, j, xt, lng(0), lnb(0), xa_wq, kv, xa_wo, l, lng(1), lnb(1),
            wr, br, strict_upper)
        xt = _moe(xt, xt_tiles, eidx, gates, rank, cnt, moe_w_gate, moe_w_up, moe_w_down, l,
                  lng(2), lnb(2))
    return xt.reshape(BATCH, SEQ, D_MODEL)
```

```python
import functools

import jax
import jax.numpy as jnp
import numpy as np
from jax import lax
from jax.experimental import pallas as pl
from jax.experimental.pallas import tpu as pltpu

F32 = jnp.float32
BF16 = jnp.bfloat16
I32 = jnp.int32

D_MODEL = 1024
BATCH = 8
SEQ = 4096
DEPTH = 2
TOKENS = BATCH * SEQ
MEM_LEN = 256
HALF = D_MODEL // 2
POOL_WINDOWS = (2, 4, 8, 16)
POOL_DIM = 128
FOURIER_HEADS = 4
FOURIER_DIM = 128
HG_HEADS = 8
HG_DIM = 128
HG_N_PROJ = 5
HG_CHUNK = 64
XA_HEADS = 4
XA_HEAD_DIM = 256
MOE_GROUPS = 4
MOE_PER_GROUP = 8
MOE_EXPERTS = 32
MOE_TOPK = 2
MOE_HIDDEN = 512
DN_ALPHA = (2.0 * DEPTH) ** 0.25
LN_EPS = 1e-5

V7X_LANES = 128
V7X_SUBLANES = 8
V7X_VMEM_LIMIT_BYTES = 52 * 1024 * 1024

TM = 512
TM_ROUTE = 1024
TM_PERM = 2048
TM_COMB = 512
TE = 512
SORTED_ROWS = TOKENS * MOE_TOPK + MOE_EXPERTS * TE
N_EXPERT_TILES = SORTED_ROWS // TE
GLA_GROUP = 16
GLA_ROWS = GLA_GROUP * HG_CHUNK
FOURIER_TR = 512
ROUTER_ROWS = 40

_NT = (((1,), (1,)), ((), ()))


def _params(*sem):
    return pltpu.CompilerParams(dimension_semantics=sem, vmem_limit_bytes=V7X_VMEM_LIMIT_BYTES)


def _dot(a, b):
    return jnp.dot(a, b, preferred_element_type=F32)


def _layer_norm(y, g, b):
    mu = jnp.mean(y, axis=-1, keepdims=True)
    yc = y - mu
    var = jnp.mean(yc * yc, axis=-1, keepdims=True)
    return yc * lax.rsqrt(var + LN_EPS) * g + b


def _post_ln(x, h, g, b):
    return _layer_norm(DN_ALPHA * x + h, g, b)


def _silu(x):
    return x * jax.nn.sigmoid(x)


def _resident(shape, index):
    return pl.BlockSpec(shape, lambda *_: index, pipeline_mode=pl.Buffered(1))


def _cast_weight_once(w_ref, wbf_ref):
    @pl.when(pl.program_id(0) == 0)
    def _():
        wbf_ref[...] = w_ref[0].astype(BF16)


TM_PF = 1024
_POOL_HALO = 8


def _k_pf_in(x_ref, xp_ref, xn_ref, w_ref, lng_ref, wc_ref, pw_ref, ps_ref, ya_ref, pq_ref, wbf_ref, ext_ref):
    _cast_weight_once(w_ref, wbf_ref)
    tiles_per_seq = SEQ // TM_PF
    j = lax.rem(pl.program_id(0), tiles_per_seq)
    u = _dot(x_ref[...].astype(BF16), wbf_ref[...])
    halo = jnp.concatenate([xp_ref[...], xn_ref[...]], axis=0).astype(BF16)
    uh = _dot(halo, wbf_ref[:, :HALF])
    ext_ref[_POOL_HALO:_POOL_HALO + TM_PF, :] = u[:, :HALF]
    ext_ref[0:_POOL_HALO, :] = jnp.where(j == 0, 0.0, uh[:_POOL_HALO])
    ext_ref[_POOL_HALO + TM_PF:, :] = jnp.where(j == tiles_per_seq - 1, 0.0, uh[_POOL_HALO:])
    parts = []
    for h in range(FOURIER_HEADS):
        ub = u[:, HALF + FOURIER_DIM * h:HALF + FOURIER_DIM * (h + 1)]
        mu = jnp.mean(ub, axis=-1, keepdims=True)
        uc = ub - mu
        var = jnp.mean(uc * uc, axis=-1, keepdims=True)
        parts.append(uc * lax.rsqrt(var + LN_EPS))
    un = jnp.concatenate(parts, axis=-1) * lng_ref[...]
    pq_ref[...] = _dot(un.astype(BF16), wc_ref[...]).astype(BF16)
    pos = lax.broadcasted_iota(I32, (TM_PF, POOL_DIM), 0) + j * TM_PF
    outs = []
    for g, w in enumerate(POOL_WINDOWS):
        hw = w // 2
        cols = slice(g * POOL_DIM, (g + 1) * POOL_DIM)
        acc = ext_ref[_POOL_HALO - hw:_POOL_HALO - hw + TM_PF, cols]
        for d in range(-hw + 1, hw):
            acc = acc + ext_ref[_POOL_HALO + d:_POOL_HALO + d + TM_PF, cols]
        cnt = jnp.minimum(pos + hw, SEQ) - jnp.maximum(pos - hw, 0)
        pooled = acc / cnt.astype(F32) - ext_ref[_POOL_HALO:_POOL_HALO + TM_PF, cols]
        outs.append(_dot(pooled.astype(BF16), pw_ref[g]))
    ya_ref[...] = (jnp.concatenate(outs, axis=-1) * ps_ref[...]).astype(BF16)


def _pf_in(x, w_in_all, j, ln_g, wc, pool_w, pool_scale):
    blocks_per_tile = TM_PF // _POOL_HALO
    last_block = TOKENS // _POOL_HALO - 1
    row = lambda i: (i, 0)
    fixed = lambda i: (0, 0)
    return pl.pallas_call(
        _k_pf_in,
        out_shape=(jax.ShapeDtypeStruct((TOKENS, HALF), BF16),
                   jax.ShapeDtypeStruct((TOKENS, D_MODEL), BF16)),
        grid=(TOKENS // TM_PF,),
        in_specs=[pl.BlockSpec((TM_PF, D_MODEL), row),
                  pl.BlockSpec((_POOL_HALO, D_MODEL), lambda i: (jnp.maximum(i * blocks_per_tile - 1, 0), 0)),
                  pl.BlockSpec((_POOL_HALO, D_MODEL),
                               lambda i: (jnp.minimum((i + 1) * blocks_per_tile, last_block), 0)),
                  _resident((1, D_MODEL, D_MODEL), (j, 0, 0)),
                  pl.BlockSpec((1, HALF), fixed),
                  _resident((HALF, D_MODEL), (0, 0)),
                  pl.BlockSpec((4, POOL_DIM, POOL_DIM), lambda i: (0, 0, 0)),
                  pl.BlockSpec((1, HALF), fixed)],
        out_specs=(pl.BlockSpec((TM_PF, HALF), row),
                   pl.BlockSpec((TM_PF, D_MODEL), row)),
        scratch_shapes=[pltpu.VMEM((D_MODEL, D_MODEL), BF16),
                        pltpu.VMEM((TM_PF + 2 * _POOL_HALO, HALF), F32)],
        compiler_params=_params("arbitrary"),
        name="pf_in",
    )(x, x, x, w_in_all, ln_g, wc, pool_w, pool_scale)


_FOURIER_SCALE = 1.0 / float(np.sqrt(SEQ * FOURIER_DIM))


_DFT_SIDE = 64
_DFT_COL_CHUNK = 1024


_DFT_HALF = SEQ // 2
_REV_BLOCK = 256


def _k_fourier(tab_ref, e1_ref, e2_ref, rw_ref, sg_ref, pq_ref, fw_ref, yb_ref,
               cs_ref, ss_ref, pf_ref, qf_ref, z_ref):
    h_rows = _DFT_HALF

    @pl.when(pl.program_id(0) == 0)
    def _():
        def expand(v, e):
            hi = v.astype(BF16)
            lo = (v - hi.astype(F32)).astype(BF16)
            return _dot(hi, e) + _dot(lo, e)

        s = _DFT_SIDE
        for rb in range(h_rows // FOURIER_TR):
            rows = slice(rb * FOURIER_TR, (rb + 1) * FOURIER_TR)
            tab = tab_ref[rows, :]
            for c in range(h_rows // _DFT_COL_CHUNK):
                cols = slice(c * _DFT_COL_CHUNK, (c + 1) * _DFT_COL_CHUNK)
                ca = expand(tab[:, 0:s], e1_ref[:, cols])
                sa = expand(tab[:, s:2 * s], e1_ref[:, cols])
                cb = expand(tab[:, 2 * s:3 * s], e2_ref[:, cols])
                sb = expand(tab[:, 3 * s:4 * s], e2_ref[:, cols])
                cs_ref[rows, cols] = (ca * cb - sa * sb).astype(BF16)
                ss_ref[rows, cols] = (sa * cb + ca * sb).astype(BF16)

    def head_maps(y):
        return jnp.concatenate(
            [_dot(y[:, FOURIER_DIM * h:FOURIER_DIM * (h + 1)].astype(BF16), fw_ref[h])
             for h in range(FOURIER_HEADS)], axis=-1)

    def reversed_block(src_ref, blk):
        lo = _REV_BLOCK * (h_rows // _REV_BLOCK - 1 - blk)
        if blk == 0:
            tail = src_ref[lo:lo + _REV_BLOCK, :]
            win = jnp.concatenate([tail, jnp.zeros_like(tail)], axis=0)
        else:
            win = src_ref[lo:lo + 2 * _REV_BLOCK, :]
        return _dot(rw_ref[...], win)

    n_blocks = h_rows // _REV_BLOCK
    upper = pq_ref.at[h_rows:SEQ]
    for blk in range(n_blocks):
        rows = slice(blk * _REV_BLOCK, (blk + 1) * _REV_BLOCK)
        rev = reversed_block(upper, blk)
        low = pq_ref[rows, :].astype(F32)
        pf_ref[rows, :] = (low[:, :HALF] + rev[:, :HALF]).astype(BF16)
        qf_ref[rows, :] = (low[:, HALF:] - rev[:, HALF:]).astype(BF16)

    p_mid = pq_ref[h_rows:h_rows + 1, :HALF].astype(F32)
    for rb in range(h_rows // FOURIER_TR):
        rows = slice(rb * FOURIER_TR, (rb + 1) * FOURIER_TR)
        a = _dot(cs_ref[rows, :], pf_ref[...])
        bm = _dot(ss_ref[rows, :], qf_ref[...])
        j = lax.broadcasted_iota(I32, (FOURIER_TR, 1), 0) + rb * FOURIER_TR
        t = jnp.where((j & 1) == 0, 1.0, -1.0) * p_mid
        yb_ref[rows, :] = head_maps((a - bm + t) * _FOURIER_SCALE).astype(BF16)
        z_ref[rows, :] = head_maps((a + bm + t) * _FOURIER_SCALE).astype(BF16)

    y_mid = head_maps(_dot(sg_ref[...], pq_ref[:, :HALF]) * _FOURIER_SCALE)[0:1]
    for blk in range(n_blocks):
        up = reversed_block(z_ref, blk)
        if blk == 0:
            r = lax.broadcasted_iota(I32, (_REV_BLOCK, 1), 0)
            up = jnp.where(r == 0, y_mid, up)
        yb_ref[h_rows + blk * _REV_BLOCK:h_rows + (blk + 1) * _REV_BLOCK, :] = up.astype(BF16)


def _fourier(tables, e1, e2, rev_window, signs, pq, fw):
    h_rows = _DFT_HALF
    return pl.pallas_call(
        _k_fourier,
        out_shape=jax.ShapeDtypeStruct((TOKENS, HALF), BF16),
        grid=(BATCH,),
        in_specs=[_resident((h_rows, 4 * _DFT_SIDE), (0, 0)),
                  _resident((_DFT_SIDE, h_rows), (0, 0)),
                  _resident((_DFT_SIDE, h_rows), (0, 0)),
                  _resident((_REV_BLOCK, 2 * _REV_BLOCK), (0, 0)),
                  _resident((V7X_SUBLANES, SEQ), (0, 0)),
                  pl.BlockSpec((SEQ, D_MODEL), lambda b: (b, 0), pipeline_mode=pl.Buffered(1)),
                  pl.BlockSpec((4, FOURIER_DIM, FOURIER_DIM), lambda b: (0, 0, 0))],
        out_specs=pl.BlockSpec((SEQ, HALF), lambda b: (b, 0)),
        scratch_shapes=[pltpu.VMEM((h_rows, h_rows), BF16), pltpu.VMEM((h_rows, h_rows), BF16),
                        pltpu.VMEM((h_rows, HALF), BF16), pltpu.VMEM((h_rows, HALF), BF16),
                        pltpu.VMEM((h_rows, HALF), BF16)],
        compiler_params=_params("arbitrary"),
        name="pf_fourier",
    )(tables, e1, e2, rev_window, signs, pq, fw)


_HG_STAGE_COLS = 512


def _k_hg_in(x_ref, w_hbm, o_ref, wbf_ref, stage_ref, sem):
    @pl.when(pl.program_id(0) == 0)
    def _():
        n_chunks = HG_N_PROJ * D_MODEL // _HG_STAGE_COLS

        def chunk_copy(c):
            cols = pl.ds(c * _HG_STAGE_COLS, _HG_STAGE_COLS)
            return pltpu.make_async_copy(w_hbm.at[:, cols], stage_ref.at[c % 2], sem.at[c % 2])

        chunk_copy(0).start()
        for c in range(n_chunks):
            if c + 1 < n_chunks:
                chunk_copy(c + 1).start()
            chunk_copy(c).wait()
            wbf_ref[:, c * _HG_STAGE_COLS:(c + 1) * _HG_STAGE_COLS] = stage_ref[c % 2].astype(BF16)

    xb = x_ref[...].astype(BF16)
    for p in range(HG_N_PROJ):
        acc = _dot(xb, wbf_ref[:, p * D_MODEL:(p + 1) * D_MODEL])
        if p == 0:
            acc = _silu(acc)
        for h in range(HG_HEADS):
            o_ref[p, h] = acc[:, HG_DIM * h:HG_DIM * (h + 1)].astype(BF16)


def _hg_in(x, w_in):
    return pl.pallas_call(
        _k_hg_in,
        out_shape=jax.ShapeDtypeStruct((HG_N_PROJ, HG_HEADS, TOKENS, HG_DIM), BF16),
        grid=(TOKENS // TM,),
        in_specs=[pl.BlockSpec((TM, D_MODEL), lambda i: (i, 0)),
                  pl.BlockSpec(memory_space=pl.ANY)],
        out_specs=pl.BlockSpec((HG_N_PROJ, HG_HEADS, TM, HG_DIM), lambda i: (0, 0, i, 0)),
        scratch_shapes=[pltpu.VMEM((D_MODEL, HG_N_PROJ * D_MODEL), BF16),
                        pltpu.VMEM((2, D_MODEL, _HG_STAGE_COLS), F32),
                        pltpu.SemaphoreType.DMA((2,))],
        compiler_params=_params("arbitrary"),
        name="hg_in",
    )(x, w_in)


def _gla_prep(q, z, lbv, tri, rev):
    L = HG_CHUNK
    G = q.shape[0] // L
    f = lbv + (1.0 - lbv) * jax.nn.sigmoid(z.astype(F32))
    lf = jnp.log(f).reshape(G, L, HG_DIM)
    k3 = (1.0 - f).reshape(G, L, HG_DIM)
    q3 = q.astype(F32).reshape(G, L, HG_DIM)
    hi32 = lax.bitcast_convert_type(
        lax.bitcast_convert_type(lf, jnp.uint32) & jnp.uint32(0xFFFF0000), F32)
    lf_hi = hi32.astype(BF16)
    lf_lo = (lf - hi32).astype(BF16)
    b2 = jnp.einsum('glm,gmk->glk', jnp.broadcast_to(tri, (G, L, L)),
                    jnp.concatenate([lf_hi, lf_lo], axis=-1), preferred_element_type=F32)
    b = b2[..., :HG_DIM] + b2[..., HG_DIM:]
    mid = L // 2 if not rev else L - 1 - L // 2
    end = L - 1 if not rev else 0
    b_ref = b[:, mid:mid + 1, :]
    b_last = b[:, end:end + 1, :]
    e_pos = jnp.exp(b - b_ref)
    qt32 = q3 * e_pos
    kt32 = k3 * (1.0 / e_pos)
    ku = kt32 * jnp.exp(b_last - b_ref)
    qe = qt32 * jnp.exp(b_ref)
    dec = jnp.exp(b_last)
    return qt32.astype(BF16), kt32.astype(BF16), ku.astype(BF16), qe.astype(BF16), dec


def _gla_intra(qt, kt, ku, v, rev):
    L = HG_CHUNK
    v3 = v.reshape(qt.shape[0], L, HG_DIM)
    sc = jnp.einsum('glk,gmk->glm', qt, kt, preferred_element_type=F32)
    li = lax.broadcasted_iota(I32, (L, L), 0)
    mi = lax.broadcasted_iota(I32, (L, L), 1)
    causal = (li >= mi) if not rev else (li <= mi)
    sc = jnp.where(causal[None], sc, 0.0).astype(BF16)
    u_t = jnp.einsum('glv,glk->gvk', v3, ku, preferred_element_type=F32)
    return sc, v3, u_t


def _gla_inter(qe, dec, sc, v3, u_t, st, rev):
    G = qe.shape[0]
    outs = [None] * G
    for n in (range(G) if not rev else range(G - 1, -1, -1)):
        outs[n] = (_dot(sc[n], v3[n])
                   + lax.dot_general(qe[n], st.astype(BF16), _NT, preferred_element_type=F32))
        st = dec[n] * st + u_t[n]
    return jnp.concatenate(outs, axis=0), st


_GLA_N_OPERANDS = 5


def _k_gla(q_ref, v_ref, zf_ref, zb_ref, g_ref, lb_ref, ng_ref, lt_ref, ut_ref, o_ref, of_ref, ob_ref,
           *operand_refs):
    n_groups = SEQ // GLA_ROWS
    per_slot = 2 * _GLA_N_OPERANDS
    slots = [[operand_refs[s * per_slot + d * _GLA_N_OPERANDS:s * per_slot + (d + 1) * _GLA_N_OPERANDS]
              for d in range(2)] for s in range(2)]
    directions = ((zf_ref, lt_ref, False), (zb_ref, ut_ref, True))

    def group_rows(t, rev):
        gi = t if not rev else n_groups - 1 - t
        return slice(gi * GLA_ROWS, (gi + 1) * GLA_ROWS)

    def prep(slot, t):
        for d, (z_ref, tri_ref, rev) in enumerate(directions):
            rows = group_rows(t, rev)
            vals = _gla_prep(q_ref[0, 0, rows, :], z_ref[0, 0, rows, :], lb_ref[d, 0], tri_ref[...], rev)
            for ref, val in zip(slots[slot][d], vals):
                ref[...] = val

    def mix(slot, t, states):
        out_refs = (of_ref, ob_ref)
        intra = []
        for d, (_, _, rev) in enumerate(directions):
            qt_ref, kt_ref, ku_ref, _, _ = slots[slot][d]
            intra.append(_gla_intra(qt_ref[...], kt_ref[...], ku_ref[...],
                                    v_ref[0, 0, group_rows(t, rev), :], rev))
        new_states = []
        for d, (_, _, rev) in enumerate(directions):
            _, _, _, qe_ref, dec_ref = slots[slot][d]
            o, st = _gla_inter(qe_ref[...], dec_ref[...], *intra[d], states[d], rev)
            out_refs[d][group_rows(t, rev), :] = o
            new_states.append(st)
        return new_states

    def finish(gi):
        rows = slice(gi * GLA_ROWS, (gi + 1) * GLA_ROWS)
        o = of_ref[rows, :] + ob_ref[rows, :]
        o = o * lax.rsqrt(jnp.mean(o * o, axis=-1, keepdims=True) + LN_EPS)
        o = o * ng_ref[...] * _silu(g_ref[0, 0, rows, :].astype(F32))
        o_ref[0, rows, :] = o.astype(BF16)

    states = [jnp.zeros((HG_DIM, HG_DIM), F32)] * 2
    prep(0, 0)
    for t in range(n_groups):
        if t + 1 < n_groups:
            prep((t + 1) % 2, t + 1)
        states = mix(t % 2, t, states)
        for gi in {t, n_groups - 1 - t}:
            if max(gi, n_groups - 1 - gi) == t:
                finish(gi)


def _gla(proj, lb, norm_g, tril, triu):
    def pspec(p):
        return pl.BlockSpec((1, 1, SEQ, HG_DIM), lambda b, h: (p, h, b, 0))
    fixed = lambda b, h: (0, 0)
    return pl.pallas_call(
        _k_gla,
        out_shape=jax.ShapeDtypeStruct((HG_HEADS, TOKENS, HG_DIM), BF16),
        grid=(BATCH, HG_HEADS),
        in_specs=[pspec(0), pspec(1), pspec(2), pspec(3), pspec(4),
                  pl.BlockSpec((2, 1, 1, HG_DIM), lambda b, h: (0, h, 0, 0)),
                  pl.BlockSpec((1, HG_DIM), fixed),
                  pl.BlockSpec((HG_CHUNK, HG_CHUNK), fixed),
                  pl.BlockSpec((HG_CHUNK, HG_CHUNK), fixed)],
        out_specs=pl.BlockSpec((1, SEQ, HG_DIM), lambda b, h: (h, b, 0)),
        scratch_shapes=[pltpu.VMEM((SEQ, HG_DIM), F32), pltpu.VMEM((SEQ, HG_DIM), F32)]
        + 4 * ([pltpu.VMEM((GLA_GROUP, HG_CHUNK, HG_DIM), BF16)] * (_GLA_N_OPERANDS - 1)
               + [pltpu.VMEM((GLA_GROUP, 1, HG_DIM), F32)]),
        compiler_params=_params("parallel", "parallel"),
        name="hg_gla",
    )(proj, proj, proj, proj, proj, lb, norm_g, tril, triu)


def _k_mm(x_ref, w_ref, o_ref):
    o_ref[0] = _dot(x_ref[...].astype(BF16), w_ref[0].astype(BF16)).astype(o_ref.dtype)


def _kv_proj(mem2d, wkv_all):
    rows, cols = mem2d.shape[0], wkv_all.shape[2]
    return pl.pallas_call(
        _k_mm,
        out_shape=jax.ShapeDtypeStruct((DEPTH, rows, cols), BF16),
        grid=(DEPTH, cols // D_MODEL, rows // TM),
        in_specs=[pl.BlockSpec((TM, D_MODEL), lambda l, j, i: (i, 0)),
                  pl.BlockSpec((1, D_MODEL, D_MODEL), lambda l, j, i: (l, 0, j))],
        out_specs=pl.BlockSpec((1, TM, D_MODEL), lambda l, j, i: (l, i, j)),
        compiler_params=_params("parallel", "parallel", "parallel"),
        name="xa_kv",
    )(mem2d, wkv_all)


ROW_WORDS = D_MODEL // 2
ROW_SLABS = ROW_WORDS // V7X_LANES
U32 = jnp.uint32
_HI_MASK = 0xFFFF0000


def _to_row_tiles(ref, x):
    n = x.shape[0]
    as_bits = lambda v: lax.bitcast_convert_type(v.astype(BF16).astype(F32), U32)
    words = (as_bits(x[:, :ROW_WORDS]) >> 16) | (as_bits(x[:, ROW_WORDS:]) & U32(_HI_MASK))
    for s in range(ROW_SLABS):
        ref[pl.ds(s, n, stride=ROW_SLABS), :] = words[:, V7X_LANES * s:V7X_LANES * (s + 1)]


def _from_row_tiles(ref):
    n = ref.shape[0] // ROW_SLABS
    words = jnp.concatenate([ref[pl.ds(s, n, stride=ROW_SLABS), :] for s in range(ROW_SLABS)], axis=-1)
    lo = lax.bitcast_convert_type(words << 16, F32)
    hi = lax.bitcast_convert_type(words & U32(_HI_MASK), F32)
    return jnp.concatenate([lo, hi], axis=-1)


def _first_argmax_rows(v, n_rows):
    m = jnp.max(v, axis=0, keepdims=True)
    rows = lax.broadcasted_iota(I32, v.shape, 0)
    idx = jnp.min(jnp.where(v == m, rows, n_rows), axis=0, keepdims=True)
    return m, idx


def _route(lt):
    gl = lt[0:MOE_GROUPS, :]
    gmax, gi = _first_argmax_rows(gl, MOE_GROUPS)
    p_grp = 1.0 / jnp.sum(jnp.exp(gl - gmax), axis=0, keepdims=True)
    sel = lt[MOE_GROUPS:MOE_GROUPS + MOE_PER_GROUP, :]
    for g in range(1, MOE_GROUPS):
        lo = MOE_GROUPS + MOE_PER_GROUP * g
        sel = jnp.where(gi == g, lt[lo:lo + MOE_PER_GROUP, :], sel)
    m1, i1 = _first_argmax_rows(sel, MOE_PER_GROUP)
    rows = lax.broadcasted_iota(I32, sel.shape, 0)
    m2, i2 = _first_argmax_rows(jnp.where(rows == i1, -jnp.inf, sel), MOE_PER_GROUP)
    e2 = jnp.exp(m2 - m1)
    g1 = p_grp / (1.0 + e2)
    return (gi * MOE_PER_GROUP + i1, gi * MOE_PER_GROUP + i2), (g1, g1 * e2)


def _rank_in_expert(e0, e1, strict_upper, carry_ref, counted):
    rows = lax.broadcasted_iota(I32, (MOE_EXPERTS, e0.shape[1]), 0)
    oh0 = rows == e0
    oh1 = rows == e1
    oh = jnp.where(oh0 | oh1, 1.0, 0.0)
    before = _dot(oh.astype(BF16), strict_upper) + carry_ref[:, 0:1]
    r0 = jnp.sum(jnp.where(oh0, before, 0.0), axis=0, keepdims=True).astype(I32)
    r1 = jnp.sum(jnp.where(oh1, before, 0.0), axis=0, keepdims=True).astype(I32)
    carry_ref[...] = carry_ref[...] + counted * jnp.sum(oh, axis=1, keepdims=True)
    return r0, r1


def _k_block(heads_major, *refs):
    n_mix = 1 if heads_major else 2
    mix_refs = refs[:n_mix]
    (wm_ref, x_ref, g0_ref, b0_ref, wq_ref, kv_ref, wo_ref, g_ref, b_ref, wr_ref, br_ref, tri_ref,
     o_ref, o3_ref, e_ref, gate_ref, rank_ref, cnt_ref,
     wmbf_ref, wqbf_ref, wobf_ref, carry_ref, pre_ref) = refs[n_mix:]
    step = pl.program_id(0)
    _cast_weight_once(wm_ref, wmbf_ref)
    _cast_weight_once(wq_ref, wqbf_ref)
    _cast_weight_once(wo_ref, wobf_ref)

    @pl.when(step == 0)
    def _():
        carry_ref[...] = jnp.zeros_like(carry_ref)
        pre_ref[...] = jnp.zeros_like(pre_ref)

    quarter = TM // 4
    routed = []

    def stage2(c):
        rows = slice(c * quarter, (c + 1) * quarter)
        y = _layer_norm(pre_ref[rows, :], g_ref[...], b_ref[...])
        o_ref[rows, :] = y
        _to_row_tiles(o3_ref.at[pl.ds(c * quarter * ROW_SLABS, quarter * ROW_SLABS)], y)
        lt = lax.dot_general(wr_ref[...], y.astype(BF16), _NT, preferred_element_type=F32) + br_ref[...]
        routed.append(_route(lt))

    stage2(0)
    if heads_major:
        a = jnp.concatenate([mix_refs[0][h] for h in range(HG_HEADS)], axis=-1)
    else:
        a = jnp.concatenate([mix_refs[0][...], mix_refs[1][...]], axis=-1)
    h1 = _dot(a, wmbf_ref[...])
    stage2(1)
    x = _post_ln(x_ref[...], h1, g0_ref[...], b0_ref[...])
    q = _dot(x.astype(BF16), wqbf_ref[...]).astype(BF16)
    stage2(2)
    outs = []
    for h in range(XA_HEADS):
        cols = slice(XA_HEAD_DIM * h, XA_HEAD_DIM * (h + 1))
        kh = kv_ref[:, cols]
        vh = kv_ref[:, D_MODEL + XA_HEAD_DIM * h:D_MODEL + XA_HEAD_DIM * (h + 1)]
        s = lax.dot_general(q[:, cols], kh, _NT, preferred_element_type=F32) * (XA_HEAD_DIM ** -0.5)
        e = jnp.exp(s - jnp.max(s, axis=-1, keepdims=True))
        p = e / jnp.sum(e, axis=-1, keepdims=True)
        outs.append(_dot(p.astype(BF16), vh))
    o = jnp.concatenate(outs, axis=-1).astype(BF16)
    h2 = _dot(o, wobf_ref[...])
    stage2(3)
    e0, e1, gate0, gate1 = [jnp.concatenate([r[a_][b_] for r in routed], axis=-1)
                            for a_ in range(2) for b_ in range(2)]
    r0, r1 = _rank_in_expert(e0, e1, tri_ref[...], carry_ref, jnp.where(step > 0, 1.0, 0.0))
    e_ref[0:1, :], e_ref[1:2, :] = e0, e1
    gate_ref[0:1, :], gate_ref[1:2, :] = gate0, gate1
    rank_ref[0:1, :], rank_ref[1:2, :] = r0, r1
    cnt_ref[...] = carry_ref[...]
    pre_ref[...] = DN_ALPHA * x + h2


def _block(mix, heads_major, w_mix_all, j, x, g0, b0, wq_all, kv, wo_all, l, g1, b1, wr, br, strict_upper):
    n_tiles = TOKENS // TM
    tiles_per_seq = SEQ // TM
    cur = lambda i: jnp.minimum(i, n_tiles - 1)
    prev = lambda i: jnp.maximum(i - 1, 0)
    row_in = lambda i: (cur(i), 0)
    row_out = lambda i: (prev(i), 0)
    fixed = lambda i: (0, 0)
    tok = pl.BlockSpec((MOE_TOPK, TM), lambda i: (0, prev(i)))
    if heads_major:
        mix_specs = [pl.BlockSpec((HG_HEADS, TM, HG_DIM), lambda i: (0, cur(i), 0))]
    else:
        mix_specs = [pl.BlockSpec((TM, HALF), row_in), pl.BlockSpec((TM, HALF), row_in)]
    return pl.pallas_call(
        functools.partial(_k_block, heads_major),
        out_shape=(jax.ShapeDtypeStruct((TOKENS, D_MODEL), F32),
                   jax.ShapeDtypeStruct((TOKENS * ROW_SLABS, V7X_LANES), U32),
                   jax.ShapeDtypeStruct((MOE_TOPK, TOKENS), I32),
                   jax.ShapeDtypeStruct((MOE_TOPK, TOKENS), F32),
                   jax.ShapeDtypeStruct((MOE_TOPK, TOKENS), I32),
                   jax.ShapeDtypeStruct((MOE_EXPERTS, V7X_LANES), F32)),
        grid=(n_tiles + 1,),
        in_specs=mix_specs + [
            _resident((1, D_MODEL, D_MODEL), (j, 0, 0)),
            pl.BlockSpec((TM, D_MODEL), row_in),
            pl.BlockSpec((1, D_MODEL), fixed), pl.BlockSpec((1, D_MODEL), fixed),
            _resident((1, D_MODEL, D_MODEL), (l, 0, 0)),
            pl.BlockSpec((None, MEM_LEN, 2 * D_MODEL), lambda i: (l, cur(i) // tiles_per_seq, 0)),
            _resident((1, D_MODEL, D_MODEL), (l, 0, 0)),
            pl.BlockSpec((1, D_MODEL), fixed), pl.BlockSpec((1, D_MODEL), fixed),
            _resident((ROUTER_ROWS, D_MODEL), (0, 0)),
            _resident((ROUTER_ROWS, 1), (0, 0)),
            _resident((TM, TM), (0, 0))],
        out_specs=(pl.BlockSpec((TM, D_MODEL), row_out),
                   pl.BlockSpec((TM * ROW_SLABS, V7X_LANES), row_out),
                   tok, tok, tok,
                   pl.BlockSpec((MOE_EXPERTS, V7X_LANES), fixed)),
        scratch_shapes=[pltpu.VMEM((D_MODEL, D_MODEL), BF16), pltpu.VMEM((D_MODEL, D_MODEL), BF16),
                        pltpu.VMEM((D_MODEL, D_MODEL), BF16), pltpu.VMEM((MOE_EXPERTS, V7X_LANES), F32),
                        pltpu.VMEM((TM, D_MODEL), F32)],
        compiler_params=_params("arbitrary"),
        name="attn_block",
    )(*mix, w_mix_all, x, g0, b0, wq_all, kv, wo_all, g1, b1, wr, br, strict_upper)


def _k_pos(e_ref, rank_ref, off_ref, pos_ref):
    rows = lax.broadcasted_iota(I32, (MOE_EXPERTS, TM_ROUTE), 0)
    for k in range(MOE_TOPK):
        start = jnp.sum(jnp.where(rows == e_ref[k:k + 1, :], off_ref[...], 0), axis=0, keepdims=True)
        pos_ref[k:k + 1, :] = start + rank_ref[k:k + 1, :]


def _positions(eidx, rank, offsets):
    tok = pl.BlockSpec((MOE_TOPK, TM_ROUTE), lambda i: (0, i))
    return pl.pallas_call(
        _k_pos,
        out_shape=jax.ShapeDtypeStruct((MOE_TOPK, TOKENS), I32),
        grid=(TOKENS // TM_ROUTE,),
        in_specs=[tok, tok, pl.BlockSpec((MOE_EXPERTS, 1), lambda i: (0, 0))],
        out_specs=tok,
        compiler_params=_params("parallel"),
        name="moe_pos",
    )(eidx, rank, offsets)


_ISSUE_BATCH = 8


def _row_copy(src_ref, src_row, dst_ref, dst_row, sem):
    src = pl.ds(pl.multiple_of(src_row * ROW_SLABS, ROW_SLABS), ROW_SLABS)
    dst = pl.ds(pl.multiple_of(dst_row * ROW_SLABS, ROW_SLABS), ROW_SLABS)
    return pltpu.make_async_copy(src_ref.at[src], dst_ref.at[dst], sem)


_DISPATCH_SLOTS = 4


def _k_dispatch(pos_ref, ends_ref, x_ref, xs_ref, zero_ref, stage_ref, row_sems, load_sems, zsem):
    step = pl.program_id(0)
    n_steps = pl.num_programs(0)
    block_rows = TM_PERM * ROW_SLABS

    def load(tile):
        slot = lax.rem(tile, _DISPATCH_SLOTS)
        src = x_ref.at[pl.ds(pl.multiple_of(tile * block_rows, block_rows), block_rows)]
        return pltpu.make_async_copy(src, stage_ref.at[slot], load_sems.at[slot])

    def drain_rows(tile):
        slot = lax.rem(tile, _DISPATCH_SLOTS)
        for _ in range(MOE_TOPK):
            pltpu.make_async_copy(stage_ref.at[slot], xs_ref.at[pl.ds(0, block_rows)],
                                  row_sems.at[slot]).wait()

    @pl.when(step == 0)
    def _():
        load(0).start()
        pl.when(n_steps > 1)(load(1).start)
        zero_ref[...] = jnp.zeros_like(zero_ref)

        def last_tile(e):
            seg_start = ends_ref[e - 1] if e > 0 else 0
            start = ends_ref[e] - TE
            return ends_ref[e] > seg_start, zero_fill(start)

        def unused_tile(t):
            start = ends_ref[MOE_EXPERTS - 1] + t * TE
            return start < SORTED_ROWS, zero_fill(start)

        def zero_fill(start_row):
            rows = pl.ds(pl.multiple_of(start_row * ROW_SLABS, TE * ROW_SLABS), TE * ROW_SLABS)
            return pltpu.make_async_copy(zero_ref, xs_ref.at[rows], zsem)

        fills = [last_tile(e) for e in range(MOE_EXPERTS)] + [unused_tile(t) for t in range(MOE_EXPERTS)]
        for needed, copy in fills:
            pl.when(needed)(copy.start)
        for needed, copy in fills:
            pl.when(needed)(copy.wait)

    pl.when(step >= 2)(lambda: drain_rows(step - 2))
    pl.when(step + 2 < n_steps)(lambda: load(step + 2).start())
    load(step).wait()

    slot = lax.rem(step, _DISPATCH_SLOTS)
    base = step * TM_PERM

    def body(jb, c):
        j0 = jb * _ISSUE_BATCH
        dst = [[pos_ref[k * TOKENS + base + j0 + u] for k in range(MOE_TOPK)]
               for u in range(_ISSUE_BATCH)]
        for u in range(_ISSUE_BATCH):
            for k in range(MOE_TOPK):
                _row_copy(stage_ref.at[slot], j0 + u, xs_ref, dst[u][k],
                          row_sems.at[slot]).start(priority=k)
        return c

    lax.fori_loop(0, TM_PERM // _ISSUE_BATCH, body, 0)

    @pl.when(step == n_steps - 1)
    def _():
        pl.when(step >= 1)(lambda: drain_rows(step - 1))
        drain_rows(step)


def _dispatch(pos_flat, ends, x):
    return pl.pallas_call(
        _k_dispatch,
        out_shape=jax.ShapeDtypeStruct((SORTED_ROWS * ROW_SLABS, V7X_LANES), U32),
        grid_spec=pltpu.PrefetchScalarGridSpec(
            num_scalar_prefetch=2,
            grid=(TOKENS // TM_PERM,),
            in_specs=[pl.BlockSpec(memory_space=pl.ANY)],
            out_specs=pl.BlockSpec(memory_space=pl.ANY),
            scratch_shapes=[pltpu.VMEM((TE * ROW_SLABS, V7X_LANES), U32),
                            pltpu.VMEM((_DISPATCH_SLOTS, TM_PERM * ROW_SLABS, V7X_LANES), U32),
                            pltpu.SemaphoreType.DMA((_DISPATCH_SLOTS,)),
                            pltpu.SemaphoreType.DMA((_DISPATCH_SLOTS,)),
                            pltpu.SemaphoreType.DMA],
        ),
        compiler_params=_params("arbitrary"),
        name="moe_dispatch",
    )(pos_flat, ends, x)


_EXPERT_SUBTILES = 1


def _k_experts(te_ref, na_ref, xs_ref, wg_ref, wu_ref, wd_ref, ys_ref, wgubf_ref, wdbf_ref):
    i = pl.program_id(0)
    active = i < na_ref[0]
    new_expert = jnp.logical_or(i == 0, te_ref[jnp.maximum(i - 1, 0)] != te_ref[i])

    @pl.when(jnp.logical_and(active, new_expert))
    def _():
        wgubf_ref[:, :MOE_HIDDEN] = wg_ref[0, 0].astype(BF16)
        wgubf_ref[:, MOE_HIDDEN:] = wu_ref[0, 0].astype(BF16)
        wdbf_ref[...] = wd_ref[0, 0].astype(BF16)

    @pl.when(active)
    def _():
        sub_rows = TE // _EXPERT_SUBTILES * ROW_SLABS
        for t in range(_EXPERT_SUBTILES):
            rows = pl.ds(t * sub_rows, sub_rows)
            xb = _from_row_tiles(xs_ref.at[rows]).astype(BF16)
            gu = _dot(xb, wgubf_ref[...])
            h = _silu(gu[:, :MOE_HIDDEN]) * gu[:, MOE_HIDDEN:]
            _to_row_tiles(ys_ref.at[rows], _dot(h.astype(BF16), wdbf_ref[...]))

    @pl.when(jnp.logical_not(active))
    def _():
        ys_ref[...] = jnp.zeros_like(ys_ref)


def _experts(tile_expert, n_active, xs, wg_all, wu_all, wd_all, l):
    def xs_index(i, te, na):
        return (jnp.minimum(i, na[0] - 1), 0)

    def w_index(i, te, na):
        return (l, te[i], 0, 0)

    return pl.pallas_call(
        _k_experts,
        out_shape=jax.ShapeDtypeStruct((SORTED_ROWS * ROW_SLABS, V7X_LANES), U32),
        grid_spec=pltpu.PrefetchScalarGridSpec(
            num_scalar_prefetch=2,
            grid=(N_EXPERT_TILES,),
            in_specs=[pl.BlockSpec((TE * ROW_SLABS, V7X_LANES), xs_index),
                      pl.BlockSpec((1, 1, D_MODEL, MOE_HIDDEN), w_index),
                      pl.BlockSpec((1, 1, D_MODEL, MOE_HIDDEN), w_index),
                      pl.BlockSpec((1, 1, MOE_HIDDEN, D_MODEL), w_index)],
            out_specs=pl.BlockSpec((TE * ROW_SLABS, V7X_LANES), lambda i, te, na: (i, 0)),
            scratch_shapes=[pltpu.VMEM((D_MODEL, 2 * MOE_HIDDEN), BF16),
                            pltpu.VMEM((MOE_HIDDEN, D_MODEL), BF16)],
        ),
        compiler_params=_params("arbitrary"),
        name="moe_experts",
    )(tile_expert, n_active, xs, wg_all, wu_all, wd_all)


_COMBINE_CHUNKS = 8


def _k_combine(pos_ref, ys_ref, gt_ref, x_ref, g_ref, b_ref, o_ref, buf_ref, sem):
    step = pl.program_id(0)
    last = pl.num_programs(0) - 1

    def issue_batch(base, j0, slot):
        src = [[pos_ref[k * TOKENS + base + j0 + u] for k in range(MOE_TOPK)]
               for u in range(_ISSUE_BATCH)]
        for u in range(_ISSUE_BATCH):
            for k in range(MOE_TOPK):
                _row_copy(ys_ref, src[u][k], buf_ref.at[slot, k], j0 + u,
                          sem.at[slot]).start(priority=k)

    def drain(slot):
        for k in range(MOE_TOPK):
            pltpu.make_async_copy(ys_ref.at[pl.ds(0, TM_COMB * ROW_SLABS)], buf_ref.at[slot, k],
                                  sem.at[slot]).wait()

    @pl.when(step == 0)
    def _():
        def body(jb, c):
            issue_batch(0, jb * _ISSUE_BATCH, 0)
            return c
        lax.fori_loop(0, TM_COMB // _ISSUE_BATCH, body, 0)

    slot = lax.rem(step, 2)
    drain(slot)
    next_base = jnp.minimum(step + 1, last) * TM_COMB
    chunk = TM_COMB // _COMBINE_CHUNKS
    for c in range(_COMBINE_CHUNKS):
        for jb in range(chunk // _ISSUE_BATCH):
            issue_batch(next_base, c * chunk + jb * _ISSUE_BATCH, 1 - slot)
        rows = slice(c * chunk, (c + 1) * chunk)
        tiles = pl.ds(c * chunk * ROW_SLABS, chunk * ROW_SLABS)
        h = (gt_ref[rows, 0:1] * _from_row_tiles(buf_ref.at[slot, 0, tiles])
             + gt_ref[rows, 1:2] * _from_row_tiles(buf_ref.at[slot, 1, tiles]))
        o_ref[rows, :] = _post_ln(x_ref[rows, :], h, g_ref[...], b_ref[...])

    pl.when(step == last)(lambda: drain(1 - slot))


def _combine(pos_flat, ys, gates_t, x, g, b):
    row = lambda i, pos: (i, 0)
    fixed = lambda i, pos: (0, 0)
    return pl.pallas_call(
        _k_combine,
        out_shape=jax.ShapeDtypeStruct((TOKENS, D_MODEL), F32),
        grid_spec=pltpu.PrefetchScalarGridSpec(
            num_scalar_prefetch=1,
            grid=(TOKENS // TM_COMB,),
            in_specs=[pl.BlockSpec(memory_space=pl.ANY),
                      pl.BlockSpec((TM_COMB, MOE_TOPK), row),
                      pl.BlockSpec((TM_COMB, D_MODEL), row),
                      pl.BlockSpec((1, D_MODEL), fixed), pl.BlockSpec((1, D_MODEL), fixed)],
            out_specs=pl.BlockSpec((TM_COMB, D_MODEL), row),
            scratch_shapes=[pltpu.VMEM((2, MOE_TOPK, TM_COMB * ROW_SLABS, V7X_LANES), U32),
                            pltpu.SemaphoreType.DMA((2,))],
        ),
        compiler_params=_params("arbitrary"),
        name="moe_combine",
    )(pos_flat, ys, gates_t, x, g, b)


def _router_params(w_group, b_group, w_expert, b_expert):
    pad = ROUTER_ROWS - MOE_GROUPS - MOE_EXPERTS
    wr = jnp.concatenate([w_group.T, w_expert.T, jnp.zeros((pad, D_MODEL), F32)], axis=0)
    br = jnp.concatenate([b_group, b_expert, jnp.zeros((pad,), F32)])[:, None]
    return wr.astype(BF16), br.astype(F32)


def _moe(x, x_tiles, eidx, gates, rank, cnt, w_gate_all, w_up_all, w_down_all, l, g, b):
    counts = cnt[:, 0].astype(I32)
    padded = ((counts + TE - 1) // TE) * TE
    ends = jnp.cumsum(padded)
    offsets = ends - padded
    tile_start = jnp.arange(N_EXPERT_TILES, dtype=I32) * TE
    tile_expert = jnp.sum((tile_start[:, None] >= ends[None, :]).astype(I32), axis=1)
    tile_expert = jnp.minimum(tile_expert, MOE_EXPERTS - 1)
    n_active = (ends[-1:] // TE).astype(I32)
    pos = _positions(eidx, rank, offsets[:, None])
    pos_flat = pos.reshape(-1)
    xs = _dispatch(pos_flat, ends.astype(I32), x_tiles)
    ys = _experts(tile_expert, n_active, xs, w_gate_all, w_up_all, w_down_all, l)
    return _combine(pos_flat, ys, gates.T, x, g, b)


def _channel_dft_matrix():
    c = np.arange(FOURIER_DIM)
    ang = 2.0 * np.pi * ((c[:, None] * c[None, :]) % FOURIER_DIM) / FOURIER_DIM
    wc = np.zeros((HALF, D_MODEL), np.float32)
    for h in range(FOURIER_HEADS):
        rows = slice(FOURIER_DIM * h, FOURIER_DIM * (h + 1))
        wc[rows, FOURIER_DIM * h:FOURIER_DIM * (h + 1)] = np.cos(ang)
        wc[rows, HALF + FOURIER_DIM * h:HALF + FOURIER_DIM * (h + 1)] = np.sin(ang)
    return jnp.asarray(wc, BF16)


def _sequence_dft_tables():
    side = _DFT_SIDE
    j = np.arange(_DFT_HALF, dtype=np.int64)[:, None]
    k = np.arange(side, dtype=np.int64)[None, :]
    ang_a = ((j * k) % side).astype(np.float64) * (2.0 * np.pi / side)
    ang_b = ((j * k) % SEQ).astype(np.float64) * (2.0 * np.pi / SEQ)
    tables = np.concatenate([np.cos(ang_a), np.sin(ang_a), np.cos(ang_b), np.sin(ang_b)], axis=1)
    col = np.arange(_DFT_HALF)
    e1 = (col[None, :] // side == np.arange(side)[:, None]).astype(np.float32)
    e2 = (col[None, :] % side == np.arange(side)[:, None]).astype(np.float32)
    r = np.arange(_REV_BLOCK)
    rev_window = (np.arange(2 * _REV_BLOCK)[None, :] == (_REV_BLOCK - r)[:, None]).astype(np.float32)
    signs = np.zeros((V7X_SUBLANES, SEQ), np.float32)
    signs[0] = 1.0 - 2.0 * (np.arange(SEQ) % 2)
    return (jnp.asarray(tables, F32), jnp.asarray(e1, BF16), jnp.asarray(e2, BF16),
            jnp.asarray(rev_window, BF16), jnp.asarray(signs, BF16))


def kernel(x, mem, pf_w_in, pf_pool_w, pf_pool_scale, pf_fourier_ln_g, pf_fourier_w, pf_w_out, hg_w_in, hg_lower_bounds, hg_norm_g, hg_w_out, xa_wq, xa_wkv, xa_wo, moe_w_group, moe_b_group, moe_w_expert, moe_b_expert, moe_w_gate, moe_w_up, moe_w_down, ln_g, ln_b):
    bf = lambda a: a.astype(BF16)
    xt = x.reshape(TOKENS, D_MODEL)
    mem2d = mem.reshape(BATCH * MEM_LEN, D_MODEL)
    lb_all = jnp.cumsum(jax.nn.softmax(hg_lower_bounds.astype(F32), axis=0), axis=0)
    lb_all = lb_all - lb_all[:1]
    tril = jnp.asarray(np.tril(np.ones((HG_CHUNK, HG_CHUNK), np.float32)), BF16)
    triu = jnp.asarray(np.triu(np.ones((HG_CHUNK, HG_CHUNK), np.float32)), BF16)
    strict_upper = jnp.asarray(np.triu(np.ones((TM, TM), np.float32), k=1), BF16)
    kv_all = _kv_proj(mem2d, xa_wkv)

    for l in range(DEPTH):
        j = l // 2
        lng = lambda s: ln_g[l, s][None, :]
        lnb = lambda s: ln_b[l, s][None, :]
        if l % 2 == 0:
            dft_consts = _sequence_dft_tables()
            ya, pq = _pf_in(xt, pf_w_in, j, pf_fourier_ln_g[j].reshape(1, HALF), _channel_dft_matrix(),
                            bf(pf_pool_w[j]), pf_pool_scale[j][None, :])
            yb = _fourier(*dft_consts, pq, bf(pf_fourier_w[j]))
            mix, heads_major, w_mix = (ya, yb), False, pf_w_out
        else:
            proj = _hg_in(xt, hg_w_in[j])
            lb = lb_all[l].reshape(2, HG_HEADS, 1, HG_DIM)
            og = _gla(proj, lb, hg_norm_g[j][None, :], tril, triu)
            mix, heads_major, w_mix = (og,), True, hg_w_out
        kv = kv_all
        wr, br = _router_params(moe_w_group[l], moe_b_group[l], moe_w_expert[l], moe_b_expert[l])
        xt, xt_tiles, eidx, gates, rank, cnt = _block(
            mix, heads_major, w_mix, j, xt, lng(0), lnb(0), xa_wq, kv, xa_wo, l, lng(1), lnb(1),
            wr, br, strict_upper)
        xt = _moe(xt, xt_tiles, eidx, gates, rank, cnt, moe_w_gate, moe_w_up, moe_w_down, l,
                  lng(2), lnb(2))
    return xt.reshape(BATCH, SEQ, D_MODEL)
```

```python
import functools

import jax
import jax.numpy as jnp
import numpy as np
from jax import lax
from jax.experimental import pallas as pl
from jax.experimental.pallas import tpu as pltpu

F32 = jnp.float32
BF16 = jnp.bfloat16
I32 = jnp.int32

D_MODEL = 1024
BATCH = 8
SEQ = 4096
DEPTH = 2
TOKENS = BATCH * SEQ
MEM_LEN = 256
HALF = D_MODEL // 2
POOL_WINDOWS = (2, 4, 8, 16)
POOL_DIM = 128
FOURIER_HEADS = 4
FOURIER_DIM = 128
HG_HEADS = 8
HG_DIM = 128
HG_N_PROJ = 5
HG_CHUNK = 64
XA_HEADS = 4
XA_HEAD_DIM = 256
MOE_GROUPS = 4
MOE_PER_GROUP = 8
MOE_EXPERTS = 32
MOE_TOPK = 2
MOE_HIDDEN = 512
DN_ALPHA = (2.0 * DEPTH) ** 0.25
LN_EPS = 1e-5

V7X_LANES = 128
V7X_SUBLANES = 8
V7X_VMEM_LIMIT_BYTES = 52 * 1024 * 1024

TM = 512
TM_ROUTE = 1024
TM_PERM = 2048
TM_COMB = 512
TE = 512
SORTED_ROWS = TOKENS * MOE_TOPK + MOE_EXPERTS * TE
N_EXPERT_TILES = SORTED_ROWS // TE
GLA_GROUP = 16
GLA_ROWS = GLA_GROUP * HG_CHUNK
FOURIER_TR = 512
ROUTER_ROWS = 40

_NT = (((1,), (1,)), ((), ()))


def _params(*sem):
    return pltpu.CompilerParams(dimension_semantics=sem, vmem_limit_bytes=V7X_VMEM_LIMIT_BYTES)


def _dot(a, b):
    return jnp.dot(a, b, preferred_element_type=F32)


def _layer_norm(y, g, b):
    mu = jnp.mean(y, axis=-1, keepdims=True)
    yc = y - mu
    var = jnp.mean(yc * yc, axis=-1, keepdims=True)
    return yc * lax.rsqrt(var + LN_EPS) * g + b


def _post_ln(x, h, g, b):
    return _layer_norm(DN_ALPHA * x + h, g, b)


def _silu(x):
    return x * jax.nn.sigmoid(x)


def _resident(shape, index):
    return pl.BlockSpec(shape, lambda *_: index, pipeline_mode=pl.Buffered(1))


def _cast_weight_once(w_ref, wbf_ref):
    @pl.when(pl.program_id(0) == 0)
    def _():
        wbf_ref[...] = w_ref[0].astype(BF16)


TM_PF = 1024
_POOL_HALO = 8


def _k_pf_in(x_ref, xp_ref, xn_ref, w_ref, lng_ref, wc_ref, pw_ref, ps_ref, ya_ref, pq_ref, wbf_ref, ext_ref):
    _cast_weight_once(w_ref, wbf_ref)
    tiles_per_seq = SEQ // TM_PF
    j = lax.rem(pl.program_id(0), tiles_per_seq)
    u = _dot(x_ref[...].astype(BF16), wbf_ref[...])
    halo = jnp.concatenate([xp_ref[...], xn_ref[...]], axis=0).astype(BF16)
    uh = _dot(halo, wbf_ref[:, :HALF])
    ext_ref[_POOL_HALO:_POOL_HALO + TM_PF, :] = u[:, :HALF]
    ext_ref[0:_POOL_HALO, :] = jnp.where(j == 0, 0.0, uh[:_POOL_HALO])
    ext_ref[_POOL_HALO + TM_PF:, :] = jnp.where(j == tiles_per_seq - 1, 0.0, uh[_POOL_HALO:])
    parts = []
    for h in range(FOURIER_HEADS):
        ub = u[:, HALF + FOURIER_DIM * h:HALF + FOURIER_DIM * (h + 1)]
        mu = jnp.mean(ub, axis=-1, keepdims=True)
        uc = ub - mu
        var = jnp.mean(uc * uc, axis=-1, keepdims=True)
        parts.append(uc * lax.rsqrt(var + LN_EPS))
    un = jnp.concatenate(parts, axis=-1) * lng_ref[...]
    pq_ref[...] = _dot(un.astype(BF16), wc_ref[...]).astype(BF16)
    pos = lax.broadcasted_iota(I32, (TM_PF, POOL_DIM), 0) + j * TM_PF
    outs = []
    for g, w in enumerate(POOL_WINDOWS):
        hw = w // 2
        cols = slice(g * POOL_DIM, (g + 1) * POOL_DIM)
        acc = ext_ref[_POOL_HALO - hw:_POOL_HALO - hw + TM_PF, cols]
        for d in range(-hw + 1, hw):
            acc = acc + ext_ref[_POOL_HALO + d:_POOL_HALO + d + TM_PF, cols]
        cnt = jnp.minimum(pos + hw, SEQ) - jnp.maximum(pos - hw, 0)
        pooled = acc / cnt.astype(F32) - ext_ref[_POOL_HALO:_POOL_HALO + TM_PF, cols]
        outs.append(_dot(pooled.astype(BF16), pw_ref[g]))
    ya_ref[...] = (jnp.concatenate(outs, axis=-1) * ps_ref[...]).astype(BF16)


def _pf_in(x, w_in_all, j, ln_g, wc, pool_w, pool_scale):
    blocks_per_tile = TM_PF // _POOL_HALO
    last_block = TOKENS // _POOL_HALO - 1
    row = lambda i: (i, 0)
    fixed = lambda i: (0, 0)
    return pl.pallas_call(
        _k_pf_in,
        out_shape=(jax.ShapeDtypeStruct((TOKENS, HALF), BF16),
                   jax.ShapeDtypeStruct((TOKENS, D_MODEL), BF16)),
        grid=(TOKENS // TM_PF,),
        in_specs=[pl.BlockSpec((TM_PF, D_MODEL), row),
                  pl.BlockSpec((_POOL_HALO, D_MODEL), lambda i: (jnp.maximum(i * blocks_per_tile - 1, 0), 0)),
                  pl.BlockSpec((_POOL_HALO, D_MODEL),
                               lambda i: (jnp.minimum((i + 1) * blocks_per_tile, last_block), 0)),
                  _resident((1, D_MODEL, D_MODEL), (j, 0, 0)),
                  pl.BlockSpec((1, HALF), fixed),
                  _resident((HALF, D_MODEL), (0, 0)),
                  pl.BlockSpec((4, POOL_DIM, POOL_DIM), lambda i: (0, 0, 0)),
                  pl.BlockSpec((1, HALF), fixed)],
        out_specs=(pl.BlockSpec((TM_PF, HALF), row),
                   pl.BlockSpec((TM_PF, D_MODEL), row)),
        scratch_shapes=[pltpu.VMEM((D_MODEL, D_MODEL), BF16),
                        pltpu.VMEM((TM_PF + 2 * _POOL_HALO, HALF), F32)],
        compiler_params=_params("arbitrary"),
        name="pf_in",
    )(x, x, x, w_in_all, ln_g, wc, pool_w, pool_scale)


_FOURIER_SCALE = 1.0 / float(np.sqrt(SEQ * FOURIER_DIM))


_DFT_SIDE = 64
_DFT_COL_CHUNK = 1024


_DFT_HALF = SEQ // 2
_REV_BLOCK = 256


def _k_fourier(tab_ref, e1_ref, e2_ref, rw_ref, sg_ref, pq_ref, fw_ref, yb_ref,
               cs_ref, ss_ref, pf_ref, qf_ref, z_ref):
    h_rows = _DFT_HALF

    @pl.when(pl.program_id(0) == 0)
    def _():
        def expand(v, e):
            hi = v.astype(BF16)
            lo = (v - hi.astype(F32)).astype(BF16)
            return _dot(hi, e) + _dot(lo, e)

        s = _DFT_SIDE
        for rb in range(h_rows // FOURIER_TR):
            rows = slice(rb * FOURIER_TR, (rb + 1) * FOURIER_TR)
            tab = tab_ref[rows, :]
            for c in range(h_rows // _DFT_COL_CHUNK):
                cols = slice(c * _DFT_COL_CHUNK, (c + 1) * _DFT_COL_CHUNK)
                ca = expand(tab[:, 0:s], e1_ref[:, cols])
                sa = expand(tab[:, s:2 * s], e1_ref[:, cols])
                cb = expand(tab[:, 2 * s:3 * s], e2_ref[:, cols])
                sb = expand(tab[:, 3 * s:4 * s], e2_ref[:, cols])
                cs_ref[rows, cols] = (ca * cb - sa * sb).astype(BF16)
                ss_ref[rows, cols] = (sa * cb + ca * sb).astype(BF16)

    def head_maps(y):
        return jnp.concatenate(
            [_dot(y[:, FOURIER_DIM * h:FOURIER_DIM * (h + 1)].astype(BF16), fw_ref[h])
             for h in range(FOURIER_HEADS)], axis=-1)

    def reversed_block(src_ref, blk):
        lo = _REV_BLOCK * (h_rows // _REV_BLOCK - 1 - blk)
        if blk == 0:
            tail = src_ref[lo:lo + _REV_BLOCK, :]
            win = jnp.concatenate([tail, jnp.zeros_like(tail)], axis=0)
        else:
            win = src_ref[lo:lo + 2 * _REV_BLOCK, :]
        return _dot(rw_ref[...], win)

    n_blocks = h_rows // _REV_BLOCK
    upper = pq_ref.at[h_rows:SEQ]
    for blk in range(n_blocks):
        rows = slice(blk * _REV_BLOCK, (blk + 1) * _REV_BLOCK)
        rev = reversed_block(upper, blk)
        low = pq_ref[rows, :].astype(F32)
        pf_ref[rows, :] = (low[:, :HALF] + rev[:, :HALF]).astype(BF16)
        qf_ref[rows, :] = (low[:, HALF:] - rev[:, HALF:]).astype(BF16)

    p_mid = pq_ref[h_rows:h_rows + 1, :HALF].astype(F32)
    for rb in range(h_rows // FOURIER_TR):
        rows = slice(rb * FOURIER_TR, (rb + 1) * FOURIER_TR)
        a = _dot(cs_ref[rows, :], pf_ref[...])
        bm = _dot(ss_ref[rows, :], qf_ref[...])
        j = lax.broadcasted_iota(I32, (FOURIER_TR, 1), 0) + rb * FOURIER_TR
        t = jnp.where((j & 1) == 0, 1.0, -1.0) * p_mid
        yb_ref[rows, :] = head_maps((a - bm + t) * _FOURIER_SCALE).astype(BF16)
        z_ref[rows, :] = head_maps((a + bm + t) * _FOURIER_SCALE).astype(BF16)

    y_mid = head_maps(_dot(sg_ref[...], pq_ref[:, :HALF]) * _FOURIER_SCALE)[0:1]
    for blk in range(n_blocks):
        up = reversed_block(z_ref, blk)
        if blk == 0:
            r = lax.broadcasted_iota(I32, (_REV_BLOCK, 1), 0)
            up = jnp.where(r == 0, y_mid, up)
        yb_ref[h_rows + blk * _REV_BLOCK:h_rows + (blk + 1) * _REV_BLOCK, :] = up.astype(BF16)


def _fourier(tables, e1, e2, rev_window, signs, pq, fw):
    h_rows = _DFT_HALF
    return pl.pallas_call(
        _k_fourier,
        out_shape=jax.ShapeDtypeStruct((TOKENS, HALF), BF16),
        grid=(BATCH,),
        in_specs=[_resident((h_rows, 4 * _DFT_SIDE), (0, 0)),
                  _resident((_DFT_SIDE, h_rows), (0, 0)),
                  _resident((_DFT_SIDE, h_rows), (0, 0)),
                  _resident((_REV_BLOCK, 2 * _REV_BLOCK), (0, 0)),
                  _resident((V7X_SUBLANES, SEQ), (0, 0)),
                  pl.BlockSpec((SEQ, D_MODEL), lambda b: (b, 0), pipeline_mode=pl.Buffered(1)),
                  pl.BlockSpec((4, FOURIER_DIM, FOURIER_DIM), lambda b: (0, 0, 0))],
        out_specs=pl.BlockSpec((SEQ, HALF), lambda b: (b, 0)),
        scratch_shapes=[pltpu.VMEM((h_rows, h_rows), BF16), pltpu.VMEM((h_rows, h_rows), BF16),
                        pltpu.VMEM((h_rows, HALF), BF16), pltpu.VMEM((h_rows, HALF), BF16),
                        pltpu.VMEM((h_rows, HALF), BF16)],
        compiler_params=_params("arbitrary"),
        name="pf_fourier",
    )(tables, e1, e2, rev_window, signs, pq, fw)


_HG_STAGE_COLS = 512


def _k_hg_in(x_ref, w_hbm, o_ref, wbf_ref, stage_ref, sem):
    @pl.when(pl.program_id(0) == 0)
    def _():
        n_chunks = HG_N_PROJ * D_MODEL // _HG_STAGE_COLS

        def chunk_copy(c):
            cols = pl.ds(c * _HG_STAGE_COLS, _HG_STAGE_COLS)
            return pltpu.make_async_copy(w_hbm.at[:, cols], stage_ref.at[c % 2], sem.at[c % 2])

        chunk_copy(0).start()
        for c in range(n_chunks):
            if c + 1 < n_chunks:
                chunk_copy(c + 1).start()
            chunk_copy(c).wait()
            wbf_ref[:, c * _HG_STAGE_COLS:(c + 1) * _HG_STAGE_COLS] = stage_ref[c % 2].astype(BF16)

    xb = x_ref[...].astype(BF16)
    for p in range(HG_N_PROJ):
        acc = _dot(xb, wbf_ref[:, p * D_MODEL:(p + 1) * D_MODEL])
        if p == 0:
            acc = _silu(acc)
        for h in range(HG_HEADS):
            o_ref[p, h] = acc[:, HG_DIM * h:HG_DIM * (h + 1)].astype(BF16)


def _hg_in(x, w_in):
    return pl.pallas_call(
        _k_hg_in,
        out_shape=jax.ShapeDtypeStruct((HG_N_PROJ, HG_HEADS, TOKENS, HG_DIM), BF16),
        grid=(TOKENS // TM,),
        in_specs=[pl.BlockSpec((TM, D_MODEL), lambda i: (i, 0)),
                  pl.BlockSpec(memory_space=pl.ANY)],
        out_specs=pl.BlockSpec((HG_N_PROJ, HG_HEADS, TM, HG_DIM), lambda i: (0, 0, i, 0)),
        scratch_shapes=[pltpu.VMEM((D_MODEL, HG_N_PROJ * D_MODEL), BF16),
                        pltpu.VMEM((2, D_MODEL, _HG_STAGE_COLS), F32),
                        pltpu.SemaphoreType.DMA((2,))],
        compiler_params=_params("arbitrary"),
        name="hg_in",
    )(x, w_in)


def _gla_prep(q, z, lbv, tri, rev):
    L = HG_CHUNK
    G = q.shape[0] // L
    f = lbv + (1.0 - lbv) * jax.nn.sigmoid(z.astype(F32))
    lf = jnp.log(f).reshape(G, L, HG_DIM)
    k3 = (1.0 - f).reshape(G, L, HG_DIM)
    q3 = q.astype(F32).reshape(G, L, HG_DIM)
    hi32 = lax.bitcast_convert_type(
        lax.bitcast_convert_type(lf, jnp.uint32) & jnp.uint32(0xFFFF0000), F32)
    lf_hi = hi32.astype(BF16)
    lf_lo = (lf - hi32).astype(BF16)
    b2 = jnp.einsum('glm,gmk->glk', jnp.broadcast_to(tri, (G, L, L)),
                    jnp.concatenate([lf_hi, lf_lo], axis=-1), preferred_element_type=F32)
    b = b2[..., :HG_DIM] + b2[..., HG_DIM:]
    mid = L // 2 if not rev else L - 1 - L // 2
    end = L - 1 if not rev else 0
    b_ref = b[:, mid:mid + 1, :]
    b_last = b[:, end:end + 1, :]
    e_pos = jnp.exp(b - b_ref)
    qt32 = q3 * e_pos
    kt32 = k3 * (1.0 / e_pos)
    ku = kt32 * jnp.exp(b_last - b_ref)
    qe = qt32 * jnp.exp(b_ref)
    dec = jnp.exp(b_last)
    return qt32.astype(BF16), kt32.astype(BF16), ku.astype(BF16), qe.astype(BF16), dec


def _gla_intra(qt, kt, ku, v, rev):
    L = HG_CHUNK
    v3 = v.reshape(qt.shape[0], L, HG_DIM)
    sc = jnp.einsum('glk,gmk->glm', qt, kt, preferred_element_type=F32)
    li = lax.broadcasted_iota(I32, (L, L), 0)
    mi = lax.broadcasted_iota(I32, (L, L), 1)
    causal = (li >= mi) if not rev else (li <= mi)
    sc = jnp.where(causal[None], sc, 0.0).astype(BF16)
    u_t = jnp.einsum('glv,glk->gvk', v3, ku, preferred_element_type=F32)
    return sc, v3, u_t


def _gla_inter(qe, dec, sc, v3, u_t, st, rev):
    G = qe.shape[0]
    outs = [None] * G
    for n in (range(G) if not rev else range(G - 1, -1, -1)):
        outs[n] = (_dot(sc[n], v3[n])
                   + lax.dot_general(qe[n], st.astype(BF16), _NT, preferred_element_type=F32))
        st = dec[n] * st + u_t[n]
    return jnp.concatenate(outs, axis=0), st


_GLA_N_OPERANDS = 5


def _k_gla(q_ref, v_ref, zf_ref, zb_ref, g_ref, lb_ref, ng_ref, lt_ref, ut_ref, o_ref, of_ref, ob_ref,
           *operand_refs):
    n_groups = SEQ // GLA_ROWS
    per_slot = 2 * _GLA_N_OPERANDS
    slots = [[operand_refs[s * per_slot + d * _GLA_N_OPERANDS:s * per_slot + (d + 1) * _GLA_N_OPERANDS]
              for d in range(2)] for s in range(2)]
    directions = ((zf_ref, lt_ref, False), (zb_ref, ut_ref, True))

    def group_rows(t, rev):
        gi = t if not rev else n_groups - 1 - t
        return slice(gi * GLA_ROWS, (gi + 1) * GLA_ROWS)

    def prep(slot, t):
        for d, (z_ref, tri_ref, rev) in enumerate(directions):
            rows = group_rows(t, rev)
            vals = _gla_prep(q_ref[0, 0, rows, :], z_ref[0, 0, rows, :], lb_ref[d, 0], tri_ref[...], rev)
            for ref, val in zip(slots[slot][d], vals):
                ref[...] = val

    def mix(slot, t, states):
        out_refs = (of_ref, ob_ref)
        intra = []
        for d, (_, _, rev) in enumerate(directions):
            qt_ref, kt_ref, ku_ref, _, _ = slots[slot][d]
            intra.append(_gla_intra(qt_ref[...], kt_ref[...], ku_ref[...],
                                    v_ref[0, 0, group_rows(t, rev), :], rev))
        new_states = []
        for d, (_, _, rev) in enumerate(directions):
            _, _, _, qe_ref, dec_ref = slots[slot][d]
            o, st = _gla_inter(qe_ref[...], dec_ref[...], *intra[d], states[d], rev)
            out_refs[d][group_rows(t, rev), :] = o
            new_states.append(st)
        return new_states

    def finish(gi):
        rows = slice(gi * GLA_ROWS, (gi + 1) * GLA_ROWS)
        o = of_ref[rows, :] + ob_ref[rows, :]
        o = o * lax.rsqrt(jnp.mean(o * o, axis=-1, keepdims=True) + LN_EPS)
        o = o * ng_ref[...] * _silu(g_ref[0, 0, rows, :].astype(F32))
        o_ref[0, rows, :] = o.astype(BF16)

    states = [jnp.zeros((HG_DIM, HG_DIM), F32)] * 2
    prep(0, 0)
    for t in range(n_groups):
        if t + 1 < n_groups:
            prep((t + 1) % 2, t + 1)
        states = mix(t % 2, t, states)
        for gi in {t, n_groups - 1 - t}:
            if max(gi, n_groups - 1 - gi) == t:
                finish(gi)


def _gla(proj, lb, norm_g, tril, triu):
    def pspec(p):
        return pl.BlockSpec((1, 1, SEQ, HG_DIM), lambda b, h: (p, h, b, 0))
    fixed = lambda b, h: (0, 0)
    return pl.pallas_call(
        _k_gla,
        out_shape=jax.ShapeDtypeStruct((HG_HEADS, TOKENS, HG_DIM), BF16),
        grid=(BATCH, HG_HEADS),
        in_specs=[pspec(0), pspec(1), pspec(2), pspec(3), pspec(4),
                  pl.BlockSpec((2, 1, 1, HG_DIM), lambda b, h: (0, h, 0, 0)),
                  pl.BlockSpec((1, HG_DIM), fixed),
                  pl.BlockSpec((HG_CHUNK, HG_CHUNK), fixed),
                  pl.BlockSpec((HG_CHUNK, HG_CHUNK), fixed)],
        out_specs=pl.BlockSpec((1, SEQ, HG_DIM), lambda b, h: (h, b, 0)),
        scratch_shapes=[pltpu.VMEM((SEQ, HG_DIM), F32), pltpu.VMEM((SEQ, HG_DIM), F32)]
        + 4 * ([pltpu.VMEM((GLA_GROUP, HG_CHUNK, HG_DIM), BF16)] * (_GLA_N_OPERANDS - 1)
               + [pltpu.VMEM((GLA_GROUP, 1, HG_DIM), F32)]),
        compiler_params=_params("parallel", "parallel"),
        name="hg_gla",
    )(proj, proj, proj, proj, proj, lb, norm_g, tril, triu)


def _k_mm(x_ref, w_ref, o_ref):
    o_ref[0] = _dot(x_ref[...].astype(BF16), w_ref[0].astype(BF16)).astype(o_ref.dtype)


def _kv_proj(mem2d, wkv_all):
    rows, cols = mem2d.shape[0], wkv_all.shape[2]
    return pl.pallas_call(
        _k_mm,
        out_shape=jax.ShapeDtypeStruct((DEPTH, rows, cols), BF16),
        grid=(DEPTH, cols // D_MODEL, rows // TM),
        in_specs=[pl.BlockSpec((TM, D_MODEL), lambda l, j, i: (i, 0)),
                  pl.BlockSpec((1, D_MODEL, D_MODEL), lambda l, j, i: (l, 0, j))],
        out_specs=pl.BlockSpec((1, TM, D_MODEL), lambda l, j, i: (l, i, j)),
        compiler_params=_params("parallel", "parallel", "parallel"),
        name="xa_kv",
    )(mem2d, wkv_all)


ROW_WORDS = D_MODEL // 2
ROW_SLABS = ROW_WORDS // V7X_LANES
U32 = jnp.uint32
_HI_MASK = 0xFFFF0000


def _to_row_tiles(ref, x):
    n = x.shape[0]
    as_bits = lambda v: lax.bitcast_convert_type(v.astype(BF16).astype(F32), U32)
    words = (as_bits(x[:, :ROW_WORDS]) >> 16) | (as_bits(x[:, ROW_WORDS:]) & U32(_HI_MASK))
    for s in range(ROW_SLABS):
        ref[pl.ds(s, n, stride=ROW_SLABS), :] = words[:, V7X_LANES * s:V7X_LANES * (s + 1)]


def _from_row_tiles(ref):
    n = ref.shape[0] // ROW_SLABS
    words = jnp.concatenate([ref[pl.ds(s, n, stride=ROW_SLABS), :] for s in range(ROW_SLABS)], axis=-1)
    lo = lax.bitcast_convert_type(words << 16, F32)
    hi = lax.bitcast_convert_type(words & U32(_HI_MASK), F32)
    return jnp.concatenate([lo, hi], axis=-1)


def _first_argmax_rows(v, n_rows):
    m = jnp.max(v, axis=0, keepdims=True)
    rows = lax.broadcasted_iota(I32, v.shape, 0)
    idx = jnp.min(jnp.where(v == m, rows, n_rows), axis=0, keepdims=True)
    return m, idx


def _route(lt):
    gl = lt[0:MOE_GROUPS, :]
    gmax, gi = _first_argmax_rows(gl, MOE_GROUPS)
    p_grp = 1.0 / jnp.sum(jnp.exp(gl - gmax), axis=0, keepdims=True)
    sel = lt[MOE_GROUPS:MOE_GROUPS + MOE_PER_GROUP, :]
    for g in range(1, MOE_GROUPS):
        lo = MOE_GROUPS + MOE_PER_GROUP * g
        sel = jnp.where(gi == g, lt[lo:lo + MOE_PER_GROUP, :], sel)
    m1, i1 = _first_argmax_rows(sel, MOE_PER_GROUP)
    rows = lax.broadcasted_iota(I32, sel.shape, 0)
    m2, i2 = _first_argmax_rows(jnp.where(rows == i1, -jnp.inf, sel), MOE_PER_GROUP)
    e2 = jnp.exp(m2 - m1)
    g1 = p_grp / (1.0 + e2)
    return (gi * MOE_PER_GROUP + i1, gi * MOE_PER_GROUP + i2), (g1, g1 * e2)


def _rank_in_expert(e0, e1, strict_upper, carry_ref, counted):
    rows = lax.broadcasted_iota(I32, (MOE_EXPERTS, e0.shape[1]), 0)
    oh0 = rows == e0
    oh1 = rows == e1
    oh = jnp.where(oh0 | oh1, 1.0, 0.0)
    before = _dot(oh.astype(BF16), strict_upper) + carry_ref[:, 0:1]
    r0 = jnp.sum(jnp.where(oh0, before, 0.0), axis=0, keepdims=True).astype(I32)
    r1 = jnp.sum(jnp.where(oh1, before, 0.0), axis=0, keepdims=True).astype(I32)
    carry_ref[...] = carry_ref[...] + counted * jnp.sum(oh, axis=1, keepdims=True)
    return r0, r1


def _k_block(heads_major, *refs):
    n_mix = 1 if heads_major else 2
    mix_refs = refs[:n_mix]
    (wm_ref, x_ref, g0_ref, b0_ref, wq_ref, kv_ref, wo_ref, g_ref, b_ref, wr_ref, br_ref, tri_ref,
     o_ref, o3_ref, e_ref, gate_ref, rank_ref, cnt_ref,
     wmbf_ref, wqbf_ref, wobf_ref, carry_ref, pre_ref) = refs[n_mix:]
    step = pl.program_id(0)
    _cast_weight_once(wm_ref, wmbf_ref)
    _cast_weight_once(wq_ref, wqbf_ref)
    _cast_weight_once(wo_ref, wobf_ref)

    @pl.when(step == 0)
    def _():
        carry_ref[...] = jnp.zeros_like(carry_ref)
        pre_ref[...] = jnp.zeros_like(pre_ref)

    quarter = TM // 4
    routed = []

    def stage2(c):
        rows = slice(c * quarter, (c + 1) * quarter)
        y = _layer_norm(pre_ref[rows, :], g_ref[...], b_ref[...])
        o_ref[rows, :] = y
        _to_row_tiles(o3_ref.at[pl.ds(c * quarter * ROW_SLABS, quarter * ROW_SLABS)], y)
        lt = lax.dot_general(wr_ref[...], y.astype(BF16), _NT, preferred_element_type=F32) + br_ref[...]
        routed.append(_route(lt))

    stage2(0)
    if heads_major:
        a = jnp.concatenate([mix_refs[0][h] for h in range(HG_HEADS)], axis=-1)
    else:
        a = jnp.concatenate([mix_refs[0][...], mix_refs[1][...]], axis=-1)
    h1 = _dot(a, wmbf_ref[...])
    stage2(1)
    x = _post_ln(x_ref[...], h1, g0_ref[...], b0_ref[...])
    q = _dot(x.astype(BF16), wqbf_ref[...]).astype(BF16)
    stage2(2)
    outs = []
    for h in range(XA_HEADS):
        cols = slice(XA_HEAD_DIM * h, XA_HEAD_DIM * (h + 1))
        kh = kv_ref[:, cols]
        vh = kv_ref[:, D_MODEL + XA_HEAD_DIM * h:D_MODEL + XA_HEAD_DIM * (h + 1)]
        s = lax.dot_general(q[:, cols], kh, _NT, preferred_element_type=F32) * (XA_HEAD_DIM ** -0.5)
        e = jnp.exp(s - jnp.max(s, axis=-1, keepdims=True))
        p = e / jnp.sum(e, axis=-1, keepdims=True)
        outs.append(_dot(p.astype(BF16), vh))
    o = jnp.concatenate(outs, axis=-1).astype(BF16)
    h2 = _dot(o, wobf_ref[...])
    stage2(3)
    e0, e1, gate0, gate1 = [jnp.concatenate([r[a_][b_] for r in routed], axis=-1)
                            for a_ in range(2) for b_ in range(2)]
    r0, r1 = _rank_in_expert(e0, e1, tri_ref[...], carry_ref, jnp.where(step > 0, 1.0, 0.0))
    e_ref[0:1, :], e_ref[1:2, :] = e0, e1
    gate_ref[0:1, :], gate_ref[1:2, :] = gate0, gate1
    rank_ref[0:1, :], rank_ref[1:2, :] = r0, r1
    cnt_ref[...] = carry_ref[...]
    pre_ref[...] = DN_ALPHA * x + h2


def _block(mix, heads_major, w_mix_all, j, x, g0, b0, wq_all, kv, wo_all, l, g1, b1, wr, br, strict_upper):
    n_tiles = TOKENS // TM
    tiles_per_seq = SEQ // TM
    cur = lambda i: jnp.minimum(i, n_tiles - 1)
    prev = lambda i: jnp.maximum(i - 1, 0)
    row_in = lambda i: (cur(i), 0)
    row_out = lambda i: (prev(i), 0)
    fixed = lambda i: (0, 0)
    tok = pl.BlockSpec((MOE_TOPK, TM), lambda i: (0, prev(i)))
    if heads_major:
        mix_specs = [pl.BlockSpec((HG_HEADS, TM, HG_DIM), lambda i: (0, cur(i), 0))]
    else:
        mix_specs = [pl.BlockSpec((TM, HALF), row_in), pl.BlockSpec((TM, HALF), row_in)]
    return pl.pallas_call(
        functools.partial(_k_block, heads_major),
        out_shape=(jax.ShapeDtypeStruct((TOKENS, D_MODEL), F32),
                   jax.ShapeDtypeStruct((TOKENS * ROW_SLABS, V7X_LANES), U32),
                   jax.ShapeDtypeStruct((MOE_TOPK, TOKENS), I32),
                   jax.ShapeDtypeStruct((MOE_TOPK, TOKENS), F32),
                   jax.ShapeDtypeStruct((MOE_TOPK, TOKENS), I32),
                   jax.ShapeDtypeStruct((MOE_EXPERTS, V7X_LANES), F32)),
        grid=(n_tiles + 1,),
        in_specs=mix_specs + [
            _resident((1, D_MODEL, D_MODEL), (j, 0, 0)),
            pl.BlockSpec((TM, D_MODEL), row_in),
            pl.BlockSpec((1, D_MODEL), fixed), pl.BlockSpec((1, D_MODEL), fixed),
            _resident((1, D_MODEL, D_MODEL), (l, 0, 0)),
            pl.BlockSpec((None, MEM_LEN, 2 * D_MODEL), lambda i: (l, cur(i) // tiles_per_seq, 0)),
            _resident((1, D_MODEL, D_MODEL), (l, 0, 0)),
            pl.BlockSpec((1, D_MODEL), fixed), pl.BlockSpec((1, D_MODEL), fixed),
            _resident((ROUTER_ROWS, D_MODEL), (0, 0)),
            _resident((ROUTER_ROWS, 1), (0, 0)),
            _resident((TM, TM), (0, 0))],
        out_specs=(pl.BlockSpec((TM, D_MODEL), row_out),
                   pl.BlockSpec((TM * ROW_SLABS, V7X_LANES), row_out),
                   tok, tok, tok,
                   pl.BlockSpec((MOE_EXPERTS, V7X_LANES), fixed)),
        scratch_shapes=[pltpu.VMEM((D_MODEL, D_MODEL), BF16), pltpu.VMEM((D_MODEL, D_MODEL), BF16),
                        pltpu.VMEM((D_MODEL, D_MODEL), BF16), pltpu.VMEM((MOE_EXPERTS, V7X_LANES), F32),
                        pltpu.VMEM((TM, D_MODEL), F32)],
        compiler_params=_params("arbitrary"),
        name="attn_block",
    )(*mix, w_mix_all, x, g0, b0, wq_all, kv, wo_all, g1, b1, wr, br, strict_upper)


def _k_pos(e_ref, rank_ref, off_ref, pos_ref):
    rows = lax.broadcasted_iota(I32, (MOE_EXPERTS, TM_ROUTE), 0)
    for k in range(MOE_TOPK):
        start = jnp.sum(jnp.where(rows == e_ref[k:k + 1, :], off_ref[...], 0), axis=0, keepdims=True)
        pos_ref[k:k + 1, :] = start + rank_ref[k:k + 1, :]


def _positions(eidx, rank, offsets):
    tok = pl.BlockSpec((MOE_TOPK, TM_ROUTE), lambda i: (0, i))
    return pl.pallas_call(
        _k_pos,
        out_shape=jax.ShapeDtypeStruct((MOE_TOPK, TOKENS), I32),
        grid=(TOKENS // TM_ROUTE,),
        in_specs=[tok, tok, pl.BlockSpec((MOE_EXPERTS, 1), lambda i: (0, 0))],
        out_specs=tok,
        compiler_params=_params("parallel"),
        name="moe_pos",
    )(eidx, rank, offsets)


_ISSUE_BATCH = 8


def _row_copy(src_ref, src_row, dst_ref, dst_row, sem):
    src = pl.ds(pl.multiple_of(src_row * ROW_SLABS, ROW_SLABS), ROW_SLABS)
    dst = pl.ds(pl.multiple_of(dst_row * ROW_SLABS, ROW_SLABS), ROW_SLABS)
    return pltpu.make_async_copy(src_ref.at[src], dst_ref.at[dst], sem)


_DISPATCH_SLOTS = 4


def _k_dispatch(pos_ref, ends_ref, x_ref, xs_ref, zero_ref, stage_ref, row_sems, load_sems, zsem):
    step = pl.program_id(0)
    n_steps = pl.num_programs(0)
    block_rows = TM_PERM * ROW_SLABS

    def load(tile):
        slot = lax.rem(tile, _DISPATCH_SLOTS)
        src = x_ref.at[pl.ds(pl.multiple_of(tile * block_rows, block_rows), block_rows)]
        return pltpu.make_async_copy(src, stage_ref.at[slot], load_sems.at[slot])

    def drain_rows(tile):
        slot = lax.rem(tile, _DISPATCH_SLOTS)
        for _ in range(MOE_TOPK):
            pltpu.make_async_copy(stage_ref.at[slot], xs_ref.at[pl.ds(0, block_rows)],
                                  row_sems.at[slot]).wait()

    @pl.when(step == 0)
    def _():
        load(0).start()
        pl.when(n_steps > 1)(load(1).start)
        zero_ref[...] = jnp.zeros_like(zero_ref)

        def last_tile(e):
            seg_start = ends_ref[e - 1] if e > 0 else 0
            start = ends_ref[e] - TE
            return ends_ref[e] > seg_start, zero_fill(start)

        def unused_tile(t):
            start = ends_ref[MOE_EXPERTS - 1] + t * TE
            return start < SORTED_ROWS, zero_fill(start)

        def zero_fill(start_row):
            rows = pl.ds(pl.multiple_of(start_row * ROW_SLABS, TE * ROW_SLABS), TE * ROW_SLABS)
            return pltpu.make_async_copy(zero_ref, xs_ref.at[rows], zsem)

        fills = [last_tile(e) for e in range(MOE_EXPERTS)] + [unused_tile(t) for t in range(MOE_EXPERTS)]
        for needed, copy in fills:
            pl.when(needed)(copy.start)
        for needed, copy in fills:
            pl.when(needed)(copy.wait)

    pl.when(step >= 2)(lambda: drain_rows(step - 2))
    pl.when(step + 2 < n_steps)(lambda: load(step + 2).start())
    load(step).wait()

    slot = lax.rem(step, _DISPATCH_SLOTS)
    base = step * TM_PERM

    def body(jb, c):
        j0 = jb * _ISSUE_BATCH
        dst = [[pos_ref[k * TOKENS + base + j0 + u] for k in range(MOE_TOPK)]
               for u in range(_ISSUE_BATCH)]
        for u in range(_ISSUE_BATCH):
            for k in range(MOE_TOPK):
                _row_copy(stage_ref.at[slot], j0 + u, xs_ref, dst[u][k],
                          row_sems.at[slot]).start(priority=k)
        return c

    lax.fori_loop(0, TM_PERM // _ISSUE_BATCH, body, 0)

    @pl.when(step == n_steps - 1)
    def _():
        pl.when(step >= 1)(lambda: drain_rows(step - 1))
        drain_rows(step)


def _dispatch(pos_flat, ends, x):
    return pl.pallas_call(
        _k_dispatch,
        out_shape=jax.ShapeDtypeStruct((SORTED_ROWS * ROW_SLABS, V7X_LANES), U32),
        grid_spec=pltpu.PrefetchScalarGridSpec(
            num_scalar_prefetch=2,
            grid=(TOKENS // TM_PERM,),
            in_specs=[pl.BlockSpec(memory_space=pl.ANY)],
            out_specs=pl.BlockSpec(memory_space=pl.ANY),
            scratch_shapes=[pltpu.VMEM((TE * ROW_SLABS, V7X_LANES), U32),
                            pltpu.VMEM((_DISPATCH_SLOTS, TM_PERM * ROW_SLABS, V7X_LANES), U32),
                            pltpu.SemaphoreType.DMA((_DISPATCH_SLOTS,)),
                            pltpu.SemaphoreType.DMA((_DISPATCH_SLOTS,)),
                            pltpu.SemaphoreType.DMA],
        ),
        compiler_params=_params("arbitrary"),
        name="moe_dispatch",
    )(pos_flat, ends, x)


def _k_experts(te_ref, na_ref, next_ref, xs_ref, wg_hbm, wu_hbm, wd_hbm, ys_ref,
               wgubf_ref, wdbf_ref, stage_g, stage_u, stage_d, sems, *, layer):
    i = pl.program_id(0)
    active = i < na_ref[0]
    expert = te_ref[i]
    new_expert = jnp.logical_or(i == 0, te_ref[jnp.maximum(i - 1, 0)] != expert)

    def weight_copies(e, slot):
        return (pltpu.make_async_copy(wg_hbm.at[layer, e], stage_g.at[slot], sems.at[slot, 0]),
                pltpu.make_async_copy(wu_hbm.at[layer, e], stage_u.at[slot], sems.at[slot, 1]),
                pltpu.make_async_copy(wd_hbm.at[layer, e], stage_d.at[slot], sems.at[slot, 2]))

    @pl.when(jnp.logical_and(active, new_expert))
    def _():
        slot = next_ref[i, 1]

        @pl.when(i == 0)
        def _():
            for c in weight_copies(expert, slot):
                c.start()

        upcoming = next_ref[i, 0]

        @pl.when(upcoming >= 0)
        def _():
            for c in weight_copies(upcoming, 1 - slot):
                c.start()

        for c in weight_copies(expert, slot):
            c.wait()
        wgubf_ref[:, :MOE_HIDDEN] = stage_g[slot].astype(BF16)
        wgubf_ref[:, MOE_HIDDEN:] = stage_u[slot].astype(BF16)
        wdbf_ref[...] = stage_d[slot].astype(BF16)

    @pl.when(active)
    def _():
        xb = _from_row_tiles(xs_ref).astype(BF16)
        gu = _dot(xb, wgubf_ref[...])
        h = _silu(gu[:, :MOE_HIDDEN]) * gu[:, MOE_HIDDEN:]
        _to_row_tiles(ys_ref, _dot(h.astype(BF16), wdbf_ref[...]))

    @pl.when(jnp.logical_not(active))
    def _():
        ys_ref[...] = jnp.zeros_like(ys_ref)


def _experts(tile_expert, n_active, tile_next, xs, wg_all, wu_all, wd_all, l):
    def xs_index(i, te, na, nxt):
        return (jnp.minimum(i, na[0] - 1), 0)

    return pl.pallas_call(
        functools.partial(_k_experts, layer=l),
        out_shape=jax.ShapeDtypeStruct((SORTED_ROWS * ROW_SLABS, V7X_LANES), U32),
        grid_spec=pltpu.PrefetchScalarGridSpec(
            num_scalar_prefetch=3,
            grid=(N_EXPERT_TILES,),
            in_specs=[pl.BlockSpec((TE * ROW_SLABS, V7X_LANES), xs_index),
                      pl.BlockSpec(memory_space=pl.ANY),
                      pl.BlockSpec(memory_space=pl.ANY),
                      pl.BlockSpec(memory_space=pl.ANY)],
            out_specs=pl.BlockSpec((TE * ROW_SLABS, V7X_LANES), lambda i, te, na, nxt: (i, 0)),
            scratch_shapes=[pltpu.VMEM((D_MODEL, 2 * MOE_HIDDEN), BF16),
                            pltpu.VMEM((MOE_HIDDEN, D_MODEL), BF16),
                            pltpu.VMEM((2, D_MODEL, MOE_HIDDEN), F32),
                            pltpu.VMEM((2, D_MODEL, MOE_HIDDEN), F32),
                            pltpu.VMEM((2, MOE_HIDDEN, D_MODEL), F32),
                            pltpu.SemaphoreType.DMA((2, 3))],
        ),
        compiler_params=_params("arbitrary"),
        name="moe_experts",
    )(tile_expert, n_active, tile_next, xs, wg_all, wu_all, wd_all)


_COMBINE_CHUNKS = 8


def _k_combine(pos_ref, ys_ref, gt_ref, x_ref, g_ref, b_ref, o_ref, buf_ref, sem):
    step = pl.program_id(0)
    last = pl.num_programs(0) - 1

    def issue_batch(base, j0, slot):
        src = [[pos_ref[k * TOKENS + base + j0 + u] for k in range(MOE_TOPK)]
               for u in range(_ISSUE_BATCH)]
        for u in range(_ISSUE_BATCH):
            for k in range(MOE_TOPK):
                _row_copy(ys_ref, src[u][k], buf_ref.at[slot, k], j0 + u,
                          sem.at[slot]).start(priority=k)

    def drain(slot):
        for k in range(MOE_TOPK):
            pltpu.make_async_copy(ys_ref.at[pl.ds(0, TM_COMB * ROW_SLABS)], buf_ref.at[slot, k],
                                  sem.at[slot]).wait()

    @pl.when(step == 0)
    def _():
        def body(jb, c):
            issue_batch(0, jb * _ISSUE_BATCH, 0)
            return c
        lax.fori_loop(0, TM_COMB // _ISSUE_BATCH, body, 0)

    slot = lax.rem(step, 2)
    drain(slot)
    next_base = jnp.minimum(step + 1, last) * TM_COMB
    chunk = TM_COMB // _COMBINE_CHUNKS
    for c in range(_COMBINE_CHUNKS):
        for jb in range(chunk // _ISSUE_BATCH):
            issue_batch(next_base, c * chunk + jb * _ISSUE_BATCH, 1 - slot)
        rows = slice(c * chunk, (c + 1) * chunk)
        tiles = pl.ds(c * chunk * ROW_SLABS, chunk * ROW_SLABS)
        h = (gt_ref[rows, 0:1] * _from_row_tiles(buf_ref.at[slot, 0, tiles])
             + gt_ref[rows, 1:2] * _from_row_tiles(buf_ref.at[slot, 1, tiles]))
        o_ref[rows, :] = _post_ln(x_ref[rows, :], h, g_ref[...], b_ref[...])

    pl.when(step == last)(lambda: drain(1 - slot))


def _combine(pos_flat, ys, gates_t, x, g, b):
    row = lambda i, pos: (i, 0)
    fixed = lambda i, pos: (0, 0)
    return pl.pallas_call(
        _k_combine,
        out_shape=jax.ShapeDtypeStruct((TOKENS, D_MODEL), F32),
        grid_spec=pltpu.PrefetchScalarGridSpec(
            num_scalar_prefetch=1,
            grid=(TOKENS // TM_COMB,),
            in_specs=[pl.BlockSpec(memory_space=pl.ANY),
                      pl.BlockSpec((TM_COMB, MOE_TOPK), row),
                      pl.BlockSpec((TM_COMB, D_MODEL), row),
                      pl.BlockSpec((1, D_MODEL), fixed), pl.BlockSpec((1, D_MODEL), fixed)],
            out_specs=pl.BlockSpec((TM_COMB, D_MODEL), row),
            scratch_shapes=[pltpu.VMEM((2, MOE_TOPK, TM_COMB * ROW_SLABS, V7X_LANES), U32),
                            pltpu.SemaphoreType.DMA((2,))],
        ),
        compiler_params=_params("arbitrary"),
        name="moe_combine",
    )(pos_flat, ys, gates_t, x, g, b)


def _router_params(w_group, b_group, w_expert, b_expert):
    pad = ROUTER_ROWS - MOE_GROUPS - MOE_EXPERTS
    wr = jnp.concatenate([w_group.T, w_expert.T, jnp.zeros((pad, D_MODEL), F32)], axis=0)
    br = jnp.concatenate([b_group, b_expert, jnp.zeros((pad,), F32)])[:, None]
    return wr.astype(BF16), br.astype(F32)


def _moe(x, x_tiles, eidx, gates, rank, cnt, w_gate_all, w_up_all, w_down_all, l, g, b):
    counts = cnt[:, 0].astype(I32)
    padded = ((counts + TE - 1) // TE) * TE
    ends = jnp.cumsum(padded)
    offsets = ends - padded
    tile_start = jnp.arange(N_EXPERT_TILES, dtype=I32) * TE
    tile_expert = jnp.sum((tile_start[:, None] >= ends[None, :]).astype(I32), axis=1)
    tile_expert = jnp.minimum(tile_expert, MOE_EXPERTS - 1)
    n_active = (ends[-1:] // TE).astype(I32)
    pos = _positions(eidx, rank, offsets[:, None])
    pos_flat = pos.reshape(-1)
    xs = _dispatch(pos_flat, ends.astype(I32), x_tiles)
    present = padded > 0
    ids = jnp.arange(MOE_EXPERTS, dtype=I32)
    later = jnp.where(present[None, :] & (ids[None, :] > ids[:, None]), ids[None, :], MOE_EXPERTS)
    next_present = jnp.min(later, axis=1)
    next_present = jnp.where(next_present >= MOE_EXPERTS, -1, next_present)
    appearance = jnp.cumsum(present.astype(I32)) - 1
    tile_next = jnp.stack([next_present[tile_expert], appearance[tile_expert] % 2], axis=1).astype(I32)
    ys = _experts(tile_expert, n_active, tile_next, xs, w_gate_all, w_up_all, w_down_all, l)
    return _combine(pos_flat, ys, gates.T, x, g, b)


def _channel_dft_matrix():
    c = np.arange(FOURIER_DIM)
    ang = 2.0 * np.pi * ((c[:, None] * c[None, :]) % FOURIER_DIM) / FOURIER_DIM
    wc = np.zeros((HALF, D_MODEL), np.float32)
    for h in range(FOURIER_HEADS):
        rows = slice(FOURIER_DIM * h, FOURIER_DIM * (h + 1))
        wc[rows, FOURIER_DIM * h:FOURIER_DIM * (h + 1)] = np.cos(ang)
        wc[rows, HALF + FOURIER_DIM * h:HALF + FOURIER_DIM * (h + 1)] = np.sin(ang)
    return jnp.asarray(wc, BF16)


def _sequence_dft_tables():
    side = _DFT_SIDE
    j = np.arange(_DFT_HALF, dtype=np.int64)[:, None]
    k = np.arange(side, dtype=np.int64)[None, :]
    ang_a = ((j * k) % side).astype(np.float64) * (2.0 * np.pi / side)
    ang_b = ((j * k) % SEQ).astype(np.float64) * (2.0 * np.pi / SEQ)
    tables = np.concatenate([np.cos(ang_a), np.sin(ang_a), np.cos(ang_b), np.sin(ang_b)], axis=1)
    col = np.arange(_DFT_HALF)
    e1 = (col[None, :] // side == np.arange(side)[:, None]).astype(np.float32)
    e2 = (col[None, :] % side == np.arange(side)[:, None]).astype(np.float32)
    r = np.arange(_REV_BLOCK)
    rev_window = (np.arange(2 * _REV_BLOCK)[None, :] == (_REV_BLOCK - r)[:, None]).astype(np.float32)
    signs = np.zeros((V7X_SUBLANES, SEQ), np.float32)
    signs[0] = 1.0 - 2.0 * (np.arange(SEQ) % 2)
    return (jnp.asarray(tables, F32), jnp.asarray(e1, BF16), jnp.asarray(e2, BF16),
            jnp.asarray(rev_window, BF16), jnp.asarray(signs, BF16))


def kernel(x, mem, pf_w_in, pf_pool_w, pf_pool_scale, pf_fourier_ln_g, pf_fourier_w, pf_w_out, hg_w_in, hg_lower_bounds, hg_norm_g, hg_w_out, xa_wq, xa_wkv, xa_wo, moe_w_group, moe_b_group, moe_w_expert, moe_b_expert, moe_w_gate, moe_w_up, moe_w_down, ln_g, ln_b):
    bf = lambda a: a.astype(BF16)
    xt = x.reshape(TOKENS, D_MODEL)
    mem2d = mem.reshape(BATCH * MEM_LEN, D_MODEL)
    lb_all = jnp.cumsum(jax.nn.softmax(hg_lower_bounds.astype(F32), axis=0), axis=0)
    lb_all = lb_all - lb_all[:1]
    tril = jnp.asarray(np.tril(np.ones((HG_CHUNK, HG_CHUNK), np.float32)), BF16)
    triu = jnp.asarray(np.triu(np.ones((HG_CHUNK, HG_CHUNK), np.float32)), BF16)
    strict_upper = jnp.asarray(np.triu(np.ones((TM, TM), np.float32), k=1), BF16)
    kv_all = _kv_proj(mem2d, xa_wkv)

    for l in range(DEPTH):
        j = l // 2
        lng = lambda s: ln_g[l, s][None, :]
        lnb = lambda s: ln_b[l, s][None, :]
        if l % 2 == 0:
            dft_consts = _sequence_dft_tables()
            ya, pq = _pf_in(xt, pf_w_in, j, pf_fourier_ln_g[j].reshape(1, HALF), _channel_dft_matrix(),
                            bf(pf_pool_w[j]), pf_pool_scale[j][None, :])
            yb = _fourier(*dft_consts, pq, bf(pf_fourier_w[j]))
            mix, heads_major, w_mix = (ya, yb), False, pf_w_out
        else:
            proj = _hg_in(xt, hg_w_in[j])
            lb = lb_all[l].reshape(2, HG_HEADS, 1, HG_DIM)
            og = _gla(proj, lb, hg_norm_g[j][None, :], tril, triu)
            mix, heads_major, w_mix = (og,), True, hg_w_out
        kv = kv_all
        wr, br = _router_params(moe_w_group[l], moe_b_group[l], moe_w_expert[l], moe_b_expert[l])
        xt, xt_tiles, eidx, gates, rank, cnt = _block(
            mix, heads_major, w_mix, j, xt, lng(0), lnb(0), xa_wq, kv, xa_wo, l, lng(1), lnb(1),
            wr, br, strict_upper)
        xt = _moe(xt, xt_tiles, eidx, gates, rank, cnt, moe_w_gate, moe_w_up, moe_w_down, l,
                  lng(2), lnb(2))
    return xt.reshape(BATCH, SEQ, D_MODEL)
```

```python
import functools

import jax
import jax.numpy as jnp
import numpy as np
from jax import lax
from jax.experimental import pallas as pl
from jax.experimental.pallas import tpu as pltpu

F32 = jnp.float32
BF16 = jnp.bfloat16
I32 = jnp.int32

D_MODEL = 1024
BATCH = 8
SEQ = 4096
DEPTH = 2
TOKENS = BATCH * SEQ
MEM_LEN = 256
HALF = D_MODEL // 2
POOL_WINDOWS = (2, 4, 8, 16)
POOL_DIM = 128
FOURIER_HEADS = 4
FOURIER_DIM = 128
HG_HEADS = 8
HG_DIM = 128
HG_N_PROJ = 5
HG_CHUNK = 64
XA_HEADS = 4
XA_HEAD_DIM = 256
MOE_GROUPS = 4
MOE_PER_GROUP = 8
MOE_EXPERTS = 32
MOE_TOPK = 2
MOE_HIDDEN = 512
DN_ALPHA = (2.0 * DEPTH) ** 0.25
LN_EPS = 1e-5

V7X_LANES = 128
V7X_SUBLANES = 8
V7X_VMEM_LIMIT_BYTES = 52 * 1024 * 1024

TM = 512
TM_ROUTE = 1024
TM_PERM = 2048
TM_COMB = 512
TE = 512
SORTED_ROWS = TOKENS * MOE_TOPK + MOE_EXPERTS * TE
N_EXPERT_TILES = SORTED_ROWS // TE
GLA_GROUP = 16
GLA_ROWS = GLA_GROUP * HG_CHUNK
FOURIER_TR = 512
ROUTER_ROWS = 40

_NT = (((1,), (1,)), ((), ()))


def _params(*sem):
    return pltpu.CompilerParams(dimension_semantics=sem, vmem_limit_bytes=V7X_VMEM_LIMIT_BYTES)


def _dot(a, b):
    return jnp.dot(a, b, preferred_element_type=F32)


def _layer_norm(y, g, b):
    mu = jnp.mean(y, axis=-1, keepdims=True)
    yc = y - mu
    var = jnp.mean(yc * yc, axis=-1, keepdims=True)
    return yc * lax.rsqrt(var + LN_EPS) * g + b


def _post_ln(x, h, g, b):
    return _layer_norm(DN_ALPHA * x + h, g, b)


def _silu(x):
    return x * jax.nn.sigmoid(x)


def _resident(shape, index):
    return pl.BlockSpec(shape, lambda *_: index, pipeline_mode=pl.Buffered(1))


def _cast_weight_once(w_ref, wbf_ref):
    @pl.when(pl.program_id(0) == 0)
    def _():
        wbf_ref[...] = w_ref[0].astype(BF16)


TM_PF = 1024
_POOL_HALO = 8


def _k_pf_in(x_ref, xp_ref, xn_ref, w_ref, lng_ref, wc_ref, pw_ref, ps_ref, ya_ref, pq_ref, wbf_ref, ext_ref):
    _cast_weight_once(w_ref, wbf_ref)
    tiles_per_seq = SEQ // TM_PF
    j = lax.rem(pl.program_id(0), tiles_per_seq)
    u = _dot(x_ref[...].astype(BF16), wbf_ref[...])
    halo = jnp.concatenate([xp_ref[...], xn_ref[...]], axis=0).astype(BF16)
    uh = _dot(halo, wbf_ref[:, :HALF])
    ext_ref[_POOL_HALO:_POOL_HALO + TM_PF, :] = u[:, :HALF]
    ext_ref[0:_POOL_HALO, :] = jnp.where(j == 0, 0.0, uh[:_POOL_HALO])
    ext_ref[_POOL_HALO + TM_PF:, :] = jnp.where(j == tiles_per_seq - 1, 0.0, uh[_POOL_HALO:])
    parts = []
    for h in range(FOURIER_HEADS):
        ub = u[:, HALF + FOURIER_DIM * h:HALF + FOURIER_DIM * (h + 1)]
        mu = jnp.mean(ub, axis=-1, keepdims=True)
        uc = ub - mu
        var = jnp.mean(uc * uc, axis=-1, keepdims=True)
        parts.append(uc * lax.rsqrt(var + LN_EPS))
    un = jnp.concatenate(parts, axis=-1) * lng_ref[...]
    pq_ref[...] = _dot(un.astype(BF16), wc_ref[...]).astype(BF16)
    pos = lax.broadcasted_iota(I32, (TM_PF, POOL_DIM), 0) + j * TM_PF
    outs = []
    for g, w in enumerate(POOL_WINDOWS):
        hw = w // 2
        cols = slice(g * POOL_DIM, (g + 1) * POOL_DIM)
        acc = ext_ref[_POOL_HALO - hw:_POOL_HALO - hw + TM_PF, cols]
        for d in range(-hw + 1, hw):
            acc = acc + ext_ref[_POOL_HALO + d:_POOL_HALO + d + TM_PF, cols]
        cnt = jnp.minimum(pos + hw, SEQ) - jnp.maximum(pos - hw, 0)
        pooled = acc / cnt.astype(F32) - ext_ref[_POOL_HALO:_POOL_HALO + TM_PF, cols]
        outs.append(_dot(pooled.astype(BF16), pw_ref[g]))
    ya_ref[...] = (jnp.concatenate(outs, axis=-1) * ps_ref[...]).astype(BF16)


def _pf_in(x, w_in_all, j, ln_g, wc, pool_w, pool_scale):
    blocks_per_tile = TM_PF // _POOL_HALO
    last_block = TOKENS // _POOL_HALO - 1
    row = lambda i: (i, 0)
    fixed = lambda i: (0, 0)
    return pl.pallas_call(
        _k_pf_in,
        out_shape=(jax.ShapeDtypeStruct((TOKENS, HALF), BF16),
                   jax.ShapeDtypeStruct((TOKENS, D_MODEL), BF16)),
        grid=(TOKENS // TM_PF,),
        in_specs=[pl.BlockSpec((TM_PF, D_MODEL), row),
                  pl.BlockSpec((_POOL_HALO, D_MODEL), lambda i: (jnp.maximum(i * blocks_per_tile - 1, 0), 0)),
                  pl.BlockSpec((_POOL_HALO, D_MODEL),
                               lambda i: (jnp.minimum((i + 1) * blocks_per_tile, last_block), 0)),
                  _resident((1, D_MODEL, D_MODEL), (j, 0, 0)),
                  pl.BlockSpec((1, HALF), fixed),
                  _resident((HALF, D_MODEL), (0, 0)),
                  pl.BlockSpec((4, POOL_DIM, POOL_DIM), lambda i: (0, 0, 0)),
                  pl.BlockSpec((1, HALF), fixed)],
        out_specs=(pl.BlockSpec((TM_PF, HALF), row),
                   pl.BlockSpec((TM_PF, D_MODEL), row)),
        scratch_shapes=[pltpu.VMEM((D_MODEL, D_MODEL), BF16),
                        pltpu.VMEM((TM_PF + 2 * _POOL_HALO, HALF), F32)],
        compiler_params=_params("arbitrary"),
        name="pf_in",
    )(x, x, x, w_in_all, ln_g, wc, pool_w, pool_scale)


_FOURIER_SCALE = 1.0 / float(np.sqrt(SEQ * FOURIER_DIM))


_DFT_SIDE = 64
_DFT_COL_CHUNK = 1024


_DFT_HALF = SEQ // 2
_REV_BLOCK = 256


def _k_fourier(tab_ref, e1_ref, e2_ref, rw_ref, sg_ref, pq_ref, fw_ref, yb_ref,
               cs_ref, ss_ref, pf_ref, qf_ref, z_ref):
    h_rows = _DFT_HALF

    @pl.when(pl.program_id(0) == 0)
    def _():
        def expand(v, e):
            hi = v.astype(BF16)
            lo = (v - hi.astype(F32)).astype(BF16)
            return _dot(hi, e) + _dot(lo, e)

        s = _DFT_SIDE
        for rb in range(h_rows // FOURIER_TR):
            rows = slice(rb * FOURIER_TR, (rb + 1) * FOURIER_TR)
            tab = tab_ref[rows, :]
            for c in range(h_rows // _DFT_COL_CHUNK):
                cols = slice(c * _DFT_COL_CHUNK, (c + 1) * _DFT_COL_CHUNK)
                ca = expand(tab[:, 0:s], e1_ref[:, cols])
                sa = expand(tab[:, s:2 * s], e1_ref[:, cols])
                cb = expand(tab[:, 2 * s:3 * s], e2_ref[:, cols])
                sb = expand(tab[:, 3 * s:4 * s], e2_ref[:, cols])
                cs_ref[rows, cols] = (ca * cb - sa * sb).astype(BF16)
                ss_ref[rows, cols] = (sa * cb + ca * sb).astype(BF16)

    def head_maps(y):
        return jnp.concatenate(
            [_dot(y[:, FOURIER_DIM * h:FOURIER_DIM * (h + 1)].astype(BF16), fw_ref[h])
             for h in range(FOURIER_HEADS)], axis=-1)

    def reversed_block(src_ref, blk):
        lo = _REV_BLOCK * (h_rows // _REV_BLOCK - 1 - blk)
        if blk == 0:
            tail = src_ref[lo:lo + _REV_BLOCK, :]
            win = jnp.concatenate([tail, jnp.zeros_like(tail)], axis=0)
        else:
            win = src_ref[lo:lo + 2 * _REV_BLOCK, :]
        return _dot(rw_ref[...], win)

    n_blocks = h_rows // _REV_BLOCK
    upper = pq_ref.at[h_rows:SEQ]
    for blk in range(n_blocks):
        rows = slice(blk * _REV_BLOCK, (blk + 1) * _REV_BLOCK)
        rev = reversed_block(upper, blk)
        low = pq_ref[rows, :].astype(F32)
        pf_ref[rows, :] = (low[:, :HALF] + rev[:, :HALF]).astype(BF16)
        qf_ref[rows, :] = (low[:, HALF:] - rev[:, HALF:]).astype(BF16)

    p_mid = pq_ref[h_rows:h_rows + 1, :HALF].astype(F32)
    for rb in range(h_rows // FOURIER_TR):
        rows = slice(rb * FOURIER_TR, (rb + 1) * FOURIER_TR)
        a = _dot(cs_ref[rows, :], pf_ref[...])
        bm = _dot(ss_ref[rows, :], qf_ref[...])
        j = lax.broadcasted_iota(I32, (FOURIER_TR, 1), 0) + rb * FOURIER_TR
        t = jnp.where((j & 1) == 0, 1.0, -1.0) * p_mid
        yb_ref[rows, :] = head_maps((a - bm + t) * _FOURIER_SCALE).astype(BF16)
        z_ref[rows, :] = head_maps((a + bm + t) * _FOURIER_SCALE).astype(BF16)

    y_mid = head_maps(_dot(sg_ref[...], pq_ref[:, :HALF]) * _FOURIER_SCALE)[0:1]
    for blk in range(n_blocks):
        up = reversed_block(z_ref, blk)
        if blk == 0:
            r = lax.broadcasted_iota(I32, (_REV_BLOCK, 1), 0)
            up = jnp.where(r == 0, y_mid, up)
        yb_ref[h_rows + blk * _REV_BLOCK:h_rows + (blk + 1) * _REV_BLOCK, :] = up.astype(BF16)


def _fourier(tables, e1, e2, rev_window, signs, pq, fw):
    h_rows = _DFT_HALF
    return pl.pallas_call(
        _k_fourier,
        out_shape=jax.ShapeDtypeStruct((TOKENS, HALF), BF16),
        grid=(BATCH,),
        in_specs=[_resident((h_rows, 4 * _DFT_SIDE), (0, 0)),
                  _resident((_DFT_SIDE, h_rows), (0, 0)),
                  _resident((_DFT_SIDE, h_rows), (0, 0)),
                  _resident((_REV_BLOCK, 2 * _REV_BLOCK), (0, 0)),
                  _resident((V7X_SUBLANES, SEQ), (0, 0)),
                  pl.BlockSpec((SEQ, D_MODEL), lambda b: (b, 0), pipeline_mode=pl.Buffered(1)),
                  pl.BlockSpec((4, FOURIER_DIM, FOURIER_DIM), lambda b: (0, 0, 0))],
        out_specs=pl.BlockSpec((SEQ, HALF), lambda b: (b, 0)),
        scratch_shapes=[pltpu.VMEM((h_rows, h_rows), BF16), pltpu.VMEM((h_rows, h_rows), BF16),
                        pltpu.VMEM((h_rows, HALF), BF16), pltpu.VMEM((h_rows, HALF), BF16),
                        pltpu.VMEM((h_rows, HALF), BF16)],
        compiler_params=_params("arbitrary"),
        name="pf_fourier",
    )(tables, e1, e2, rev_window, signs, pq, fw)


_HG_STAGE_COLS = 512


def _k_hg_in(x_ref, w_hbm, o_ref, wbf_ref, stage_ref, sem):
    @pl.when(pl.program_id(0) == 0)
    def _():
        n_chunks = HG_N_PROJ * D_MODEL // _HG_STAGE_COLS

        def chunk_copy(c):
            cols = pl.ds(c * _HG_STAGE_COLS, _HG_STAGE_COLS)
            return pltpu.make_async_copy(w_hbm.at[:, cols], stage_ref.at[c % 2], sem.at[c % 2])

        chunk_copy(0).start()
        for c in range(n_chunks):
            if c + 1 < n_chunks:
                chunk_copy(c + 1).start()
            chunk_copy(c).wait()
            wbf_ref[:, c * _HG_STAGE_COLS:(c + 1) * _HG_STAGE_COLS] = stage_ref[c % 2].astype(BF16)

    xb = x_ref[...].astype(BF16)
    for p in range(HG_N_PROJ):
        acc = _dot(xb, wbf_ref[:, p * D_MODEL:(p + 1) * D_MODEL])
        if p == 0:
            acc = _silu(acc)
        for h in range(HG_HEADS):
            o_ref[p, h] = acc[:, HG_DIM * h:HG_DIM * (h + 1)].astype(BF16)


def _hg_in(x, w_in):
    return pl.pallas_call(
        _k_hg_in,
        out_shape=jax.ShapeDtypeStruct((HG_N_PROJ, HG_HEADS, TOKENS, HG_DIM), BF16),
        grid=(TOKENS // TM,),
        in_specs=[pl.BlockSpec((TM, D_MODEL), lambda i: (i, 0)),
                  pl.BlockSpec(memory_space=pl.ANY)],
        out_specs=pl.BlockSpec((HG_N_PROJ, HG_HEADS, TM, HG_DIM), lambda i: (0, 0, i, 0)),
        scratch_shapes=[pltpu.VMEM((D_MODEL, HG_N_PROJ * D_MODEL), BF16),
                        pltpu.VMEM((2, D_MODEL, _HG_STAGE_COLS), F32),
                        pltpu.SemaphoreType.DMA((2,))],
        compiler_params=_params("arbitrary"),
        name="hg_in",
    )(x, w_in)


def _gla_prep(q, z, lbv, tri, rev):
    L = HG_CHUNK
    G = q.shape[0] // L
    f = lbv + (1.0 - lbv) * jax.nn.sigmoid(z.astype(F32))
    lf = jnp.log(f).reshape(G, L, HG_DIM)
    k3 = (1.0 - f).reshape(G, L, HG_DIM)
    q3 = q.astype(F32).reshape(G, L, HG_DIM)
    hi32 = lax.bitcast_convert_type(
        lax.bitcast_convert_type(lf, jnp.uint32) & jnp.uint32(0xFFFF0000), F32)
    lf_hi = hi32.astype(BF16)
    lf_lo = (lf - hi32).astype(BF16)
    b2 = jnp.einsum('glm,gmk->glk', jnp.broadcast_to(tri, (G, L, L)),
                    jnp.concatenate([lf_hi, lf_lo], axis=-1), preferred_element_type=F32)
    b = b2[..., :HG_DIM] + b2[..., HG_DIM:]
    mid = L // 2 if not rev else L - 1 - L // 2
    end = L - 1 if not rev else 0
    b_ref = b[:, mid:mid + 1, :]
    b_last = b[:, end:end + 1, :]
    e_pos = jnp.exp(b - b_ref)
    qt32 = q3 * e_pos
    kt32 = k3 * (1.0 / e_pos)
    ku = kt32 * jnp.exp(b_last - b_ref)
    qe = qt32 * jnp.exp(b_ref)
    dec = jnp.exp(b_last)
    return qt32.astype(BF16), kt32.astype(BF16), ku.astype(BF16), qe.astype(BF16), dec


def _gla_intra(qt, kt, ku, v, rev):
    L = HG_CHUNK
    v3 = v.reshape(qt.shape[0], L, HG_DIM)
    sc = jnp.einsum('glk,gmk->glm', qt, kt, preferred_element_type=F32)
    li = lax.broadcasted_iota(I32, (L, L), 0)
    mi = lax.broadcasted_iota(I32, (L, L), 1)
    causal = (li >= mi) if not rev else (li <= mi)
    sc = jnp.where(causal[None], sc, 0.0).astype(BF16)
    u_t = jnp.einsum('glv,glk->gvk', v3, ku, preferred_element_type=F32)
    return sc, v3, u_t


def _gla_inter(qe, dec, sc, v3, u_t, st, rev):
    G = qe.shape[0]
    outs = [None] * G
    for n in (range(G) if not rev else range(G - 1, -1, -1)):
        outs[n] = (_dot(sc[n], v3[n])
                   + lax.dot_general(qe[n], st.astype(BF16), _NT, preferred_element_type=F32))
        st = dec[n] * st + u_t[n]
    return jnp.concatenate(outs, axis=0), st


_GLA_N_OPERANDS = 5


def _k_gla(q_ref, v_ref, zf_ref, zb_ref, g_ref, lb_ref, ng_ref, lt_ref, ut_ref, o_ref, of_ref, ob_ref,
           *operand_refs):
    n_groups = SEQ // GLA_ROWS
    per_slot = 2 * _GLA_N_OPERANDS
    slots = [[operand_refs[s * per_slot + d * _GLA_N_OPERANDS:s * per_slot + (d + 1) * _GLA_N_OPERANDS]
              for d in range(2)] for s in range(2)]
    directions = ((zf_ref, lt_ref, False), (zb_ref, ut_ref, True))

    def group_rows(t, rev):
        gi = t if not rev else n_groups - 1 - t
        return slice(gi * GLA_ROWS, (gi + 1) * GLA_ROWS)

    def prep(slot, t):
        for d, (z_ref, tri_ref, rev) in enumerate(directions):
            rows = group_rows(t, rev)
            vals = _gla_prep(q_ref[0, 0, rows, :], z_ref[0, 0, rows, :], lb_ref[d, 0], tri_ref[...], rev)
            for ref, val in zip(slots[slot][d], vals):
                ref[...] = val

    def mix(slot, t, states):
        out_refs = (of_ref, ob_ref)
        intra = []
        for d, (_, _, rev) in enumerate(directions):
            qt_ref, kt_ref, ku_ref, _, _ = slots[slot][d]
            intra.append(_gla_intra(qt_ref[...], kt_ref[...], ku_ref[...],
                                    v_ref[0, 0, group_rows(t, rev), :], rev))
        new_states = []
        for d, (_, _, rev) in enumerate(directions):
            _, _, _, qe_ref, dec_ref = slots[slot][d]
            o, st = _gla_inter(qe_ref[...], dec_ref[...], *intra[d], states[d], rev)
            out_refs[d][group_rows(t, rev), :] = o
            new_states.append(st)
        return new_states

    def finish(gi):
        rows = slice(gi * GLA_ROWS, (gi + 1) * GLA_ROWS)
        o = of_ref[rows, :] + ob_ref[rows, :]
        o = o * lax.rsqrt(jnp.mean(o * o, axis=-1, keepdims=True) + LN_EPS)
        o = o * ng_ref[...] * _silu(g_ref[0, 0, rows, :].astype(F32))
        o_ref[0, rows, :] = o.astype(BF16)

    states = [jnp.zeros((HG_DIM, HG_DIM), F32)] * 2
    prep(0, 0)
    for t in range(n_groups):
        if t + 1 < n_groups:
            prep((t + 1) % 2, t + 1)
        states = mix(t % 2, t, states)
        for gi in {t, n_groups - 1 - t}:
            if max(gi, n_groups - 1 - gi) == t:
                finish(gi)


def _gla(proj, lb, norm_g, tril, triu):
    def pspec(p):
        return pl.BlockSpec((1, 1, SEQ, HG_DIM), lambda b, h: (p, h, b, 0))
    fixed = lambda b, h: (0, 0)
    return pl.pallas_call(
        _k_gla,
        out_shape=jax.ShapeDtypeStruct((HG_HEADS, TOKENS, HG_DIM), BF16),
        grid=(BATCH, HG_HEADS),
        in_specs=[pspec(0), pspec(1), pspec(2), pspec(3), pspec(4),
                  pl.BlockSpec((2, 1, 1, HG_DIM), lambda b, h: (0, h, 0, 0)),
                  pl.BlockSpec((1, HG_DIM), fixed),
                  pl.BlockSpec((HG_CHUNK, HG_CHUNK), fixed),
                  pl.BlockSpec((HG_CHUNK, HG_CHUNK), fixed)],
        out_specs=pl.BlockSpec((1, SEQ, HG_DIM), lambda b, h: (h, b, 0)),
        scratch_shapes=[pltpu.VMEM((SEQ, HG_DIM), F32), pltpu.VMEM((SEQ, HG_DIM), F32)]
        + 4 * ([pltpu.VMEM((GLA_GROUP, HG_CHUNK, HG_DIM), BF16)] * (_GLA_N_OPERANDS - 1)
               + [pltpu.VMEM((GLA_GROUP, 1, HG_DIM), F32)]),
        compiler_params=_params("parallel", "parallel"),
        name="hg_gla",
    )(proj, proj, proj, proj, proj, lb, norm_g, tril, triu)


def _k_mm(x_ref, w_ref, o_ref):
    o_ref[0] = _dot(x_ref[...].astype(BF16), w_ref[0].astype(BF16)).astype(o_ref.dtype)


def _kv_proj(mem2d, wkv_all):
    rows, cols = mem2d.shape[0], wkv_all.shape[2]
    return pl.pallas_call(
        _k_mm,
        out_shape=jax.ShapeDtypeStruct((DEPTH, rows, cols), BF16),
        grid=(DEPTH, cols // D_MODEL, rows // TM),
        in_specs=[pl.BlockSpec((TM, D_MODEL), lambda l, j, i: (i, 0)),
                  pl.BlockSpec((1, D_MODEL, D_MODEL), lambda l, j, i: (l, 0, j))],
        out_specs=pl.BlockSpec((1, TM, D_MODEL), lambda l, j, i: (l, i, j)),
        compiler_params=_params("parallel", "parallel", "parallel"),
        name="xa_kv",
    )(mem2d, wkv_all)


ROW_WORDS = D_MODEL // 2
ROW_SLABS = ROW_WORDS // V7X_LANES
U32 = jnp.uint32
_HI_MASK = 0xFFFF0000


def _to_row_tiles(ref, x):
    n = x.shape[0]
    as_bits = lambda v: lax.bitcast_convert_type(v.astype(BF16).astype(F32), U32)
    words = (as_bits(x[:, :ROW_WORDS]) >> 16) | (as_bits(x[:, ROW_WORDS:]) & U32(_HI_MASK))
    for s in range(ROW_SLABS):
        ref[pl.ds(s, n, stride=ROW_SLABS), :] = words[:, V7X_LANES * s:V7X_LANES * (s + 1)]


def _from_row_tiles(ref):
    n = ref.shape[0] // ROW_SLABS
    words = jnp.concatenate([ref[pl.ds(s, n, stride=ROW_SLABS), :] for s in range(ROW_SLABS)], axis=-1)
    lo = lax.bitcast_convert_type(words << 16, F32)
    hi = lax.bitcast_convert_type(words & U32(_HI_MASK), F32)
    return jnp.concatenate([lo, hi], axis=-1)


def _first_argmax_rows(v, n_rows):
    m = jnp.max(v, axis=0, keepdims=True)
    rows = lax.broadcasted_iota(I32, v.shape, 0)
    idx = jnp.min(jnp.where(v == m, rows, n_rows), axis=0, keepdims=True)
    return m, idx


def _route(lt):
    gl = lt[0:MOE_GROUPS, :]
    gmax, gi = _first_argmax_rows(gl, MOE_GROUPS)
    p_grp = 1.0 / jnp.sum(jnp.exp(gl - gmax), axis=0, keepdims=True)
    sel = lt[MOE_GROUPS:MOE_GROUPS + MOE_PER_GROUP, :]
    for g in range(1, MOE_GROUPS):
        lo = MOE_GROUPS + MOE_PER_GROUP * g
        sel = jnp.where(gi == g, lt[lo:lo + MOE_PER_GROUP, :], sel)
    m1, i1 = _first_argmax_rows(sel, MOE_PER_GROUP)
    rows = lax.broadcasted_iota(I32, sel.shape, 0)
    m2, i2 = _first_argmax_rows(jnp.where(rows == i1, -jnp.inf, sel), MOE_PER_GROUP)
    e2 = jnp.exp(m2 - m1)
    g1 = p_grp / (1.0 + e2)
    return (gi * MOE_PER_GROUP + i1, gi * MOE_PER_GROUP + i2), (g1, g1 * e2)


def _rank_in_expert(e0, e1, strict_upper, carry_ref, counted):
    rows = lax.broadcasted_iota(I32, (MOE_EXPERTS, e0.shape[1]), 0)
    oh0 = rows == e0
    oh1 = rows == e1
    oh = jnp.where(oh0 | oh1, 1.0, 0.0)
    before = _dot(oh.astype(BF16), strict_upper) + carry_ref[:, 0:1]
    r0 = jnp.sum(jnp.where(oh0, before, 0.0), axis=0, keepdims=True).astype(I32)
    r1 = jnp.sum(jnp.where(oh1, before, 0.0), axis=0, keepdims=True).astype(I32)
    carry_ref[...] = carry_ref[...] + counted * jnp.sum(oh, axis=1, keepdims=True)
    return r0, r1


def _k_block(heads_major, *refs):
    n_mix = 1 if heads_major else 2
    mix_refs = refs[:n_mix]
    (wm_ref, x_ref, g0_ref, b0_ref, wq_ref, kv_ref, wo_ref, g_ref, b_ref, wr_ref, br_ref, tri_ref,
     o_ref, o3_ref, e_ref, gate_ref, rank_ref, cnt_ref,
     wmbf_ref, wqbf_ref, wobf_ref, carry_ref, pre_ref) = refs[n_mix:]
    step = pl.program_id(0)
    _cast_weight_once(wm_ref, wmbf_ref)
    _cast_weight_once(wq_ref, wqbf_ref)
    _cast_weight_once(wo_ref, wobf_ref)

    @pl.when(step == 0)
    def _():
        carry_ref[...] = jnp.zeros_like(carry_ref)
        pre_ref[...] = jnp.zeros_like(pre_ref)

    quarter = TM // 4
    routed = []

    def stage2(c):
        rows = slice(c * quarter, (c + 1) * quarter)
        y = _layer_norm(pre_ref[rows, :], g_ref[...], b_ref[...])
        o_ref[rows, :] = y
        _to_row_tiles(o3_ref.at[pl.ds(c * quarter * ROW_SLABS, quarter * ROW_SLABS)], y)
        lt = lax.dot_general(wr_ref[...], y.astype(BF16), _NT, preferred_element_type=F32) + br_ref[...]
        routed.append(_route(lt))

    stage2(0)
    if heads_major:
        a = jnp.concatenate([mix_refs[0][h] for h in range(HG_HEADS)], axis=-1)
    else:
        a = jnp.concatenate([mix_refs[0][...], mix_refs[1][...]], axis=-1)
    h1 = _dot(a, wmbf_ref[...])
    stage2(1)
    x = _post_ln(x_ref[...], h1, g0_ref[...], b0_ref[...])
    q = _dot(x.astype(BF16), wqbf_ref[...]).astype(BF16)
    stage2(2)
    outs = []
    for h in range(XA_HEADS):
        cols = slice(XA_HEAD_DIM * h, XA_HEAD_DIM * (h + 1))
        kh = kv_ref[:, cols]
        vh = kv_ref[:, D_MODEL + XA_HEAD_DIM * h:D_MODEL + XA_HEAD_DIM * (h + 1)]
        s = lax.dot_general(q[:, cols], kh, _NT, preferred_element_type=F32) * (XA_HEAD_DIM ** -0.5)
        e = jnp.exp(s - jnp.max(s, axis=-1, keepdims=True))
        p = e / jnp.sum(e, axis=-1, keepdims=True)
        outs.append(_dot(p.astype(BF16), vh))
    o = jnp.concatenate(outs, axis=-1).astype(BF16)
    h2 = _dot(o, wobf_ref[...])
    stage2(3)
    e0, e1, gate0, gate1 = [jnp.concatenate([r[a_][b_] for r in routed], axis=-1)
                            for a_ in range(2) for b_ in range(2)]
    r0, r1 = _rank_in_expert(e0, e1, tri_ref[...], carry_ref, jnp.where(step > 0, 1.0, 0.0))
    e_ref[0:1, :], e_ref[1:2, :] = e0, e1
    gate_ref[0:1, :], gate_ref[1:2, :] = gate0, gate1
    rank_ref[0:1, :], rank_ref[1:2, :] = r0, r1
    cnt_ref[...] = carry_ref[...]
    pre_ref[...] = DN_ALPHA * x + h2


def _block(mix, heads_major, w_mix_all, j, x, g0, b0, wq_all, kv, wo_all, l, g1, b1, wr, br, strict_upper):
    n_tiles = TOKENS // TM
    tiles_per_seq = SEQ // TM
    cur = lambda i: jnp.minimum(i, n_tiles - 1)
    prev = lambda i: jnp.maximum(i - 1, 0)
    row_in = lambda i: (cur(i), 0)
    row_out = lambda i: (prev(i), 0)
    fixed = lambda i: (0, 0)
    tok = pl.BlockSpec((MOE_TOPK, TM), lambda i: (0, prev(i)))
    if heads_major:
        mix_specs = [pl.BlockSpec((HG_HEADS, TM, HG_DIM), lambda i: (0, cur(i), 0))]
    else:
        mix_specs = [pl.BlockSpec((TM, HALF), row_in), pl.BlockSpec((TM, HALF), row_in)]
    return pl.pallas_call(
        functools.partial(_k_block, heads_major),
        out_shape=(jax.ShapeDtypeStruct((TOKENS, D_MODEL), F32),
                   jax.ShapeDtypeStruct((TOKENS * ROW_SLABS, V7X_LANES), U32),
                   jax.ShapeDtypeStruct((MOE_TOPK, TOKENS), I32),
                   jax.ShapeDtypeStruct((MOE_TOPK, TOKENS), F32),
                   jax.ShapeDtypeStruct((MOE_TOPK, TOKENS), I32),
                   jax.ShapeDtypeStruct((MOE_EXPERTS, V7X_LANES), F32)),
        grid=(n_tiles + 1,),
        in_specs=mix_specs + [
            _resident((1, D_MODEL, D_MODEL), (j, 0, 0)),
            pl.BlockSpec((TM, D_MODEL), row_in),
            pl.BlockSpec((1, D_MODEL), fixed), pl.BlockSpec((1, D_MODEL), fixed),
            _resident((1, D_MODEL, D_MODEL), (l, 0, 0)),
            pl.BlockSpec((None, MEM_LEN, 2 * D_MODEL), lambda i: (l, cur(i) // tiles_per_seq, 0)),
            _resident((1, D_MODEL, D_MODEL), (l, 0, 0)),
            pl.BlockSpec((1, D_MODEL), fixed), pl.BlockSpec((1, D_MODEL), fixed),
            _resident((ROUTER_ROWS, D_MODEL), (0, 0)),
            _resident((ROUTER_ROWS, 1), (0, 0)),
            _resident((TM, TM), (0, 0))],
        out_specs=(pl.BlockSpec((TM, D_MODEL), row_out),
                   pl.BlockSpec((TM * ROW_SLABS, V7X_LANES), row_out),
                   tok, tok, tok,
                   pl.BlockSpec((MOE_EXPERTS, V7X_LANES), fixed)),
        scratch_shapes=[pltpu.VMEM((D_MODEL, D_MODEL), BF16), pltpu.VMEM((D_MODEL, D_MODEL), BF16),
                        pltpu.VMEM((D_MODEL, D_MODEL), BF16), pltpu.VMEM((MOE_EXPERTS, V7X_LANES), F32),
                        pltpu.VMEM((TM, D_MODEL), F32)],
        compiler_params=_params("arbitrary"),
        name="attn_block",
    )(*mix, w_mix_all, x, g0, b0, wq_all, kv, wo_all, g1, b1, wr, br, strict_upper)


def _k_pos(e_ref, rank_ref, off_ref, pos_ref):
    rows = lax.broadcasted_iota(I32, (MOE_EXPERTS, TM_ROUTE), 0)
    for k in range(MOE_TOPK):
        start = jnp.sum(jnp.where(rows == e_ref[k:k + 1, :], off_ref[...], 0), axis=0, keepdims=True)
        pos_ref[k:k + 1, :] = start + rank_ref[k:k + 1, :]


def _positions(eidx, rank, offsets):
    tok = pl.BlockSpec((MOE_TOPK, TM_ROUTE), lambda i: (0, i))
    return pl.pallas_call(
        _k_pos,
        out_shape=jax.ShapeDtypeStruct((MOE_TOPK, TOKENS), I32),
        grid=(TOKENS // TM_ROUTE,),
        in_specs=[tok, tok, pl.BlockSpec((MOE_EXPERTS, 1), lambda i: (0, 0))],
        out_specs=tok,
        compiler_params=_params("parallel"),
        name="moe_pos",
    )(eidx, rank, offsets)


_ISSUE_BATCH = 8


def _row_copy(src_ref, src_row, dst_ref, dst_row, sem):
    src = pl.ds(pl.multiple_of(src_row * ROW_SLABS, ROW_SLABS), ROW_SLABS)
    dst = pl.ds(pl.multiple_of(dst_row * ROW_SLABS, ROW_SLABS), ROW_SLABS)
    return pltpu.make_async_copy(src_ref.at[src], dst_ref.at[dst], sem)


_DISPATCH_SLOTS = 4


def _k_dispatch(pos_ref, ends_ref, x_ref, xs_ref, zero_ref, stage_ref, row_sems, load_sems, zsem):
    step = pl.program_id(0)
    n_steps = pl.num_programs(0)
    block_rows = TM_PERM * ROW_SLABS

    def load(tile):
        slot = lax.rem(tile, _DISPATCH_SLOTS)
        src = x_ref.at[pl.ds(pl.multiple_of(tile * block_rows, block_rows), block_rows)]
        return pltpu.make_async_copy(src, stage_ref.at[slot], load_sems.at[slot])

    def drain_rows(tile):
        slot = lax.rem(tile, _DISPATCH_SLOTS)
        for _ in range(MOE_TOPK):
            pltpu.make_async_copy(stage_ref.at[slot], xs_ref.at[pl.ds(0, block_rows)],
                                  row_sems.at[slot]).wait()

    @pl.when(step == 0)
    def _():
        load(0).start()
        pl.when(n_steps > 1)(load(1).start)
        zero_ref[...] = jnp.zeros_like(zero_ref)

        def last_tile(e):
            seg_start = ends_ref[e - 1] if e > 0 else 0
            start = ends_ref[e] - TE
            return ends_ref[e] > seg_start, zero_fill(start)

        def unused_tile(t):
            start = ends_ref[MOE_EXPERTS - 1] + t * TE
            return start < SORTED_ROWS, zero_fill(start)

        def zero_fill(start_row):
            rows = pl.ds(pl.multiple_of(start_row * ROW_SLABS, TE * ROW_SLABS), TE * ROW_SLABS)
            return pltpu.make_async_copy(zero_ref, xs_ref.at[rows], zsem)

        fills = [last_tile(e) for e in range(MOE_EXPERTS)] + [unused_tile(t) for t in range(MOE_EXPERTS)]
        for needed, copy in fills:
            pl.when(needed)(copy.start)
        for needed, copy in fills:
            pl.when(needed)(copy.wait)

    pl.when(step >= 2)(lambda: drain_rows(step - 2))
    pl.when(step + 2 < n_steps)(lambda: load(step + 2).start())
    load(step).wait()

    slot = lax.rem(step, _DISPATCH_SLOTS)
    base = step * TM_PERM

    def body(jb, c):
        j0 = jb * _ISSUE_BATCH
        dst = [[pos_ref[k * TOKENS + base + j0 + u] for k in range(MOE_TOPK)]
               for u in range(_ISSUE_BATCH)]
        for u in range(_ISSUE_BATCH):
            for k in range(MOE_TOPK):
                _row_copy(stage_ref.at[slot], j0 + u, xs_ref, dst[u][k],
                          row_sems.at[slot]).start(priority=k)
        return c

    lax.fori_loop(0, TM_PERM // _ISSUE_BATCH, body, 0)

    @pl.when(step == n_steps - 1)
    def _():
        pl.when(step >= 1)(lambda: drain_rows(step - 1))
        drain_rows(step)


def _dispatch(pos_flat, ends, x):
    return pl.pallas_call(
        _k_dispatch,
        out_shape=jax.ShapeDtypeStruct((SORTED_ROWS * ROW_SLABS, V7X_LANES), U32),
        grid_spec=pltpu.PrefetchScalarGridSpec(
            num_scalar_prefetch=2,
            grid=(TOKENS // TM_PERM,),
            in_specs=[pl.BlockSpec(memory_space=pl.ANY)],
            out_specs=pl.BlockSpec(memory_space=pl.ANY),
            scratch_shapes=[pltpu.VMEM((TE * ROW_SLABS, V7X_LANES), U32),
                            pltpu.VMEM((_DISPATCH_SLOTS, TM_PERM * ROW_SLABS, V7X_LANES), U32),
                            pltpu.SemaphoreType.DMA((_DISPATCH_SLOTS,)),
                            pltpu.SemaphoreType.DMA((_DISPATCH_SLOTS,)),
                            pltpu.SemaphoreType.DMA],
        ),
        compiler_params=_params("arbitrary"),
        name="moe_dispatch",
    )(pos_flat, ends, x)


def _k_experts(te_ref, na_ref, next_ref, xs_ref, wg_hbm, wu_hbm, wd_hbm, ys_ref,
               wgubf_ref, wdbf_ref, stage_g, stage_u, stage_d, sems, *, layer):
    i = pl.program_id(0)
    active = i < na_ref[0]
    expert = te_ref[i]
    new_expert = jnp.logical_or(i == 0, te_ref[jnp.maximum(i - 1, 0)] != expert)

    def weight_copies(e, slot):
        return (pltpu.make_async_copy(wg_hbm.at[layer, e], stage_g.at[slot], sems.at[slot, 0]),
                pltpu.make_async_copy(wu_hbm.at[layer, e], stage_u.at[slot], sems.at[slot, 1]),
                pltpu.make_async_copy(wd_hbm.at[layer, e], stage_d.at[slot], sems.at[slot, 2]))

    @pl.when(jnp.logical_and(active, new_expert))
    def _():
        slot = next_ref[i, 1]

        @pl.when(i == 0)
        def _():
            for c in weight_copies(expert, slot):
                c.start()

        upcoming = next_ref[i, 0]

        @pl.when(upcoming >= 0)
        def _():
            for c in weight_copies(upcoming, 1 - slot):
                c.start()

        for c in weight_copies(expert, slot):
            c.wait()
        wgubf_ref[:, :MOE_HIDDEN] = stage_g[slot].astype(BF16)
        wgubf_ref[:, MOE_HIDDEN:] = stage_u[slot].astype(BF16)
        wdbf_ref[...] = stage_d[slot].astype(BF16)

    @pl.when(active)
    def _():
        xb = _from_row_tiles(xs_ref).astype(BF16)
        gu = _dot(xb, wgubf_ref[...])
        h = _silu(gu[:, :MOE_HIDDEN]) * gu[:, MOE_HIDDEN:]
        _to_row_tiles(ys_ref, _dot(h.astype(BF16), wdbf_ref[...]))

    @pl.when(jnp.logical_not(active))
    def _():
        ys_ref[...] = jnp.zeros_like(ys_ref)


def _experts(tile_expert, n_active, tile_next, xs, wg_all, wu_all, wd_all, l):
    def xs_index(i, te, na, nxt):
        return (jnp.minimum(i, na[0] - 1), 0)

    return pl.pallas_call(
        functools.partial(_k_experts, layer=l),
        out_shape=jax.ShapeDtypeStruct((SORTED_ROWS * ROW_SLABS, V7X_LANES), U32),
        grid_spec=pltpu.PrefetchScalarGridSpec(
            num_scalar_prefetch=3,
            grid=(N_EXPERT_TILES,),
            in_specs=[pl.BlockSpec((TE * ROW_SLABS, V7X_LANES), xs_index),
                      pl.BlockSpec(memory_space=pl.ANY),
                      pl.BlockSpec(memory_space=pl.ANY),
                      pl.BlockSpec(memory_space=pl.ANY)],
            out_specs=pl.BlockSpec((TE * ROW_SLABS, V7X_LANES), lambda i, te, na, nxt: (i, 0)),
            scratch_shapes=[pltpu.VMEM((D_MODEL, 2 * MOE_HIDDEN), BF16),
                            pltpu.VMEM((MOE_HIDDEN, D_MODEL), BF16),
                            pltpu.VMEM((2, D_MODEL, MOE_HIDDEN), F32),
                            pltpu.VMEM((2, D_MODEL, MOE_HIDDEN), F32),
                            pltpu.VMEM((2, MOE_HIDDEN, D_MODEL), F32),
                            pltpu.SemaphoreType.DMA((2, 3))],
        ),
        compiler_params=_params("arbitrary"),
        name="moe_experts",
    )(tile_expert, n_active, tile_next, xs, wg_all, wu_all, wd_all)


_COMBINE_CHUNKS = 8


def _k_combine(pos_ref, ys_ref, gt_ref, x_ref, g_ref, b_ref, o_ref, buf_ref, sem):
    step = pl.program_id(0)
    last = pl.num_programs(0) - 1

    def issue_batch(base, j0, slot):
        src = [[pos_ref[k * TOKENS + base + j0 + u] for k in range(MOE_TOPK)]
               for u in range(_ISSUE_BATCH)]
        for u in range(_ISSUE_BATCH):
            for k in range(MOE_TOPK):
                _row_copy(ys_ref, src[u][k], buf_ref.at[slot, k], j0 + u,
                          sem.at[slot]).start(priority=k)

    def drain(slot):
        for k in range(MOE_TOPK):
            pltpu.make_async_copy(ys_ref.at[pl.ds(0, TM_COMB * ROW_SLABS)], buf_ref.at[slot, k],
                                  sem.at[slot]).wait()

    @pl.when(step == 0)
    def _():
        def body(jb, c):
            issue_batch(0, jb * _ISSUE_BATCH, 0)
            return c
        lax.fori_loop(0, TM_COMB // _ISSUE_BATCH, body, 0)

    slot = lax.rem(step, 2)
    drain(slot)
    next_base = jnp.minimum(step + 1, last) * TM_COMB
    chunk = TM_COMB // _COMBINE_CHUNKS
    for c in range(_COMBINE_CHUNKS):
        for jb in range(chunk // _ISSUE_BATCH):
            issue_batch(next_base, c * chunk + jb * _ISSUE_BATCH, 1 - slot)
        rows = slice(c * chunk, (c + 1) * chunk)
        tiles = pl.ds(c * chunk * ROW_SLABS, chunk * ROW_SLABS)
        h = (gt_ref[rows, 0:1] * _from_row_tiles(buf_ref.at[slot, 0, tiles])
             + gt_ref[rows, 1:2] * _from_row_tiles(buf_ref.at[slot, 1, tiles]))
        o_ref[rows, :] = _post_ln(x_ref[rows, :], h, g_ref[...], b_ref[...])

    pl.when(step == last)(lambda: drain(1 - slot))


def _combine(pos_flat, ys, gates_t, x, g, b):
    row = lambda i, pos: (i, 0)
    fixed = lambda i, pos: (0, 0)
    return pl.pallas_call(
        _k_combine,
        out_shape=jax.ShapeDtypeStruct((TOKENS, D_MODEL), F32),
        grid_spec=pltpu.PrefetchScalarGridSpec(
            num_scalar_prefetch=1,
            grid=(TOKENS // TM_COMB,),
            in_specs=[pl.BlockSpec(memory_space=pl.ANY),
                      pl.BlockSpec((TM_COMB, MOE_TOPK), row),
                      pl.BlockSpec((TM_COMB, D_MODEL), row),
                      pl.BlockSpec((1, D_MODEL), fixed), pl.BlockSpec((1, D_MODEL), fixed)],
            out_specs=pl.BlockSpec((TM_COMB, D_MODEL), row),
            scratch_shapes=[pltpu.VMEM((2, MOE_TOPK, TM_COMB * ROW_SLABS, V7X_LANES), U32),
                            pltpu.SemaphoreType.DMA((2,))],
        ),
        compiler_params=_params("arbitrary"),
        name="moe_combine",
    )(pos_flat, ys, gates_t, x, g, b)


def _router_params(w_group, b_group, w_expert, b_expert):
    pad = ROUTER_ROWS - MOE_GROUPS - MOE_EXPERTS
    wr = jnp.concatenate([w_group.T, w_expert.T, jnp.zeros((pad, D_MODEL), F32)], axis=0)
    br = jnp.concatenate([b_group, b_expert, jnp.zeros((pad,), F32)])[:, None]
    return wr.astype(BF16), br.astype(F32)


def _moe(x, x_tiles, eidx, gates, rank, cnt, w_gate_all, w_up_all, w_down_all, l, g, b):
    counts = cnt[:, 0].astype(I32)
    padded = ((counts + TE - 1) // TE) * TE
    ends = jnp.cumsum(padded)
    offsets = ends - padded
    tile_start = jnp.arange(N_EXPERT_TILES, dtype=I32) * TE
    tile_expert = jnp.sum((tile_start[:, None] >= ends[None, :]).astype(I32), axis=1)
    tile_expert = jnp.minimum(tile_expert, MOE_EXPERTS - 1)
    n_active = (ends[-1:] // TE).astype(I32)
    pos = _positions(eidx, rank, offsets[:, None])
    pos_flat = pos.reshape(-1)
    xs = _dispatch(pos_flat, ends.astype(I32), x_tiles)
    present = padded > 0
    ids = jnp.arange(MOE_EXPERTS, dtype=I32)
    later = jnp.where(present[None, :] & (ids[None, :] > ids[:, None]), ids[None, :], MOE_EXPERTS)
    next_present = jnp.min(later, axis=1)
    next_present = jnp.where(next_present >= MOE_EXPERTS, -1, next_present)
    appearance = jnp.cumsum(present.astype(I32)) - 1
    mine = tile_expert[:, None] == ids[None, :]
    lookup = lambda table: jnp.sum(jnp.where(mine, table[None, :], 0), axis=1)
    tile_next = jnp.stack([lookup(next_present), lookup(appearance % 2)], axis=1).astype(I32)
    ys = _experts(tile_expert, n_active, tile_next, xs, w_gate_all, w_up_all, w_down_all, l)
    return _combine(pos_flat, ys, gates.T, x, g, b)


def _channel_dft_matrix():
    c = np.arange(FOURIER_DIM)
    ang = 2.0 * np.pi * ((c[:, None] * c[None, :]) % FOURIER_DIM) / FOURIER_DIM
    wc = np.zeros((HALF, D_MODEL), np.float32)
    for h in range(FOURIER_HEADS):
        rows = slice(FOURIER_DIM * h, FOURIER_DIM * (h + 1))
        wc[rows, FOURIER_DIM * h:FOURIER_DIM * (h + 1)] = np.cos(ang)
        wc[rows, HALF + FOURIER_DIM * h:HALF + FOURIER_DIM * (h + 1)] = np.sin(ang)
    return jnp.asarray(wc, BF16)


def _sequence_dft_tables():
    side = _DFT_SIDE
    j = np.arange(_DFT_HALF, dtype=np.int64)[:, None]
    k = np.arange(side, dtype=np.int64)[None, :]
    ang_a = ((j * k) % side).astype(np.float64) * (2.0 * np.pi / side)
    ang_b = ((j * k) % SEQ).astype(np.float64) * (2.0 * np.pi / SEQ)
    tables = np.concatenate([np.cos(ang_a), np.sin(ang_a), np.cos(ang_b), np.sin(ang_b)], axis=1)
    col = np.arange(_DFT_HALF)
    e1 = (col[None, :] // side == np.arange(side)[:, None]).astype(np.float32)
    e2 = (col[None, :] % side == np.arange(side)[:, None]).astype(np.float32)
    r = np.arange(_REV_BLOCK)
    rev_window = (np.arange(2 * _REV_BLOCK)[None, :] == (_REV_BLOCK - r)[:, None]).astype(np.float32)
    signs = np.zeros((V7X_SUBLANES, SEQ), np.float32)
    signs[0] = 1.0 - 2.0 * (np.arange(SEQ) % 2)
    return (jnp.asarray(tables, F32), jnp.asarray(e1, BF16), jnp.asarray(e2, BF16),
            jnp.asarray(rev_window, BF16), jnp.asarray(signs, BF16))


def kernel(x, mem, pf_w_in, pf_pool_w, pf_pool_scale, pf_fourier_ln_g, pf_fourier_w, pf_w_out, hg_w_in, hg_lower_bounds, hg_norm_g, hg_w_out, xa_wq, xa_wkv, xa_wo, moe_w_group, moe_b_group, moe_w_expert, moe_b_expert, moe_w_gate, moe_w_up, moe_w_down, ln_g, ln_b):
    bf = lambda a: a.astype(BF16)
    xt = x.reshape(TOKENS, D_MODEL)
    mem2d = mem.reshape(BATCH * MEM_LEN, D_MODEL)
    lb_all = jnp.cumsum(jax.nn.softmax(hg_lower_bounds.astype(F32), axis=0), axis=0)
    lb_all = lb_all - lb_all[:1]
    tril = jnp.asarray(np.tril(np.ones((HG_CHUNK, HG_CHUNK), np.float32)), BF16)
    triu = jnp.asarray(np.triu(np.ones((HG_CHUNK, HG_CHUNK), np.float32)), BF16)
    strict_upper = jnp.asarray(np.triu(np.ones((TM, TM), np.float32), k=1), BF16)
    kv_all = _kv_proj(mem2d, xa_wkv)

    for l in range(DEPTH):
        j = l // 2
        lng = lambda s: ln_g[l, s][None, :]
        lnb = lambda s: ln_b[l, s][None, :]
        if l % 2 == 0:
            dft_consts = _sequence_dft_tables()
            ya, pq = _pf_in(xt, pf_w_in, j, pf_fourier_ln_g[j].reshape(1, HALF), _channel_dft_matrix(),
                            bf(pf_pool_w[j]), pf_pool_scale[j][None, :])
            yb = _fourier(*dft_consts, pq, bf(pf_fourier_w[j]))
            mix, heads_major, w_mix = (ya, yb), False, pf_w_out
        else:
            proj = _hg_in(xt, hg_w_in[j])
            lb = lb_all[l].reshape(2, HG_HEADS, 1, HG_DIM)
            og = _gla(proj, lb, hg_norm_g[j][None, :], tril, triu)
            mix, heads_major, w_mix = (og,), True, hg_w_out
        kv = kv_all
        wr, br = _router_params(moe_w_group[l], moe_b_group[l], moe_w_expert[l], moe_b_expert[l])
        xt, xt_tiles, eidx, gates, rank, cnt = _block(
            mix, heads_major, w_mix, j, xt, lng(0), lnb(0), xa_wq, kv, xa_wo, l, lng(1), lnb(1),
            wr, br, strict_upper)
        xt = _moe(xt, xt_tiles, eidx, gates, rank, cnt, moe_w_gate, moe_w_up, moe_w_down, l,
                  lng(2), lnb(2))
    return xt.reshape(BATCH, SEQ, D_MODEL)
```

```python
import functools

import jax
import jax.numpy as jnp
import numpy as np
from jax import lax
from jax.experimental import pallas as pl
from jax.experimental.pallas import tpu as pltpu

F32 = jnp.float32
BF16 = jnp.bfloat16
I32 = jnp.int32

D_MODEL = 1024
BATCH = 8
SEQ = 4096
DEPTH = 2
TOKENS = BATCH * SEQ
MEM_LEN = 256
HALF = D_MODEL // 2
POOL_WINDOWS = (2, 4, 8, 16)
POOL_DIM = 128
FOURIER_HEADS = 4
FOURIER_DIM = 128
HG_HEADS = 8
HG_DIM = 128
HG_N_PROJ = 5
HG_CHUNK = 64
XA_HEADS = 4
XA_HEAD_DIM = 256
MOE_GROUPS = 4
MOE_PER_GROUP = 8
MOE_EXPERTS = 32
MOE_TOPK = 2
MOE_HIDDEN = 512
DN_ALPHA = (2.0 * DEPTH) ** 0.25
LN_EPS = 1e-5

V7X_LANES = 128
V7X_SUBLANES = 8
V7X_VMEM_LIMIT_BYTES = 52 * 1024 * 1024

TM = 512
TM_ROUTE = 1024
TM_PERM = 2048
TM_COMB = 512
TE = 512
SORTED_ROWS = TOKENS * MOE_TOPK + MOE_EXPERTS * TE
N_EXPERT_TILES = SORTED_ROWS // TE
GLA_GROUP = 16
GLA_ROWS = GLA_GROUP * HG_CHUNK
FOURIER_TR = 512
ROUTER_ROWS = 40

_NT = (((1,), (1,)), ((), ()))


def _params(*sem):
    return pltpu.CompilerParams(dimension_semantics=sem, vmem_limit_bytes=V7X_VMEM_LIMIT_BYTES)


def _dot(a, b):
    return jnp.dot(a, b, preferred_element_type=F32)


def _layer_norm(y, g, b):
    mu = jnp.mean(y, axis=-1, keepdims=True)
    yc = y - mu
    var = jnp.mean(yc * yc, axis=-1, keepdims=True)
    return yc * lax.rsqrt(var + LN_EPS) * g + b


def _post_ln(x, h, g, b):
    return _layer_norm(DN_ALPHA * x + h, g, b)


def _silu(x):
    return x * jax.nn.sigmoid(x)


def _resident(shape, index):
    return pl.BlockSpec(shape, lambda *_: index, pipeline_mode=pl.Buffered(1))


def _cast_weight_once(w_ref, wbf_ref):
    @pl.when(pl.program_id(0) == 0)
    def _():
        wbf_ref[...] = w_ref[0].astype(BF16)


TM_PF = 1024
_POOL_HALO = 8


def _k_pf_in(x_ref, xp_ref, xn_ref, w_ref, lng_ref, wc_ref, pw_ref, ps_ref, ya_ref, pq_ref, wbf_ref, ext_ref):
    _cast_weight_once(w_ref, wbf_ref)
    tiles_per_seq = SEQ // TM_PF
    j = lax.rem(pl.program_id(0), tiles_per_seq)
    u = _dot(x_ref[...].astype(BF16), wbf_ref[...])
    halo = jnp.concatenate([xp_ref[...], xn_ref[...]], axis=0).astype(BF16)
    uh = _dot(halo, wbf_ref[:, :HALF])
    ext_ref[_POOL_HALO:_POOL_HALO + TM_PF, :] = u[:, :HALF]
    ext_ref[0:_POOL_HALO, :] = jnp.where(j == 0, 0.0, uh[:_POOL_HALO])
    ext_ref[_POOL_HALO + TM_PF:, :] = jnp.where(j == tiles_per_seq - 1, 0.0, uh[_POOL_HALO:])
    parts = []
    for h in range(FOURIER_HEADS):
        ub = u[:, HALF + FOURIER_DIM * h:HALF + FOURIER_DIM * (h + 1)]
        mu = jnp.mean(ub, axis=-1, keepdims=True)
        uc = ub - mu
        var = jnp.mean(uc * uc, axis=-1, keepdims=True)
        parts.append(uc * lax.rsqrt(var + LN_EPS))
    un = jnp.concatenate(parts, axis=-1) * lng_ref[...]
    pq_ref[...] = _dot(un.astype(BF16), wc_ref[...]).astype(BF16)
    pos = lax.broadcasted_iota(I32, (TM_PF, POOL_DIM), 0) + j * TM_PF
    outs = []
    for g, w in enumerate(POOL_WINDOWS):
        hw = w // 2
        cols = slice(g * POOL_DIM, (g + 1) * POOL_DIM)
        acc = ext_ref[_POOL_HALO - hw:_POOL_HALO - hw + TM_PF, cols]
        for d in range(-hw + 1, hw):
            acc = acc + ext_ref[_POOL_HALO + d:_POOL_HALO + d + TM_PF, cols]
        cnt = jnp.minimum(pos + hw, SEQ) - jnp.maximum(pos - hw, 0)
        pooled = acc / cnt.astype(F32) - ext_ref[_POOL_HALO:_POOL_HALO + TM_PF, cols]
        outs.append(_dot(pooled.astype(BF16), pw_ref[g]))
    ya_ref[...] = (jnp.concatenate(outs, axis=-1) * ps_ref[...]).astype(BF16)


def _pf_in(x, w_in_all, j, ln_g, wc, pool_w, pool_scale):
    blocks_per_tile = TM_PF // _POOL_HALO
    last_block = TOKENS // _POOL_HALO - 1
    row = lambda i: (i, 0)
    fixed = lambda i: (0, 0)
    return pl.pallas_call(
        _k_pf_in,
        out_shape=(jax.ShapeDtypeStruct((TOKENS, HALF), BF16),
                   jax.ShapeDtypeStruct((TOKENS, D_MODEL), BF16)),
        grid=(TOKENS // TM_PF,),
        in_specs=[pl.BlockSpec((TM_PF, D_MODEL), row),
                  pl.BlockSpec((_POOL_HALO, D_MODEL), lambda i: (jnp.maximum(i * blocks_per_tile - 1, 0), 0)),
                  pl.BlockSpec((_POOL_HALO, D_MODEL),
                               lambda i: (jnp.minimum((i + 1) * blocks_per_tile, last_block), 0)),
                  _resident((1, D_MODEL, D_MODEL), (j, 0, 0)),
                  pl.BlockSpec((1, HALF), fixed),
                  _resident((HALF, D_MODEL), (0, 0)),
                  pl.BlockSpec((4, POOL_DIM, POOL_DIM), lambda i: (0, 0, 0)),
                  pl.BlockSpec((1, HALF), fixed)],
        out_specs=(pl.BlockSpec((TM_PF, HALF), row),
                   pl.BlockSpec((TM_PF, D_MODEL), row)),
        scratch_shapes=[pltpu.VMEM((D_MODEL, D_MODEL), BF16),
                        pltpu.VMEM((TM_PF + 2 * _POOL_HALO, HALF), F32)],
        compiler_params=_params("arbitrary"),
        name="pf_in",
    )(x, x, x, w_in_all, ln_g, wc, pool_w, pool_scale)


_FOURIER_SCALE = 1.0 / float(np.sqrt(SEQ * FOURIER_DIM))


_DFT_SIDE = 64
_DFT_COL_CHUNK = 1024


_DFT_HALF = SEQ // 2
_REV_BLOCK = 256


def _k_fourier(tab_ref, e1_ref, e2_ref, rw_ref, sg_ref, pq_ref, fw_ref, yb_ref,
               cs_ref, ss_ref, pf_ref, qf_ref, z_ref):
    h_rows = _DFT_HALF

    @pl.when(pl.program_id(0) == 0)
    def _():
        def expand(v, e):
            hi = v.astype(BF16)
            lo = (v - hi.astype(F32)).astype(BF16)
            return _dot(hi, e) + _dot(lo, e)

        s = _DFT_SIDE
        for rb in range(h_rows // FOURIER_TR):
            rows = slice(rb * FOURIER_TR, (rb + 1) * FOURIER_TR)
            tab = tab_ref[rows, :]
            for c in range(h_rows // _DFT_COL_CHUNK):
                cols = slice(c * _DFT_COL_CHUNK, (c + 1) * _DFT_COL_CHUNK)
                ca = expand(tab[:, 0:s], e1_ref[:, cols])
                sa = expand(tab[:, s:2 * s], e1_ref[:, cols])
                cb = expand(tab[:, 2 * s:3 * s], e2_ref[:, cols])
                sb = expand(tab[:, 3 * s:4 * s], e2_ref[:, cols])
                cs_ref[rows, cols] = (ca * cb - sa * sb).astype(BF16)
                ss_ref[rows, cols] = (sa * cb + ca * sb).astype(BF16)

    def head_maps(y):
        return jnp.concatenate(
            [_dot(y[:, FOURIER_DIM * h:FOURIER_DIM * (h + 1)].astype(BF16), fw_ref[h])
             for h in range(FOURIER_HEADS)], axis=-1)

    def reversed_block(src_ref, blk):
        lo = _REV_BLOCK * (h_rows // _REV_BLOCK - 1 - blk)
        if blk == 0:
            tail = src_ref[lo:lo + _REV_BLOCK, :]
            win = jnp.concatenate([tail, jnp.zeros_like(tail)], axis=0)
        else:
            win = src_ref[lo:lo + 2 * _REV_BLOCK, :]
        return _dot(rw_ref[...], win)

    n_blocks = h_rows // _REV_BLOCK
    upper = pq_ref.at[h_rows:SEQ]
    for blk in range(n_blocks):
        rows = slice(blk * _REV_BLOCK, (blk + 1) * _REV_BLOCK)
        rev = reversed_block(upper, blk)
        low = pq_ref[rows, :].astype(F32)
        pf_ref[rows, :] = (low[:, :HALF] + rev[:, :HALF]).astype(BF16)
        qf_ref[rows, :] = (low[:, HALF:] - rev[:, HALF:]).astype(BF16)

    p_mid = pq_ref[h_rows:h_rows + 1, :HALF].astype(F32)
    for rb in range(h_rows // FOURIER_TR):
        rows = slice(rb * FOURIER_TR, (rb + 1) * FOURIER_TR)
        a = _dot(cs_ref[rows, :], pf_ref[...])
        bm = _dot(ss_ref[rows, :], qf_ref[...])
        j = lax.broadcasted_iota(I32, (FOURIER_TR, 1), 0) + rb * FOURIER_TR
        t = jnp.where((j & 1) == 0, 1.0, -1.0) * p_mid
        yb_ref[rows, :] = head_maps((a - bm + t) * _FOURIER_SCALE).astype(BF16)
        z_ref[rows, :] = head_maps((a + bm + t) * _FOURIER_SCALE).astype(BF16)

    y_mid = head_maps(_dot(sg_ref[...], pq_ref[:, :HALF]) * _FOURIER_SCALE)[0:1]
    for blk in range(n_blocks):
        up = reversed_block(z_ref, blk)
        if blk == 0:
            r = lax.broadcasted_iota(I32, (_REV_BLOCK, 1), 0)
            up = jnp.where(r == 0, y_mid, up)
        yb_ref[h_rows + blk * _REV_BLOCK:h_rows + (blk + 1) * _REV_BLOCK, :] = up.astype(BF16)


def _fourier(tables, e1, e2, rev_window, signs, pq, fw):
    h_rows = _DFT_HALF
    return pl.pallas_call(
        _k_fourier,
        out_shape=jax.ShapeDtypeStruct((TOKENS, HALF), BF16),
        grid=(BATCH,),
        in_specs=[_resident((h_rows, 4 * _DFT_SIDE), (0, 0)),
                  _resident((_DFT_SIDE, h_rows), (0, 0)),
                  _resident((_DFT_SIDE, h_rows), (0, 0)),
                  _resident((_REV_BLOCK, 2 * _REV_BLOCK), (0, 0)),
                  _resident((V7X_SUBLANES, SEQ), (0, 0)),
                  pl.BlockSpec((SEQ, D_MODEL), lambda b: (b, 0), pipeline_mode=pl.Buffered(1)),
                  pl.BlockSpec((4, FOURIER_DIM, FOURIER_DIM), lambda b: (0, 0, 0))],
        out_specs=pl.BlockSpec((SEQ, HALF), lambda b: (b, 0)),
        scratch_shapes=[pltpu.VMEM((h_rows, h_rows), BF16), pltpu.VMEM((h_rows, h_rows), BF16),
                        pltpu.VMEM((h_rows, HALF), BF16), pltpu.VMEM((h_rows, HALF), BF16),
                        pltpu.VMEM((h_rows, HALF), BF16)],
        compiler_params=_params("arbitrary"),
        name="pf_fourier",
    )(tables, e1, e2, rev_window, signs, pq, fw)


_HG_STAGE_COLS = 512


def _k_hg_in(x_ref, w_hbm, o_ref, wbf_ref, stage_ref, sem):
    @pl.when(pl.program_id(0) == 0)
    def _():
        n_chunks = HG_N_PROJ * D_MODEL // _HG_STAGE_COLS

        def chunk_copy(c):
            cols = pl.ds(c * _HG_STAGE_COLS, _HG_STAGE_COLS)
            return pltpu.make_async_copy(w_hbm.at[:, cols], stage_ref.at[c % 2], sem.at[c % 2])

        chunk_copy(0).start()
        for c in range(n_chunks):
            if c + 1 < n_chunks:
                chunk_copy(c + 1).start()
            chunk_copy(c).wait()
            wbf_ref[:, c * _HG_STAGE_COLS:(c + 1) * _HG_STAGE_COLS] = stage_ref[c % 2].astype(BF16)

    xb = x_ref[...].astype(BF16)
    for p in range(HG_N_PROJ):
        acc = _dot(xb, wbf_ref[:, p * D_MODEL:(p + 1) * D_MODEL])
        if p == 0:
            acc = _silu(acc)
        for h in range(HG_HEADS):
            o_ref[p, h] = acc[:, HG_DIM * h:HG_DIM * (h + 1)].astype(BF16)


def _hg_in(x, w_in):
    return pl.pallas_call(
        _k_hg_in,
        out_shape=jax.ShapeDtypeStruct((HG_N_PROJ, HG_HEADS, TOKENS, HG_DIM), BF16),
        grid=(TOKENS // TM,),
        in_specs=[pl.BlockSpec((TM, D_MODEL), lambda i: (i, 0)),
                  pl.BlockSpec(memory_space=pl.ANY)],
        out_specs=pl.BlockSpec((HG_N_PROJ, HG_HEADS, TM, HG_DIM), lambda i: (0, 0, i, 0)),
        scratch_shapes=[pltpu.VMEM((D_MODEL, HG_N_PROJ * D_MODEL), BF16),
                        pltpu.VMEM((2, D_MODEL, _HG_STAGE_COLS), F32),
                        pltpu.SemaphoreType.DMA((2,))],
        compiler_params=_params("arbitrary"),
        name="hg_in",
    )(x, w_in)


def _gla_prep(q, z, lbv, tri, rev):
    L = HG_CHUNK
    G = q.shape[0] // L
    f = lbv + (1.0 - lbv) * jax.nn.sigmoid(z.astype(F32))
    lf = jnp.log(f).reshape(G, L, HG_DIM)
    k3 = (1.0 - f).reshape(G, L, HG_DIM)
    q3 = q.astype(F32).reshape(G, L, HG_DIM)
    hi32 = lax.bitcast_convert_type(
        lax.bitcast_convert_type(lf, jnp.uint32) & jnp.uint32(0xFFFF0000), F32)
    lf_hi = hi32.astype(BF16)
    lf_lo = (lf - hi32).astype(BF16)
    b2 = jnp.einsum('glm,gmk->glk', jnp.broadcast_to(tri, (G, L, L)),
                    jnp.concatenate([lf_hi, lf_lo], axis=-1), preferred_element_type=F32)
    b = b2[..., :HG_DIM] + b2[..., HG_DIM:]
    mid = L // 2 if not rev else L - 1 - L // 2
    end = L - 1 if not rev else 0
    b_ref = b[:, mid:mid + 1, :]
    b_last = b[:, end:end + 1, :]
    e_pos = jnp.exp(b - b_ref)
    qt32 = q3 * e_pos
    kt32 = k3 * (1.0 / e_pos)
    ku = kt32 * jnp.exp(b_last - b_ref)
    qe = qt32 * jnp.exp(b_ref)
    dec = jnp.exp(b_last)
    return qt32.astype(BF16), kt32.astype(BF16), ku.astype(BF16), qe.astype(BF16), dec


def _gla_intra(qt, kt, ku, v, rev):
    L = HG_CHUNK
    v3 = v.reshape(qt.shape[0], L, HG_DIM)
    sc = jnp.einsum('glk,gmk->glm', qt, kt, preferred_element_type=F32)
    li = lax.broadcasted_iota(I32, (L, L), 0)
    mi = lax.broadcasted_iota(I32, (L, L), 1)
    causal = (li >= mi) if not rev else (li <= mi)
    sc = jnp.where(causal[None], sc, 0.0).astype(BF16)
    u_t = jnp.einsum('glv,glk->gvk', v3, ku, preferred_element_type=F32)
    return sc, v3, u_t


def _gla_inter(qe, dec, sc, v3, u_t, st, rev):
    G = qe.shape[0]
    outs = [None] * G
    for n in (range(G) if not rev else range(G - 1, -1, -1)):
        outs[n] = (_dot(sc[n], v3[n])
                   + lax.dot_general(qe[n], st.astype(BF16), _NT, preferred_element_type=F32))
        st = dec[n] * st + u_t[n]
    return jnp.concatenate(outs, axis=0), st


_GLA_N_OPERANDS = 5


def _k_gla(q_ref, v_ref, zf_ref, zb_ref, g_ref, lb_ref, ng_ref, lt_ref, ut_ref, o_ref, of_ref, ob_ref,
           *operand_refs):
    n_groups = SEQ // GLA_ROWS
    per_slot = 2 * _GLA_N_OPERANDS
    slots = [[operand_refs[s * per_slot + d * _GLA_N_OPERANDS:s * per_slot + (d + 1) * _GLA_N_OPERANDS]
              for d in range(2)] for s in range(2)]
    directions = ((zf_ref, lt_ref, False), (zb_ref, ut_ref, True))

    def group_rows(t, rev):
        gi = t if not rev else n_groups - 1 - t
        return slice(gi * GLA_ROWS, (gi + 1) * GLA_ROWS)

    def prep(slot, t):
        for d, (z_ref, tri_ref, rev) in enumerate(directions):
            rows = group_rows(t, rev)
            vals = _gla_prep(q_ref[0, 0, rows, :], z_ref[0, 0, rows, :], lb_ref[d, 0], tri_ref[...], rev)
            for ref, val in zip(slots[slot][d], vals):
                ref[...] = val

    def mix(slot, t, states):
        out_refs = (of_ref, ob_ref)
        intra = []
        for d, (_, _, rev) in enumerate(directions):
            qt_ref, kt_ref, ku_ref, _, _ = slots[slot][d]
            intra.append(_gla_intra(qt_ref[...], kt_ref[...], ku_ref[...],
                                    v_ref[0, 0, group_rows(t, rev), :], rev))
        new_states = []
        for d, (_, _, rev) in enumerate(directions):
            _, _, _, qe_ref, dec_ref = slots[slot][d]
            o, st = _gla_inter(qe_ref[...], dec_ref[...], *intra[d], states[d], rev)
            out_refs[d][group_rows(t, rev), :] = o
            new_states.append(st)
        return new_states

    def finish(gi):
        rows = slice(gi * GLA_ROWS, (gi + 1) * GLA_ROWS)
        o = of_ref[rows, :] + ob_ref[rows, :]
        o = o * lax.rsqrt(jnp.mean(o * o, axis=-1, keepdims=True) + LN_EPS)
        o = o * ng_ref[...] * _silu(g_ref[0, 0, rows, :].astype(F32))
        o_ref[0, rows, :] = o.astype(BF16)

    states = [jnp.zeros((HG_DIM, HG_DIM), F32)] * 2
    prep(0, 0)
    for t in range(n_groups):
        if t + 1 < n_groups:
            prep((t + 1) % 2, t + 1)
        states = mix(t % 2, t, states)
        for gi in {t, n_groups - 1 - t}:
            if max(gi, n_groups - 1 - gi) == t:
                finish(gi)


def _gla(proj, lb, norm_g, tril, triu):
    def pspec(p):
        return pl.BlockSpec((1, 1, SEQ, HG_DIM), lambda b, h: (p, h, b, 0))
    fixed = lambda b, h: (0, 0)
    return pl.pallas_call(
        _k_gla,
        out_shape=jax.ShapeDtypeStruct((HG_HEADS, TOKENS, HG_DIM), BF16),
        grid=(BATCH, HG_HEADS),
        in_specs=[pspec(0), pspec(1), pspec(2), pspec(3), pspec(4),
                  pl.BlockSpec((2, 1, 1, HG_DIM), lambda b, h: (0, h, 0, 0)),
                  pl.BlockSpec((1, HG_DIM), fixed),
                  pl.BlockSpec((HG_CHUNK, HG_CHUNK), fixed),
                  pl.BlockSpec((HG_CHUNK, HG_CHUNK), fixed)],
        out_specs=pl.BlockSpec((1, SEQ, HG_DIM), lambda b, h: (h, b, 0)),
        scratch_shapes=[pltpu.VMEM((SEQ, HG_DIM), F32), pltpu.VMEM((SEQ, HG_DIM), F32)]
        + 4 * ([pltpu.VMEM((GLA_GROUP, HG_CHUNK, HG_DIM), BF16)] * (_GLA_N_OPERANDS - 1)
               + [pltpu.VMEM((GLA_GROUP, 1, HG_DIM), F32)]),
        compiler_params=_params("arbitrary", "arbitrary"),
        name="hg_gla",
    )(proj, proj, proj, proj, proj, lb, norm_g, tril, triu)


def _k_mm(x_ref, w_ref, o_ref):
    o_ref[0] = _dot(x_ref[...].astype(BF16), w_ref[0].astype(BF16)).astype(o_ref.dtype)


def _kv_proj(mem2d, wkv_all):
    rows, cols = mem2d.shape[0], wkv_all.shape[2]
    return pl.pallas_call(
        _k_mm,
        out_shape=jax.ShapeDtypeStruct((DEPTH, rows, cols), BF16),
        grid=(DEPTH, cols // D_MODEL, rows // TM),
        in_specs=[pl.BlockSpec((TM, D_MODEL), lambda l, j, i: (i, 0)),
                  pl.BlockSpec((1, D_MODEL, D_MODEL), lambda l, j, i: (l, 0, j))],
        out_specs=pl.BlockSpec((1, TM, D_MODEL), lambda l, j, i: (l, i, j)),
        compiler_params=_params("parallel", "parallel", "parallel"),
        name="xa_kv",
    )(mem2d, wkv_all)


ROW_WORDS = D_MODEL // 2
ROW_SLABS = ROW_WORDS // V7X_LANES
U32 = jnp.uint32
_HI_MASK = 0xFFFF0000


def _to_row_tiles(ref, x):
    n = x.shape[0]
    as_bits = lambda v: lax.bitcast_convert_type(v.astype(BF16).astype(F32), U32)
    words = (as_bits(x[:, :ROW_WORDS]) >> 16) | (as_bits(x[:, ROW_WORDS:]) & U32(_HI_MASK))
    for s in range(ROW_SLABS):
        ref[pl.ds(s, n, stride=ROW_SLABS), :] = words[:, V7X_LANES * s:V7X_LANES * (s + 1)]


def _from_row_tiles(ref):
    n = ref.shape[0] // ROW_SLABS
    words = jnp.concatenate([ref[pl.ds(s, n, stride=ROW_SLABS), :] for s in range(ROW_SLABS)], axis=-1)
    lo = lax.bitcast_convert_type(words << 16, F32)
    hi = lax.bitcast_convert_type(words & U32(_HI_MASK), F32)
    return jnp.concatenate([lo, hi], axis=-1)


def _first_argmax_rows(v, n_rows):
    m = jnp.max(v, axis=0, keepdims=True)
    rows = lax.broadcasted_iota(I32, v.shape, 0)
    idx = jnp.min(jnp.where(v == m, rows, n_rows), axis=0, keepdims=True)
    return m, idx


def _route(lt):
    gl = lt[0:MOE_GROUPS, :]
    gmax, gi = _first_argmax_rows(gl, MOE_GROUPS)
    p_grp = 1.0 / jnp.sum(jnp.exp(gl - gmax), axis=0, keepdims=True)
    sel = lt[MOE_GROUPS:MOE_GROUPS + MOE_PER_GROUP, :]
    for g in range(1, MOE_GROUPS):
        lo = MOE_GROUPS + MOE_PER_GROUP * g
        sel = jnp.where(gi == g, lt[lo:lo + MOE_PER_GROUP, :], sel)
    m1, i1 = _first_argmax_rows(sel, MOE_PER_GROUP)
    rows = lax.broadcasted_iota(I32, sel.shape, 0)
    m2, i2 = _first_argmax_rows(jnp.where(rows == i1, -jnp.inf, sel), MOE_PER_GROUP)
    e2 = jnp.exp(m2 - m1)
    g1 = p_grp / (1.0 + e2)
    return (gi * MOE_PER_GROUP + i1, gi * MOE_PER_GROUP + i2), (g1, g1 * e2)


def _rank_in_expert(e0, e1, strict_upper, carry_ref, counted):
    rows = lax.broadcasted_iota(I32, (MOE_EXPERTS, e0.shape[1]), 0)
    oh0 = rows == e0
    oh1 = rows == e1
    oh = jnp.where(oh0 | oh1, 1.0, 0.0)
    before = _dot(oh.astype(BF16), strict_upper) + carry_ref[:, 0:1]
    r0 = jnp.sum(jnp.where(oh0, before, 0.0), axis=0, keepdims=True).astype(I32)
    r1 = jnp.sum(jnp.where(oh1, before, 0.0), axis=0, keepdims=True).astype(I32)
    carry_ref[...] = carry_ref[...] + counted * jnp.sum(oh, axis=1, keepdims=True)
    return r0, r1


def _k_block(heads_major, *refs):
    n_mix = 1 if heads_major else 2
    mix_refs = refs[:n_mix]
    (wm_ref, x_ref, g0_ref, b0_ref, wq_ref, kv_ref, wo_ref, g_ref, b_ref, wr_ref, br_ref, tri_ref,
     o_ref, o3_ref, e_ref, gate_ref, rank_ref, cnt_ref,
     wmbf_ref, wqbf_ref, wobf_ref, carry_ref, pre_ref) = refs[n_mix:]
    step = pl.program_id(0)
    _cast_weight_once(wm_ref, wmbf_ref)
    _cast_weight_once(wq_ref, wqbf_ref)
    _cast_weight_once(wo_ref, wobf_ref)

    @pl.when(step == 0)
    def _():
        carry_ref[...] = jnp.zeros_like(carry_ref)
        pre_ref[...] = jnp.zeros_like(pre_ref)

    quarter = TM // 4
    routed = []

    def stage2(c):
        rows = slice(c * quarter, (c + 1) * quarter)
        y = _layer_norm(pre_ref[rows, :], g_ref[...], b_ref[...])
        o_ref[rows, :] = y
        _to_row_tiles(o3_ref.at[pl.ds(c * quarter * ROW_SLABS, quarter * ROW_SLABS)], y)
        lt = lax.dot_general(wr_ref[...], y.astype(BF16), _NT, preferred_element_type=F32) + br_ref[...]
        routed.append(_route(lt))

    stage2(0)
    if heads_major:
        a = jnp.concatenate([mix_refs[0][h] for h in range(HG_HEADS)], axis=-1)
    else:
        a = jnp.concatenate([mix_refs[0][...], mix_refs[1][...]], axis=-1)
    h1 = _dot(a, wmbf_ref[...])
    stage2(1)
    x = _post_ln(x_ref[...], h1, g0_ref[...], b0_ref[...])
    q = _dot(x.astype(BF16), wqbf_ref[...]).astype(BF16)
    stage2(2)
    outs = []
    for h in range(XA_HEADS):
        cols = slice(XA_HEAD_DIM * h, XA_HEAD_DIM * (h + 1))
        kh = kv_ref[:, cols]
        vh = kv_ref[:, D_MODEL + XA_HEAD_DIM * h:D_MODEL + XA_HEAD_DIM * (h + 1)]
        s = lax.dot_general(q[:, cols], kh, _NT, preferred_element_type=F32) * (XA_HEAD_DIM ** -0.5)
        e = jnp.exp(s - jnp.max(s, axis=-1, keepdims=True))
        p = e / jnp.sum(e, axis=-1, keepdims=True)
        outs.append(_dot(p.astype(BF16), vh))
    o = jnp.concatenate(outs, axis=-1).astype(BF16)
    h2 = _dot(o, wobf_ref[...])
    stage2(3)
    e0, e1, gate0, gate1 = [jnp.concatenate([r[a_][b_] for r in routed], axis=-1)
                            for a_ in range(2) for b_ in range(2)]
    r0, r1 = _rank_in_expert(e0, e1, tri_ref[...], carry_ref, jnp.where(step > 0, 1.0, 0.0))
    e_ref[0:1, :], e_ref[1:2, :] = e0, e1
    gate_ref[0:1, :], gate_ref[1:2, :] = gate0, gate1
    rank_ref[0:1, :], rank_ref[1:2, :] = r0, r1
    cnt_ref[...] = carry_ref[...]
    pre_ref[...] = DN_ALPHA * x + h2


def _block(mix, heads_major, w_mix_all, j, x, g0, b0, wq_all, kv, wo_all, l, g1, b1, wr, br, strict_upper):
    n_tiles = TOKENS // TM
    tiles_per_seq = SEQ // TM
    cur = lambda i: jnp.minimum(i, n_tiles - 1)
    prev = lambda i: jnp.maximum(i - 1, 0)
    row_in = lambda i: (cur(i), 0)
    row_out = lambda i: (prev(i), 0)
    fixed = lambda i: (0, 0)
    tok = pl.BlockSpec((MOE_TOPK, TM), lambda i: (0, prev(i)))
    if heads_major:
        mix_specs = [pl.BlockSpec((HG_HEADS, TM, HG_DIM), lambda i: (0, cur(i), 0))]
    else:
        mix_specs = [pl.BlockSpec((TM, HALF), row_in), pl.BlockSpec((TM, HALF), row_in)]
    return pl.pallas_call(
        functools.partial(_k_block, heads_major),
        out_shape=(jax.ShapeDtypeStruct((TOKENS, D_MODEL), F32),
                   jax.ShapeDtypeStruct((TOKENS * ROW_SLABS, V7X_LANES), U32),
                   jax.ShapeDtypeStruct((MOE_TOPK, TOKENS), I32),
                   jax.ShapeDtypeStruct((MOE_TOPK, TOKENS), F32),
                   jax.ShapeDtypeStruct((MOE_TOPK, TOKENS), I32),
                   jax.ShapeDtypeStruct((MOE_EXPERTS, V7X_LANES), F32)),
        grid=(n_tiles + 1,),
        in_specs=mix_specs + [
            _resident((1, D_MODEL, D_MODEL), (j, 0, 0)),
            pl.BlockSpec((TM, D_MODEL), row_in),
            pl.BlockSpec((1, D_MODEL), fixed), pl.BlockSpec((1, D_MODEL), fixed),
            _resident((1, D_MODEL, D_MODEL), (l, 0, 0)),
            pl.BlockSpec((None, MEM_LEN, 2 * D_MODEL), lambda i: (l, cur(i) // tiles_per_seq, 0)),
            _resident((1, D_MODEL, D_MODEL), (l, 0, 0)),
            pl.BlockSpec((1, D_MODEL), fixed), pl.BlockSpec((1, D_MODEL), fixed),
            _resident((ROUTER_ROWS, D_MODEL), (0, 0)),
            _resident((ROUTER_ROWS, 1), (0, 0)),
            _resident((TM, TM), (0, 0))],
        out_specs=(pl.BlockSpec((TM, D_MODEL), row_out),
                   pl.BlockSpec((TM * ROW_SLABS, V7X_LANES), row_out),
                   tok, tok, tok,
                   pl.BlockSpec((MOE_EXPERTS, V7X_LANES), fixed)),
        scratch_shapes=[pltpu.VMEM((D_MODEL, D_MODEL), BF16), pltpu.VMEM((D_MODEL, D_MODEL), BF16),
                        pltpu.VMEM((D_MODEL, D_MODEL), BF16), pltpu.VMEM((MOE_EXPERTS, V7X_LANES), F32),
                        pltpu.VMEM((TM, D_MODEL), F32)],
        compiler_params=_params("arbitrary"),
        name="attn_block",
    )(*mix, w_mix_all, x, g0, b0, wq_all, kv, wo_all, g1, b1, wr, br, strict_upper)


def _k_pos(e_ref, rank_ref, off_ref, pos_ref):
    rows = lax.broadcasted_iota(I32, (MOE_EXPERTS, TM_ROUTE), 0)
    for k in range(MOE_TOPK):
        start = jnp.sum(jnp.where(rows == e_ref[k:k + 1, :], off_ref[...], 0), axis=0, keepdims=True)
        pos_ref[k:k + 1, :] = start + rank_ref[k:k + 1, :]


def _positions(eidx, rank, offsets):
    tok = pl.BlockSpec((MOE_TOPK, TM_ROUTE), lambda i: (0, i))
    return pl.pallas_call(
        _k_pos,
        out_shape=jax.ShapeDtypeStruct((MOE_TOPK, TOKENS), I32),
        grid=(TOKENS // TM_ROUTE,),
        in_specs=[tok, tok, pl.BlockSpec((MOE_EXPERTS, 1), lambda i: (0, 0))],
        out_specs=tok,
        compiler_params=_params("parallel"),
        name="moe_pos",
    )(eidx, rank, offsets)


_ISSUE_BATCH = 8


def _row_copy(src_ref, src_row, dst_ref, dst_row, sem):
    src = pl.ds(pl.multiple_of(src_row * ROW_SLABS, ROW_SLABS), ROW_SLABS)
    dst = pl.ds(pl.multiple_of(dst_row * ROW_SLABS, ROW_SLABS), ROW_SLABS)
    return pltpu.make_async_copy(src_ref.at[src], dst_ref.at[dst], sem)


_DISPATCH_SLOTS = 4


def _k_dispatch(pos_ref, ends_ref, x_ref, xs_ref, zero_ref, stage_ref, row_sems, load_sems, zsem):
    step = pl.program_id(0)
    n_steps = pl.num_programs(0)
    block_rows = TM_PERM * ROW_SLABS

    def load(tile):
        slot = lax.rem(tile, _DISPATCH_SLOTS)
        src = x_ref.at[pl.ds(pl.multiple_of(tile * block_rows, block_rows), block_rows)]
        return pltpu.make_async_copy(src, stage_ref.at[slot], load_sems.at[slot])

    def drain_rows(tile):
        slot = lax.rem(tile, _DISPATCH_SLOTS)
        for _ in range(MOE_TOPK):
            pltpu.make_async_copy(stage_ref.at[slot], xs_ref.at[pl.ds(0, block_rows)],
                                  row_sems.at[slot]).wait()

    @pl.when(step == 0)
    def _():
        load(0).start()
        pl.when(n_steps > 1)(load(1).start)
        zero_ref[...] = jnp.zeros_like(zero_ref)

        def last_tile(e):
            seg_start = ends_ref[e - 1] if e > 0 else 0
            start = ends_ref[e] - TE
            return ends_ref[e] > seg_start, zero_fill(start)

        def unused_tile(t):
            start = ends_ref[MOE_EXPERTS - 1] + t * TE
            return start < SORTED_ROWS, zero_fill(start)

        def zero_fill(start_row):
            rows = pl.ds(pl.multiple_of(start_row * ROW_SLABS, TE * ROW_SLABS), TE * ROW_SLABS)
            return pltpu.make_async_copy(zero_ref, xs_ref.at[rows], zsem)

        fills = [last_tile(e) for e in range(MOE_EXPERTS)] + [unused_tile(t) for t in range(MOE_EXPERTS)]
        for needed, copy in fills:
            pl.when(needed)(copy.start)
        for needed, copy in fills:
            pl.when(needed)(copy.wait)

    pl.when(step >= 2)(lambda: drain_rows(step - 2))
    pl.when(step + 2 < n_steps)(lambda: load(step + 2).start())
    load(step).wait()

    slot = lax.rem(step, _DISPATCH_SLOTS)
    base = step * TM_PERM

    def body(jb, c):
        j0 = jb * _ISSUE_BATCH
        dst = [[pos_ref[k * TOKENS + base + j0 + u] for k in range(MOE_TOPK)]
               for u in range(_ISSUE_BATCH)]
        for u in range(_ISSUE_BATCH):
            for k in range(MOE_TOPK):
                _row_copy(stage_ref.at[slot], j0 + u, xs_ref, dst[u][k],
                          row_sems.at[slot]).start(priority=k)
        return c

    lax.fori_loop(0, TM_PERM // _ISSUE_BATCH, body, 0)

    @pl.when(step == n_steps - 1)
    def _():
        pl.when(step >= 1)(lambda: drain_rows(step - 1))
        drain_rows(step)


def _dispatch(pos_flat, ends, x):
    return pl.pallas_call(
        _k_dispatch,
        out_shape=jax.ShapeDtypeStruct((SORTED_ROWS * ROW_SLABS, V7X_LANES), U32),
        grid_spec=pltpu.PrefetchScalarGridSpec(
            num_scalar_prefetch=2,
            grid=(TOKENS // TM_PERM,),
            in_specs=[pl.BlockSpec(memory_space=pl.ANY)],
            out_specs=pl.BlockSpec(memory_space=pl.ANY),
            scratch_shapes=[pltpu.VMEM((TE * ROW_SLABS, V7X_LANES), U32),
                            pltpu.VMEM((_DISPATCH_SLOTS, TM_PERM * ROW_SLABS, V7X_LANES), U32),
                            pltpu.SemaphoreType.DMA((_DISPATCH_SLOTS,)),
                            pltpu.SemaphoreType.DMA((_DISPATCH_SLOTS,)),
                            pltpu.SemaphoreType.DMA],
        ),
        compiler_params=_params("arbitrary"),
        name="moe_dispatch",
    )(pos_flat, ends, x)


def _k_experts(te_ref, na_ref, next_ref, xs_ref, wg_hbm, wu_hbm, wd_hbm, ys_ref,
               wgubf_ref, wdbf_ref, stage_g, stage_u, stage_d, sems, *, layer):
    i = pl.program_id(0)
    active = i < na_ref[0]
    expert = te_ref[i]
    new_expert = jnp.logical_or(i == 0, te_ref[jnp.maximum(i - 1, 0)] != expert)

    def weight_copies(e, slot):
        return (pltpu.make_async_copy(wg_hbm.at[layer, e], stage_g.at[slot], sems.at[slot, 0]),
                pltpu.make_async_copy(wu_hbm.at[layer, e], stage_u.at[slot], sems.at[slot, 1]),
                pltpu.make_async_copy(wd_hbm.at[layer, e], stage_d.at[slot], sems.at[slot, 2]))

    @pl.when(jnp.logical_and(active, new_expert))
    def _():
        slot = next_ref[i, 1]

        @pl.when(i == 0)
        def _():
            for c in weight_copies(expert, slot):
                c.start()

        upcoming = next_ref[i, 0]

        @pl.when(upcoming >= 0)
        def _():
            for c in weight_copies(upcoming, 1 - slot):
                c.start()

        for c in weight_copies(expert, slot):
            c.wait()
        wgubf_ref[:, :MOE_HIDDEN] = stage_g[slot].astype(BF16)
        wgubf_ref[:, MOE_HIDDEN:] = stage_u[slot].astype(BF16)
        wdbf_ref[...] = stage_d[slot].astype(BF16)

    @pl.when(active)
    def _():
        xb = _from_row_tiles(xs_ref).astype(BF16)
        gu = _dot(xb, wgubf_ref[...])
        h = _silu(gu[:, :MOE_HIDDEN]) * gu[:, MOE_HIDDEN:]
        _to_row_tiles(ys_ref, _dot(h.astype(BF16), wdbf_ref[...]))

    @pl.when(jnp.logical_not(active))
    def _():
        ys_ref[...] = jnp.zeros_like(ys_ref)


def _experts(tile_expert, n_active, tile_next, xs, wg_all, wu_all, wd_all, l):
    def xs_index(i, te, na, nxt):
        return (jnp.minimum(i, na[0] - 1), 0)

    return pl.pallas_call(
        functools.partial(_k_experts, layer=l),
        out_shape=jax.ShapeDtypeStruct((SORTED_ROWS * ROW_SLABS, V7X_LANES), U32),
        grid_spec=pltpu.PrefetchScalarGridSpec(
            num_scalar_prefetch=3,
            grid=(N_EXPERT_TILES,),
            in_specs=[pl.BlockSpec((TE * ROW_SLABS, V7X_LANES), xs_index),
                      pl.BlockSpec(memory_space=pl.ANY),
                      pl.BlockSpec(memory_space=pl.ANY),
                      pl.BlockSpec(memory_space=pl.ANY)],
            out_specs=pl.BlockSpec((TE * ROW_SLABS, V7X_LANES), lambda i, te, na, nxt: (i, 0)),
            scratch_shapes=[pltpu.VMEM((D_MODEL, 2 * MOE_HIDDEN), BF16),
                            pltpu.VMEM((MOE_HIDDEN, D_MODEL), BF16),
                            pltpu.VMEM((2, D_MODEL, MOE_HIDDEN), F32),
                            pltpu.VMEM((2, D_MODEL, MOE_HIDDEN), F32),
                            pltpu.VMEM((2, MOE_HIDDEN, D_MODEL), F32),
                            pltpu.SemaphoreType.DMA((2, 3))],
        ),
        compiler_params=_params("arbitrary"),
        name="moe_experts",
    )(tile_expert, n_active, tile_next, xs, wg_all, wu_all, wd_all)


_COMBINE_CHUNKS = 8


def _k_combine(pos_ref, ys_ref, gt_ref, x_ref, g_ref, b_ref, o_ref, buf_ref, sem):
    step = pl.program_id(0)
    last = pl.num_programs(0) - 1

    def issue_batch(base, j0, slot):
        src = [[pos_ref[k * TOKENS + base + j0 + u] for k in range(MOE_TOPK)]
               for u in range(_ISSUE_BATCH)]
        for u in range(_ISSUE_BATCH):
            for k in range(MOE_TOPK):
                _row_copy(ys_ref, src[u][k], buf_ref.at[slot, k], j0 + u,
                          sem.at[slot]).start(priority=k)

    def drain(slot):
        for k in range(MOE_TOPK):
            pltpu.make_async_copy(ys_ref.at[pl.ds(0, TM_COMB * ROW_SLABS)], buf_ref.at[slot, k],
                                  sem.at[slot]).wait()

    @pl.when(step == 0)
    def _():
        def body(jb, c):
            issue_batch(0, jb * _ISSUE_BATCH, 0)
            return c
        lax.fori_loop(0, TM_COMB // _ISSUE_BATCH, body, 0)

    slot = lax.rem(step, 2)
    drain(slot)
    next_base = jnp.minimum(step + 1, last) * TM_COMB
    chunk = TM_COMB // _COMBINE_CHUNKS
    for c in range(_COMBINE_CHUNKS):
        for jb in range(chunk // _ISSUE_BATCH):
            issue_batch(next_base, c * chunk + jb * _ISSUE_BATCH, 1 - slot)
        rows = slice(c * chunk, (c + 1) * chunk)
        tiles = pl.ds(c * chunk * ROW_SLABS, chunk * ROW_SLABS)
        h = (gt_ref[rows, 0:1] * _from_row_tiles(buf_ref.at[slot, 0, tiles])
             + gt_ref[rows, 1:2] * _from_row_tiles(buf_ref.at[slot, 1, tiles]))
        o_ref[rows, :] = _post_ln(x_ref[rows, :], h, g_ref[...], b_ref[...])

    pl.when(step == last)(lambda: drain(1 - slot))


def _combine(pos_flat, ys, gates_t, x, g, b):
    row = lambda i, pos: (i, 0)
    fixed = lambda i, pos: (0, 0)
    return pl.pallas_call(
        _k_combine,
        out_shape=jax.ShapeDtypeStruct((TOKENS, D_MODEL), F32),
        grid_spec=pltpu.PrefetchScalarGridSpec(
            num_scalar_prefetch=1,
            grid=(TOKENS // TM_COMB,),
            in_specs=[pl.BlockSpec(memory_space=pl.ANY),
                      pl.BlockSpec((TM_COMB, MOE_TOPK), row),
                      pl.BlockSpec((TM_COMB, D_MODEL), row),
                      pl.BlockSpec((1, D_MODEL), fixed), pl.BlockSpec((1, D_MODEL), fixed)],
            out_specs=pl.BlockSpec((TM_COMB, D_MODEL), row),
            scratch_shapes=[pltpu.VMEM((2, MOE_TOPK, TM_COMB * ROW_SLABS, V7X_LANES), U32),
                            pltpu.SemaphoreType.DMA((2,))],
        ),
        compiler_params=_params("arbitrary"),
        name="moe_combine",
    )(pos_flat, ys, gates_t, x, g, b)


def _router_params(w_group, b_group, w_expert, b_expert):
    pad = ROUTER_ROWS - MOE_GROUPS - MOE_EXPERTS
    wr = jnp.concatenate([w_group.T, w_expert.T, jnp.zeros((pad, D_MODEL), F32)], axis=0)
    br = jnp.concatenate([b_group, b_expert, jnp.zeros((pad,), F32)])[:, None]
    return wr.astype(BF16), br.astype(F32)


def _moe(x, x_tiles, eidx, gates, rank, cnt, w_gate_all, w_up_all, w_down_all, l, g, b):
    counts = cnt[:, 0].astype(I32)
    padded = ((counts + TE - 1) // TE) * TE
    ends = jnp.cumsum(padded)
    offsets = ends - padded
    tile_start = jnp.arange(N_EXPERT_TILES, dtype=I32) * TE
    tile_expert = jnp.sum((tile_start[:, None] >= ends[None, :]).astype(I32), axis=1)
    tile_expert = jnp.minimum(tile_expert, MOE_EXPERTS - 1)
    n_active = (ends[-1:] // TE).astype(I32)
    pos = _positions(eidx, rank, offsets[:, None])
    pos_flat = pos.reshape(-1)
    xs = _dispatch(pos_flat, ends.astype(I32), x_tiles)
    present = padded > 0
    ids = jnp.arange(MOE_EXPERTS, dtype=I32)
    later = jnp.where(present[None, :] & (ids[None, :] > ids[:, None]), ids[None, :], MOE_EXPERTS)
    next_present = jnp.min(later, axis=1)
    next_present = jnp.where(next_present >= MOE_EXPERTS, -1, next_present)
    appearance = jnp.cumsum(present.astype(I32)) - 1
    mine = tile_expert[:, None] == ids[None, :]
    lookup = lambda table: jnp.sum(jnp.where(mine, table[None, :], 0), axis=1)
    tile_next = jnp.stack([lookup(next_present), lookup(appearance % 2)], axis=1).astype(I32)
    ys = _experts(tile_expert, n_active, tile_next, xs, w_gate_all, w_up_all, w_down_all, l)
    return _combine(pos_flat, ys, gates.T, x, g, b)


def _channel_dft_matrix():
    c = np.arange(FOURIER_DIM)
    ang = 2.0 * np.pi * ((c[:, None] * c[None, :]) % FOURIER_DIM) / FOURIER_DIM
    wc = np.zeros((HALF, D_MODEL), np.float32)
    for h in range(FOURIER_HEADS):
        rows = slice(FOURIER_DIM * h, FOURIER_DIM * (h + 1))
        wc[rows, FOURIER_DIM * h:FOURIER_DIM * (h + 1)] = np.cos(ang)
        wc[rows, HALF + FOURIER_DIM * h:HALF + FOURIER_DIM * (h + 1)] = np.sin(ang)
    return jnp.asarray(wc, BF16)


def _sequence_dft_tables():
    side = _DFT_SIDE
    j = np.arange(_DFT_HALF, dtype=np.int64)[:, None]
    k = np.arange(side, dtype=np.int64)[None, :]
    ang_a = ((j * k) % side).astype(np.float64) * (2.0 * np.pi / side)
    ang_b = ((j * k) % SEQ).astype(np.float64) * (2.0 * np.pi / SEQ)
    tables = np.concatenate([np.cos(ang_a), np.sin(ang_a), np.cos(ang_b), np.sin(ang_b)], axis=1)
    col = np.arange(_DFT_HALF)
    e1 = (col[None, :] // side == np.arange(side)[:, None]).astype(np.float32)
    e2 = (col[None, :] % side == np.arange(side)[:, None]).astype(np.float32)
    r = np.arange(_REV_BLOCK)
    rev_window = (np.arange(2 * _REV_BLOCK)[None, :] == (_REV_BLOCK - r)[:, None]).astype(np.float32)
    signs = np.zeros((V7X_SUBLANES, SEQ), np.float32)
    signs[0] = 1.0 - 2.0 * (np.arange(SEQ) % 2)
    return (jnp.asarray(tables, F32), jnp.asarray(e1, BF16), jnp.asarray(e2, BF16),
            jnp.asarray(rev_window, BF16), jnp.asarray(signs, BF16))


def kernel(x, mem, pf_w_in, pf_pool_w, pf_pool_scale, pf_fourier_ln_g, pf_fourier_w, pf_w_out, hg_w_in, hg_lower_bounds, hg_norm_g, hg_w_out, xa_wq, xa_wkv, xa_wo, moe_w_group, moe_b_group, moe_w_expert, moe_b_expert, moe_w_gate, moe_w_up, moe_w_down, ln_g, ln_b):
    bf = lambda a: a.astype(BF16)
    xt = x.reshape(TOKENS, D_MODEL)
    mem2d = mem.reshape(BATCH * MEM_LEN, D_MODEL)
    lb_all = jnp.cumsum(jax.nn.softmax(hg_lower_bounds.astype(F32), axis=0), axis=0)
    lb_all = lb_all - lb_all[:1]
    tril = jnp.asarray(np.tril(np.ones((HG_CHUNK, HG_CHUNK), np.float32)), BF16)
    triu = jnp.asarray(np.triu(np.ones((HG_CHUNK, HG_CHUNK), np.float32)), BF16)
    strict_upper = jnp.asarray(np.triu(np.ones((TM, TM), np.float32), k=1), BF16)
    kv_all = _kv_proj(mem2d, xa_wkv)

    for l in range(DEPTH):
        j = l // 2
        lng = lambda s: ln_g[l, s][None, :]
        lnb = lambda s: ln_b[l, s][None, :]
        if l % 2 == 0:
            dft_consts = _sequence_dft_tables()
            ya, pq = _pf_in(xt, pf_w_in, j, pf_fourier_ln_g[j].reshape(1, HALF), _channel_dft_matrix(),
                            bf(pf_pool_w[j]), pf_pool_scale[j][None, :])
            yb = _fourier(*dft_consts, pq, bf(pf_fourier_w[j]))
            mix, heads_major, w_mix = (ya, yb), False, pf_w_out
        else:
            proj = _hg_in(xt, hg_w_in[j])
            lb = lb_all[l].reshape(2, HG_HEADS, 1, HG_DIM)
            og = _gla(proj, lb, hg_norm_g[j][None, :], tril, triu)
            mix, heads_major, w_mix = (og,), True, hg_w_out
        kv = kv_all
        wr, br = _router_params(moe_w_group[l], moe_b_group[l], moe_w_expert[l], moe_b_expert[l])
        xt, xt_tiles, eidx, gates, rank, cnt = _block(
            mix, heads_major, w_mix, j, xt, lng(0), lnb(0), xa_wq, kv, xa_wo, l, lng(1), lnb(1),
            wr, br, strict_upper)
        xt = _moe(xt, xt_tiles, eidx, gates, rank, cnt, moe_w_gate, moe_w_up, moe_w_down, l,
                  lng(2), lnb(2))
    return xt.reshape(BATCH, SEQ, D_MODEL)
```

```python
import functools

import jax
import jax.numpy as jnp
import numpy as np
from jax import lax
from jax.experimental import pallas as pl
from jax.experimental.pallas import tpu as pltpu

F32 = jnp.float32
BF16 = jnp.bfloat16
I32 = jnp.int32

D_MODEL = 1024
BATCH = 8
SEQ = 4096
DEPTH = 2
TOKENS = BATCH * SEQ
MEM_LEN = 256
HALF = D_MODEL // 2
POOL_WINDOWS = (2, 4, 8, 16)
POOL_DIM = 128
FOURIER_HEADS = 4
FOURIER_DIM = 128
HG_HEADS = 8
HG_DIM = 128
HG_N_PROJ = 5
HG_CHUNK = 64
XA_HEADS = 4
XA_HEAD_DIM = 256
MOE_GROUPS = 4
MOE_PER_GROUP = 8
MOE_EXPERTS = 32
MOE_TOPK = 2
MOE_HIDDEN = 512
DN_ALPHA = (2.0 * DEPTH) ** 0.25
LN_EPS = 1e-5

V7X_LANES = 128
V7X_SUBLANES = 8
V7X_VMEM_LIMIT_BYTES = 52 * 1024 * 1024

TM = 512
TM_ROUTE = 1024
TM_PERM = 2048
TM_COMB = 512
TE = 512
SORTED_ROWS = TOKENS * MOE_TOPK + MOE_EXPERTS * TE
N_EXPERT_TILES = SORTED_ROWS // TE
GLA_GROUP = 16
GLA_ROWS = GLA_GROUP * HG_CHUNK
FOURIER_TR = 512
ROUTER_ROWS = 40

_NT = (((1,), (1,)), ((), ()))


def _params(*sem):
    return pltpu.CompilerParams(dimension_semantics=sem, vmem_limit_bytes=V7X_VMEM_LIMIT_BYTES)


def _dot(a, b):
    return jnp.dot(a, b, preferred_element_type=F32)


def _layer_norm(y, g, b):
    mu = jnp.mean(y, axis=-1, keepdims=True)
    yc = y - mu
    var = jnp.mean(yc * yc, axis=-1, keepdims=True)
    return yc * lax.rsqrt(var + LN_EPS) * g + b


def _post_ln(x, h, g, b):
    return _layer_norm(DN_ALPHA * x + h, g, b)


def _silu(x):
    return x * jax.nn.sigmoid(x)


def _resident(shape, index):
    return pl.BlockSpec(shape, lambda *_: index, pipeline_mode=pl.Buffered(1))


def _cast_weight_once(w_ref, wbf_ref):
    @pl.when(pl.program_id(0) == 0)
    def _():
        wbf_ref[...] = w_ref[0].astype(BF16)


TM_PF = 1024
_POOL_HALO = 8


def _k_pf_in(x_ref, xp_ref, xn_ref, w_ref, lng_ref, wc_ref, pw_ref, ps_ref, ya_ref, pq_ref, wbf_ref, ext_ref):
    _cast_weight_once(w_ref, wbf_ref)
    tiles_per_seq = SEQ // TM_PF
    j = lax.rem(pl.program_id(0), tiles_per_seq)
    u = _dot(x_ref[...].astype(BF16), wbf_ref[...])
    halo = jnp.concatenate([xp_ref[...], xn_ref[...]], axis=0).astype(BF16)
    uh = _dot(halo, wbf_ref[:, :HALF])
    ext_ref[_POOL_HALO:_POOL_HALO + TM_PF, :] = u[:, :HALF]
    ext_ref[0:_POOL_HALO, :] = jnp.where(j == 0, 0.0, uh[:_POOL_HALO])
    ext_ref[_POOL_HALO + TM_PF:, :] = jnp.where(j == tiles_per_seq - 1, 0.0, uh[_POOL_HALO:])
    parts = []
    for h in range(FOURIER_HEADS):
        ub = u[:, HALF + FOURIER_DIM * h:HALF + FOURIER_DIM * (h + 1)]
        mu = jnp.mean(ub, axis=-1, keepdims=True)
        uc = ub - mu
        var = jnp.mean(uc * uc, axis=-1, keepdims=True)
        parts.append(uc * lax.rsqrt(var + LN_EPS))
    un = jnp.concatenate(parts, axis=-1) * lng_ref[...]
    pq_ref[...] = _dot(un.astype(BF16), wc_ref[...]).astype(BF16)
    pos = lax.broadcasted_iota(I32, (TM_PF, POOL_DIM), 0) + j * TM_PF
    outs = []
    for g, w in enumerate(POOL_WINDOWS):
        hw = w // 2
        cols = slice(g * POOL_DIM, (g + 1) * POOL_DIM)
        acc = ext_ref[_POOL_HALO - hw:_POOL_HALO - hw + TM_PF, cols]
        for d in range(-hw + 1, hw):
            acc = acc + ext_ref[_POOL_HALO + d:_POOL_HALO + d + TM_PF, cols]
        cnt = jnp.minimum(pos + hw, SEQ) - jnp.maximum(pos - hw, 0)
        pooled = acc / cnt.astype(F32) - ext_ref[_POOL_HALO:_POOL_HALO + TM_PF, cols]
        outs.append(_dot(pooled.astype(BF16), pw_ref[g]))
    ya_ref[...] = (jnp.concatenate(outs, axis=-1) * ps_ref[...]).astype(BF16)


def _pf_in(x, w_in_all, j, ln_g, wc, pool_w, pool_scale):
    blocks_per_tile = TM_PF // _POOL_HALO
    last_block = TOKENS // _POOL_HALO - 1
    row = lambda i: (i, 0)
    fixed = lambda i: (0, 0)
    return pl.pallas_call(
        _k_pf_in,
        out_shape=(jax.ShapeDtypeStruct((TOKENS, HALF), BF16),
                   jax.ShapeDtypeStruct((TOKENS, D_MODEL), BF16)),
        grid=(TOKENS // TM_PF,),
        in_specs=[pl.BlockSpec((TM_PF, D_MODEL), row),
                  pl.BlockSpec((_POOL_HALO, D_MODEL), lambda i: (jnp.maximum(i * blocks_per_tile - 1, 0), 0)),
                  pl.BlockSpec((_POOL_HALO, D_MODEL),
                               lambda i: (jnp.minimum((i + 1) * blocks_per_tile, last_block), 0)),
                  _resident((1, D_MODEL, D_MODEL), (j, 0, 0)),
                  pl.BlockSpec((1, HALF), fixed),
                  _resident((HALF, D_MODEL), (0, 0)),
                  pl.BlockSpec((4, POOL_DIM, POOL_DIM), lambda i: (0, 0, 0)),
                  pl.BlockSpec((1, HALF), fixed)],
        out_specs=(pl.BlockSpec((TM_PF, HALF), row),
                   pl.BlockSpec((TM_PF, D_MODEL), row)),
        scratch_shapes=[pltpu.VMEM((D_MODEL, D_MODEL), BF16),
                        pltpu.VMEM((TM_PF + 2 * _POOL_HALO, HALF), F32)],
        compiler_params=_params("arbitrary"),
        name="pf_in",
    )(x, x, x, w_in_all, ln_g, wc, pool_w, pool_scale)


_FOURIER_SCALE = 1.0 / float(np.sqrt(SEQ * FOURIER_DIM))


_DFT_SIDE = 64
_DFT_COL_CHUNK = 1024


_DFT_HALF = SEQ // 2
_REV_BLOCK = 256


def _k_fourier(tab_ref, e1_ref, e2_ref, rw_ref, sg_ref, pq_ref, fw_ref, yb_ref,
               cs_ref, ss_ref, pf_ref, qf_ref, z_ref):
    h_rows = _DFT_HALF

    @pl.when(pl.program_id(0) == 0)
    def _():
        def expand(v, e):
            hi = v.astype(BF16)
            lo = (v - hi.astype(F32)).astype(BF16)
            return _dot(hi, e) + _dot(lo, e)

        s = _DFT_SIDE
        for rb in range(h_rows // FOURIER_TR):
            rows = slice(rb * FOURIER_TR, (rb + 1) * FOURIER_TR)
            tab = tab_ref[rows, :]
            for c in range(h_rows // _DFT_COL_CHUNK):
                cols = slice(c * _DFT_COL_CHUNK, (c + 1) * _DFT_COL_CHUNK)
                ca = expand(tab[:, 0:s], e1_ref[:, cols])
                sa = expand(tab[:, s:2 * s], e1_ref[:, cols])
                cb = expand(tab[:, 2 * s:3 * s], e2_ref[:, cols])
                sb = expand(tab[:, 3 * s:4 * s], e2_ref[:, cols])
                cs_ref[rows, cols] = (ca * cb - sa * sb).astype(BF16)
                ss_ref[rows, cols] = (sa * cb + ca * sb).astype(BF16)

    def head_maps(y):
        return jnp.concatenate(
            [_dot(y[:, FOURIER_DIM * h:FOURIER_DIM * (h + 1)].astype(BF16), fw_ref[h])
             for h in range(FOURIER_HEADS)], axis=-1)

    def reversed_block(src_ref, blk):
        lo = _REV_BLOCK * (h_rows // _REV_BLOCK - 1 - blk)
        if blk == 0:
            tail = src_ref[lo:lo + _REV_BLOCK, :]
            win = jnp.concatenate([tail, jnp.zeros_like(tail)], axis=0)
        else:
            win = src_ref[lo:lo + 2 * _REV_BLOCK, :]
        return _dot(rw_ref[...], win)

    n_blocks = h_rows // _REV_BLOCK
    upper = pq_ref.at[h_rows:SEQ]
    for blk in range(n_blocks):
        rows = slice(blk * _REV_BLOCK, (blk + 1) * _REV_BLOCK)
        rev = reversed_block(upper, blk)
        low = pq_ref[rows, :].astype(F32)
        pf_ref[rows, :] = (low[:, :HALF] + rev[:, :HALF]).astype(BF16)
        qf_ref[rows, :] = (low[:, HALF:] - rev[:, HALF:]).astype(BF16)

    p_mid = pq_ref[h_rows:h_rows + 1, :HALF].astype(F32)
    for rb in range(h_rows // FOURIER_TR):
        rows = slice(rb * FOURIER_TR, (rb + 1) * FOURIER_TR)
        a = _dot(cs_ref[rows, :], pf_ref[...])
        bm = _dot(ss_ref[rows, :], qf_ref[...])
        j = lax.broadcasted_iota(I32, (FOURIER_TR, 1), 0) + rb * FOURIER_TR
        t = jnp.where((j & 1) == 0, 1.0, -1.0) * p_mid
        yb_ref[rows, :] = head_maps((a - bm + t) * _FOURIER_SCALE).astype(BF16)
        z_ref[rows, :] = head_maps((a + bm + t) * _FOURIER_SCALE).astype(BF16)

    y_mid = head_maps(_dot(sg_ref[...], pq_ref[:, :HALF]) * _FOURIER_SCALE)[0:1]
    for blk in range(n_blocks):
        up = reversed_block(z_ref, blk)
        if blk == 0:
            r = lax.broadcasted_iota(I32, (_REV_BLOCK, 1), 0)
            up = jnp.where(r == 0, y_mid, up)
        yb_ref[h_rows + blk * _REV_BLOCK:h_rows + (blk + 1) * _REV_BLOCK, :] = up.astype(BF16)


def _fourier(tables, e1, e2, rev_window, signs, pq, fw):
    h_rows = _DFT_HALF
    return pl.pallas_call(
        _k_fourier,
        out_shape=jax.ShapeDtypeStruct((TOKENS, HALF), BF16),
        grid=(BATCH,),
        in_specs=[_resident((h_rows, 4 * _DFT_SIDE), (0, 0)),
                  _resident((_DFT_SIDE, h_rows), (0, 0)),
                  _resident((_DFT_SIDE, h_rows), (0, 0)),
                  _resident((_REV_BLOCK, 2 * _REV_BLOCK), (0, 0)),
                  _resident((V7X_SUBLANES, SEQ), (0, 0)),
                  pl.BlockSpec((SEQ, D_MODEL), lambda b: (b, 0), pipeline_mode=pl.Buffered(1)),
                  pl.BlockSpec((4, FOURIER_DIM, FOURIER_DIM), lambda b: (0, 0, 0))],
        out_specs=pl.BlockSpec((SEQ, HALF), lambda b: (b, 0)),
        scratch_shapes=[pltpu.VMEM((h_rows, h_rows), BF16), pltpu.VMEM((h_rows, h_rows), BF16),
                        pltpu.VMEM((h_rows, HALF), BF16), pltpu.VMEM((h_rows, HALF), BF16),
                        pltpu.VMEM((h_rows, HALF), BF16)],
        compiler_params=_params("arbitrary"),
        name="pf_fourier",
    )(tables, e1, e2, rev_window, signs, pq, fw)


_HG_STAGE_COLS = 512


def _k_hg_in(x_ref, w_hbm, o_ref, wbf_ref, stage_ref, sem):
    @pl.when(pl.program_id(0) == 0)
    def _():
        n_chunks = HG_N_PROJ * D_MODEL // _HG_STAGE_COLS

        def chunk_copy(c):
            cols = pl.ds(c * _HG_STAGE_COLS, _HG_STAGE_COLS)
            return pltpu.make_async_copy(w_hbm.at[:, cols], stage_ref.at[c % 2], sem.at[c % 2])

        chunk_copy(0).start()
        for c in range(n_chunks):
            if c + 1 < n_chunks:
                chunk_copy(c + 1).start()
            chunk_copy(c).wait()
            wbf_ref[:, c * _HG_STAGE_COLS:(c + 1) * _HG_STAGE_COLS] = stage_ref[c % 2].astype(BF16)

    xb = x_ref[...].astype(BF16)
    for p in range(HG_N_PROJ):
        acc = _dot(xb, wbf_ref[:, p * D_MODEL:(p + 1) * D_MODEL])
        if p == 0:
            acc = _silu(acc)
        for h in range(HG_HEADS):
            o_ref[p, h] = acc[:, HG_DIM * h:HG_DIM * (h + 1)].astype(BF16)


def _hg_in(x, w_in):
    return pl.pallas_call(
        _k_hg_in,
        out_shape=jax.ShapeDtypeStruct((HG_N_PROJ, HG_HEADS, TOKENS, HG_DIM), BF16),
        grid=(TOKENS // TM,),
        in_specs=[pl.BlockSpec((TM, D_MODEL), lambda i: (i, 0)),
                  pl.BlockSpec(memory_space=pl.ANY)],
        out_specs=pl.BlockSpec((HG_N_PROJ, HG_HEADS, TM, HG_DIM), lambda i: (0, 0, i, 0)),
        scratch_shapes=[pltpu.VMEM((D_MODEL, HG_N_PROJ * D_MODEL), BF16),
                        pltpu.VMEM((2, D_MODEL, _HG_STAGE_COLS), F32),
                        pltpu.SemaphoreType.DMA((2,))],
        compiler_params=_params("arbitrary"),
        name="hg_in",
    )(x, w_in)


def _gla_prep(q, z, lbv, tri, rev):
    L = HG_CHUNK
    G = q.shape[0] // L
    f = lbv + (1.0 - lbv) * jax.nn.sigmoid(z.astype(F32))
    lf = jnp.log(f).reshape(G, L, HG_DIM)
    k3 = (1.0 - f).reshape(G, L, HG_DIM)
    q3 = q.astype(F32).reshape(G, L, HG_DIM)
    hi32 = lax.bitcast_convert_type(
        lax.bitcast_convert_type(lf, jnp.uint32) & jnp.uint32(0xFFFF0000), F32)
    lf_hi = hi32.astype(BF16)
    lf_lo = (lf - hi32).astype(BF16)
    b2 = jnp.einsum('glm,gmk->glk', jnp.broadcast_to(tri, (G, L, L)),
                    jnp.concatenate([lf_hi, lf_lo], axis=-1), preferred_element_type=F32)
    b = b2[..., :HG_DIM] + b2[..., HG_DIM:]
    mid = L // 2 if not rev else L - 1 - L // 2
    end = L - 1 if not rev else 0
    b_ref = b[:, mid:mid + 1, :]
    b_last = b[:, end:end + 1, :]
    e_pos = jnp.exp(b - b_ref)
    qt32 = q3 * e_pos
    kt32 = k3 * (1.0 / e_pos)
    ku = kt32 * jnp.exp(b_last - b_ref)
    qe = qt32 * jnp.exp(b_ref)
    dec = jnp.exp(b_last)
    return qt32.astype(BF16), kt32.astype(BF16), ku.astype(BF16), qe.astype(BF16), dec


def _gla_intra(qt, kt, ku, v, rev):
    L = HG_CHUNK
    v3 = v.reshape(qt.shape[0], L, HG_DIM)
    sc = jnp.einsum('glk,gmk->glm', qt, kt, preferred_element_type=F32)
    li = lax.broadcasted_iota(I32, (L, L), 0)
    mi = lax.broadcasted_iota(I32, (L, L), 1)
    causal = (li >= mi) if not rev else (li <= mi)
    sc = jnp.where(causal[None], sc, 0.0).astype(BF16)
    u_t = jnp.einsum('glv,glk->gvk', v3, ku, preferred_element_type=F32)
    return sc, v3, u_t


def _gla_inter(qe, dec, sc, v3, u_t, st, rev):
    G = qe.shape[0]
    outs = [None] * G
    for n in (range(G) if not rev else range(G - 1, -1, -1)):
        outs[n] = (_dot(sc[n], v3[n])
                   + lax.dot_general(qe[n], st.astype(BF16), _NT, preferred_element_type=F32))
        st = dec[n] * st + u_t[n]
    return jnp.concatenate(outs, axis=0), st


_GLA_N_OPERANDS = 5


def _k_gla(q_ref, v_ref, zf_ref, zb_ref, g_ref, lb_ref, ng_ref, lt_ref, ut_ref, o_ref, of_ref, ob_ref,
           *operand_refs):
    n_groups = SEQ // GLA_ROWS
    per_slot = 2 * _GLA_N_OPERANDS
    slots = [[operand_refs[s * per_slot + d * _GLA_N_OPERANDS:s * per_slot + (d + 1) * _GLA_N_OPERANDS]
              for d in range(2)] for s in range(2)]
    directions = ((zf_ref, lt_ref, False), (zb_ref, ut_ref, True))

    def group_rows(t, rev):
        gi = t if not rev else n_groups - 1 - t
        return slice(gi * GLA_ROWS, (gi + 1) * GLA_ROWS)

    def prep(slot, t):
        for d, (z_ref, tri_ref, rev) in enumerate(directions):
            rows = group_rows(t, rev)
            vals = _gla_prep(q_ref[0, 0, rows, :], z_ref[0, 0, rows, :], lb_ref[d, 0], tri_ref[...], rev)
            for ref, val in zip(slots[slot][d], vals):
                ref[...] = val

    def mix(slot, t, states):
        out_refs = (of_ref, ob_ref)
        intra = []
        for d, (_, _, rev) in enumerate(directions):
            qt_ref, kt_ref, ku_ref, _, _ = slots[slot][d]
            intra.append(_gla_intra(qt_ref[...], kt_ref[...], ku_ref[...],
                                    v_ref[0, 0, group_rows(t, rev), :], rev))
        new_states = []
        for d, (_, _, rev) in enumerate(directions):
            _, _, _, qe_ref, dec_ref = slots[slot][d]
            o, st = _gla_inter(qe_ref[...], dec_ref[...], *intra[d], states[d], rev)
            out_refs[d][group_rows(t, rev), :] = o
            new_states.append(st)
        return new_states

    def finish(gi):
        rows = slice(gi * GLA_ROWS, (gi + 1) * GLA_ROWS)
        o = of_ref[rows, :] + ob_ref[rows, :]
        o = o * lax.rsqrt(jnp.mean(o * o, axis=-1, keepdims=True) + LN_EPS)
        o = o * ng_ref[...] * _silu(g_ref[0, 0, rows, :].astype(F32))
        o_ref[0, rows, :] = o.astype(BF16)

    states = [jnp.zeros((HG_DIM, HG_DIM), F32)] * 2
    prep(0, 0)
    for t in range(n_groups):
        if t + 1 < n_groups:
            prep((t + 1) % 2, t + 1)
        states = mix(t % 2, t, states)
        for gi in {t, n_groups - 1 - t}:
            if max(gi, n_groups - 1 - gi) == t:
                finish(gi)


def _gla(proj, lb, norm_g, tril, triu):
    def pspec(p):
        return pl.BlockSpec((1, 1, SEQ, HG_DIM), lambda b, h: (p, h, b, 0))
    fixed = lambda b, h: (0, 0)
    return pl.pallas_call(
        _k_gla,
        out_shape=jax.ShapeDtypeStruct((HG_HEADS, TOKENS, HG_DIM), BF16),
        grid=(BATCH, HG_HEADS),
        in_specs=[pspec(0), pspec(1), pspec(2), pspec(3), pspec(4),
                  pl.BlockSpec((2, 1, 1, HG_DIM), lambda b, h: (0, h, 0, 0)),
                  pl.BlockSpec((1, HG_DIM), fixed),
                  pl.BlockSpec((HG_CHUNK, HG_CHUNK), fixed),
                  pl.BlockSpec((HG_CHUNK, HG_CHUNK), fixed)],
        out_specs=pl.BlockSpec((1, SEQ, HG_DIM), lambda b, h: (h, b, 0)),
        scratch_shapes=[pltpu.VMEM((SEQ, HG_DIM), F32), pltpu.VMEM((SEQ, HG_DIM), F32)]
        + 4 * ([pltpu.VMEM((GLA_GROUP, HG_CHUNK, HG_DIM), BF16)] * (_GLA_N_OPERANDS - 1)
               + [pltpu.VMEM((GLA_GROUP, 1, HG_DIM), F32)]),
        compiler_params=_params("parallel", "parallel"),
        name="hg_gla",
    )(proj, proj, proj, proj, proj, lb, norm_g, tril, triu)


def _k_mm(x_ref, w_ref, o_ref):
    o_ref[0] = _dot(x_ref[...].astype(BF16), w_ref[0].astype(BF16)).astype(o_ref.dtype)


def _kv_proj(mem2d, wkv_all):
    rows, cols = mem2d.shape[0], wkv_all.shape[2]
    return pl.pallas_call(
        _k_mm,
        out_shape=jax.ShapeDtypeStruct((DEPTH, rows, cols), BF16),
        grid=(DEPTH, cols // D_MODEL, rows // TM),
        in_specs=[pl.BlockSpec((TM, D_MODEL), lambda l, j, i: (i, 0)),
                  pl.BlockSpec((1, D_MODEL, D_MODEL), lambda l, j, i: (l, 0, j))],
        out_specs=pl.BlockSpec((1, TM, D_MODEL), lambda l, j, i: (l, i, j)),
        compiler_params=_params("parallel", "parallel", "parallel"),
        name="xa_kv",
    )(mem2d, wkv_all)


ROW_WORDS = D_MODEL // 2
ROW_SLABS = ROW_WORDS // V7X_LANES
U32 = jnp.uint32
_HI_MASK = 0xFFFF0000


def _to_row_tiles(ref, x):
    n = x.shape[0]
    as_bits = lambda v: lax.bitcast_convert_type(v.astype(BF16).astype(F32), U32)
    words = (as_bits(x[:, :ROW_WORDS]) >> 16) | (as_bits(x[:, ROW_WORDS:]) & U32(_HI_MASK))
    for s in range(ROW_SLABS):
        ref[pl.ds(s, n, stride=ROW_SLABS), :] = words[:, V7X_LANES * s:V7X_LANES * (s + 1)]


def _from_row_tiles(ref):
    n = ref.shape[0] // ROW_SLABS
    words = jnp.concatenate([ref[pl.ds(s, n, stride=ROW_SLABS), :] for s in range(ROW_SLABS)], axis=-1)
    lo = lax.bitcast_convert_type(words << 16, F32)
    hi = lax.bitcast_convert_type(words & U32(_HI_MASK), F32)
    return jnp.concatenate([lo, hi], axis=-1)


def _first_argmax_rows(v, n_rows):
    m = jnp.max(v, axis=0, keepdims=True)
    rows = lax.broadcasted_iota(I32, v.shape, 0)
    idx = jnp.min(jnp.where(v == m, rows, n_rows), axis=0, keepdims=True)
    return m, idx


def _route(lt):
    gl = lt[0:MOE_GROUPS, :]
    gmax, gi = _first_argmax_rows(gl, MOE_GROUPS)
    p_grp = 1.0 / jnp.sum(jnp.exp(gl - gmax), axis=0, keepdims=True)
    sel = lt[MOE_GROUPS:MOE_GROUPS + MOE_PER_GROUP, :]
    for g in range(1, MOE_GROUPS):
        lo = MOE_GROUPS + MOE_PER_GROUP * g
        sel = jnp.where(gi == g, lt[lo:lo + MOE_PER_GROUP, :], sel)
    m1, i1 = _first_argmax_rows(sel, MOE_PER_GROUP)
    rows = lax.broadcasted_iota(I32, sel.shape, 0)
    m2, i2 = _first_argmax_rows(jnp.where(rows == i1, -jnp.inf, sel), MOE_PER_GROUP)
    e2 = jnp.exp(m2 - m1)
    g1 = p_grp / (1.0 + e2)
    return (gi * MOE_PER_GROUP + i1, gi * MOE_PER_GROUP + i2), (g1, g1 * e2)


def _rank_in_expert(e0, e1, strict_upper, carry_ref, counted):
    rows = lax.broadcasted_iota(I32, (MOE_EXPERTS, e0.shape[1]), 0)
    oh0 = rows == e0
    oh1 = rows == e1
    oh = jnp.where(oh0 | oh1, 1.0, 0.0)
    before = _dot(oh.astype(BF16), strict_upper) + carry_ref[:, 0:1]
    r0 = jnp.sum(jnp.where(oh0, before, 0.0), axis=0, keepdims=True).astype(I32)
    r1 = jnp.sum(jnp.where(oh1, before, 0.0), axis=0, keepdims=True).astype(I32)
    carry_ref[...] = carry_ref[...] + counted * jnp.sum(oh, axis=1, keepdims=True)
    return r0, r1


def _k_block(heads_major, *refs):
    n_mix = 1 if heads_major else 2
    mix_refs = refs[:n_mix]
    (wm_ref, x_ref, g0_ref, b0_ref, wq_ref, kv_ref, wo_ref, g_ref, b_ref, wr_ref, br_ref, tri_ref,
     o_ref, o3_ref, e_ref, gate_ref, rank_ref, cnt_ref,
     wmbf_ref, wqbf_ref, wobf_ref, carry_ref, pre_ref) = refs[n_mix:]
    step = pl.program_id(0)
    _cast_weight_once(wm_ref, wmbf_ref)
    _cast_weight_once(wq_ref, wqbf_ref)
    _cast_weight_once(wo_ref, wobf_ref)

    @pl.when(step == 0)
    def _():
        carry_ref[...] = jnp.zeros_like(carry_ref)
        pre_ref[...] = jnp.zeros_like(pre_ref)

    quarter = TM // 4
    routed = []

    def stage2(c):
        rows = slice(c * quarter, (c + 1) * quarter)
        y = _layer_norm(pre_ref[rows, :], g_ref[...], b_ref[...])
        o_ref[rows, :] = y
        _to_row_tiles(o3_ref.at[pl.ds(c * quarter * ROW_SLABS, quarter * ROW_SLABS)], y)
        lt = lax.dot_general(wr_ref[...], y.astype(BF16), _NT, preferred_element_type=F32) + br_ref[...]
        routed.append(_route(lt))

    stage2(0)
    if heads_major:
        a = jnp.concatenate([mix_refs[0][h] for h in range(HG_HEADS)], axis=-1)
    else:
        a = jnp.concatenate([mix_refs[0][...], mix_refs[1][...]], axis=-1)
    h1 = _dot(a, wmbf_ref[...])
    stage2(1)
    x = _post_ln(x_ref[...], h1, g0_ref[...], b0_ref[...])
    q = _dot(x.astype(BF16), wqbf_ref[...]).astype(BF16)
    stage2(2)
    outs = []
    for h in range(XA_HEADS):
        cols = slice(XA_HEAD_DIM * h, XA_HEAD_DIM * (h + 1))
        kh = kv_ref[:, cols]
        vh = kv_ref[:, D_MODEL + XA_HEAD_DIM * h:D_MODEL + XA_HEAD_DIM * (h + 1)]
        s = lax.dot_general(q[:, cols], kh, _NT, preferred_element_type=F32) * (XA_HEAD_DIM ** -0.5)
        e = jnp.exp(s - jnp.max(s, axis=-1, keepdims=True))
        p = e / jnp.sum(e, axis=-1, keepdims=True)
        outs.append(_dot(p.astype(BF16), vh))
    o = jnp.concatenate(outs, axis=-1).astype(BF16)
    h2 = _dot(o, wobf_ref[...])
    stage2(3)
    e0, e1, gate0, gate1 = [jnp.concatenate([r[a_][b_] for r in routed], axis=-1)
                            for a_ in range(2) for b_ in range(2)]
    r0, r1 = _rank_in_expert(e0, e1, tri_ref[...], carry_ref, jnp.where(step > 0, 1.0, 0.0))
    e_ref[0:1, :], e_ref[1:2, :] = e0, e1
    gate_ref[0:1, :], gate_ref[1:2, :] = gate0, gate1
    rank_ref[0:1, :], rank_ref[1:2, :] = r0, r1
    cnt_ref[...] = carry_ref[...]
    pre_ref[...] = DN_ALPHA * x + h2


def _block(mix, heads_major, w_mix_all, j, x, g0, b0, wq_all, kv, wo_all, l, g1, b1, wr, br, strict_upper):
    n_tiles = TOKENS // TM
    tiles_per_seq = SEQ // TM
    cur = lambda i: jnp.minimum(i, n_tiles - 1)
    prev = lambda i: jnp.maximum(i - 1, 0)
    row_in = lambda i: (cur(i), 0)
    row_out = lambda i: (prev(i), 0)
    fixed = lambda i: (0, 0)
    tok = pl.BlockSpec((MOE_TOPK, TM), lambda i: (0, prev(i)))
    if heads_major:
        mix_specs = [pl.BlockSpec((HG_HEADS, TM, HG_DIM), lambda i: (0, cur(i), 0))]
    else:
        mix_specs = [pl.BlockSpec((TM, HALF), row_in), pl.BlockSpec((TM, HALF), row_in)]
    return pl.pallas_call(
        functools.partial(_k_block, heads_major),
        out_shape=(jax.ShapeDtypeStruct((TOKENS, D_MODEL), F32),
                   jax.ShapeDtypeStruct((TOKENS * ROW_SLABS, V7X_LANES), U32),
                   jax.ShapeDtypeStruct((MOE_TOPK, TOKENS), I32),
                   jax.ShapeDtypeStruct((MOE_TOPK, TOKENS), F32),
                   jax.ShapeDtypeStruct((MOE_TOPK, TOKENS), I32),
                   jax.ShapeDtypeStruct((MOE_EXPERTS, V7X_LANES), F32)),
        grid=(n_tiles + 1,),
        in_specs=mix_specs + [
            _resident((1, D_MODEL, D_MODEL), (j, 0, 0)),
            pl.BlockSpec((TM, D_MODEL), row_in),
            pl.BlockSpec((1, D_MODEL), fixed), pl.BlockSpec((1, D_MODEL), fixed),
            _resident((1, D_MODEL, D_MODEL), (l, 0, 0)),
            pl.BlockSpec((None, MEM_LEN, 2 * D_MODEL), lambda i: (l, cur(i) // tiles_per_seq, 0)),
            _resident((1, D_MODEL, D_MODEL), (l, 0, 0)),
            pl.BlockSpec((1, D_MODEL), fixed), pl.BlockSpec((1, D_MODEL), fixed),
            _resident((ROUTER_ROWS, D_MODEL), (0, 0)),
            _resident((ROUTER_ROWS, 1), (0, 0)),
            _resident((TM, TM), (0, 0))],
        out_specs=(pl.BlockSpec((TM, D_MODEL), row_out),
                   pl.BlockSpec((TM * ROW_SLABS, V7X_LANES), row_out),
                   tok, tok, tok,
                   pl.BlockSpec((MOE_EXPERTS, V7X_LANES), fixed)),
        scratch_shapes=[pltpu.VMEM((D_MODEL, D_MODEL), BF16), pltpu.VMEM((D_MODEL, D_MODEL), BF16),
                        pltpu.VMEM((D_MODEL, D_MODEL), BF16), pltpu.VMEM((MOE_EXPERTS, V7X_LANES), F32),
                        pltpu.VMEM((TM, D_MODEL), F32)],
        compiler_params=_params("arbitrary"),
        name="attn_block",
    )(*mix, w_mix_all, x, g0, b0, wq_all, kv, wo_all, g1, b1, wr, br, strict_upper)


def _k_pos(e_ref, rank_ref, off_ref, pos_ref):
    rows = lax.broadcasted_iota(I32, (MOE_EXPERTS, TM_ROUTE), 0)
    for k in range(MOE_TOPK):
        start = jnp.sum(jnp.where(rows == e_ref[k:k + 1, :], off_ref[...], 0), axis=0, keepdims=True)
        pos_ref[k:k + 1, :] = start + rank_ref[k:k + 1, :]


def _positions(eidx, rank, offsets):
    tok = pl.BlockSpec((MOE_TOPK, TM_ROUTE), lambda i: (0, i))
    return pl.pallas_call(
        _k_pos,
        out_shape=jax.ShapeDtypeStruct((MOE_TOPK, TOKENS), I32),
        grid=(TOKENS // TM_ROUTE,),
        in_specs=[tok, tok, pl.BlockSpec((MOE_EXPERTS, 1), lambda i: (0, 0))],
        out_specs=tok,
        compiler_params=_params("parallel"),
        name="moe_pos",
    )(eidx, rank, offsets)


_ISSUE_BATCH = 8


def _row_copy(src_ref, src_row, dst_ref, dst_row, sem):
    src = pl.ds(pl.multiple_of(src_row * ROW_SLABS, ROW_SLABS), ROW_SLABS)
    dst = pl.ds(pl.multiple_of(dst_row * ROW_SLABS, ROW_SLABS), ROW_SLABS)
    return pltpu.make_async_copy(src_ref.at[src], dst_ref.at[dst], sem)


_DISPATCH_SLOTS = 4


def _k_dispatch(pos_ref, ends_ref, x_ref, xs_ref, zero_ref, stage_ref, row_sems, load_sems, zsem):
    step = pl.program_id(0)
    n_steps = pl.num_programs(0)
    block_rows = TM_PERM * ROW_SLABS

    def load(tile):
        slot = lax.rem(tile, _DISPATCH_SLOTS)
        src = x_ref.at[pl.ds(pl.multiple_of(tile * block_rows, block_rows), block_rows)]
        return pltpu.make_async_copy(src, stage_ref.at[slot], load_sems.at[slot])

    def drain_rows(tile):
        slot = lax.rem(tile, _DISPATCH_SLOTS)
        for _ in range(MOE_TOPK):
            pltpu.make_async_copy(stage_ref.at[slot], xs_ref.at[pl.ds(0, block_rows)],
                                  row_sems.at[slot]).wait()

    @pl.when(step == 0)
    def _():
        load(0).start()
        pl.when(n_steps > 1)(load(1).start)
        zero_ref[...] = jnp.zeros_like(zero_ref)

        def last_tile(e):
            seg_start = ends_ref[e - 1] if e > 0 else 0
            start = ends_ref[e] - TE
            return ends_ref[e] > seg_start, zero_fill(start)

        def unused_tile(t):
            start = ends_ref[MOE_EXPERTS - 1] + t * TE
            return start < SORTED_ROWS, zero_fill(start)

        def zero_fill(start_row):
            rows = pl.ds(pl.multiple_of(start_row * ROW_SLABS, TE * ROW_SLABS), TE * ROW_SLABS)
            return pltpu.make_async_copy(zero_ref, xs_ref.at[rows], zsem)

        fills = [last_tile(e) for e in range(MOE_EXPERTS)] + [unused_tile(t) for t in range(MOE_EXPERTS)]
        for needed, copy in fills:
            pl.when(needed)(copy.start)
        for needed, copy in fills:
            pl.when(needed)(copy.wait)

    pl.when(step >= 2)(lambda: drain_rows(step - 2))
    pl.when(step + 2 < n_steps)(lambda: load(step + 2).start())
    load(step).wait()

    slot = lax.rem(step, _DISPATCH_SLOTS)
    base = step * TM_PERM

    def body(jb, c):
        j0 = jb * _ISSUE_BATCH
        dst = [[pos_ref[k * TOKENS + base + j0 + u] for k in range(MOE_TOPK)]
               for u in range(_ISSUE_BATCH)]
        for u in range(_ISSUE_BATCH):
            for k in range(MOE_TOPK):
                _row_copy(stage_ref.at[slot], j0 + u, xs_ref, dst[u][k],
                          row_sems.at[slot]).start(priority=k)
        return c

    lax.fori_loop(0, TM_PERM // _ISSUE_BATCH, body, 0)

    @pl.when(step == n_steps - 1)
    def _():
        pl.when(step >= 1)(lambda: drain_rows(step - 1))
        drain_rows(step)


def _dispatch(pos_flat, ends, x):
    return pl.pallas_call(
        _k_dispatch,
        out_shape=jax.ShapeDtypeStruct((SORTED_ROWS * ROW_SLABS, V7X_LANES), U32),
        grid_spec=pltpu.PrefetchScalarGridSpec(
            num_scalar_prefetch=2,
            grid=(TOKENS // TM_PERM,),
            in_specs=[pl.BlockSpec(memory_space=pl.ANY)],
            out_specs=pl.BlockSpec(memory_space=pl.ANY),
            scratch_shapes=[pltpu.VMEM((TE * ROW_SLABS, V7X_LANES), U32),
                            pltpu.VMEM((_DISPATCH_SLOTS, TM_PERM * ROW_SLABS, V7X_LANES), U32),
                            pltpu.SemaphoreType.DMA((_DISPATCH_SLOTS,)),
                            pltpu.SemaphoreType.DMA((_DISPATCH_SLOTS,)),
                            pltpu.SemaphoreType.DMA],
        ),
        compiler_params=_params("arbitrary"),
        name="moe_dispatch",
    )(pos_flat, ends, x)


def _k_experts(te_ref, na_ref, next_ref, xs_ref, wg_hbm, wu_hbm, wd_hbm, ys_ref,
               wgubf_ref, wdbf_ref, stage_g, stage_u, stage_d, sems, *, layer):
    i = pl.program_id(0)
    active = i < na_ref[0]
    expert = te_ref[i]
    new_expert = jnp.logical_or(i == 0, te_ref[jnp.maximum(i - 1, 0)] != expert)

    def weight_copies(e, slot):
        return (pltpu.make_async_copy(wg_hbm.at[layer, e], stage_g.at[slot], sems.at[slot, 0]),
                pltpu.make_async_copy(wu_hbm.at[layer, e], stage_u.at[slot], sems.at[slot, 1]),
                pltpu.make_async_copy(wd_hbm.at[layer, e], stage_d.at[slot], sems.at[slot, 2]))

    @pl.when(jnp.logical_and(active, new_expert))
    def _():
        slot = next_ref[i, 1]

        @pl.when(i == 0)
        def _():
            for c in weight_copies(expert, slot):
                c.start()

        upcoming = next_ref[i, 0]

        @pl.when(upcoming >= 0)
        def _():
            for c in weight_copies(upcoming, 1 - slot):
                c.start()

        for c in weight_copies(expert, slot):
            c.wait()
        wgubf_ref[:, :MOE_HIDDEN] = stage_g[slot].astype(BF16)
        wgubf_ref[:, MOE_HIDDEN:] = stage_u[slot].astype(BF16)
        wdbf_ref[...] = stage_d[slot].astype(BF16)

    @pl.when(active)
    def _():
        xb = _from_row_tiles(xs_ref).astype(BF16)
        gu = _dot(xb, wgubf_ref[...])
        h = _silu(gu[:, :MOE_HIDDEN]) * gu[:, MOE_HIDDEN:]
        _to_row_tiles(ys_ref, _dot(h.astype(BF16), wdbf_ref[...]))

    @pl.when(jnp.logical_not(active))
    def _():
        ys_ref[...] = jnp.zeros_like(ys_ref)


def _experts(tile_expert, n_active, tile_next, xs, wg_all, wu_all, wd_all, l):
    def xs_index(i, te, na, nxt):
        return (jnp.minimum(i, na[0] - 1), 0)

    return pl.pallas_call(
        functools.partial(_k_experts, layer=l),
        out_shape=jax.ShapeDtypeStruct((SORTED_ROWS * ROW_SLABS, V7X_LANES), U32),
        grid_spec=pltpu.PrefetchScalarGridSpec(
            num_scalar_prefetch=3,
            grid=(N_EXPERT_TILES,),
            in_specs=[pl.BlockSpec((TE * ROW_SLABS, V7X_LANES), xs_index),
                      pl.BlockSpec(memory_space=pl.ANY),
                      pl.BlockSpec(memory_space=pl.ANY),
                      pl.BlockSpec(memory_space=pl.ANY)],
            out_specs=pl.BlockSpec((TE * ROW_SLABS, V7X_LANES), lambda i, te, na, nxt: (i, 0)),
            scratch_shapes=[pltpu.VMEM((D_MODEL, 2 * MOE_HIDDEN), BF16),
                            pltpu.VMEM((MOE_HIDDEN, D_MODEL), BF16),
                            pltpu.VMEM((2, D_MODEL, MOE_HIDDEN), F32),
                            pltpu.VMEM((2, D_MODEL, MOE_HIDDEN), F32),
                            pltpu.VMEM((2, MOE_HIDDEN, D_MODEL), F32),
                            pltpu.SemaphoreType.DMA((2, 3))],
        ),
        compiler_params=_params("arbitrary"),
        name="moe_experts",
    )(tile_expert, n_active, tile_next, xs, wg_all, wu_all, wd_all)


_COMBINE_CHUNKS = 8
_COMBINE_SLOTS = 3


def _k_combine(pos_ref, ys_ref, gt_ref, x_ref, g_ref, b_ref, o_ref, buf_ref, sem):
    step = pl.program_id(0)
    last = pl.num_programs(0) - 1

    def issue_batch(base, j0, slot):
        src = [[pos_ref[k * TOKENS + base + j0 + u] for k in range(MOE_TOPK)]
               for u in range(_ISSUE_BATCH)]
        for u in range(_ISSUE_BATCH):
            for k in range(MOE_TOPK):
                _row_copy(ys_ref, src[u][k], buf_ref.at[slot, k], j0 + u,
                          sem.at[slot]).start(priority=k)

    def drain(slot):
        for k in range(MOE_TOPK):
            pltpu.make_async_copy(ys_ref.at[pl.ds(0, TM_COMB * ROW_SLABS)], buf_ref.at[slot, k],
                                  sem.at[slot]).wait()

    @pl.when(step == 0)
    def _():
        for tile in range(_COMBINE_SLOTS - 1):
            def body(jb, c, tile=tile):
                issue_batch(jnp.minimum(tile, last) * TM_COMB, jb * _ISSUE_BATCH, tile)
                return c
            lax.fori_loop(0, TM_COMB // _ISSUE_BATCH, body, 0)

    slot = lax.rem(step, _COMBINE_SLOTS)
    ahead_slot = lax.rem(step + _COMBINE_SLOTS - 1, _COMBINE_SLOTS)
    drain(slot)
    next_base = jnp.minimum(step + _COMBINE_SLOTS - 1, last) * TM_COMB
    chunk = TM_COMB // _COMBINE_CHUNKS
    for c in range(_COMBINE_CHUNKS):
        for jb in range(chunk // _ISSUE_BATCH):
            issue_batch(next_base, c * chunk + jb * _ISSUE_BATCH, ahead_slot)
        rows = slice(c * chunk, (c + 1) * chunk)
        tiles = pl.ds(c * chunk * ROW_SLABS, chunk * ROW_SLABS)
        h = (gt_ref[rows, 0:1] * _from_row_tiles(buf_ref.at[slot, 0, tiles])
             + gt_ref[rows, 1:2] * _from_row_tiles(buf_ref.at[slot, 1, tiles]))
        o_ref[rows, :] = _post_ln(x_ref[rows, :], h, g_ref[...], b_ref[...])

    @pl.when(step == last)
    def _():
        for ahead in range(1, _COMBINE_SLOTS):
            drain(lax.rem(step + ahead, _COMBINE_SLOTS))


def _combine(pos_flat, ys, gates_t, x, g, b):
    row = lambda i, pos: (i, 0)
    fixed = lambda i, pos: (0, 0)
    return pl.pallas_call(
        _k_combine,
        out_shape=jax.ShapeDtypeStruct((TOKENS, D_MODEL), F32),
        grid_spec=pltpu.PrefetchScalarGridSpec(
            num_scalar_prefetch=1,
            grid=(TOKENS // TM_COMB,),
            in_specs=[pl.BlockSpec(memory_space=pl.ANY),
                      pl.BlockSpec((TM_COMB, MOE_TOPK), row),
                      pl.BlockSpec((TM_COMB, D_MODEL), row),
                      pl.BlockSpec((1, D_MODEL), fixed), pl.BlockSpec((1, D_MODEL), fixed)],
            out_specs=pl.BlockSpec((TM_COMB, D_MODEL), row),
            scratch_shapes=[pltpu.VMEM((_COMBINE_SLOTS, MOE_TOPK, TM_COMB * ROW_SLABS, V7X_LANES), U32),
                            pltpu.SemaphoreType.DMA((_COMBINE_SLOTS,))],
        ),
        compiler_params=_params("arbitrary"),
        name="moe_combine",
    )(pos_flat, ys, gates_t, x, g, b)


def _router_params(w_group, b_group, w_expert, b_expert):
    pad = ROUTER_ROWS - MOE_GROUPS - MOE_EXPERTS
    wr = jnp.concatenate([w_group.T, w_expert.T, jnp.zeros((pad, D_MODEL), F32)], axis=0)
    br = jnp.concatenate([b_group, b_expert, jnp.zeros((pad,), F32)])[:, None]
    return wr.astype(BF16), br.astype(F32)


def _moe(x, x_tiles, eidx, gates, rank, cnt, w_gate_all, w_up_all, w_down_all, l, g, b):
    counts = cnt[:, 0].astype(I32)
    padded = ((counts + TE - 1) // TE) * TE
    ends = jnp.cumsum(padded)
    offsets = ends - padded
    tile_start = jnp.arange(N_EXPERT_TILES, dtype=I32) * TE
    tile_expert = jnp.sum((tile_start[:, None] >= ends[None, :]).astype(I32), axis=1)
    tile_expert = jnp.minimum(tile_expert, MOE_EXPERTS - 1)
    n_active = (ends[-1:] // TE).astype(I32)
    pos = _positions(eidx, rank, offsets[:, None])
    pos_flat = pos.reshape(-1)
    xs = _dispatch(pos_flat, ends.astype(I32), x_tiles)
    present = padded > 0
    ids = jnp.arange(MOE_EXPERTS, dtype=I32)
    later = jnp.where(present[None, :] & (ids[None, :] > ids[:, None]), ids[None, :], MOE_EXPERTS)
    next_present = jnp.min(later, axis=1)
    next_present = jnp.where(next_present >= MOE_EXPERTS, -1, next_present)
    appearance = jnp.cumsum(present.astype(I32)) - 1
    mine = tile_expert[:, None] == ids[None, :]
    lookup = lambda table: jnp.sum(jnp.where(mine, table[None, :], 0), axis=1)
    tile_next = jnp.stack([lookup(next_present), lookup(appearance % 2)], axis=1).astype(I32)
    ys = _experts(tile_expert, n_active, tile_next, xs, w_gate_all, w_up_all, w_down_all, l)
    return _combine(pos_flat, ys, gates.T, x, g, b)


def _channel_dft_matrix():
    c = np.arange(FOURIER_DIM)
    ang = 2.0 * np.pi * ((c[:, None] * c[None, :]) % FOURIER_DIM) / FOURIER_DIM
    wc = np.zeros((HALF, D_MODEL), np.float32)
    for h in range(FOURIER_HEADS):
        rows = slice(FOURIER_DIM * h, FOURIER_DIM * (h + 1))
        wc[rows, FOURIER_DIM * h:FOURIER_DIM * (h + 1)] = np.cos(ang)
        wc[rows, HALF + FOURIER_DIM * h:HALF + FOURIER_DIM * (h + 1)] = np.sin(ang)
    return jnp.asarray(wc, BF16)


def _sequence_dft_tables():
    side = _DFT_SIDE
    j = np.arange(_DFT_HALF, dtype=np.int64)[:, None]
    k = np.arange(side, dtype=np.int64)[None, :]
    ang_a = ((j * k) % side).astype(np.float64) * (2.0 * np.pi / side)
    ang_b = ((j * k) % SEQ).astype(np.float64) * (2.0 * np.pi / SEQ)
    tables = np.concatenate([np.cos(ang_a), np.sin(ang_a), np.cos(ang_b), np.sin(ang_b)], axis=1)
    col = np.arange(_DFT_HALF)
    e1 = (col[None, :] // side == np.arange(side)[:, None]).astype(np.float32)
    e2 = (col[None, :] % side == np.arange(side)[:, None]).astype(np.float32)
    r = np.arange(_REV_BLOCK)
    rev_window = (np.arange(2 * _REV_BLOCK)[None, :] == (_REV_BLOCK - r)[:, None]).astype(np.float32)
    signs = np.zeros((V7X_SUBLANES, SEQ), np.float32)
    signs[0] = 1.0 - 2.0 * (np.arange(SEQ) % 2)
    return (jnp.asarray(tables, F32), jnp.asarray(e1, BF16), jnp.asarray(e2, BF16),
            jnp.asarray(rev_window, BF16), jnp.asarray(signs, BF16))


def kernel(x, mem, pf_w_in, pf_pool_w, pf_pool_scale, pf_fourier_ln_g, pf_fourier_w, pf_w_out, hg_w_in, hg_lower_bounds, hg_norm_g, hg_w_out, xa_wq, xa_wkv, xa_wo, moe_w_group, moe_b_group, moe_w_expert, moe_b_expert, moe_w_gate, moe_w_up, moe_w_down, ln_g, ln_b):
    bf = lambda a: a.astype(BF16)
    xt = x.reshape(TOKENS, D_MODEL)
    mem2d = mem.reshape(BATCH * MEM_LEN, D_MODEL)
    lb_all = jnp.cumsum(jax.nn.softmax(hg_lower_bounds.astype(F32), axis=0), axis=0)
    lb_all = lb_all - lb_all[:1]
    tril = jnp.asarray(np.tril(np.ones((HG_CHUNK, HG_CHUNK), np.float32)), BF16)
    triu = jnp.asarray(np.triu(np.ones((HG_CHUNK, HG_CHUNK), np.float32)), BF16)
    strict_upper = jnp.asarray(np.triu(np.ones((TM, TM), np.float32), k=1), BF16)
    kv_all = _kv_proj(mem2d, xa_wkv)

    for l in range(DEPTH):
        j = l // 2
        lng = lambda s: ln_g[l, s][None, :]
        lnb = lambda s: ln_b[l, s][None, :]
        if l % 2 == 0:
            dft_consts = _sequence_dft_tables()
            ya, pq = _pf_in(xt, pf_w_in, j, pf_fourier_ln_g[j].reshape(1, HALF), _channel_dft_matrix(),
                            bf(pf_pool_w[j]), pf_pool_scale[j][None, :])
            yb = _fourier(*dft_consts, pq, bf(pf_fourier_w[j]))
            mix, heads_major, w_mix = (ya, yb), False, pf_w_out
        else:
            proj = _hg_in(xt, hg_w_in[j])
            lb = lb_all[l].reshape(2, HG_HEADS, 1, HG_DIM)
            og = _gla(proj, lb, hg_norm_g[j][None, :], tril, triu)
            mix, heads_major, w_mix = (og,), True, hg_w_out
        kv = kv_all
        wr, br = _router_params(moe_w_group[l], moe_b_group[l], moe_w_expert[l], moe_b_expert[l])
        xt, xt_tiles, eidx, gates, rank, cnt = _block(
            mix, heads_major, w_mix, j, xt, lng(0), lnb(0), xa_wq, kv, xa_wo, l, lng(1), lnb(1),
            wr, br, strict_upper)
        xt = _moe(xt, xt_tiles, eidx, gates, rank, cnt, moe_w_gate, moe_w_up, moe_w_down, l,
                  lng(2), lnb(2))
    return xt.reshape(BATCH, SEQ, D_MODEL)
```

```python
import functools

import jax
import jax.numpy as jnp
import numpy as np
from jax import lax
from jax.experimental import pallas as pl
from jax.experimental.pallas import tpu as pltpu

F32 = jnp.float32
BF16 = jnp.bfloat16
I32 = jnp.int32

D_MODEL = 1024
BATCH = 8
SEQ = 4096
DEPTH = 2
TOKENS = BATCH * SEQ
MEM_LEN = 256
HALF = D_MODEL // 2
POOL_WINDOWS = (2, 4, 8, 16)
POOL_DIM = 128
FOURIER_HEADS = 4
FOURIER_DIM = 128
HG_HEADS = 8
HG_DIM = 128
HG_N_PROJ = 5
HG_CHUNK = 64
XA_HEADS = 4
XA_HEAD_DIM = 256
MOE_GROUPS = 4
MOE_PER_GROUP = 8
MOE_EXPERTS = 32
MOE_TOPK = 2
MOE_HIDDEN = 512
DN_ALPHA = (2.0 * DEPTH) ** 0.25
LN_EPS = 1e-5

V7X_LANES = 128
V7X_SUBLANES = 8
V7X_VMEM_LIMIT_BYTES = 52 * 1024 * 1024

TM = 512
TM_ROUTE = 1024
TM_PERM = 2048
TM_COMB = 512
TE = 512
SORTED_ROWS = TOKENS * MOE_TOPK + MOE_EXPERTS * TE
N_EXPERT_TILES = SORTED_ROWS // TE
GLA_GROUP = 16
GLA_ROWS = GLA_GROUP * HG_CHUNK
FOURIER_TR = 512
ROUTER_ROWS = 40

_NT = (((1,), (1,)), ((), ()))


def _params(*sem):
    return pltpu.CompilerParams(dimension_semantics=sem, vmem_limit_bytes=V7X_VMEM_LIMIT_BYTES)


def _dot(a, b):
    return jnp.dot(a, b, preferred_element_type=F32)


def _layer_norm(y, g, b):
    mu = jnp.mean(y, axis=-1, keepdims=True)
    yc = y - mu
    var = jnp.mean(yc * yc, axis=-1, keepdims=True)
    return yc * lax.rsqrt(var + LN_EPS) * g + b


def _post_ln(x, h, g, b):
    return _layer_norm(DN_ALPHA * x + h, g, b)


def _silu(x):
    return x * jax.nn.sigmoid(x)


def _resident(shape, index):
    return pl.BlockSpec(shape, lambda *_: index, pipeline_mode=pl.Buffered(1))


def _cast_weight_once(w_ref, wbf_ref):
    @pl.when(pl.program_id(0) == 0)
    def _():
        wbf_ref[...] = w_ref[0].astype(BF16)


TM_PF = 1024
_POOL_HALO = 8


def _k_pf_in(x_ref, xp_ref, xn_ref, w_ref, lng_ref, wc_ref, pw_ref, ps_ref, ya_ref, pq_ref, wbf_ref, ext_ref):
    _cast_weight_once(w_ref, wbf_ref)
    tiles_per_seq = SEQ // TM_PF
    j = lax.rem(pl.program_id(0), tiles_per_seq)
    u = _dot(x_ref[...].astype(BF16), wbf_ref[...])
    halo = jnp.concatenate([xp_ref[...], xn_ref[...]], axis=0).astype(BF16)
    uh = _dot(halo, wbf_ref[:, :HALF])
    ext_ref[_POOL_HALO:_POOL_HALO + TM_PF, :] = u[:, :HALF]
    ext_ref[0:_POOL_HALO, :] = jnp.where(j == 0, 0.0, uh[:_POOL_HALO])
    ext_ref[_POOL_HALO + TM_PF:, :] = jnp.where(j == tiles_per_seq - 1, 0.0, uh[_POOL_HALO:])
    parts = []
    for h in range(FOURIER_HEADS):
        ub = u[:, HALF + FOURIER_DIM * h:HALF + FOURIER_DIM * (h + 1)]
        mu = jnp.mean(ub, axis=-1, keepdims=True)
        uc = ub - mu
        var = jnp.mean(uc * uc, axis=-1, keepdims=True)
        parts.append(uc * lax.rsqrt(var + LN_EPS))
    un = jnp.concatenate(parts, axis=-1) * lng_ref[...]
    pq_ref[...] = _dot(un.astype(BF16), wc_ref[...]).astype(BF16)
    pos = lax.broadcasted_iota(I32, (TM_PF, POOL_DIM), 0) + j * TM_PF
    outs = []
    for g, w in enumerate(POOL_WINDOWS):
        hw = w // 2
        cols = slice(g * POOL_DIM, (g + 1) * POOL_DIM)
        acc = ext_ref[_POOL_HALO - hw:_POOL_HALO - hw + TM_PF, cols]
        for d in range(-hw + 1, hw):
            acc = acc + ext_ref[_POOL_HALO + d:_POOL_HALO + d + TM_PF, cols]
        cnt = jnp.minimum(pos + hw, SEQ) - jnp.maximum(pos - hw, 0)
        pooled = acc / cnt.astype(F32) - ext_ref[_POOL_HALO:_POOL_HALO + TM_PF, cols]
        outs.append(_dot(pooled.astype(BF16), pw_ref[g]))
    ya_ref[...] = (jnp.concatenate(outs, axis=-1) * ps_ref[...]).astype(BF16)


def _pf_in(x, w_in_all, j, ln_g, wc, pool_w, pool_scale):
    blocks_per_tile = TM_PF // _POOL_HALO
    last_block = TOKENS // _POOL_HALO - 1
    row = lambda i: (i, 0)
    fixed = lambda i: (0, 0)
    return pl.pallas_call(
        _k_pf_in,
        out_shape=(jax.ShapeDtypeStruct((TOKENS, HALF), BF16),
                   jax.ShapeDtypeStruct((TOKENS, D_MODEL), BF16)),
        grid=(TOKENS // TM_PF,),
        in_specs=[pl.BlockSpec((TM_PF, D_MODEL), row),
                  pl.BlockSpec((_POOL_HALO, D_MODEL), lambda i: (jnp.maximum(i * blocks_per_tile - 1, 0), 0)),
                  pl.BlockSpec((_POOL_HALO, D_MODEL),
                               lambda i: (jnp.minimum((i + 1) * blocks_per_tile, last_block), 0)),
                  _resident((1, D_MODEL, D_MODEL), (j, 0, 0)),
                  pl.BlockSpec((1, HALF), fixed),
                  _resident((HALF, D_MODEL), (0, 0)),
                  pl.BlockSpec((4, POOL_DIM, POOL_DIM), lambda i: (0, 0, 0)),
                  pl.BlockSpec((1, HALF), fixed)],
        out_specs=(pl.BlockSpec((TM_PF, HALF), row),
                   pl.BlockSpec((TM_PF, D_MODEL), row)),
        scratch_shapes=[pltpu.VMEM((D_MODEL, D_MODEL), BF16),
                        pltpu.VMEM((TM_PF + 2 * _POOL_HALO, HALF), F32)],
        compiler_params=_params("arbitrary"),
        name="pf_in",
    )(x, x, x, w_in_all, ln_g, wc, pool_w, pool_scale)


_FOURIER_SCALE = 1.0 / float(np.sqrt(SEQ * FOURIER_DIM))


_DFT_SIDE = 64
_DFT_COL_CHUNK = 1024


_DFT_HALF = SEQ // 2
_REV_BLOCK = 256


def _k_fourier(tab_ref, e1_ref, e2_ref, rw_ref, sg_ref, pq_ref, fw_ref, yb_ref,
               cs_ref, ss_ref, pf_ref, qf_ref, z_ref):
    h_rows = _DFT_HALF

    @pl.when(pl.program_id(0) == 0)
    def _():
        def expand(v, e):
            hi = v.astype(BF16)
            lo = (v - hi.astype(F32)).astype(BF16)
            return _dot(hi, e) + _dot(lo, e)

        s = _DFT_SIDE
        for rb in range(h_rows // FOURIER_TR):
            rows = slice(rb * FOURIER_TR, (rb + 1) * FOURIER_TR)
            tab = tab_ref[rows, :]
            for c in range(h_rows // _DFT_COL_CHUNK):
                cols = slice(c * _DFT_COL_CHUNK, (c + 1) * _DFT_COL_CHUNK)
                ca = expand(tab[:, 0:s], e1_ref[:, cols])
                sa = expand(tab[:, s:2 * s], e1_ref[:, cols])
                cb = expand(tab[:, 2 * s:3 * s], e2_ref[:, cols])
                sb = expand(tab[:, 3 * s:4 * s], e2_ref[:, cols])
                cs_ref[rows, cols] = (ca * cb - sa * sb).astype(BF16)
                ss_ref[rows, cols] = (sa * cb + ca * sb).astype(BF16)

    def head_maps(y):
        return jnp.concatenate(
            [_dot(y[:, FOURIER_DIM * h:FOURIER_DIM * (h + 1)].astype(BF16), fw_ref[h])
             for h in range(FOURIER_HEADS)], axis=-1)

    def reversed_block(src_ref, blk):
        lo = _REV_BLOCK * (h_rows // _REV_BLOCK - 1 - blk)
        if blk == 0:
            tail = src_ref[lo:lo + _REV_BLOCK, :]
            win = jnp.concatenate([tail, jnp.zeros_like(tail)], axis=0)
        else:
            win = src_ref[lo:lo + 2 * _REV_BLOCK, :]
        return _dot(rw_ref[...], win)

    n_blocks = h_rows // _REV_BLOCK
    upper = pq_ref.at[h_rows:SEQ]
    for blk in range(n_blocks):
        rows = slice(blk * _REV_BLOCK, (blk + 1) * _REV_BLOCK)
        rev = reversed_block(upper, blk)
        low = pq_ref[rows, :].astype(F32)
        pf_ref[rows, :] = (low[:, :HALF] + rev[:, :HALF]).astype(BF16)
        qf_ref[rows, :] = (low[:, HALF:] - rev[:, HALF:]).astype(BF16)

    p_mid = pq_ref[h_rows:h_rows + 1, :HALF].astype(F32)
    for rb in range(h_rows // FOURIER_TR):
        rows = slice(rb * FOURIER_TR, (rb + 1) * FOURIER_TR)
        a = _dot(cs_ref[rows, :], pf_ref[...])
        bm = _dot(ss_ref[rows, :], qf_ref[...])
        j = lax.broadcasted_iota(I32, (FOURIER_TR, 1), 0) + rb * FOURIER_TR
        t = jnp.where((j & 1) == 0, 1.0, -1.0) * p_mid
        yb_ref[rows, :] = head_maps((a - bm + t) * _FOURIER_SCALE).astype(BF16)
        z_ref[rows, :] = head_maps((a + bm + t) * _FOURIER_SCALE).astype(BF16)

    y_mid = head_maps(_dot(sg_ref[...], pq_ref[:, :HALF]) * _FOURIER_SCALE)[0:1]
    for blk in range(n_blocks):
        up = reversed_block(z_ref, blk)
        if blk == 0:
            r = lax.broadcasted_iota(I32, (_REV_BLOCK, 1), 0)
            up = jnp.where(r == 0, y_mid, up)
        yb_ref[h_rows + blk * _REV_BLOCK:h_rows + (blk + 1) * _REV_BLOCK, :] = up.astype(BF16)


def _fourier(tables, e1, e2, rev_window, signs, pq, fw):
    h_rows = _DFT_HALF
    return pl.pallas_call(
        _k_fourier,
        out_shape=jax.ShapeDtypeStruct((TOKENS, HALF), BF16),
        grid=(BATCH,),
        in_specs=[_resident((h_rows, 4 * _DFT_SIDE), (0, 0)),
                  _resident((_DFT_SIDE, h_rows), (0, 0)),
                  _resident((_DFT_SIDE, h_rows), (0, 0)),
                  _resident((_REV_BLOCK, 2 * _REV_BLOCK), (0, 0)),
                  _resident((V7X_SUBLANES, SEQ), (0, 0)),
                  pl.BlockSpec((SEQ, D_MODEL), lambda b: (b, 0), pipeline_mode=pl.Buffered(1)),
                  pl.BlockSpec((4, FOURIER_DIM, FOURIER_DIM), lambda b: (0, 0, 0))],
        out_specs=pl.BlockSpec((SEQ, HALF), lambda b: (b, 0)),
        scratch_shapes=[pltpu.VMEM((h_rows, h_rows), BF16), pltpu.VMEM((h_rows, h_rows), BF16),
                        pltpu.VMEM((h_rows, HALF), BF16), pltpu.VMEM((h_rows, HALF), BF16),
                        pltpu.VMEM((h_rows, HALF), BF16)],
        compiler_params=_params("arbitrary"),
        name="pf_fourier",
    )(tables, e1, e2, rev_window, signs, pq, fw)


_HG_STAGE_COLS = 512


def _k_hg_in(x_ref, w_hbm, o_ref, wbf_ref, stage_ref, sem):
    @pl.when(pl.program_id(0) == 0)
    def _():
        n_chunks = HG_N_PROJ * D_MODEL // _HG_STAGE_COLS

        def chunk_copy(c):
            cols = pl.ds(c * _HG_STAGE_COLS, _HG_STAGE_COLS)
            return pltpu.make_async_copy(w_hbm.at[:, cols], stage_ref.at[c % 2], sem.at[c % 2])

        chunk_copy(0).start()
        for c in range(n_chunks):
            if c + 1 < n_chunks:
                chunk_copy(c + 1).start()
            chunk_copy(c).wait()
            wbf_ref[:, c * _HG_STAGE_COLS:(c + 1) * _HG_STAGE_COLS] = stage_ref[c % 2].astype(BF16)

    xb = x_ref[...].astype(BF16)
    for p in range(HG_N_PROJ):
        acc = _dot(xb, wbf_ref[:, p * D_MODEL:(p + 1) * D_MODEL])
        if p == 0:
            acc = _silu(acc)
        for h in range(HG_HEADS):
            o_ref[p, h] = acc[:, HG_DIM * h:HG_DIM * (h + 1)].astype(BF16)


def _hg_in(x, w_in):
    return pl.pallas_call(
        _k_hg_in,
        out_shape=jax.ShapeDtypeStruct((HG_N_PROJ, HG_HEADS, TOKENS, HG_DIM), BF16),
        grid=(TOKENS // TM,),
        in_specs=[pl.BlockSpec((TM, D_MODEL), lambda i: (i, 0)),
                  pl.BlockSpec(memory_space=pl.ANY)],
        out_specs=pl.BlockSpec((HG_N_PROJ, HG_HEADS, TM, HG_DIM), lambda i: (0, 0, i, 0)),
        scratch_shapes=[pltpu.VMEM((D_MODEL, HG_N_PROJ * D_MODEL), BF16),
                        pltpu.VMEM((2, D_MODEL, _HG_STAGE_COLS), F32),
                        pltpu.SemaphoreType.DMA((2,))],
        compiler_params=_params("arbitrary"),
        name="hg_in",
    )(x, w_in)


def _gla_prep(q, z, lbv, tri, rev):
    L = HG_CHUNK
    G = q.shape[0] // L
    f = lbv + (1.0 - lbv) * jax.nn.sigmoid(z.astype(F32))
    lf = jnp.log(f).reshape(G, L, HG_DIM)
    k3 = (1.0 - f).reshape(G, L, HG_DIM)
    q3 = q.astype(F32).reshape(G, L, HG_DIM)
    hi32 = lax.bitcast_convert_type(
        lax.bitcast_convert_type(lf, jnp.uint32) & jnp.uint32(0xFFFF0000), F32)
    lf_hi = hi32.astype(BF16)
    lf_lo = (lf - hi32).astype(BF16)
    b2 = jnp.einsum('glm,gmk->glk', jnp.broadcast_to(tri, (G, L, L)),
                    jnp.concatenate([lf_hi, lf_lo], axis=-1), preferred_element_type=F32)
    b = b2[..., :HG_DIM] + b2[..., HG_DIM:]
    mid = L // 2 if not rev else L - 1 - L // 2
    end = L - 1 if not rev else 0
    b_ref = b[:, mid:mid + 1, :]
    b_last = b[:, end:end + 1, :]
    e_pos = jnp.exp(b - b_ref)
    qt32 = q3 * e_pos
    kt32 = k3 * (1.0 / e_pos)
    ku = kt32 * jnp.exp(b_last - b_ref)
    qe = qt32 * jnp.exp(b_ref)
    dec = jnp.exp(b_last)
    return qt32.astype(BF16), kt32.astype(BF16), ku.astype(BF16), qe.astype(BF16), dec


def _gla_intra(qt, kt, ku, v, rev):
    L = HG_CHUNK
    v3 = v.reshape(qt.shape[0], L, HG_DIM)
    sc = jnp.einsum('glk,gmk->glm', qt, kt, preferred_element_type=F32)
    li = lax.broadcasted_iota(I32, (L, L), 0)
    mi = lax.broadcasted_iota(I32, (L, L), 1)
    causal = (li >= mi) if not rev else (li <= mi)
    sc = jnp.where(causal[None], sc, 0.0).astype(BF16)
    u_t = jnp.einsum('glv,glk->gvk', v3, ku, preferred_element_type=F32)
    return sc, v3, u_t


def _gla_inter(qe, dec, sc, v3, u_t, st, rev):
    G = qe.shape[0]
    outs = [None] * G
    for n in (range(G) if not rev else range(G - 1, -1, -1)):
        outs[n] = (_dot(sc[n], v3[n])
                   + lax.dot_general(qe[n], st.astype(BF16), _NT, preferred_element_type=F32))
        st = dec[n] * st + u_t[n]
    return jnp.concatenate(outs, axis=0), st


_GLA_N_OPERANDS = 5


def _k_gla(q_ref, v_ref, zf_ref, zb_ref, g_ref, lb_ref, ng_ref, lt_ref, ut_ref, o_ref, of_ref, ob_ref,
           *operand_refs):
    n_groups = SEQ // GLA_ROWS
    per_slot = 2 * _GLA_N_OPERANDS
    slots = [[operand_refs[s * per_slot + d * _GLA_N_OPERANDS:s * per_slot + (d + 1) * _GLA_N_OPERANDS]
              for d in range(2)] for s in range(2)]
    directions = ((zf_ref, lt_ref, False), (zb_ref, ut_ref, True))

    def group_rows(t, rev):
        gi = t if not rev else n_groups - 1 - t
        return slice(gi * GLA_ROWS, (gi + 1) * GLA_ROWS)

    def prep(slot, t):
        for d, (z_ref, tri_ref, rev) in enumerate(directions):
            rows = group_rows(t, rev)
            vals = _gla_prep(q_ref[0, 0, rows, :], z_ref[0, 0, rows, :], lb_ref[d, 0], tri_ref[...], rev)
            for ref, val in zip(slots[slot][d], vals):
                ref[...] = val

    def mix(slot, t, states):
        out_refs = (of_ref, ob_ref)
        intra = []
        for d, (_, _, rev) in enumerate(directions):
            qt_ref, kt_ref, ku_ref, _, _ = slots[slot][d]
            intra.append(_gla_intra(qt_ref[...], kt_ref[...], ku_ref[...],
                                    v_ref[0, 0, group_rows(t, rev), :], rev))
        new_states = []
        for d, (_, _, rev) in enumerate(directions):
            _, _, _, qe_ref, dec_ref = slots[slot][d]
            o, st = _gla_inter(qe_ref[...], dec_ref[...], *intra[d], states[d], rev)
            out_refs[d][group_rows(t, rev), :] = o
            new_states.append(st)
        return new_states

    def finish(gi):
        rows = slice(gi * GLA_ROWS, (gi + 1) * GLA_ROWS)
        o = of_ref[rows, :] + ob_ref[rows, :]
        o = o * lax.rsqrt(jnp.mean(o * o, axis=-1, keepdims=True) + LN_EPS)
        o = o * ng_ref[...] * _silu(g_ref[0, 0, rows, :].astype(F32))
        o_ref[0, rows, :] = o.astype(BF16)

    states = [jnp.zeros((HG_DIM, HG_DIM), F32)] * 2
    prep(0, 0)
    for t in range(n_groups):
        if t + 1 < n_groups:
            prep((t + 1) % 2, t + 1)
        states = mix(t % 2, t, states)
        for gi in {t, n_groups - 1 - t}:
            if max(gi, n_groups - 1 - gi) == t:
                finish(gi)


def _gla(proj, lb, norm_g, tril, triu):
    def pspec(p):
        return pl.BlockSpec((1, 1, SEQ, HG_DIM), lambda b, h: (p, h, b, 0))
    fixed = lambda b, h: (0, 0)
    return pl.pallas_call(
        _k_gla,
        out_shape=jax.ShapeDtypeStruct((HG_HEADS, TOKENS, HG_DIM), BF16),
        grid=(BATCH, HG_HEADS),
        in_specs=[pspec(0), pspec(1), pspec(2), pspec(3), pspec(4),
                  pl.BlockSpec((2, 1, 1, HG_DIM), lambda b, h: (0, h, 0, 0)),
                  pl.BlockSpec((1, HG_DIM), fixed),
                  pl.BlockSpec((HG_CHUNK, HG_CHUNK), fixed),
                  pl.BlockSpec((HG_CHUNK, HG_CHUNK), fixed)],
        out_specs=pl.BlockSpec((1, SEQ, HG_DIM), lambda b, h: (h, b, 0)),
        scratch_shapes=[pltpu.VMEM((SEQ, HG_DIM), F32), pltpu.VMEM((SEQ, HG_DIM), F32)]
        + 4 * ([pltpu.VMEM((GLA_GROUP, HG_CHUNK, HG_DIM), BF16)] * (_GLA_N_OPERANDS - 1)
               + [pltpu.VMEM((GLA_GROUP, 1, HG_DIM), F32)]),
        compiler_params=_params("parallel", "parallel"),
        name="hg_gla",
    )(proj, proj, proj, proj, proj, lb, norm_g, tril, triu)


def _k_mm(x_ref, w_ref, o_ref):
    o_ref[0] = _dot(x_ref[...].astype(BF16), w_ref[0].astype(BF16)).astype(o_ref.dtype)


def _kv_proj(mem2d, wkv_all):
    rows, cols = mem2d.shape[0], wkv_all.shape[2]
    return pl.pallas_call(
        _k_mm,
        out_shape=jax.ShapeDtypeStruct((DEPTH, rows, cols), BF16),
        grid=(DEPTH, cols // D_MODEL),
        in_specs=[pl.BlockSpec((rows, D_MODEL), lambda l, j: (0, 0)),
                  pl.BlockSpec((1, D_MODEL, D_MODEL), lambda l, j: (l, 0, j))],
        out_specs=pl.BlockSpec((1, rows, D_MODEL), lambda l, j: (l, 0, j)),
        compiler_params=_params("parallel", "parallel"),
        name="xa_kv",
    )(mem2d, wkv_all)


ROW_WORDS = D_MODEL // 2
ROW_SLABS = ROW_WORDS // V7X_LANES
U32 = jnp.uint32
_HI_MASK = 0xFFFF0000


def _to_row_tiles(ref, x):
    n = x.shape[0]
    as_bits = lambda v: lax.bitcast_convert_type(v.astype(BF16).astype(F32), U32)
    words = (as_bits(x[:, :ROW_WORDS]) >> 16) | (as_bits(x[:, ROW_WORDS:]) & U32(_HI_MASK))
    for s in range(ROW_SLABS):
        ref[pl.ds(s, n, stride=ROW_SLABS), :] = words[:, V7X_LANES * s:V7X_LANES * (s + 1)]


def _from_row_tiles(ref):
    n = ref.shape[0] // ROW_SLABS
    words = jnp.concatenate([ref[pl.ds(s, n, stride=ROW_SLABS), :] for s in range(ROW_SLABS)], axis=-1)
    lo = lax.bitcast_convert_type(words << 16, F32)
    hi = lax.bitcast_convert_type(words & U32(_HI_MASK), F32)
    return jnp.concatenate([lo, hi], axis=-1)


def _first_argmax_rows(v, n_rows):
    m = jnp.max(v, axis=0, keepdims=True)
    rows = lax.broadcasted_iota(I32, v.shape, 0)
    idx = jnp.min(jnp.where(v == m, rows, n_rows), axis=0, keepdims=True)
    return m, idx


def _route(lt):
    gl = lt[0:MOE_GROUPS, :]
    gmax, gi = _first_argmax_rows(gl, MOE_GROUPS)
    p_grp = 1.0 / jnp.sum(jnp.exp(gl - gmax), axis=0, keepdims=True)
    sel = lt[MOE_GROUPS:MOE_GROUPS + MOE_PER_GROUP, :]
    for g in range(1, MOE_GROUPS):
        lo = MOE_GROUPS + MOE_PER_GROUP * g
        sel = jnp.where(gi == g, lt[lo:lo + MOE_PER_GROUP, :], sel)
    m1, i1 = _first_argmax_rows(sel, MOE_PER_GROUP)
    rows = lax.broadcasted_iota(I32, sel.shape, 0)
    m2, i2 = _first_argmax_rows(jnp.where(rows == i1, -jnp.inf, sel), MOE_PER_GROUP)
    e2 = jnp.exp(m2 - m1)
    g1 = p_grp / (1.0 + e2)
    return (gi * MOE_PER_GROUP + i1, gi * MOE_PER_GROUP + i2), (g1, g1 * e2)


def _rank_in_expert(e0, e1, strict_upper, carry_ref, counted):
    rows = lax.broadcasted_iota(I32, (MOE_EXPERTS, e0.shape[1]), 0)
    oh0 = rows == e0
    oh1 = rows == e1
    oh = jnp.where(oh0 | oh1, 1.0, 0.0)
    before = _dot(oh.astype(BF16), strict_upper) + carry_ref[:, 0:1]
    r0 = jnp.sum(jnp.where(oh0, before, 0.0), axis=0, keepdims=True).astype(I32)
    r1 = jnp.sum(jnp.where(oh1, before, 0.0), axis=0, keepdims=True).astype(I32)
    carry_ref[...] = carry_ref[...] + counted * jnp.sum(oh, axis=1, keepdims=True)
    return r0, r1


def _k_block(heads_major, *refs):
    n_mix = 1 if heads_major else 2
    mix_refs = refs[:n_mix]
    (wm_ref, x_ref, g0_ref, b0_ref, wq_ref, kv_ref, wo_ref, g_ref, b_ref, wr_ref, br_ref, tri_ref,
     o_ref, o3_ref, e_ref, gate_ref, rank_ref, cnt_ref,
     wmbf_ref, wqbf_ref, wobf_ref, carry_ref, pre_ref) = refs[n_mix:]
    step = pl.program_id(0)
    _cast_weight_once(wm_ref, wmbf_ref)
    _cast_weight_once(wq_ref, wqbf_ref)
    _cast_weight_once(wo_ref, wobf_ref)

    @pl.when(step == 0)
    def _():
        carry_ref[...] = jnp.zeros_like(carry_ref)
        pre_ref[...] = jnp.zeros_like(pre_ref)

    quarter = TM // 4
    routed = []

    def stage2(c):
        rows = slice(c * quarter, (c + 1) * quarter)
        y = _layer_norm(pre_ref[rows, :], g_ref[...], b_ref[...])
        o_ref[rows, :] = y
        _to_row_tiles(o3_ref.at[pl.ds(c * quarter * ROW_SLABS, quarter * ROW_SLABS)], y)
        lt = lax.dot_general(wr_ref[...], y.astype(BF16), _NT, preferred_element_type=F32) + br_ref[...]
        routed.append(_route(lt))

    stage2(0)
    if heads_major:
        a = jnp.concatenate([mix_refs[0][h] for h in range(HG_HEADS)], axis=-1)
    else:
        a = jnp.concatenate([mix_refs[0][...], mix_refs[1][...]], axis=-1)
    h1 = _dot(a, wmbf_ref[...])
    stage2(1)
    x = _post_ln(x_ref[...], h1, g0_ref[...], b0_ref[...])
    q = _dot(x.astype(BF16), wqbf_ref[...]).astype(BF16)
    stage2(2)
    outs = []
    for h in range(XA_HEADS):
        cols = slice(XA_HEAD_DIM * h, XA_HEAD_DIM * (h + 1))
        kh = kv_ref[:, cols]
        vh = kv_ref[:, D_MODEL + XA_HEAD_DIM * h:D_MODEL + XA_HEAD_DIM * (h + 1)]
        s = lax.dot_general(q[:, cols], kh, _NT, preferred_element_type=F32) * (XA_HEAD_DIM ** -0.5)
        e = jnp.exp(s - jnp.max(s, axis=-1, keepdims=True))
        p = e / jnp.sum(e, axis=-1, keepdims=True)
        outs.append(_dot(p.astype(BF16), vh))
    o = jnp.concatenate(outs, axis=-1).astype(BF16)
    h2 = _dot(o, wobf_ref[...])
    stage2(3)
    e0, e1, gate0, gate1 = [jnp.concatenate([r[a_][b_] for r in routed], axis=-1)
                            for a_ in range(2) for b_ in range(2)]
    r0, r1 = _rank_in_expert(e0, e1, tri_ref[...], carry_ref, jnp.where(step > 0, 1.0, 0.0))
    e_ref[0:1, :], e_ref[1:2, :] = e0, e1
    gate_ref[0:1, :], gate_ref[1:2, :] = gate0, gate1
    rank_ref[0:1, :], rank_ref[1:2, :] = r0, r1
    cnt_ref[...] = carry_ref[...]
    pre_ref[...] = DN_ALPHA * x + h2


def _block(mix, heads_major, w_mix_all, j, x, g0, b0, wq_all, kv, wo_all, l, g1, b1, wr, br, strict_upper):
    n_tiles = TOKENS // TM
    tiles_per_seq = SEQ // TM
    cur = lambda i: jnp.minimum(i, n_tiles - 1)
    prev = lambda i: jnp.maximum(i - 1, 0)
    row_in = lambda i: (cur(i), 0)
    row_out = lambda i: (prev(i), 0)
    fixed = lambda i: (0, 0)
    tok = pl.BlockSpec((MOE_TOPK, TM), lambda i: (0, prev(i)))
    if heads_major:
        mix_specs = [pl.BlockSpec((HG_HEADS, TM, HG_DIM), lambda i: (0, cur(i), 0))]
    else:
        mix_specs = [pl.BlockSpec((TM, HALF), row_in), pl.BlockSpec((TM, HALF), row_in)]
    return pl.pallas_call(
        functools.partial(_k_block, heads_major),
        out_shape=(jax.ShapeDtypeStruct((TOKENS, D_MODEL), F32),
                   jax.ShapeDtypeStruct((TOKENS * ROW_SLABS, V7X_LANES), U32),
                   jax.ShapeDtypeStruct((MOE_TOPK, TOKENS), I32),
                   jax.ShapeDtypeStruct((MOE_TOPK, TOKENS), F32),
                   jax.ShapeDtypeStruct((MOE_TOPK, TOKENS), I32),
                   jax.ShapeDtypeStruct((MOE_EXPERTS, V7X_LANES), F32)),
        grid=(n_tiles + 1,),
        in_specs=mix_specs + [
            _resident((1, D_MODEL, D_MODEL), (j, 0, 0)),
            pl.BlockSpec((TM, D_MODEL), row_in),
            pl.BlockSpec((1, D_MODEL), fixed), pl.BlockSpec((1, D_MODEL), fixed),
            _resident((1, D_MODEL, D_MODEL), (l, 0, 0)),
            pl.BlockSpec((None, MEM_LEN, 2 * D_MODEL), lambda i: (l, cur(i) // tiles_per_seq, 0)),
            _resident((1, D_MODEL, D_MODEL), (l, 0, 0)),
            pl.BlockSpec((1, D_MODEL), fixed), pl.BlockSpec((1, D_MODEL), fixed),
            _resident((ROUTER_ROWS, D_MODEL), (0, 0)),
            _resident((ROUTER_ROWS, 1), (0, 0)),
            _resident((TM, TM), (0, 0))],
        out_specs=(pl.BlockSpec((TM, D_MODEL), row_out),
                   pl.BlockSpec((TM * ROW_SLABS, V7X_LANES), row_out),
                   tok, tok, tok,
                   pl.BlockSpec((MOE_EXPERTS, V7X_LANES), fixed)),
        scratch_shapes=[pltpu.VMEM((D_MODEL, D_MODEL), BF16), pltpu.VMEM((D_MODEL, D_MODEL), BF16),
                        pltpu.VMEM((D_MODEL, D_MODEL), BF16), pltpu.VMEM((MOE_EXPERTS, V7X_LANES), F32),
                        pltpu.VMEM((TM, D_MODEL), F32)],
        compiler_params=_params("arbitrary"),
        name="attn_block",
    )(*mix, w_mix_all, x, g0, b0, wq_all, kv, wo_all, g1, b1, wr, br, strict_upper)


def _k_pos(e_ref, rank_ref, off_ref, pos_ref):
    rows = lax.broadcasted_iota(I32, (MOE_EXPERTS, TM_ROUTE), 0)
    for k in range(MOE_TOPK):
        start = jnp.sum(jnp.where(rows == e_ref[k:k + 1, :], off_ref[...], 0), axis=0, keepdims=True)
        pos_ref[k:k + 1, :] = start + rank_ref[k:k + 1, :]


def _positions(eidx, rank, offsets):
    tok = pl.BlockSpec((MOE_TOPK, TM_ROUTE), lambda i: (0, i))
    return pl.pallas_call(
        _k_pos,
        out_shape=jax.ShapeDtypeStruct((MOE_TOPK, TOKENS), I32),
        grid=(TOKENS // TM_ROUTE,),
        in_specs=[tok, tok, pl.BlockSpec((MOE_EXPERTS, 1), lambda i: (0, 0))],
        out_specs=tok,
        compiler_params=_params("parallel"),
        name="moe_pos",
    )(eidx, rank, offsets)


_ISSUE_BATCH = 8


def _row_copy(src_ref, src_row, dst_ref, dst_row, sem):
    src = pl.ds(pl.multiple_of(src_row * ROW_SLABS, ROW_SLABS), ROW_SLABS)
    dst = pl.ds(pl.multiple_of(dst_row * ROW_SLABS, ROW_SLABS), ROW_SLABS)
    return pltpu.make_async_copy(src_ref.at[src], dst_ref.at[dst], sem)


_DISPATCH_SLOTS = 4


def _k_dispatch(pos_ref, ends_ref, x_ref, xs_ref, zero_ref, stage_ref, row_sems, load_sems, zsem):
    step = pl.program_id(0)
    n_steps = pl.num_programs(0)
    block_rows = TM_PERM * ROW_SLABS

    def load(tile):
        slot = lax.rem(tile, _DISPATCH_SLOTS)
        src = x_ref.at[pl.ds(pl.multiple_of(tile * block_rows, block_rows), block_rows)]
        return pltpu.make_async_copy(src, stage_ref.at[slot], load_sems.at[slot])

    def drain_rows(tile):
        slot = lax.rem(tile, _DISPATCH_SLOTS)
        for _ in range(MOE_TOPK):
            pltpu.make_async_copy(stage_ref.at[slot], xs_ref.at[pl.ds(0, block_rows)],
                                  row_sems.at[slot]).wait()

    @pl.when(step == 0)
    def _():
        load(0).start()
        pl.when(n_steps > 1)(load(1).start)
        zero_ref[...] = jnp.zeros_like(zero_ref)

        def last_tile(e):
            seg_start = ends_ref[e - 1] if e > 0 else 0
            start = ends_ref[e] - TE
            return ends_ref[e] > seg_start, zero_fill(start)

        def unused_tile(t):
            start = ends_ref[MOE_EXPERTS - 1] + t * TE
            return start < SORTED_ROWS, zero_fill(start)

        def zero_fill(start_row):
            rows = pl.ds(pl.multiple_of(start_row * ROW_SLABS, TE * ROW_SLABS), TE * ROW_SLABS)
            return pltpu.make_async_copy(zero_ref, xs_ref.at[rows], zsem)

        fills = [last_tile(e) for e in range(MOE_EXPERTS)] + [unused_tile(t) for t in range(MOE_EXPERTS)]
        for needed, copy in fills:
            pl.when(needed)(copy.start)
        for needed, copy in fills:
            pl.when(needed)(copy.wait)

    pl.when(step >= 2)(lambda: drain_rows(step - 2))
    pl.when(step + 2 < n_steps)(lambda: load(step + 2).start())
    load(step).wait()

    slot = lax.rem(step, _DISPATCH_SLOTS)
    base = step * TM_PERM

    def body(jb, c):
        j0 = jb * _ISSUE_BATCH
        dst = [[pos_ref[k * TOKENS + base + j0 + u] for k in range(MOE_TOPK)]
               for u in range(_ISSUE_BATCH)]
        for u in range(_ISSUE_BATCH):
            for k in range(MOE_TOPK):
                _row_copy(stage_ref.at[slot], j0 + u, xs_ref, dst[u][k],
                          row_sems.at[slot]).start(priority=k)
        return c

    lax.fori_loop(0, TM_PERM // _ISSUE_BATCH, body, 0)

    @pl.when(step == n_steps - 1)
    def _():
        pl.when(step >= 1)(lambda: drain_rows(step - 1))
        drain_rows(step)


def _dispatch(pos_flat, ends, x):
    return pl.pallas_call(
        _k_dispatch,
        out_shape=jax.ShapeDtypeStruct((SORTED_ROWS * ROW_SLABS, V7X_LANES), U32),
        grid_spec=pltpu.PrefetchScalarGridSpec(
            num_scalar_prefetch=2,
            grid=(TOKENS // TM_PERM,),
            in_specs=[pl.BlockSpec(memory_space=pl.ANY)],
            out_specs=pl.BlockSpec(memory_space=pl.ANY),
            scratch_shapes=[pltpu.VMEM((TE * ROW_SLABS, V7X_LANES), U32),
                            pltpu.VMEM((_DISPATCH_SLOTS, TM_PERM * ROW_SLABS, V7X_LANES), U32),
                            pltpu.SemaphoreType.DMA((_DISPATCH_SLOTS,)),
                            pltpu.SemaphoreType.DMA((_DISPATCH_SLOTS,)),
                            pltpu.SemaphoreType.DMA],
        ),
        compiler_params=_params("arbitrary"),
        name="moe_dispatch",
    )(pos_flat, ends, x)


def _k_experts(te_ref, na_ref, next_ref, xs_ref, wg_hbm, wu_hbm, wd_hbm, ys_ref,
               wgubf_ref, wdbf_ref, stage_g, stage_u, stage_d, sems, *, layer):
    i = pl.program_id(0)
    active = i < na_ref[0]
    expert = te_ref[i]
    new_expert = jnp.logical_or(i == 0, te_ref[jnp.maximum(i - 1, 0)] != expert)

    def weight_copies(e, slot):
        return (pltpu.make_async_copy(wg_hbm.at[layer, e], stage_g.at[slot], sems.at[slot, 0]),
                pltpu.make_async_copy(wu_hbm.at[layer, e], stage_u.at[slot], sems.at[slot, 1]),
                pltpu.make_async_copy(wd_hbm.at[layer, e], stage_d.at[slot], sems.at[slot, 2]))

    @pl.when(jnp.logical_and(active, new_expert))
    def _():
        slot = next_ref[i, 1]

        @pl.when(i == 0)
        def _():
            for c in weight_copies(expert, slot):
                c.start()

        upcoming = next_ref[i, 0]

        @pl.when(upcoming >= 0)
        def _():
            for c in weight_copies(upcoming, 1 - slot):
                c.start()

        for c in weight_copies(expert, slot):
            c.wait()
        wgubf_ref[:, :MOE_HIDDEN] = stage_g[slot].astype(BF16)
        wgubf_ref[:, MOE_HIDDEN:] = stage_u[slot].astype(BF16)
        wdbf_ref[...] = stage_d[slot].astype(BF16)

    @pl.when(active)
    def _():
        xb = _from_row_tiles(xs_ref).astype(BF16)
        gu = _dot(xb, wgubf_ref[...])
        h = _silu(gu[:, :MOE_HIDDEN]) * gu[:, MOE_HIDDEN:]
        _to_row_tiles(ys_ref, _dot(h.astype(BF16), wdbf_ref[...]))

    @pl.when(jnp.logical_not(active))
    def _():
        ys_ref[...] = jnp.zeros_like(ys_ref)


def _experts(tile_expert, n_active, tile_next, xs, wg_all, wu_all, wd_all, l):
    def xs_index(i, te, na, nxt):
        return (jnp.minimum(i, na[0] - 1), 0)

    return pl.pallas_call(
        functools.partial(_k_experts, layer=l),
        out_shape=jax.ShapeDtypeStruct((SORTED_ROWS * ROW_SLABS, V7X_LANES), U32),
        grid_spec=pltpu.PrefetchScalarGridSpec(
            num_scalar_prefetch=3,
            grid=(N_EXPERT_TILES,),
            in_specs=[pl.BlockSpec((TE * ROW_SLABS, V7X_LANES), xs_index),
                      pl.BlockSpec(memory_space=pl.ANY),
                      pl.BlockSpec(memory_space=pl.ANY),
                      pl.BlockSpec(memory_space=pl.ANY)],
            out_specs=pl.BlockSpec((TE * ROW_SLABS, V7X_LANES), lambda i, te, na, nxt: (i, 0)),
            scratch_shapes=[pltpu.VMEM((D_MODEL, 2 * MOE_HIDDEN), BF16),
                            pltpu.VMEM((MOE_HIDDEN, D_MODEL), BF16),
                            pltpu.VMEM((2, D_MODEL, MOE_HIDDEN), F32),
                            pltpu.VMEM((2, D_MODEL, MOE_HIDDEN), F32),
                            pltpu.VMEM((2, MOE_HIDDEN, D_MODEL), F32),
                            pltpu.SemaphoreType.DMA((2, 3))],
        ),
        compiler_params=_params("arbitrary"),
        name="moe_experts",
    )(tile_expert, n_active, tile_next, xs, wg_all, wu_all, wd_all)


_COMBINE_CHUNKS = 8
_COMBINE_SLOTS = 3


def _k_combine(pos_ref, ys_ref, gt_ref, x_ref, g_ref, b_ref, o_ref, buf_ref, sem):
    step = pl.program_id(0)
    last = pl.num_programs(0) - 1

    def issue_batch(base, j0, slot):
        src = [[pos_ref[k * TOKENS + base + j0 + u] for k in range(MOE_TOPK)]
               for u in range(_ISSUE_BATCH)]
        for u in range(_ISSUE_BATCH):
            for k in range(MOE_TOPK):
                _row_copy(ys_ref, src[u][k], buf_ref.at[slot, k], j0 + u,
                          sem.at[slot]).start(priority=k)

    def drain(slot):
        for k in range(MOE_TOPK):
            pltpu.make_async_copy(ys_ref.at[pl.ds(0, TM_COMB * ROW_SLABS)], buf_ref.at[slot, k],
                                  sem.at[slot]).wait()

    @pl.when(step == 0)
    def _():
        for tile in range(_COMBINE_SLOTS - 1):
            def body(jb, c, tile=tile):
                issue_batch(jnp.minimum(tile, last) * TM_COMB, jb * _ISSUE_BATCH, tile)
                return c
            lax.fori_loop(0, TM_COMB // _ISSUE_BATCH, body, 0)

    slot = lax.rem(step, _COMBINE_SLOTS)
    ahead_slot = lax.rem(step + _COMBINE_SLOTS - 1, _COMBINE_SLOTS)
    drain(slot)
    next_base = jnp.minimum(step + _COMBINE_SLOTS - 1, last) * TM_COMB
    chunk = TM_COMB // _COMBINE_CHUNKS
    for c in range(_COMBINE_CHUNKS):
        for jb in range(chunk // _ISSUE_BATCH):
            issue_batch(next_base, c * chunk + jb * _ISSUE_BATCH, ahead_slot)
        rows = slice(c * chunk, (c + 1) * chunk)
        tiles = pl.ds(c * chunk * ROW_SLABS, chunk * ROW_SLABS)
        h = (gt_ref[rows, 0:1] * _from_row_tiles(buf_ref.at[slot, 0, tiles])
             + gt_ref[rows, 1:2] * _from_row_tiles(buf_ref.at[slot, 1, tiles]))
        o_ref[rows, :] = _post_ln(x_ref[rows, :], h, g_ref[...], b_ref[...])

    @pl.when(step == last)
    def _():
        for ahead in range(1, _COMBINE_SLOTS):
            drain(lax.rem(step + ahead, _COMBINE_SLOTS))


def _combine(pos_flat, ys, gates_t, x, g, b):
    row = lambda i, pos: (i, 0)
    fixed = lambda i, pos: (0, 0)
    return pl.pallas_call(
        _k_combine,
        out_shape=jax.ShapeDtypeStruct((TOKENS, D_MODEL), F32),
        grid_spec=pltpu.PrefetchScalarGridSpec(
            num_scalar_prefetch=1,
            grid=(TOKENS // TM_COMB,),
            in_specs=[pl.BlockSpec(memory_space=pl.ANY),
                      pl.BlockSpec((TM_COMB, MOE_TOPK), row),
                      pl.BlockSpec((TM_COMB, D_MODEL), row),
                      pl.BlockSpec((1, D_MODEL), fixed), pl.BlockSpec((1, D_MODEL), fixed)],
            out_specs=pl.BlockSpec((TM_COMB, D_MODEL), row),
            scratch_shapes=[pltpu.VMEM((_COMBINE_SLOTS, MOE_TOPK, TM_COMB * ROW_SLABS, V7X_LANES), U32),
                            pltpu.SemaphoreType.DMA((_COMBINE_SLOTS,))],
        ),
        compiler_params=_params("arbitrary"),
        name="moe_combine",
    )(pos_flat, ys, gates_t, x, g, b)


def _router_params(w_group, b_group, w_expert, b_expert):
    pad = ROUTER_ROWS - MOE_GROUPS - MOE_EXPERTS
    wr = jnp.concatenate([w_group.T, w_expert.T, jnp.zeros((pad, D_MODEL), F32)], axis=0)
    br = jnp.concatenate([b_group, b_expert, jnp.zeros((pad,), F32)])[:, None]
    return wr.astype(BF16), br.astype(F32)


def _moe(x, x_tiles, eidx, gates, rank, cnt, w_gate_all, w_up_all, w_down_all, l, g, b):
    counts = cnt[:, 0].astype(I32)
    padded = ((counts + TE - 1) // TE) * TE
    ends = jnp.cumsum(padded)
    offsets = ends - padded
    tile_start = jnp.arange(N_EXPERT_TILES, dtype=I32) * TE
    tile_expert = jnp.sum((tile_start[:, None] >= ends[None, :]).astype(I32), axis=1)
    tile_expert = jnp.minimum(tile_expert, MOE_EXPERTS - 1)
    n_active = (ends[-1:] // TE).astype(I32)
    pos = _positions(eidx, rank, offsets[:, None])
    pos_flat = pos.reshape(-1)
    xs = _dispatch(pos_flat, ends.astype(I32), x_tiles)
    present = padded > 0
    ids = jnp.arange(MOE_EXPERTS, dtype=I32)
    later = jnp.where(present[None, :] & (ids[None, :] > ids[:, None]), ids[None, :], MOE_EXPERTS)
    next_present = jnp.min(later, axis=1)
    next_present = jnp.where(next_present >= MOE_EXPERTS, -1, next_present)
    appearance = jnp.cumsum(present.astype(I32)) - 1
    mine = tile_expert[:, None] == ids[None, :]
    lookup = lambda table: jnp.sum(jnp.where(mine, table[None, :], 0), axis=1)
    tile_next = jnp.stack([lookup(next_present), lookup(appearance % 2)], axis=1).astype(I32)
    ys = _experts(tile_expert, n_active, tile_next, xs, w_gate_all, w_up_all, w_down_all, l)
    return _combine(pos_flat, ys, gates.T, x, g, b)


def _channel_dft_matrix():
    c = np.arange(FOURIER_DIM)
    ang = 2.0 * np.pi * ((c[:, None] * c[None, :]) % FOURIER_DIM) / FOURIER_DIM
    wc = np.zeros((HALF, D_MODEL), np.float32)
    for h in range(FOURIER_HEADS):
        rows = slice(FOURIER_DIM * h, FOURIER_DIM * (h + 1))
        wc[rows, FOURIER_DIM * h:FOURIER_DIM * (h + 1)] = np.cos(ang)
        wc[rows, HALF + FOURIER_DIM * h:HALF + FOURIER_DIM * (h + 1)] = np.sin(ang)
    return jnp.asarray(wc, BF16)


def _sequence_dft_tables():
    side = _DFT_SIDE
    j = np.arange(_DFT_HALF, dtype=np.int64)[:, None]
    k = np.arange(side, dtype=np.int64)[None, :]
    ang_a = ((j * k) % side).astype(np.float64) * (2.0 * np.pi / side)
    ang_b = ((j * k) % SEQ).astype(np.float64) * (2.0 * np.pi / SEQ)
    tables = np.concatenate([np.cos(ang_a), np.sin(ang_a), np.cos(ang_b), np.sin(ang_b)], axis=1)
    col = np.arange(_DFT_HALF)
    e1 = (col[None, :] // side == np.arange(side)[:, None]).astype(np.float32)
    e2 = (col[None, :] % side == np.arange(side)[:, None]).astype(np.float32)
    r = np.arange(_REV_BLOCK)
    rev_window = (np.arange(2 * _REV_BLOCK)[None, :] == (_REV_BLOCK - r)[:, None]).astype(np.float32)
    signs = np.zeros((V7X_SUBLANES, SEQ), np.float32)
    signs[0] = 1.0 - 2.0 * (np.arange(SEQ) % 2)
    return (jnp.asarray(tables, F32), jnp.asarray(e1, BF16), jnp.asarray(e2, BF16),
            jnp.asarray(rev_window, BF16), jnp.asarray(signs, BF16))


def kernel(x, mem, pf_w_in, pf_pool_w, pf_pool_scale, pf_fourier_ln_g, pf_fourier_w, pf_w_out, hg_w_in, hg_lower_bounds, hg_norm_g, hg_w_out, xa_wq, xa_wkv, xa_wo, moe_w_group, moe_b_group, moe_w_expert, moe_b_expert, moe_w_gate, moe_w_up, moe_w_down, ln_g, ln_b):
    bf = lambda a: a.astype(BF16)
    xt = x.reshape(TOKENS, D_MODEL)
    mem2d = mem.reshape(BATCH * MEM_LEN, D_MODEL)
    lb_all = jnp.cumsum(jax.nn.softmax(hg_lower_bounds.astype(F32), axis=0), axis=0)
    lb_all = lb_all - lb_all[:1]
    tril = jnp.asarray(np.tril(np.ones((HG_CHUNK, HG_CHUNK), np.float32)), BF16)
    triu = jnp.asarray(np.triu(np.ones((HG_CHUNK, HG_CHUNK), np.float32)), BF16)
    strict_upper = jnp.asarray(np.triu(np.ones((TM, TM), np.float32), k=1), BF16)
    kv_all = _kv_proj(mem2d, xa_wkv)

    for l in range(DEPTH):
        j = l // 2
        lng = lambda s: ln_g[l, s][None, :]
        lnb = lambda s: ln_b[l, s][None, :]
        if l % 2 == 0:
            dft_consts = _sequence_dft_tables()
            ya, pq = _pf_in(xt, pf_w_in, j, pf_fourier_ln_g[j].reshape(1, HALF), _channel_dft_matrix(),
                            bf(pf_pool_w[j]), pf_pool_scale[j][None, :])
            yb = _fourier(*dft_consts, pq, bf(pf_fourier_w[j]))
            mix, heads_major, w_mix = (ya, yb), False, pf_w_out
        else:
            proj = _hg_in(xt, hg_w_in[j])
            lb = lb_all[l].reshape(2, HG_HEADS, 1, HG_DIM)
            og = _gla(proj, lb, hg_norm_g[j][None, :], tril, triu)
            mix, heads_major, w_mix = (og,), True, hg_w_out
        kv = kv_all
        wr, br = _router_params(moe_w_group[l], moe_b_group[l], moe_w_expert[l], moe_b_expert[l])
        xt, xt_tiles, eidx, gates, rank, cnt = _block(
            mix, heads_major, w_mix, j, xt, lng(0), lnb(0), xa_wq, kv, xa_wo, l, lng(1), lnb(1),
            wr, br, strict_upper)
        xt = _moe(xt, xt_tiles, eidx, gates, rank, cnt, moe_w_gate, moe_w_up, moe_w_down, l,
                  lng(2), lnb(2))
    return xt.reshape(BATCH, SEQ, D_MODEL)
```
